```python
import math
import jax
import jax.numpy as jnp
from jax import lax
import numpy as np

D_MODEL = 2048
BATCH = 8
SEQ = 2048
DEPTH = 1

EPS = 1e-6
NEG = -1e30
HA = D_MODEL // 128
DK_A = 128
DV_A = 128
CONV_W = 4
CHUNK = 64
HB = D_MODEL // 128
G_KV = 4
R_GRP = HB // G_KV
DH_B = 128
ROT_DIM = DH_B // 4
ROPE_THETA = 500000.0
L_CMP = 32
STRIDE_CMP = 16
L_SLC = 64
T_SEL = 8
WINDOW = 512
Q_BLK = 32
N_EXP = 32
TOP_K = 4
D_FF = D_MODEL
SWIGLU_LIMIT = 7.0
SWIGLU_ALPHA = 1.702
MOE_BLK = 512
SPLIT_SIZES = (HA * DK_A, HA * DK_A, HA * DV_A, HA * DV_A, HA, HA,
               HB * DH_B, G_KV * DH_B, G_KV * DH_B, G_KV * DH_B, G_KV * DH_B, G_KV * DH_B, G_KV * DH_B,
               3 * HB, 2 * D_MODEL)
SPLIT_POINTS = tuple(sum(SPLIT_SIZES[:i + 1]) for i in range(len(SPLIT_SIZES) - 1))
D_IN = sum(SPLIT_SIZES)
CONV_DIM = 2 * HA * DK_A + HA * DV_A

kernel_name = 'hybrid_gdn_nsa_moe_block'


def rms_norm(x, gain):
    xf = x.astype(jnp.float32)
    y = xf * lax.rsqrt(jnp.mean(xf * xf, axis=-1, keepdims=True) + EPS)
    return (y * gain.astype(jnp.float32)).astype(x.dtype)


def l2_normalize(x):
    return x * lax.rsqrt(jnp.sum(x * x, axis=-1, keepdims=True) + EPS)


def partial_rotary(x, positions):
    half = ROT_DIM // 2
    inv_freq = ROPE_THETA ** (-jnp.arange(half, dtype=jnp.float32) * 2.0 / ROT_DIM)
    ang = positions.astype(jnp.float32)[..., None] * inv_freq
    cos = jnp.cos(ang)[:, :, None, :]
    sin = jnp.sin(ang)[:, :, None, :]
    xf = x.astype(jnp.float32)
    x1, x2 = xf[..., :half], xf[..., half:ROT_DIM]
    out = jnp.concatenate([x1 * cos - x2 * sin, x2 * cos + x1 * sin, xf[..., ROT_DIM:]], axis=-1)
    return out.astype(x.dtype)


def causal_short_conv(x, w):
    s = x.shape[1]
    xp = jnp.pad(x, ((0, 0), (CONV_W - 1, 0), (0, 0)))
    y = xp[:, 0:s] * w[0]
    for i in range(1, CONV_W):
        y = y + xp[:, i:i + s] * w[i]
    return y


def chunk_gated_delta_rule(q, k, v, g, beta):
    b, h, s, dk = q.shape
    dv = v.shape[-1]
    n = s // CHUNK
    q = q.reshape(b, h, n, CHUNK, dk)
    k = k.reshape(b, h, n, CHUNK, dk)
    v = v.reshape(b, h, n, CHUNK, dv)
    g = g.reshape(b, h, n, CHUNK)
    beta = beta.reshape(b, h, n, CHUNK)
    gc = jnp.cumsum(g, axis=-1)
    idx = jnp.arange(CHUNK)
    incl = idx[:, None] >= idx[None, :]
    strict = idx[:, None] > idx[None, :]
    decay = jnp.exp(jnp.where(incl, gc[..., :, None] - gc[..., None, :], -jnp.inf))
    k_beta = k * beta[..., None]
    v_beta = v * beta[..., None]
    lmat = jnp.where(strict, jnp.einsum('bhncd,bhnmd->bhncm', k_beta, k) * decay, 0.0)
    tmat = lmat + jnp.eye(CHUNK, dtype=q.dtype)
    u = lax.linalg.triangular_solve(tmat, v_beta, left_side=True, lower=True, unit_diagonal=True)
    w = lax.linalg.triangular_solve(tmat, k_beta * jnp.exp(gc)[..., None], left_side=True, lower=True,
                                    unit_diagonal=True)
    a_intra = jnp.einsum('bhncd,bhnmd->bhncm', q, k) * decay
    xs = tuple(jnp.moveaxis(t, 2, 0) for t in (q, k, u, w, gc, a_intra))

    def step(state, inp):
        qi, ki, ui, wi, gi, ai = inp
        v_new = ui - jnp.einsum('bhcd,bhde->bhce', wi, state)
        o = (jnp.einsum('bhcd,bhde->bhce', qi * jnp.exp(gi)[..., None], state)
             + jnp.einsum('bhcm,bhme->bhce', ai, v_new))
        g_last = gi[..., -1]
        k_dec = ki * jnp.exp(g_last[..., None] - gi)[..., None]
        state = state * jnp.exp(g_last)[..., None, None] + jnp.einsum('bhcd,bhce->bhde', k_dec, v_new)
        return state, o

    s0 = jnp.zeros((b, h, dk, dv), q.dtype)
    _, o = lax.scan(step, s0, xs)
    return jnp.moveaxis(o, 0, 2).reshape(b, h, s, dv)


def gated_deltanet(qa, ka, va, za, a_in, b_in, conv_w, a_log, dt_bias, norm_gdn):
    bsz, s, _ = qa.shape
    qkv = jax.nn.silu(causal_short_conv(jnp.concatenate([qa, ka, va], axis=-1), conv_w)).astype(jnp.float32)
    q, k, v = jnp.split(qkv, [HA * DK_A, 2 * HA * DK_A], axis=-1)
    q = l2_normalize(q.reshape(bsz, s, HA, DK_A).transpose(0, 2, 1, 3)) * (DK_A ** -0.5)
    k = l2_normalize(k.reshape(bsz, s, HA, DK_A).transpose(0, 2, 1, 3))
    v = v.reshape(bsz, s, HA, DV_A).transpose(0, 2, 1, 3)
    g = -jnp.exp(a_log.astype(jnp.float32)) * jax.nn.softplus(a_in.astype(jnp.float32) + dt_bias.astype(jnp.float32))
    beta = jax.nn.sigmoid(b_in.astype(jnp.float32))
    o = chunk_gated_delta_rule(q, k, v, g.transpose(0, 2, 1), beta.transpose(0, 2, 1))
    o = o.transpose(0, 2, 1, 3)
    o = rms_norm(o, norm_gdn) * jax.nn.silu(za.reshape(bsz, s, HA, DV_A).astype(jnp.float32))
    return o.reshape(bsz, s, HA * DV_A).astype(qa.dtype)


def compress_blocks(kv, pe, w1, w2):
    bsz, s, g, dh = kv.shape
    n_cmp = (s - L_CMP) // STRIDE_CMP + 1
    idx = np.arange(n_cmp)[:, None] * STRIDE_CMP + np.arange(L_CMP)[None, :]
    blk = kv[:, idx] + pe[None, None, :, None, :]
    flat = blk.transpose(0, 1, 3, 2, 4).reshape(bsz, n_cmp, g, L_CMP * dh)
    return jax.nn.gelu(flat @ w1) @ w2


def nsa_local_branches(q, k_slc, v_slc, k_win, v_win, sel):
    bsz, s, g, r, dh = q.shape
    scale = dh ** -0.5
    n_slc = s // L_SLC
    kblk = k_slc.reshape(bsz, n_slc, L_SLC, g, dh).transpose(0, 3, 1, 2, 4)
    vblk = v_slc.reshape(bsz, n_slc, L_SLC, g, dh).transpose(0, 3, 1, 2, 4)
    kpad = jnp.pad(k_win, ((0, 0), (WINDOW, 0), (0, 0), (0, 0)))
    vpad = jnp.pad(v_win, ((0, 0), (WINDOW, 0), (0, 0), (0, 0)))
    b_ix = jnp.arange(bsz)[:, None, None, None]
    g_ix = jnp.arange(g)[None, :, None, None]
    t_sel = sel.shape[-1]

    def body(qb):
        t0 = qb * Q_BLK
        t = t0 + jnp.arange(Q_BLK)
        qblk = lax.dynamic_slice_in_dim(q, t0, Q_BLK, axis=1)
        sb = lax.dynamic_slice_in_dim(sel, t0, Q_BLK, axis=2)
        ks = kblk[b_ix, g_ix, sb]
        vs = vblk[b_ix, g_ix, sb]
        s_sel = jnp.einsum('bqgrd,bgqkld->bgrqkl', qblk, ks).astype(jnp.float32) * scale
        key_pos = sb[..., None] * L_SLC + jnp.arange(L_SLC)
        ok = key_pos <= t[None, None, :, None, None]
        s_sel = jnp.where(ok[:, :, None], s_sel, NEG).reshape(bsz, g, r, Q_BLK, t_sel * L_SLC)
        p_sel = jax.nn.softmax(s_sel, axis=-1).astype(vs.dtype)
        o_slc = jnp.einsum('bgrqn,bgqnd->bqgrd', p_sel, vs.reshape(bsz, g, Q_BLK, t_sel * L_SLC, dh))
        kw = lax.dynamic_slice_in_dim(kpad, t0, WINDOW + Q_BLK, axis=1)
        vw = lax.dynamic_slice_in_dim(vpad, t0, WINDOW + Q_BLK, axis=1)
        j = t0 - WINDOW + jnp.arange(WINDOW + Q_BLK)
        wmask = (j[None, :] <= t[:, None]) & (j[None, :] > t[:, None] - WINDOW) & (j[None, :] >= 0)
        s_win = jnp.einsum('bqgrd,bkgd->bgrqk', qblk, kw).astype(jnp.float32) * scale
        p_win = jax.nn.softmax(jnp.where(wmask, s_win, NEG), axis=-1).astype(vw.dtype)
        o_win = jnp.einsum('bgrqk,bkgd->bqgrd', p_win, vw)
        return o_slc, o_win

    o_slc, o_win = lax.map(body, jnp.arange(s // Q_BLK))
    o_slc = jnp.moveaxis(o_slc, 0, 1).reshape(bsz, s, g, r, dh)
    o_win = jnp.moveaxis(o_win, 0, 1).reshape(bsz, s, g, r, dh)
    return o_slc, o_win


def native_sparse_attention(qb, kc, vc, ks, vs, kw, vw, gb, positions, pe_ck, w1_ck, w2_ck, pe_cv, w1_cv, w2_cv):
    bsz, s, _ = qb.shape
    scale = DH_B ** -0.5
    q = partial_rotary(qb.reshape(bsz, s, HB, DH_B), positions).reshape(bsz, s, G_KV, R_GRP, DH_B)
    kc = partial_rotary(kc.reshape(bsz, s, G_KV, DH_B), positions)
    ks = partial_rotary(ks.reshape(bsz, s, G_KV, DH_B), positions)
    kw = partial_rotary(kw.reshape(bsz, s, G_KV, DH_B), positions)
    vc = vc.reshape(bsz, s, G_KV, DH_B)
    vs = vs.reshape(bsz, s, G_KV, DH_B)
    vw = vw.reshape(bsz, s, G_KV, DH_B)
    k_cmp = compress_blocks(kc, pe_ck, w1_ck, w2_ck)
    v_cmp = compress_blocks(vc, pe_cv, w1_cv, w2_cv)
    n_cmp = k_cmp.shape[1]
    t = jnp.arange(s)
    cmp_valid = (jnp.arange(n_cmp) * STRIDE_CMP + L_CMP - 1)[None, :] <= t[:, None]
    s_cmp = jnp.einsum('bsgrd,bcgd->bgrsc', q, k_cmp).astype(jnp.float32) * scale
    p_cmp = jax.nn.softmax(jnp.where(cmp_valid, s_cmp, NEG), axis=-1)
    p_cmp = jnp.where(jnp.any(cmp_valid, axis=-1)[:, None], p_cmp, 0.0)
    o_cmp = jnp.einsum('bgrsc,bcgd->bsgrd', p_cmp.astype(v_cmp.dtype), v_cmp)
    n_slc = s // L_SLC
    c_start = np.arange(n_cmp) * STRIDE_CMP
    j_start = np.arange(n_slc) * L_SLC
    overlap = ((c_start[:, None] < j_start[None, :] + L_SLC)
               & (c_start[:, None] + L_CMP > j_start[None, :])).astype(np.float32)
    imp = jnp.einsum('bgrsc,cj->bgsj', p_cmp, jnp.asarray(overlap))
    blk = jnp.arange(n_slc)
    cur = t // L_SLC
    forced = (blk[None, :] == 0) | (blk[None, :] == cur[:, None])
    imp = jnp.where(forced, jnp.inf, jnp.where(blk[None, :] <= cur[:, None], imp, -jnp.inf))
    _, sel = lax.top_k(imp, min(T_SEL, n_slc))
    o_slc, o_win = nsa_local_branches(q, ks, vs, kw, vw, sel)
    gates = jax.nn.sigmoid(gb.astype(jnp.float32)).reshape(bsz, s, G_KV, R_GRP, 3)
    o = gates[..., 0:1] * o_cmp + gates[..., 1:2] * o_slc + gates[..., 2:3] * o_win
    return o.reshape(bsz, s, HB * DH_B).astype(qb.dtype)


def moe_ffn(h, w_router, b_router, w_gate, b_gate, w_up, b_up, w_down, b_down):
    bsz, s, d = h.shape
    xt = h.reshape(-1, d)
    n_tok = xt.shape[0]
    logits = (xt @ w_router + b_router).astype(jnp.float32)
    top_val, top_idx = lax.top_k(logits, TOP_K)
    weights = jax.nn.softmax(top_val, axis=-1)
    n_asg = n_tok * TOP_K
    e_flat = top_idx.reshape(-1)
    w_flat = weights.reshape(-1).astype(h.dtype)
    tok = jnp.arange(n_asg, dtype=jnp.int32) // TOP_K
    order = jnp.argsort(e_flat)
    e_sorted = e_flat[order]
    tok_sorted = tok[order]
    counts = jnp.zeros((N_EXP,), jnp.int32).at[e_flat].add(1)
    padded = ((counts + MOE_BLK - 1) // MOE_BLK) * MOE_BLK
    pad_end = jnp.cumsum(padded)
    pad_start = pad_end - padded
    start = jnp.cumsum(counts) - counts
    dest = pad_start[e_sorted] + jnp.arange(n_asg, dtype=jnp.int32) - start[e_sorted]
    n_blocks = (n_asg + N_EXP * (MOE_BLK - 1) + MOE_BLK - 1) // MOE_BLK
    rows = jnp.zeros((n_blocks * MOE_BLK, d), h.dtype).at[dest].set(xt[tok_sorted])
    block_e = jnp.minimum(jnp.searchsorted(pad_end, jnp.arange(n_blocks) * MOE_BLK, side='right'), N_EXP - 1)

    def expert_block(args):
        xb, e = args
        gate = jnp.minimum(xb @ w_gate[e] + b_gate[e], SWIGLU_LIMIT)
        up = jnp.clip(xb @ w_up[e] + b_up[e], -SWIGLU_LIMIT, SWIGLU_LIMIT)
        act = (up + 1.0) * gate * jax.nn.sigmoid(SWIGLU_ALPHA * gate)
        return act @ w_down[e] + b_down[e]

    yb = lax.map(expert_block, (rows.reshape(n_blocks, MOE_BLK, d), block_e))
    y_asg = yb.reshape(-1, d)[dest] * w_flat[order][:, None]
    out = jax.ops.segment_sum(y_asg, tok_sorted, num_segments=n_tok)
    return out.reshape(bsz, s, d)


def hybrid_layer(x, positions, norm_mix, w_in, conv_w, a_log, dt_bias, norm_gdn, pe_ck, w1_ck, w2_ck,
                 pe_cv, w1_cv, w2_cv, w_proj_a, w_proj_b, w_out, norm_ffn, w_router, b_router,
                 w_gate, b_gate, w_up, b_up, w_down, b_down):
    bsz, s, d = x.shape
    h = rms_norm(x, norm_mix)
    (qa, ka, va, za, aa, ba, qb, kc, vc, ks, vs, kw, vw, gb, gm) = jnp.split(h @ w_in, SPLIT_POINTS, axis=-1)
    y_a = gated_deltanet(qa, ka, va, za, aa, ba, conv_w, a_log, dt_bias, norm_gdn) @ w_proj_a
    y_b = native_sparse_attention(qb, kc, vc, ks, vs, kw, vw, gb, positions,
                                  pe_ck, w1_ck, w2_ck, pe_cv, w1_cv, w2_cv) @ w_proj_b
    gm = jax.nn.sigmoid(gm.astype(jnp.float32)).reshape(bsz, s, 2, d)
    merged = (gm[:, :, 0] * y_a + gm[:, :, 1] * y_b).astype(x.dtype)
    x = x + merged @ w_out
    x = x + moe_ffn(rms_norm(x, norm_ffn), w_router, b_router, w_gate, b_gate, w_up, b_up, w_down, b_down)
    return x


def setup_inputs(seed: int = 0) -> dict:
    key = jax.random.key(seed)
    ks = jax.random.split(key, 32)
    L = DEPTH
    f32 = jnp.float32

    def nrm(k, shape, fan_in):
        return jax.random.normal(k, shape, f32) * (fan_in ** -0.5)

    def gain(k, shape):
        return 1.0 + 0.02 * jax.random.normal(k, shape, f32)

    x = jax.random.normal(ks[0], (BATCH, SEQ, D_MODEL), f32)
    positions = (jnp.arange(SEQ, dtype=jnp.int32)[None, :]
                 + jax.random.randint(ks[1], (BATCH, 1), 0, 4096, dtype=jnp.int32))
    norm_mix = gain(ks[2], (L, D_MODEL))
    w_in = nrm(ks[3], (L, D_MODEL, D_IN), D_MODEL)
    conv_w = nrm(ks[4], (L, CONV_W, CONV_DIM), CONV_W)
    a_log = jnp.log(jax.random.uniform(ks[5], (L, HA), f32, minval=1.0, maxval=16.0))
    dt = jnp.exp(jax.random.uniform(ks[6], (L, HA), f32, minval=math.log(1e-3), maxval=math.log(1e-1)))
    dt_bias = dt + jnp.log(-jnp.expm1(-dt))
    norm_gdn = gain(ks[7], (L, DV_A))
    pe_ck = 0.02 * jax.random.normal(ks[8], (L, L_CMP, DH_B), f32)
    w1_ck = nrm(ks[9], (L, L_CMP * DH_B, DH_B), L_CMP * DH_B)
    w2_ck = nrm(ks[10], (L, DH_B, DH_B), DH_B)
    pe_cv = 0.02 * jax.random.normal(ks[11], (L, L_CMP, DH_B), f32)
    w1_cv = nrm(ks[12], (L, L_CMP * DH_B, DH_B), L_CMP * DH_B)
    w2_cv = nrm(ks[13], (L, DH_B, DH_B), DH_B)
    w_proj_a = nrm(ks[14], (L, HA * DV_A, D_MODEL), HA * DV_A)
    w_proj_b = nrm(ks[15], (L, HB * DH_B, D_MODEL), HB * DH_B)
    w_out = nrm(ks[16], (L, D_MODEL, D_MODEL), D_MODEL)
    norm_ffn = gain(ks[17], (L, D_MODEL))
    w_router = nrm(ks[18], (L, D_MODEL, N_EXP), D_MODEL)
    b_router = 0.01 * jax.random.normal(ks[19], (L, N_EXP), f32)
    w_gate = nrm(ks[20], (L, N_EXP, D_MODEL, D_FF), D_MODEL)
    b_gate = 0.01 * jax.random.normal(ks[21], (L, N_EXP, D_FF), f32)
    w_up = nrm(ks[22], (L, N_EXP, D_MODEL, D_FF), D_MODEL)
    b_up = 0.01 * jax.random.normal(ks[23], (L, N_EXP, D_FF), f32)
    w_down = nrm(ks[24], (L, N_EXP, D_FF, D_MODEL), D_FF)
    b_down = 0.01 * jax.random.normal(ks[25], (L, N_EXP, D_MODEL), f32)
    norm_final = gain(ks[26], (D_MODEL,))
    return {'x': x, 'positions': positions, 'norm_mix': norm_mix, 'w_in': w_in, 'conv_w': conv_w,
            'a_log': a_log, 'dt_bias': dt_bias, 'norm_gdn': norm_gdn,
            'pe_ck': pe_ck, 'w1_ck': w1_ck, 'w2_ck': w2_ck, 'pe_cv': pe_cv, 'w1_cv': w1_cv, 'w2_cv': w2_cv,
            'w_proj_a': w_proj_a, 'w_proj_b': w_proj_b, 'w_out': w_out, 'norm_ffn': norm_ffn,
            'w_router': w_router, 'b_router': b_router, 'w_gate': w_gate, 'b_gate': b_gate,
            'w_up': w_up, 'b_up': b_up, 'w_down': w_down, 'b_down': b_down, 'norm_final': norm_final}


def reference(x, positions, norm_mix, w_in, conv_w, a_log, dt_bias, norm_gdn, pe_ck, w1_ck, w2_ck,
              pe_cv, w1_cv, w2_cv, w_proj_a, w_proj_b, w_out, norm_ffn, w_router, b_router,
              w_gate, b_gate, w_up, b_up, w_down, b_down, norm_final):
    for l in range(DEPTH):
        x = hybrid_layer(x, positions, norm_mix[l], w_in[l], conv_w[l], a_log[l], dt_bias[l], norm_gdn[l],
                         pe_ck[l], w1_ck[l], w2_ck[l], pe_cv[l], w1_cv[l], w2_cv[l],
                         w_proj_a[l], w_proj_b[l], w_out[l], norm_ffn[l], w_router[l], b_router[l],
                         w_gate[l], b_gate[l], w_up[l], b_up[l], w_down[l], b_down[l])
    return rms_norm(x, norm_final)
```

```python
import functools
import math

import jax
import jax.numpy as jnp
import numpy as np
from jax import lax
from jax.experimental import pallas as pl
from jax.experimental.pallas import tpu as pltpu

F32 = jnp.float32
BF16 = jnp.bfloat16

D_MODEL = 2048
EPS = 1e-6
NEG = -1e30
HA = D_MODEL // 128
DK_A = 128
DV_A = 128
CONV_W = 4
CHUNK = 64
HB = D_MODEL // 128
G_KV = 4
R_GRP = HB // G_KV
DH_B = 128
ROT_DIM = DH_B // 4
ROPE_THETA = 500000.0
L_CMP = 32
STRIDE_CMP = 16
L_SLC = 64
T_SEL = 8
WINDOW = 512
N_EXP = 32
TOP_K = 4
D_FF = D_MODEL
SWIGLU_LIMIT = 7.0
SWIGLU_ALPHA = 1.702
SPLIT_SIZES = (HA * DK_A, HA * DK_A, HA * DV_A, HA * DV_A, HA, HA,
               HB * DH_B, G_KV * DH_B, G_KV * DH_B, G_KV * DH_B, G_KV * DH_B, G_KV * DH_B, G_KV * DH_B,
               3 * HB, 2 * D_MODEL)
SPLIT_POINTS = tuple(sum(SPLIT_SIZES[:i + 1]) for i in range(len(SPLIT_SIZES) - 1))

V7X_VMEM_LIMIT_BYTES = 56 * 1024 * 1024
LANES = 128
MOE_TM = 512
MOE_TF = 512
ATT_TQ = 256


def _cparams(sem):
    return pltpu.CompilerParams(dimension_semantics=sem, vmem_limit_bytes=V7X_VMEM_LIMIT_BYTES)


def _mm_kernel(x_ref, w_ref, o_ref):
    o_ref[...] = jnp.dot(x_ref[...], w_ref[...], preferred_element_type=F32).astype(o_ref.dtype)


def _matmul(x, w, out_dtype, tm=1024, tn=1024):
    m, k = x.shape
    n = w.shape[1]
    tm, tn = min(tm, m), min(tn, n)
    assert m % tm == 0 and n % tn == 0
    return pl.pallas_call(
        _mm_kernel,
        out_shape=jax.ShapeDtypeStruct((m, n), out_dtype),
        grid=(n // tn, m // tm),
        in_specs=[pl.BlockSpec((tm, k), lambda j, i: (i, 0)),
                  pl.BlockSpec((k, tn), lambda j, i: (0, j))],
        out_specs=pl.BlockSpec((tm, tn), lambda j, i: (i, j)),
        compiler_params=_cparams(("parallel", "parallel")),
        name="dense_matmul",
    )(x, w)


def _rmsnorm_kernel(x_ref, g_ref, o_ref):
    x = x_ref[...]
    y = x * lax.rsqrt(jnp.mean(x * x, axis=-1, keepdims=True) + EPS)
    o_ref[...] = (y * g_ref[...]).astype(o_ref.dtype)


def _rmsnorm(x, gain, out_dtype, tm=512):
    m, d = x.shape
    return pl.pallas_call(
        _rmsnorm_kernel,
        out_shape=jax.ShapeDtypeStruct((m, d), out_dtype),
        grid=(m // tm,),
        in_specs=[pl.BlockSpec((tm, d), lambda i: (i, 0)),
                  pl.BlockSpec((1, d), lambda i: (0, 0))],
        out_specs=pl.BlockSpec((tm, d), lambda i: (i, 0)),
        compiler_params=_cparams(("parallel",)),
        name="rmsnorm",
    )(x, gain.reshape(1, d))


def _stack_heads(q, extra, scale):
    parts = []
    for r in range(R_GRP):
        qr = (q[:, r * DH_B:(r + 1) * DH_B].astype(F32) * scale).astype(BF16)
        parts.append(qr if extra is None else jnp.concatenate([qr, extra], axis=1))
    return jnp.concatenate(parts, axis=0)


def _unstack_heads(o_ref, o, tq):
    for r in range(R_GRP):
        o_ref[0, :, r * DH_B:(r + 1) * DH_B] = o[r * tq:(r + 1) * tq].astype(o_ref.dtype)


def _slc_kernel(q_ref, mb_ref, k_ref, v_ref, o_ref, *, tq):
    i = pl.program_id(2)
    rows = R_GRP * tq
    qa = _stack_heads(q_ref[0], mb_ref[0, 0], DH_B ** -0.5)

    def scores(j):
        k = k_ref[0, 0, pl.ds(pl.multiple_of(j * tq, tq), tq), :]
        return lax.dot_general(qa, k, (((1,), (1,)), ((), ())), preferred_element_type=F32)

    def update(j, s, carry):
        m, l, acc = carry
        v = v_ref[0, pl.ds(pl.multiple_of(j * tq, tq), tq), :]
        m_new = jnp.maximum(m, jnp.max(s, axis=1, keepdims=True))
        alpha = jnp.exp(m - m_new)
        p = jnp.exp(s - m_new)
        l = alpha * l + jnp.sum(p, axis=1, keepdims=True)
        acc = alpha * acc + jnp.dot(p.astype(BF16), v, preferred_element_type=F32)
        return m_new, l, acc

    init = (jnp.full((rows, 1), -jnp.inf, F32), jnp.zeros((rows, 1), F32), jnp.zeros((rows, DH_B), F32))
    carry = lax.fori_loop(0, i, lambda j, c: update(j, scores(j), c), init)
    s = scores(i)
    t_loc = lax.broadcasted_iota(jnp.int32, (rows, tq), 0) & (tq - 1)
    k_loc = lax.broadcasted_iota(jnp.int32, (rows, tq), 1)
    s = jnp.where(k_loc <= t_loc, s, NEG)
    m, l, acc = update(i, s, carry)
    _unstack_heads(o_ref, acc / l, tq)


def _slc_attention(q, maskbias, k_aug, v, bsz, s):
    tq = ATT_TQ
    return pl.pallas_call(
        functools.partial(_slc_kernel, tq=tq),
        out_shape=jax.ShapeDtypeStruct((bsz, s, HB * DH_B), BF16),
        grid=(bsz, G_KV, s // tq),
        in_specs=[pl.BlockSpec((1, tq, R_GRP * DH_B), lambda b, g, i: (b, i, g)),
                  pl.BlockSpec((1, 1, tq, LANES), lambda b, g, i: (b, g, i, 0)),
                  pl.BlockSpec((1, 1, s, 2 * DH_B), lambda b, g, i: (b, g, 0, 0)),
                  pl.BlockSpec((1, s, DH_B), lambda b, g, i: (b, 0, g))],
        out_specs=pl.BlockSpec((1, tq, R_GRP * DH_B), lambda b, g, i: (b, i, g)),
        compiler_params=_cparams(("parallel", "parallel", "arbitrary")),
        name="nsa_selected_attention",
    )(q, maskbias, k_aug, v)


def _win_kernel(q_ref, k0_ref, k1_ref, k2_ref, v0_ref, v1_ref, v2_ref, o_ref, *, tq):
    i = pl.program_id(2)
    rows = R_GRP * tq
    nk = 3 * tq
    qa = _stack_heads(q_ref[0], None, DH_B ** -0.5)
    k = jnp.concatenate([k0_ref[0], k1_ref[0], k2_ref[0]], axis=0)
    v = jnp.concatenate([v0_ref[0], v1_ref[0], v2_ref[0]], axis=0)
    s = lax.dot_general(qa, k, (((1,), (1,)), ((), ())), preferred_element_type=F32)
    t_loc = lax.broadcasted_iota(jnp.int32, (rows, nk), 0) & (tq - 1)
    k_loc = lax.broadcasted_iota(jnp.int32, (rows, nk), 1)
    dist = t_loc + 2 * tq - k_loc
    t_abs = (lax.broadcasted_iota(jnp.int32, (rows, 1), 0) & (tq - 1)) + i * tq
    bound = jnp.minimum(t_abs + 1, WINDOW)
    valid = dist.astype(jnp.uint32) < bound.astype(jnp.uint32)
    s = jnp.where(valid, s, NEG)
    m = jnp.max(s, axis=1, keepdims=True)
    p = jnp.exp(s - m)
    l = jnp.sum(p, axis=1, keepdims=True)
    o = jnp.dot(p.astype(BF16), v, preferred_element_type=F32) / l
    _unstack_heads(o_ref, o, tq)


def _win_attention(q, k, v, bsz, s):
    tq = ATT_TQ
    assert 2 * tq >= WINDOW
    qspec = pl.BlockSpec((1, tq, R_GRP * DH_B), lambda b, g, i: (b, i, g))

    def kv_spec(back):
        return pl.BlockSpec((1, tq, DH_B), lambda b, g, i: (b, jnp.maximum(i - back, 0), g))

    return pl.pallas_call(
        functools.partial(_win_kernel, tq=tq),
        out_shape=jax.ShapeDtypeStruct((bsz, s, HB * DH_B), BF16),
        grid=(bsz, G_KV, s // tq),
        in_specs=[qspec, kv_spec(2), kv_spec(1), kv_spec(0), kv_spec(2), kv_spec(1), kv_spec(0)],
        out_specs=qspec,
        compiler_params=_cparams(("parallel", "parallel", "arbitrary")),
        name="nsa_window_attention",
    )(q, k, k, k, v, v, v)


def _moe_kernel(be_ref, nu_ref, x_ref, wg_ref, wu_ref, wd_ref, bg_ref, bu_ref, bd_ref, o_ref):
    i = pl.program_id(0)
    f = pl.program_id(1)

    @pl.when(i < nu_ref[0])
    def _():
        x = x_ref[...]
        gate = jnp.dot(x, wg_ref[0].astype(BF16), preferred_element_type=F32) + bg_ref[0]
        up = jnp.dot(x, wu_ref[0].astype(BF16), preferred_element_type=F32) + bu_ref[0]
        gate = jnp.minimum(gate, SWIGLU_LIMIT)
        up = jnp.clip(up, -SWIGLU_LIMIT, SWIGLU_LIMIT)
        act = (up + 1.0) * gate * jax.nn.sigmoid(SWIGLU_ALPHA * gate)
        y = jnp.dot(act.astype(BF16), wd_ref[0].astype(BF16), preferred_element_type=F32)

        @pl.when(f == 0)
        def _():
            o_ref[...] = y + bd_ref[0]

        @pl.when(f != 0)
        def _():
            o_ref[...] += y

    @pl.when((i >= nu_ref[0]) & (f == 0))
    def _():
        o_ref[...] = jnp.zeros_like(o_ref)


def _moe_experts(rows, block_e, n_used, w_gate, b_gate, w_up, b_up, w_down, b_down):
    n_rows, d = rows.shape
    n_blocks = n_rows // MOE_TM
    nf = D_FF // MOE_TF

    def f_eff(i, f, nu):
        return jnp.where(i < nu[0], f, nf - 1)

    return pl.pallas_call(
        _moe_kernel,
        out_shape=jax.ShapeDtypeStruct((n_rows, d), F32),
        grid_spec=pltpu.PrefetchScalarGridSpec(
            num_scalar_prefetch=2,
            grid=(n_blocks, nf),
            in_specs=[
                pl.BlockSpec((MOE_TM, d), lambda i, f, be, nu: (i, 0)),
                pl.BlockSpec((1, d, MOE_TF), lambda i, f, be, nu: (be[i], 0, f_eff(i, f, nu))),
                pl.BlockSpec((1, d, MOE_TF), lambda i, f, be, nu: (be[i], 0, f_eff(i, f, nu))),
                pl.BlockSpec((1, MOE_TF, d), lambda i, f, be, nu: (be[i], f_eff(i, f, nu), 0)),
                pl.BlockSpec((1, 1, MOE_TF), lambda i, f, be, nu: (be[i], 0, f_eff(i, f, nu))),
                pl.BlockSpec((1, 1, MOE_TF), lambda i, f, be, nu: (be[i], 0, f_eff(i, f, nu))),
                pl.BlockSpec((1, 1, d), lambda i, f, be, nu: (be[i], 0, 0)),
            ],
            out_specs=pl.BlockSpec((MOE_TM, d), lambda i, f, be, nu: (i, 0)),
        ),
        compiler_params=_cparams(("arbitrary", "arbitrary")),
        name="moe_expert_ffn",
    )(block_e, n_used, rows, w_gate, w_up, w_down,
      b_gate.reshape(N_EXP, 1, D_FF), b_up.reshape(N_EXP, 1, D_FF), b_down.reshape(N_EXP, 1, d))


def _rms_norm(x, gain):
    xf = x.astype(F32)
    y = xf * lax.rsqrt(jnp.mean(xf * xf, axis=-1, keepdims=True) + EPS)
    return y * gain.astype(F32)


def _l2_normalize(x):
    return x * lax.rsqrt(jnp.sum(x * x, axis=-1, keepdims=True) + EPS)


def _partial_rotary(x, positions):
    half = ROT_DIM // 2
    inv_freq = ROPE_THETA ** (-jnp.arange(half, dtype=F32) * 2.0 / ROT_DIM)
    ang = positions.astype(F32)[..., None] * inv_freq
    cos = jnp.cos(ang)[:, :, None, :]
    sin = jnp.sin(ang)[:, :, None, :]
    xf = x.astype(F32)
    x1, x2 = xf[..., :half], xf[..., half:ROT_DIM]
    return jnp.concatenate([x1 * cos - x2 * sin, x2 * cos + x1 * sin, xf[..., ROT_DIM:]], axis=-1)


def _causal_short_conv(x, w):
    s = x.shape[1]
    xp = jnp.pad(x, ((0, 0), (CONV_W - 1, 0), (0, 0)))
    y = xp[:, 0:s] * w[0]
    for i in range(1, CONV_W):
        y = y + xp[:, i:i + s] * w[i]
    return y


def _chunk_gated_delta_rule(q, k, v, g, beta):
    b, h, s, dk = q.shape
    dv = v.shape[-1]
    n = s // CHUNK
    q = q.reshape(b, h, n, CHUNK, dk)
    k = k.reshape(b, h, n, CHUNK, dk)
    v = v.reshape(b, h, n, CHUNK, dv)
    g = g.reshape(b, h, n, CHUNK)
    beta = beta.reshape(b, h, n, CHUNK)
    gc = jnp.cumsum(g, axis=-1)
    idx = jnp.arange(CHUNK)
    incl = idx[:, None] >= idx[None, :]
    strict = idx[:, None] > idx[None, :]
    decay = jnp.exp(jnp.where(incl, gc[..., :, None] - gc[..., None, :], -jnp.inf))
    k_beta = k * beta[..., None]
    v_beta = v * beta[..., None]
    lmat = jnp.where(strict, jnp.einsum('bhncd,bhnmd->bhncm', k_beta, k) * decay, 0.0)
    tmat = lmat + jnp.eye(CHUNK, dtype=q.dtype)
    u = lax.linalg.triangular_solve(tmat, v_beta, left_side=True, lower=True, unit_diagonal=True)
    w = lax.linalg.triangular_solve(tmat, k_beta * jnp.exp(gc)[..., None], left_side=True, lower=True,
                                    unit_diagonal=True)
    a_intra = jnp.einsum('bhncd,bhnmd->bhncm', q, k) * decay
    xs = tuple(jnp.moveaxis(t, 2, 0) for t in (q, k, u, w, gc, a_intra))

    def step(state, inp):
        qi, ki, ui, wi, gi, ai = inp
        v_new = ui - jnp.einsum('bhcd,bhde->bhce', wi, state)
        o = (jnp.einsum('bhcd,bhde->bhce', qi * jnp.exp(gi)[..., None], state)
             + jnp.einsum('bhcm,bhme->bhce', ai, v_new))
        g_last = gi[..., -1]
        k_dec = ki * jnp.exp(g_last[..., None] - gi)[..., None]
        state = state * jnp.exp(g_last)[..., None, None] + jnp.einsum('bhcd,bhce->bhde', k_dec, v_new)
        return state, o

    s0 = jnp.zeros((b, h, dk, dv), q.dtype)
    _, o = lax.scan(step, s0, xs)
    return jnp.moveaxis(o, 0, 2).reshape(b, h, s, dv)


def _gated_deltanet(qa, ka, va, za, a_in, b_in, conv_w, a_log, dt_bias, norm_gdn):
    bsz, s, _ = qa.shape
    qkv = jax.nn.silu(_causal_short_conv(jnp.concatenate([qa, ka, va], axis=-1), conv_w)).astype(F32)
    q, k, v = jnp.split(qkv, [HA * DK_A, 2 * HA * DK_A], axis=-1)
    q = _l2_normalize(q.reshape(bsz, s, HA, DK_A).transpose(0, 2, 1, 3)) * (DK_A ** -0.5)
    k = _l2_normalize(k.reshape(bsz, s, HA, DK_A).transpose(0, 2, 1, 3))
    v = v.reshape(bsz, s, HA, DV_A).transpose(0, 2, 1, 3)
    g = -jnp.exp(a_log.astype(F32)) * jax.nn.softplus(a_in.astype(F32) + dt_bias.astype(F32))
    beta = jax.nn.sigmoid(b_in.astype(F32))
    o = _chunk_gated_delta_rule(q, k, v, g.transpose(0, 2, 1), beta.transpose(0, 2, 1))
    o = o.transpose(0, 2, 1, 3)
    o = _rms_norm(o, norm_gdn) * jax.nn.silu(za.reshape(bsz, s, HA, DV_A).astype(F32))
    return o.reshape(bsz, s, HA * DV_A)


def _compress_blocks(kv, pe, w1, w2):
    bsz, s, g, dh = kv.shape
    n_cmp = (s - L_CMP) // STRIDE_CMP + 1
    idx = np.arange(n_cmp)[:, None] * STRIDE_CMP + np.arange(L_CMP)[None, :]
    blk = kv[:, idx] + pe[None, None, :, None, :]
    flat = blk.transpose(0, 1, 3, 2, 4).reshape(bsz, n_cmp, g, L_CMP * dh)
    return jax.nn.gelu(flat @ w1) @ w2


def _native_sparse_attention(qb, kc, vc, ks, vs, kw, vw, gb, positions, pe_ck, w1_ck, w2_ck, pe_cv, w1_cv, w2_cv):
    bsz, s, _ = qb.shape
    scale = DH_B ** -0.5
    q4 = _partial_rotary(qb.reshape(bsz, s, HB, DH_B), positions)
    q = q4.reshape(bsz, s, G_KV, R_GRP, DH_B)
    kc = _partial_rotary(kc.reshape(bsz, s, G_KV, DH_B), positions)
    ks = _partial_rotary(ks.reshape(bsz, s, G_KV, DH_B), positions)
    kw = _partial_rotary(kw.reshape(bsz, s, G_KV, DH_B), positions)
    vc = vc.reshape(bsz, s, G_KV, DH_B).astype(F32)
    k_cmp = _compress_blocks(kc, pe_ck, w1_ck, w2_ck)
    v_cmp = _compress_blocks(vc, pe_cv, w1_cv, w2_cv)
    n_cmp = k_cmp.shape[1]
    t = jnp.arange(s)
    cmp_valid = (jnp.arange(n_cmp) * STRIDE_CMP + L_CMP - 1)[None, :] <= t[:, None]
    s_cmp = jnp.einsum('bsgrd,bcgd->bgrsc', q, k_cmp).astype(F32) * scale
    p_cmp = jax.nn.softmax(jnp.where(cmp_valid, s_cmp, NEG), axis=-1)
    p_cmp = jnp.where(jnp.any(cmp_valid, axis=-1)[:, None], p_cmp, 0.0)
    o_cmp = jnp.einsum('bgrsc,bcgd->bsgrd', p_cmp, v_cmp)
    n_slc = s // L_SLC
    c_start = np.arange(n_cmp) * STRIDE_CMP
    j_start = np.arange(n_slc) * L_SLC
    overlap = ((c_start[:, None] < j_start[None, :] + L_SLC)
               & (c_start[:, None] + L_CMP > j_start[None, :])).astype(np.float32)
    imp = jnp.einsum('bgrsc,cj->bgsj', p_cmp, jnp.asarray(overlap))
    blk = jnp.arange(n_slc)
    cur = t // L_SLC
    forced = (blk[None, :] == 0) | (blk[None, :] == cur[:, None])
    imp = jnp.where(forced, jnp.inf, jnp.where(blk[None, :] <= cur[:, None], imp, -jnp.inf))
    _, sel = lax.top_k(imp, min(T_SEL, n_slc))
    sel_mask = jnp.sum(jax.nn.one_hot(sel, n_slc, dtype=F32), axis=-2) > 0.5
    maskbias = jnp.where(sel_mask, 0.0, NEG).astype(BF16)
    maskbias = jnp.pad(maskbias, ((0, 0), (0, 0), (0, 0), (0, LANES - n_slc)))
    onehot = (t[:, None] // L_SLC == jnp.arange(LANES)[None, :]).astype(BF16)
    k_aug = jnp.concatenate([ks.transpose(0, 2, 1, 3).astype(BF16),
                             jnp.broadcast_to(onehot, (bsz, G_KV, s, LANES))], axis=-1)
    q_bf = q4.reshape(bsz, s, HB * DH_B).astype(BF16)
    o_slc = _slc_attention(q_bf, maskbias, k_aug, vs.astype(BF16), bsz, s)
    o_win = _win_attention(q_bf, kw.reshape(bsz, s, G_KV * DH_B).astype(BF16), vw.astype(BF16), bsz, s)
    o_slc = o_slc.reshape(bsz, s, G_KV, R_GRP, DH_B).astype(F32)
    o_win = o_win.reshape(bsz, s, G_KV, R_GRP, DH_B).astype(F32)
    gates = jax.nn.sigmoid(gb.astype(F32)).reshape(bsz, s, G_KV, R_GRP, 3)
    o = gates[..., 0:1] * o_cmp + gates[..., 1:2] * o_slc + gates[..., 2:3] * o_win
    return o.reshape(bsz, s, HB * DH_B)


def _moe_ffn(h, w_router, b_router, w_gate, b_gate, w_up, b_up, w_down, b_down):
    n_tok, d = h.shape
    logits = (h @ w_router + b_router).astype(F32)
    top_val, top_idx = lax.top_k(logits, TOP_K)
    weights = jax.nn.softmax(top_val, axis=-1)
    n_asg = n_tok * TOP_K
    e_flat = top_idx.reshape(-1)
    w_flat = weights.reshape(-1)
    tok = jnp.arange(n_asg, dtype=jnp.int32) // TOP_K
    order = jnp.argsort(e_flat)
    e_sorted = e_flat[order]
    tok_sorted = tok[order]
    counts = jnp.zeros((N_EXP,), jnp.int32).at[e_flat].add(1)
    padded = ((counts + MOE_TM - 1) // MOE_TM) * MOE_TM
    pad_end = jnp.cumsum(padded)
    pad_start = pad_end - padded
    start = jnp.cumsum(counts) - counts
    dest = pad_start[e_sorted] + jnp.arange(n_asg, dtype=jnp.int32) - start[e_sorted]
    n_blocks = (n_asg + N_EXP * (MOE_TM - 1) + MOE_TM - 1) // MOE_TM
    rows = jnp.zeros((n_blocks * MOE_TM, d), BF16).at[dest].set(h.astype(BF16)[tok_sorted])
    block_e = jnp.minimum(jnp.searchsorted(pad_end, jnp.arange(n_blocks) * MOE_TM, side='right'), N_EXP - 1)
    n_used = (pad_end[-1] // MOE_TM).astype(jnp.int32).reshape(1)
    yb = _moe_experts(rows, block_e.astype(jnp.int32), n_used, w_gate, b_gate, w_up, b_up, w_down, b_down)
    y_asg = yb[dest] * w_flat[order][:, None]
    return jax.ops.segment_sum(y_asg, tok_sorted, num_segments=n_tok)


def _layer(x, positions, norm_mix, w_in, conv_w, a_log, dt_bias, norm_gdn, pe_ck, w1_ck, w2_ck,
           pe_cv, w1_cv, w2_cv, w_proj_a, w_proj_b, w_out, norm_ffn, w_router, b_router,
           w_gate, b_gate, w_up, b_up, w_down, b_down):
    bsz, s, d = x.shape
    n_tok = bsz * s
    x2 = x.reshape(n_tok, d)
    h = _rmsnorm(x2, norm_mix, BF16)
    n_small = 2 * HA + 3 * HB
    sp = SPLIT_POINTS
    w_main = jnp.concatenate([w_in[:, :sp[3]], w_in[:, sp[5]:sp[12]], w_in[:, sp[13]:]], axis=1).astype(BF16)
    w_small = jnp.concatenate([w_in[:, sp[3]:sp[5]], w_in[:, sp[12]:sp[13]]], axis=1)
    w_small = jnp.pad(w_small, ((0, 0), (0, LANES - n_small))).astype(BF16)
    main = _matmul(h, w_main, BF16).reshape(bsz, s, -1)
    small = _matmul(h, w_small, F32).reshape(bsz, s, -1)
    sizes = (HA * DK_A, HA * DK_A, HA * DV_A, HA * DV_A,
             HB * DH_B, G_KV * DH_B, G_KV * DH_B, G_KV * DH_B, G_KV * DH_B, G_KV * DH_B, G_KV * DH_B, 2 * D_MODEL)
    pts = tuple(sum(sizes[:i + 1]) for i in range(len(sizes) - 1))
    qa, ka, va, za, qb, kc, vc, ks, vs, kw, vw, gm = jnp.split(main, pts, axis=-1)
    aa, ba, gb = small[..., :HA], small[..., HA:2 * HA], small[..., 2 * HA:n_small]

    o_a = _gated_deltanet(qa.astype(F32), ka.astype(F32), va.astype(F32), za, aa, ba, conv_w, a_log, dt_bias, norm_gdn)
    o_b = _native_sparse_attention(qb, kc, vc, ks, vs, kw, vw, gb, positions,
                                   pe_ck, w1_ck, w2_ck, pe_cv, w1_cv, w2_cv)
    y_a = _matmul(o_a.reshape(n_tok, d).astype(BF16), w_proj_a.astype(BF16), F32)
    y_b = _matmul(o_b.reshape(n_tok, d).astype(BF16), w_proj_b.astype(BF16), F32)
    gmf = jax.nn.sigmoid(gm.astype(F32)).reshape(n_tok, 2, d)
    merged = (gmf[:, 0] * y_a + gmf[:, 1] * y_b).astype(BF16)
    x2 = x2 + _matmul(merged, w_out.astype(BF16), F32)
    hf = _rmsnorm(x2, norm_ffn, F32)
    x2 = x2 + _moe_ffn(hf, w_router, b_router, w_gate, b_gate, w_up, b_up, w_down, b_down)
    return x2.reshape(bsz, s, d)


def kernel(x, positions, norm_mix, w_in, conv_w, a_log, dt_bias, norm_gdn, pe_ck, w1_ck, w2_ck, pe_cv, w1_cv, w2_cv, w_proj_a, w_proj_b, w_out, norm_ffn, w_router, b_router, w_gate, b_gate, w_up, b_up, w_down, b_down, norm_final):
    depth = norm_mix.shape[0]
    for l in range(depth):
        x = _layer(x, positions, norm_mix[l], w_in[l], conv_w[l], a_log[l], dt_bias[l], norm_gdn[l],
                   pe_ck[l], w1_ck[l], w2_ck[l], pe_cv[l], w1_cv[l], w2_cv[l],
                   w_proj_a[l], w_proj_b[l], w_out[l], norm_ffn[l], w_router[l], b_router[l],
                   w_gate[l], b_gate[l], w_up[l], b_up[l], w_down[l], b_down[l])
    bsz, s, d = x.shape
    return _rmsnorm(x.reshape(bsz * s, d), norm_final, F32).reshape(bsz, s, d)
```

```python
import functools
import math

import jax
import jax.numpy as jnp
import numpy as np
from jax import lax
from jax.experimental import pallas as pl
from jax.experimental.pallas import tpu as pltpu

F32 = jnp.float32
BF16 = jnp.bfloat16

D_MODEL = 2048
EPS = 1e-6
NEG = -1e30
HA = D_MODEL // 128
DK_A = 128
DV_A = 128
CONV_W = 4
CHUNK = 64
HB = D_MODEL // 128
G_KV = 4
R_GRP = HB // G_KV
DH_B = 128
ROT_DIM = DH_B // 4
ROPE_THETA = 500000.0
L_CMP = 32
STRIDE_CMP = 16
L_SLC = 64
T_SEL = 8
WINDOW = 512
N_EXP = 32
TOP_K = 4
D_FF = D_MODEL
SWIGLU_LIMIT = 7.0
SWIGLU_ALPHA = 1.702
SPLIT_SIZES = (HA * DK_A, HA * DK_A, HA * DV_A, HA * DV_A, HA, HA,
               HB * DH_B, G_KV * DH_B, G_KV * DH_B, G_KV * DH_B, G_KV * DH_B, G_KV * DH_B, G_KV * DH_B,
               3 * HB, 2 * D_MODEL)
SPLIT_POINTS = tuple(sum(SPLIT_SIZES[:i + 1]) for i in range(len(SPLIT_SIZES) - 1))

V7X_VMEM_LIMIT_BYTES = 56 * 1024 * 1024
LANES = 128
MOE_TM = 512
MOE_TF = 512
ATT_TQ = 256


def _cparams(sem):
    return pltpu.CompilerParams(dimension_semantics=sem, vmem_limit_bytes=V7X_VMEM_LIMIT_BYTES)


def _mm_kernel(x_ref, w_ref, o_ref):
    o_ref[...] = jnp.dot(x_ref[...], w_ref[...], preferred_element_type=F32).astype(o_ref.dtype)


def _matmul(x, w, out_dtype, tm=1024, tn=1024):
    m, k = x.shape
    n = w.shape[1]
    tm, tn = min(tm, m), min(tn, n)
    assert m % tm == 0 and n % tn == 0
    return pl.pallas_call(
        _mm_kernel,
        out_shape=jax.ShapeDtypeStruct((m, n), out_dtype),
        grid=(n // tn, m // tm),
        in_specs=[pl.BlockSpec((tm, k), lambda j, i: (i, 0)),
                  pl.BlockSpec((k, tn), lambda j, i: (0, j))],
        out_specs=pl.BlockSpec((tm, tn), lambda j, i: (i, j)),
        compiler_params=_cparams(("parallel", "parallel")),
        name="dense_matmul",
    )(x, w)


def _rmsnorm_kernel(x_ref, g_ref, o_ref):
    x = x_ref[...]
    y = x * lax.rsqrt(jnp.mean(x * x, axis=-1, keepdims=True) + EPS)
    o_ref[...] = (y * g_ref[...]).astype(o_ref.dtype)


def _rmsnorm(x, gain, out_dtype, tm=512):
    m, d = x.shape
    return pl.pallas_call(
        _rmsnorm_kernel,
        out_shape=jax.ShapeDtypeStruct((m, d), out_dtype),
        grid=(m // tm,),
        in_specs=[pl.BlockSpec((tm, d), lambda i: (i, 0)),
                  pl.BlockSpec((1, d), lambda i: (0, 0))],
        out_specs=pl.BlockSpec((tm, d), lambda i: (i, 0)),
        compiler_params=_cparams(("parallel",)),
        name="rmsnorm",
    )(x, gain.reshape(1, d))


GDN_COLS = 512
GDN_GROUP = 256
GDN_HB = 4


def _gdn_conv_kernel(x_ref, w_ref, o_ref):
    sec = pl.program_id(1) // (HA * DK_A // GDN_COLS)
    x = x_ref[0].astype(F32)
    w = w_ref[...]
    row = lax.broadcasted_iota(jnp.int32, x.shape, 0)
    y = x * w[CONV_W - 1:CONV_W]
    for i in range(CONV_W - 1):
        sh = CONV_W - 1 - i
        y = y + jnp.where(row >= sh, pltpu.roll(x, sh, axis=0), 0.0) * w[i:i + 1]
    y = y * jax.nn.sigmoid(y)
    qscale = jnp.where(sec == 0, DK_A ** -0.5, 1.0)
    for h in range(GDN_COLS // DK_A):
        yh = y[:, h * DK_A:(h + 1) * DK_A]
        nrm = yh * (lax.rsqrt(jnp.sum(yh * yh, axis=-1, keepdims=True) + EPS) * qscale)
        o_ref[0, 0, h] = jnp.where(sec < 2, nrm, yh).astype(o_ref.dtype)


def _gdn_conv(main, conv_w, bsz, s):
    ncol = 3 * HA * DK_A // GDN_COLS
    hpc = GDN_COLS // DK_A
    return pl.pallas_call(
        _gdn_conv_kernel,
        out_shape=jax.ShapeDtypeStruct((3, bsz, HA, s, DK_A), BF16),
        grid=(bsz, ncol),
        in_specs=[pl.BlockSpec((1, s, GDN_COLS), lambda b, c: (b, 0, c)),
                  pl.BlockSpec((CONV_W, GDN_COLS), lambda b, c: (0, c))],
        out_specs=pl.BlockSpec((1, 1, hpc, s, DK_A), lambda b, c: (c // (HA // hpc), b, c % (HA // hpc), 0, 0)),
        compiler_params=_cparams(("parallel", "parallel")),
        name="gdn_conv_silu_l2norm",
    )(main, conv_w)


def _col_rep(row, n):
    return jnp.broadcast_to(row, (LANES, n)).T


def _dot_hilo(x, m):
    hi = x.astype(BF16)
    lo = (x - hi.astype(F32)).astype(BF16)
    return jnp.dot(hi, m, preferred_element_type=F32) + jnp.dot(lo, m, preferred_element_type=F32)


def _gdn_prep_kernel(q_ref, k_ref, v_ref, g_ref, b_ref, u_ref, w_ref, qg_ref, kd_ref, a_ref, egl_ref):
    n = GDN_GROUP
    q, k, v = q_ref[0, 0, 0], k_ref[0, 0, 0], v_ref[0, 0, 0]
    g_row, b_row = g_ref[0, 0], b_ref[0, 0]
    ri = lax.broadcasted_iota(jnp.int32, (n, n), 0)
    ci = lax.broadcasted_iota(jnp.int32, (n, n), 1)
    same = (ri // CHUNK) == (ci // CHUNK)
    incl = same & (ri >= ci)
    strict = same & (ri > ci)
    one_if = lambda m: jnp.where(m, 1.0, 0.0).astype(BF16)
    g8 = jnp.broadcast_to(g_row, (8, n))
    gc_row = _dot_hilo(g8, one_if(same & (ri <= ci)))[0:1]
    gl_row = _dot_hilo(g8, one_if(same))[0:1]
    gc_c, gl_c, b_c = _col_rep(gc_row, n), _col_rep(gl_row, n), _col_rep(b_row, n)
    wide = lambda c: jnp.concatenate([c] * (n // LANES), axis=1)
    decay = jnp.exp(jnp.where(incl, wide(gc_c) - gc_row, NEG))
    nt = (((1,), (1,)), ((), ()))
    kk = lax.dot_general(k, k, nt, preferred_element_type=F32)
    qk = lax.dot_general(q, k, nt, preferred_element_type=F32)
    x = jnp.where(strict, -(kk * wide(b_c) * decay), 0.0)
    a = qk * decay
    eye = jnp.where(ri == ci, 1.0, 0.0)
    t = eye + x
    p = x
    for _ in range(5):
        pb = p.astype(BF16)
        p = jnp.dot(pb, pb, preferred_element_type=F32)
        t = t + jnp.dot(t.astype(BF16), p.astype(BF16), preferred_element_type=F32)
    tb = t.astype(BF16)
    kf = k.astype(F32)
    egc = jnp.exp(gc_c)
    u_ref[0, 0] = jnp.dot(tb, (v.astype(F32) * b_c).astype(BF16), preferred_element_type=F32).astype(u_ref.dtype)
    w_ref[0, 0] = jnp.dot(tb, (kf * b_c * egc).astype(BF16), preferred_element_type=F32).astype(w_ref.dtype)
    qg_ref[0, 0] = (q.astype(F32) * egc).astype(qg_ref.dtype)
    kd_ref[0, 0] = (kf * jnp.exp(gl_c - gc_c)).astype(kd_ref.dtype)
    for c in range(n // CHUNK):
        a_ref[0, 0, c * CHUNK:(c + 1) * CHUNK, :] = a[c * CHUNK:(c + 1) * CHUNK, c * CHUNK:(c + 1) * CHUNK].astype(a_ref.dtype)
    egl = jnp.exp(gl_c)
    egl_ref[0, 0, 0] = jnp.concatenate([egl[c * CHUNK:c * CHUNK + 1] for c in range(n // CHUNK)], axis=0)


def _gdn_prep(qkv, g_t, beta_t, bsz, s):
    n = GDN_GROUP
    tok = lambda width, dt: jax.ShapeDtypeStruct((bsz, HA, s, width), dt)
    tspec = lambda width: pl.BlockSpec((1, 1, n, width), lambda b, h, i: (b, h, i, 0))
    qspec = lambda sec: pl.BlockSpec((1, 1, 1, n, DK_A), lambda b, h, i: (sec, b, h, i, 0))
    rspec = pl.BlockSpec((1, 1, 1, n), lambda b, h, i: (b, h, 0, i))
    return pl.pallas_call(
        _gdn_prep_kernel,
        out_shape=(tok(DV_A, BF16), tok(DK_A, BF16), tok(DK_A, BF16), tok(DK_A, BF16), tok(CHUNK, BF16),
                   jax.ShapeDtypeStruct((bsz, HA, s // n, n // CHUNK, LANES), F32)),
        grid=(bsz, HA, s // n),
        in_specs=[qspec(0), qspec(1), qspec(2), rspec, rspec],
        out_specs=(tspec(DV_A), tspec(DK_A), tspec(DK_A), tspec(DK_A), tspec(CHUNK),
                   pl.BlockSpec((1, 1, 1, n // CHUNK, LANES), lambda b, h, i: (b, h, i, 0, 0))),
        compiler_params=_cparams(("parallel", "parallel", "parallel")),
        name="gdn_chunk_prep",
    )(qkv, qkv, qkv, g_t, beta_t)


def _gdn_scan_kernel(u_ref, w_ref, qg_ref, kd_ref, a_ref, egl_ref, z_ref, ng_ref, o_ref, state_ref, *, nchunk):
    state_ref[...] = jnp.zeros_like(state_ref)
    tn = (((0,), (0,)), ((), ()))

    def body(c, carry):
        r0 = pl.multiple_of(c * CHUNK, CHUNK)
        for h in range(GDN_HB):
            st = state_ref[h]
            sb = st.astype(BF16)
            rows = pl.ds(r0, CHUNK)
            v_new = u_ref[0, h, rows, :].astype(F32) - jnp.dot(w_ref[0, h, rows, :], sb, preferred_element_type=F32)
            vb = v_new.astype(BF16)
            o = (jnp.dot(qg_ref[0, h, rows, :], sb, preferred_element_type=F32)
                 + jnp.dot(a_ref[0, h, rows, :], vb, preferred_element_type=F32))
            state_ref[h] = (st * egl_ref[0, h, pl.ds(c, 1), :]
                            + lax.dot_general(kd_ref[0, h, rows, :], vb, tn, preferred_element_type=F32))
            cols = slice(h * DV_A, (h + 1) * DV_A)
            z = z_ref[0, rows, cols].astype(F32)
            on = o * lax.rsqrt(jnp.mean(o * o, axis=-1, keepdims=True) + EPS) * ng_ref[...]
            o_ref[0, rows, cols] = (on * (z * jax.nn.sigmoid(z))).astype(o_ref.dtype)
        return carry

    lax.fori_loop(0, nchunk, body, 0)


def _gdn_scan(u, w, qg, kd, a, egl, main, norm_gdn, bsz, s):
    hb = GDN_HB
    z_blk0 = 3 * HA * DK_A // (hb * DV_A)
    hspec = lambda width: pl.BlockSpec((1, hb, s, width), lambda b, hg: (b, hg, 0, 0))
    return pl.pallas_call(
        functools.partial(_gdn_scan_kernel, nchunk=s // CHUNK),
        out_shape=jax.ShapeDtypeStruct((bsz, s, HA * DV_A), BF16),
        grid=(bsz, HA // hb),
        in_specs=[hspec(DV_A), hspec(DK_A), hspec(DK_A), hspec(DK_A), hspec(CHUNK),
                  pl.BlockSpec((1, hb, s // CHUNK, LANES), lambda b, hg: (b, hg, 0, 0)),
                  pl.BlockSpec((1, s, hb * DV_A), lambda b, hg: (b, 0, z_blk0 + hg)),
                  pl.BlockSpec((1, DV_A), lambda b, hg: (0, 0))],
        out_specs=pl.BlockSpec((1, s, hb * DV_A), lambda b, hg: (b, 0, hg)),
        scratch_shapes=[pltpu.VMEM((hb, DK_A, DV_A), F32)],
        compiler_params=_cparams(("parallel", "parallel")),
        name="gdn_delta_scan",
    )(u, w, qg, kd, a, egl, main, norm_gdn.reshape(1, DV_A))


def _gated_deltanet(main, a_in, b_in, conv_w, a_log, dt_bias, norm_gdn, bsz, s):
    qkv = _gdn_conv(main, conv_w, bsz, s)
    g = -jnp.exp(a_log.astype(F32)) * jax.nn.softplus(a_in.astype(F32) + dt_bias.astype(F32))
    beta = jax.nn.sigmoid(b_in.astype(F32))
    g_t = g.transpose(0, 2, 1).reshape(bsz, HA, 1, s)
    beta_t = beta.transpose(0, 2, 1).reshape(bsz, HA, 1, s)
    u, w, qg, kd, a, egl = _gdn_prep(qkv, g_t, beta_t, bsz, s)
    egl = egl.reshape(bsz, HA, s // CHUNK, LANES)
    return _gdn_scan(u, w, qg, kd, a, egl, main, norm_gdn, bsz, s)


def _stack_heads(q, extra, scale):
    parts = []
    for r in range(R_GRP):
        qr = (q[:, r * DH_B:(r + 1) * DH_B].astype(F32) * scale).astype(BF16)
        parts.append(qr if extra is None else jnp.concatenate([qr, extra], axis=1))
    return jnp.concatenate(parts, axis=0)


def _unstack_heads(o_ref, o, tq):
    for r in range(R_GRP):
        o_ref[0, :, r * DH_B:(r + 1) * DH_B] = o[r * tq:(r + 1) * tq].astype(o_ref.dtype)


def _slc_kernel(q_ref, mb_ref, k_ref, v_ref, o_ref, *, tq):
    i = pl.program_id(2)
    rows = R_GRP * tq
    qa = _stack_heads(q_ref[0], mb_ref[0, 0], DH_B ** -0.5)

    def scores(j):
        k = k_ref[0, 0, pl.ds(pl.multiple_of(j * tq, tq), tq), :]
        return lax.dot_general(qa, k, (((1,), (1,)), ((), ())), preferred_element_type=F32)

    def update(j, s, carry):
        m, l, acc = carry
        v = v_ref[0, pl.ds(pl.multiple_of(j * tq, tq), tq), :]
        m_new = jnp.maximum(m, jnp.max(s, axis=1, keepdims=True))
        alpha = jnp.exp(m - m_new)
        p = jnp.exp(s - m_new)
        l = alpha * l + jnp.sum(p, axis=1, keepdims=True)
        acc = alpha * acc + jnp.dot(p.astype(BF16), v, preferred_element_type=F32)
        return m_new, l, acc

    init = (jnp.full((rows, 1), -jnp.inf, F32), jnp.zeros((rows, 1), F32), jnp.zeros((rows, DH_B), F32))
    carry = lax.fori_loop(0, i, lambda j, c: update(j, scores(j), c), init)
    s = scores(i)
    t_loc = lax.broadcasted_iota(jnp.int32, (rows, tq), 0) & (tq - 1)
    k_loc = lax.broadcasted_iota(jnp.int32, (rows, tq), 1)
    s = jnp.where(k_loc <= t_loc, s, NEG)
    m, l, acc = update(i, s, carry)
    _unstack_heads(o_ref, acc / l, tq)


def _slc_attention(q, maskbias, k_aug, v, bsz, s):
    tq = ATT_TQ
    return pl.pallas_call(
        functools.partial(_slc_kernel, tq=tq),
        out_shape=jax.ShapeDtypeStruct((bsz, s, HB * DH_B), BF16),
        grid=(bsz, G_KV, s // tq),
        in_specs=[pl.BlockSpec((1, tq, R_GRP * DH_B), lambda b, g, i: (b, i, g)),
                  pl.BlockSpec((1, 1, tq, LANES), lambda b, g, i: (b, g, i, 0)),
                  pl.BlockSpec((1, 1, s, 2 * DH_B), lambda b, g, i: (b, g, 0, 0)),
                  pl.BlockSpec((1, s, DH_B), lambda b, g, i: (b, 0, g))],
        out_specs=pl.BlockSpec((1, tq, R_GRP * DH_B), lambda b, g, i: (b, i, g)),
        compiler_params=_cparams(("parallel", "parallel", "arbitrary")),
        name="nsa_selected_attention",
    )(q, maskbias, k_aug, v)


def _win_kernel(q_ref, k0_ref, k1_ref, k2_ref, v0_ref, v1_ref, v2_ref, o_ref, *, tq):
    i = pl.program_id(2)
    rows = R_GRP * tq
    nk = 3 * tq
    qa = _stack_heads(q_ref[0], None, DH_B ** -0.5)
    k = jnp.concatenate([k0_ref[0], k1_ref[0], k2_ref[0]], axis=0)
    v = jnp.concatenate([v0_ref[0], v1_ref[0], v2_ref[0]], axis=0)
    s = lax.dot_general(qa, k, (((1,), (1,)), ((), ())), preferred_element_type=F32)
    t_loc = lax.broadcasted_iota(jnp.int32, (rows, nk), 0) & (tq - 1)
    k_loc = lax.broadcasted_iota(jnp.int32, (rows, nk), 1)
    dist = t_loc + 2 * tq - k_loc
    t_abs = (lax.broadcasted_iota(jnp.int32, (rows, 1), 0) & (tq - 1)) + i * tq
    bound = jnp.minimum(t_abs + 1, WINDOW)
    valid = dist.astype(jnp.uint32) < bound.astype(jnp.uint32)
    s = jnp.where(valid, s, NEG)
    m = jnp.max(s, axis=1, keepdims=True)
    p = jnp.exp(s - m)
    l = jnp.sum(p, axis=1, keepdims=True)
    o = jnp.dot(p.astype(BF16), v, preferred_element_type=F32) / l
    _unstack_heads(o_ref, o, tq)


def _win_attention(q, k, v, bsz, s):
    tq = ATT_TQ
    assert 2 * tq >= WINDOW
    qspec = pl.BlockSpec((1, tq, R_GRP * DH_B), lambda b, g, i: (b, i, g))

    def kv_spec(back):
        return pl.BlockSpec((1, tq, DH_B), lambda b, g, i: (b, jnp.maximum(i - back, 0), g))

    return pl.pallas_call(
        functools.partial(_win_kernel, tq=tq),
        out_shape=jax.ShapeDtypeStruct((bsz, s, HB * DH_B), BF16),
        grid=(bsz, G_KV, s // tq),
        in_specs=[qspec, kv_spec(2), kv_spec(1), kv_spec(0), kv_spec(2), kv_spec(1), kv_spec(0)],
        out_specs=qspec,
        compiler_params=_cparams(("parallel", "parallel", "arbitrary")),
        name="nsa_window_attention",
    )(q, k, k, k, v, v, v)


def _moe_kernel(be_ref, nu_ref, x_ref, wg_ref, wu_ref, wd_ref, bg_ref, bu_ref, bd_ref, o_ref):
    i = pl.program_id(0)
    f = pl.program_id(1)

    @pl.when(i < nu_ref[0])
    def _():
        x = x_ref[...]
        gate = jnp.dot(x, wg_ref[0].astype(BF16), preferred_element_type=F32) + bg_ref[0]
        up = jnp.dot(x, wu_ref[0].astype(BF16), preferred_element_type=F32) + bu_ref[0]
        gate = jnp.minimum(gate, SWIGLU_LIMIT)
        up = jnp.clip(up, -SWIGLU_LIMIT, SWIGLU_LIMIT)
        act = (up + 1.0) * gate * jax.nn.sigmoid(SWIGLU_ALPHA * gate)
        y = jnp.dot(act.astype(BF16), wd_ref[0].astype(BF16), preferred_element_type=F32)

        @pl.when(f == 0)
        def _():
            o_ref[...] = y + bd_ref[0]

        @pl.when(f != 0)
        def _():
            o_ref[...] += y

    @pl.when((i >= nu_ref[0]) & (f == 0))
    def _():
        o_ref[...] = jnp.zeros_like(o_ref)


def _moe_experts(rows, block_e, n_used, w_gate, b_gate, w_up, b_up, w_down, b_down):
    n_rows, d = rows.shape
    n_blocks = n_rows // MOE_TM
    nf = D_FF // MOE_TF

    def f_eff(i, f, nu):
        return jnp.where(i < nu[0], f, nf - 1)

    return pl.pallas_call(
        _moe_kernel,
        out_shape=jax.ShapeDtypeStruct((n_rows, d), F32),
        grid_spec=pltpu.PrefetchScalarGridSpec(
            num_scalar_prefetch=2,
            grid=(n_blocks, nf),
            in_specs=[
                pl.BlockSpec((MOE_TM, d), lambda i, f, be, nu: (i, 0)),
                pl.BlockSpec((1, d, MOE_TF), lambda i, f, be, nu: (be[i], 0, f_eff(i, f, nu))),
                pl.BlockSpec((1, d, MOE_TF), lambda i, f, be, nu: (be[i], 0, f_eff(i, f, nu))),
                pl.BlockSpec((1, MOE_TF, d), lambda i, f, be, nu: (be[i], f_eff(i, f, nu), 0)),
                pl.BlockSpec((1, 1, MOE_TF), lambda i, f, be, nu: (be[i], 0, f_eff(i, f, nu))),
                pl.BlockSpec((1, 1, MOE_TF), lambda i, f, be, nu: (be[i], 0, f_eff(i, f, nu))),
                pl.BlockSpec((1, 1, d), lambda i, f, be, nu: (be[i], 0, 0)),
            ],
            out_specs=pl.BlockSpec((MOE_TM, d), lambda i, f, be, nu: (i, 0)),
        ),
        compiler_params=_cparams(("arbitrary", "arbitrary")),
        name="moe_expert_ffn",
    )(block_e, n_used, rows, w_gate, w_up, w_down,
      b_gate.reshape(N_EXP, 1, D_FF), b_up.reshape(N_EXP, 1, D_FF), b_down.reshape(N_EXP, 1, d))


def _partial_rotary(x, positions):
    half = ROT_DIM // 2
    inv_freq = ROPE_THETA ** (-jnp.arange(half, dtype=F32) * 2.0 / ROT_DIM)
    ang = positions.astype(F32)[..., None] * inv_freq
    cos = jnp.cos(ang)[:, :, None, :]
    sin = jnp.sin(ang)[:, :, None, :]
    xf = x.astype(F32)
    x1, x2 = xf[..., :half], xf[..., half:ROT_DIM]
    return jnp.concatenate([x1 * cos - x2 * sin, x2 * cos + x1 * sin, xf[..., ROT_DIM:]], axis=-1)


def _compress_blocks(kv, pe, w1, w2):
    bsz, s, g, dh = kv.shape
    n_cmp = (s - L_CMP) // STRIDE_CMP + 1
    idx = np.arange(n_cmp)[:, None] * STRIDE_CMP + np.arange(L_CMP)[None, :]
    blk = kv[:, idx] + pe[None, None, :, None, :]
    flat = blk.transpose(0, 1, 3, 2, 4).reshape(bsz, n_cmp, g, L_CMP * dh)
    return jax.nn.gelu(flat @ w1) @ w2


def _native_sparse_attention(qb, kc, vc, ks, vs, kw, vw, gb, positions, pe_ck, w1_ck, w2_ck, pe_cv, w1_cv, w2_cv):
    bsz, s, _ = qb.shape
    scale = DH_B ** -0.5
    q4 = _partial_rotary(qb.reshape(bsz, s, HB, DH_B), positions)
    q = q4.reshape(bsz, s, G_KV, R_GRP, DH_B)
    kc = _partial_rotary(kc.reshape(bsz, s, G_KV, DH_B), positions)
    ks = _partial_rotary(ks.reshape(bsz, s, G_KV, DH_B), positions)
    kw = _partial_rotary(kw.reshape(bsz, s, G_KV, DH_B), positions)
    vc = vc.reshape(bsz, s, G_KV, DH_B).astype(F32)
    k_cmp = _compress_blocks(kc, pe_ck, w1_ck, w2_ck)
    v_cmp = _compress_blocks(vc, pe_cv, w1_cv, w2_cv)
    n_cmp = k_cmp.shape[1]
    t = jnp.arange(s)
    cmp_valid = (jnp.arange(n_cmp) * STRIDE_CMP + L_CMP - 1)[None, :] <= t[:, None]
    s_cmp = jnp.einsum('bsgrd,bcgd->bgrsc', q, k_cmp).astype(F32) * scale
    p_cmp = jax.nn.softmax(jnp.where(cmp_valid, s_cmp, NEG), axis=-1)
    p_cmp = jnp.where(jnp.any(cmp_valid, axis=-1)[:, None], p_cmp, 0.0)
    o_cmp = jnp.einsum('bgrsc,bcgd->bsgrd', p_cmp, v_cmp)
    n_slc = s // L_SLC
    c_start = np.arange(n_cmp) * STRIDE_CMP
    j_start = np.arange(n_slc) * L_SLC
    overlap = ((c_start[:, None] < j_start[None, :] + L_SLC)
               & (c_start[:, None] + L_CMP > j_start[None, :])).astype(np.float32)
    imp = jnp.einsum('bgrsc,cj->bgsj', p_cmp, jnp.asarray(overlap))
    blk = jnp.arange(n_slc)
    cur = t // L_SLC
    forced = (blk[None, :] == 0) | (blk[None, :] == cur[:, None])
    imp = jnp.where(forced, jnp.inf, jnp.where(blk[None, :] <= cur[:, None], imp, -jnp.inf))
    _, sel = lax.top_k(imp, min(T_SEL, n_slc))
    sel_mask = jnp.sum(jax.nn.one_hot(sel, n_slc, dtype=F32), axis=-2) > 0.5
    maskbias = jnp.where(sel_mask, 0.0, NEG).astype(BF16)
    maskbias = jnp.pad(maskbias, ((0, 0), (0, 0), (0, 0), (0, LANES - n_slc)))
    onehot = (t[:, None] // L_SLC == jnp.arange(LANES)[None, :]).astype(BF16)
    k_aug = jnp.concatenate([ks.transpose(0, 2, 1, 3).astype(BF16),
                             jnp.broadcast_to(onehot, (bsz, G_KV, s, LANES))], axis=-1)
    q_bf = q4.reshape(bsz, s, HB * DH_B).astype(BF16)
    o_slc = _slc_attention(q_bf, maskbias, k_aug, vs.astype(BF16), bsz, s)
    o_win = _win_attention(q_bf, kw.reshape(bsz, s, G_KV * DH_B).astype(BF16), vw.astype(BF16), bsz, s)
    o_slc = o_slc.reshape(bsz, s, G_KV, R_GRP, DH_B).astype(F32)
    o_win = o_win.reshape(bsz, s, G_KV, R_GRP, DH_B).astype(F32)
    gates = jax.nn.sigmoid(gb.astype(F32)).reshape(bsz, s, G_KV, R_GRP, 3)
    o = gates[..., 0:1] * o_cmp + gates[..., 1:2] * o_slc + gates[..., 2:3] * o_win
    return o.reshape(bsz, s, HB * DH_B)


def _moe_ffn(h, w_router, b_router, w_gate, b_gate, w_up, b_up, w_down, b_down):
    n_tok, d = h.shape
    logits = (h @ w_router + b_router).astype(F32)
    top_val, top_idx = lax.top_k(logits, TOP_K)
    weights = jax.nn.softmax(top_val, axis=-1)
    n_asg = n_tok * TOP_K
    e_flat = top_idx.reshape(-1)
    w_flat = weights.reshape(-1)
    tok = jnp.arange(n_asg, dtype=jnp.int32) // TOP_K
    order = jnp.argsort(e_flat)
    e_sorted = e_flat[order]
    tok_sorted = tok[order]
    counts = jnp.zeros((N_EXP,), jnp.int32).at[e_flat].add(1)
    padded = ((counts + MOE_TM - 1) // MOE_TM) * MOE_TM
    pad_end = jnp.cumsum(padded)
    pad_start = pad_end - padded
    start = jnp.cumsum(counts) - counts
    dest = pad_start[e_sorted] + jnp.arange(n_asg, dtype=jnp.int32) - start[e_sorted]
    n_blocks = (n_asg + N_EXP * (MOE_TM - 1) + MOE_TM - 1) // MOE_TM
    src_row = jnp.zeros((n_blocks * MOE_TM,), jnp.int32).at[dest].set(tok_sorted)
    rows = h.astype(BF16)[src_row]
    block_e = jnp.minimum(jnp.searchsorted(pad_end, jnp.arange(n_blocks) * MOE_TM, side='right'), N_EXP - 1)
    n_used = (pad_end[-1] // MOE_TM).astype(jnp.int32).reshape(1)
    yb = _moe_experts(rows, block_e.astype(jnp.int32), n_used, w_gate, b_gate, w_up, b_up, w_down, b_down)
    dest_tk = jnp.zeros((n_asg,), jnp.int32).at[order].set(dest)
    y_asg = yb[dest_tk].reshape(n_tok, TOP_K, d) * weights[:, :, None]
    return jnp.sum(y_asg, axis=1)


def _layer(x, positions, norm_mix, w_in, conv_w, a_log, dt_bias, norm_gdn, pe_ck, w1_ck, w2_ck,
           pe_cv, w1_cv, w2_cv, w_proj_a, w_proj_b, w_out, norm_ffn, w_router, b_router,
           w_gate, b_gate, w_up, b_up, w_down, b_down):
    bsz, s, d = x.shape
    n_tok = bsz * s
    x2 = x.reshape(n_tok, d)
    h = _rmsnorm(x2, norm_mix, BF16)
    n_small = 2 * HA + 3 * HB
    sp = SPLIT_POINTS
    w_main = jnp.concatenate([w_in[:, :sp[3]], w_in[:, sp[5]:sp[12]], w_in[:, sp[13]:]], axis=1).astype(BF16)
    w_small = jnp.concatenate([w_in[:, sp[3]:sp[5]], w_in[:, sp[12]:sp[13]]], axis=1)
    w_small = jnp.pad(w_small, ((0, 0), (0, LANES - n_small))).astype(BF16)
    main = _matmul(h, w_main, BF16).reshape(bsz, s, -1)
    small = _matmul(h, w_small, F32).reshape(bsz, s, -1)
    sizes = (HA * DK_A, HA * DK_A, HA * DV_A, HA * DV_A,
             HB * DH_B, G_KV * DH_B, G_KV * DH_B, G_KV * DH_B, G_KV * DH_B, G_KV * DH_B, G_KV * DH_B, 2 * D_MODEL)
    pts = tuple(sum(sizes[:i + 1]) for i in range(len(sizes) - 1))
    qa, ka, va, za, qb, kc, vc, ks, vs, kw, vw, gm = jnp.split(main, pts, axis=-1)
    aa, ba, gb = small[..., :HA], small[..., HA:2 * HA], small[..., 2 * HA:n_small]

    o_a = _gated_deltanet(main, aa, ba, conv_w, a_log, dt_bias, norm_gdn, bsz, s)
    o_b = _native_sparse_attention(qb, kc, vc, ks, vs, kw, vw, gb, positions,
                                   pe_ck, w1_ck, w2_ck, pe_cv, w1_cv, w2_cv)
    y_a = _matmul(o_a.reshape(n_tok, d), w_proj_a.astype(BF16), F32)
    y_b = _matmul(o_b.reshape(n_tok, d).astype(BF16), w_proj_b.astype(BF16), F32)
    gmf = jax.nn.sigmoid(gm.astype(F32)).reshape(n_tok, 2, d)
    merged = (gmf[:, 0] * y_a + gmf[:, 1] * y_b).astype(BF16)
    x2 = x2 + _matmul(merged, w_out.astype(BF16), F32)
    hf = _rmsnorm(x2, norm_ffn, F32)
    x2 = x2 + _moe_ffn(hf, w_router, b_router, w_gate, b_gate, w_up, b_up, w_down, b_down)
    return x2.reshape(bsz, s, d)


def kernel(x, positions, norm_mix, w_in, conv_w, a_log, dt_bias, norm_gdn, pe_ck, w1_ck, w2_ck, pe_cv, w1_cv, w2_cv, w_proj_a, w_proj_b, w_out, norm_ffn, w_router, b_router, w_gate, b_gate, w_up, b_up, w_down, b_down, norm_final):
    depth = norm_mix.shape[0]
    for l in range(depth):
        x = _layer(x, positions, norm_mix[l], w_in[l], conv_w[l], a_log[l], dt_bias[l], norm_gdn[l],
                   pe_ck[l], w1_ck[l], w2_ck[l], pe_cv[l], w1_cv[l], w2_cv[l],
                   w_proj_a[l], w_proj_b[l], w_out[l], norm_ffn[l], w_router[l], b_router[l],
                   w_gate[l], b_gate[l], w_up[l], b_up[l], w_down[l], b_down[l])
    bsz, s, d = x.shape
    return _rmsnorm(x.reshape(bsz * s, d), norm_final, F32).reshape(bsz, s, d)
```

```python
import functools
import math

import jax
import jax.numpy as jnp
import numpy as np
from jax import lax
from jax.experimental import pallas as pl
from jax.experimental.pallas import tpu as pltpu

F32 = jnp.float32
BF16 = jnp.bfloat16

D_MODEL = 2048
EPS = 1e-6
NEG = -1e30
HA = D_MODEL // 128
DK_A = 128
DV_A = 128
CONV_W = 4
CHUNK = 64
HB = D_MODEL // 128
G_KV = 4
R_GRP = HB // G_KV
DH_B = 128
ROT_DIM = DH_B // 4
ROPE_THETA = 500000.0
L_CMP = 32
STRIDE_CMP = 16
L_SLC = 64
T_SEL = 8
WINDOW = 512
N_EXP = 32
TOP_K = 4
D_FF = D_MODEL
SWIGLU_LIMIT = 7.0
SWIGLU_ALPHA = 1.702
SPLIT_SIZES = (HA * DK_A, HA * DK_A, HA * DV_A, HA * DV_A, HA, HA,
               HB * DH_B, G_KV * DH_B, G_KV * DH_B, G_KV * DH_B, G_KV * DH_B, G_KV * DH_B, G_KV * DH_B,
               3 * HB, 2 * D_MODEL)
SPLIT_POINTS = tuple(sum(SPLIT_SIZES[:i + 1]) for i in range(len(SPLIT_SIZES) - 1))

V7X_VMEM_LIMIT_BYTES = 56 * 1024 * 1024
LANES = 128
MOE_TM = 1024
MOE_SUB = 256
MOE_TF = 256
ATT_TQ = 256


def _cparams(sem):
    return pltpu.CompilerParams(dimension_semantics=sem, vmem_limit_bytes=V7X_VMEM_LIMIT_BYTES)


def _mm_kernel(x_ref, w_ref, o_ref):
    o_ref[...] = jnp.dot(x_ref[...], w_ref[...], preferred_element_type=F32).astype(o_ref.dtype)


def _matmul(x, w, out_dtype, tm=1024, tn=1024):
    m, k = x.shape
    n = w.shape[1]
    tm, tn = min(tm, m), min(tn, n)
    assert m % tm == 0 and n % tn == 0
    return pl.pallas_call(
        _mm_kernel,
        out_shape=jax.ShapeDtypeStruct((m, n), out_dtype),
        grid=(n // tn, m // tm),
        in_specs=[pl.BlockSpec((tm, k), lambda j, i: (i, 0)),
                  pl.BlockSpec((k, tn), lambda j, i: (0, j))],
        out_specs=pl.BlockSpec((tm, tn), lambda j, i: (i, j)),
        compiler_params=_cparams(("parallel", "parallel")),
        name="dense_matmul",
    )(x, w)


def _rmsnorm_kernel(x_ref, g_ref, o_ref):
    x = x_ref[...]
    y = x * lax.rsqrt(jnp.mean(x * x, axis=-1, keepdims=True) + EPS)
    o_ref[...] = (y * g_ref[...]).astype(o_ref.dtype)


def _rmsnorm(x, gain, out_dtype, tm=512):
    m, d = x.shape
    return pl.pallas_call(
        _rmsnorm_kernel,
        out_shape=jax.ShapeDtypeStruct((m, d), out_dtype),
        grid=(m // tm,),
        in_specs=[pl.BlockSpec((tm, d), lambda i: (i, 0)),
                  pl.BlockSpec((1, d), lambda i: (0, 0))],
        out_specs=pl.BlockSpec((tm, d), lambda i: (i, 0)),
        compiler_params=_cparams(("parallel",)),
        name="rmsnorm",
    )(x, gain.reshape(1, d))


def _add_rmsnorm_kernel(x_ref, y_ref, g_ref, o_ref):
    x = x_ref[...] + y_ref[...]
    y = x * lax.rsqrt(jnp.mean(x * x, axis=-1, keepdims=True) + EPS)
    o_ref[...] = (y * g_ref[...]).astype(o_ref.dtype)


def _add_rmsnorm(x, y, gain, out_dtype, tm=512):
    m, d = x.shape
    row = pl.BlockSpec((tm, d), lambda i: (i, 0))
    return pl.pallas_call(
        _add_rmsnorm_kernel,
        out_shape=jax.ShapeDtypeStruct((m, d), out_dtype),
        grid=(m // tm,),
        in_specs=[row, row, pl.BlockSpec((1, d), lambda i: (0, 0))],
        out_specs=row,
        compiler_params=_cparams(("parallel",)),
        name="add_rmsnorm",
    )(x, y, gain.reshape(1, d))


def _merge_kernel(oa_ref, ob_ref, wa_ref, wb_ref, ga_ref, gb_ref, o_ref):
    ya = jnp.dot(oa_ref[...], wa_ref[...], preferred_element_type=F32)
    yb = jnp.dot(ob_ref[...], wb_ref[...], preferred_element_type=F32)
    o_ref[...] = (jax.nn.sigmoid(ga_ref[...].astype(F32)) * ya
                  + jax.nn.sigmoid(gb_ref[...].astype(F32)) * yb).astype(o_ref.dtype)


def _merge(o_a, o_b, w_a, w_b, main2d, gm_col0, tm=512, tn=1024):
    m, d = o_a.shape
    assert gm_col0 % tn == 0 and d % tn == 0
    g0 = gm_col0 // tn
    lhs = pl.BlockSpec((tm, d), lambda j, i: (i, 0))
    rhs = pl.BlockSpec((d, tn), lambda j, i: (0, j))
    return pl.pallas_call(
        _merge_kernel,
        out_shape=jax.ShapeDtypeStruct((m, d), BF16),
        grid=(d // tn, m // tm),
        in_specs=[lhs, lhs, rhs, rhs,
                  pl.BlockSpec((tm, tn), lambda j, i: (i, g0 + j)),
                  pl.BlockSpec((tm, tn), lambda j, i: (i, g0 + d // tn + j))],
        out_specs=pl.BlockSpec((tm, tn), lambda j, i: (i, j)),
        compiler_params=_cparams(("parallel", "parallel")),
        name="mixer_merge",
    )(o_a, o_b, w_a, w_b, main2d, main2d)


def _outproj_kernel(m_ref, w_ref, x_ref, g_ref, x1_ref, h_ref):
    x1 = x_ref[...] + jnp.dot(m_ref[...], w_ref[...], preferred_element_type=F32)
    x1_ref[...] = x1
    y = x1 * lax.rsqrt(jnp.mean(x1 * x1, axis=-1, keepdims=True) + EPS)
    h_ref[...] = (y * g_ref[...]).astype(h_ref.dtype)


def _outproj(merged, w_out, x, gain, tm=512):
    m, d = x.shape
    row = lambda: pl.BlockSpec((tm, d), lambda i: (i, 0))
    return pl.pallas_call(
        _outproj_kernel,
        out_shape=(jax.ShapeDtypeStruct((m, d), F32), jax.ShapeDtypeStruct((m, d), BF16)),
        grid=(m // tm,),
        in_specs=[row(), pl.BlockSpec((d, d), lambda i: (0, 0)), row(), pl.BlockSpec((1, d), lambda i: (0, 0))],
        out_specs=(row(), row()),
        compiler_params=_cparams(("parallel",)),
        name="out_proj_residual_norm",
    )(merged, w_out, x, gain.reshape(1, d))


GDN_COLS = 512
GDN_GROUP = 256
GDN_PH = 4
GDN_TS = 512


def _gdn_conv_kernel(x_ref, w_ref, o_ref):
    sec = pl.program_id(1) // (HA * DK_A // GDN_COLS)
    x = x_ref[0].astype(F32)
    w = w_ref[...]
    row = lax.broadcasted_iota(jnp.int32, x.shape, 0)
    y = x * w[CONV_W - 1:CONV_W]
    for i in range(CONV_W - 1):
        sh = CONV_W - 1 - i
        y = y + jnp.where(row >= sh, pltpu.roll(x, sh, axis=0), 0.0) * w[i:i + 1]
    y = y * jax.nn.sigmoid(y)
    qscale = jnp.where(sec == 0, DK_A ** -0.5, 1.0)
    for h in range(GDN_COLS // DK_A):
        yh = y[:, h * DK_A:(h + 1) * DK_A]
        nrm = yh * (lax.rsqrt(jnp.sum(yh * yh, axis=-1, keepdims=True) + EPS) * qscale)
        o_ref[0, 0, h] = jnp.where(sec < 2, nrm, yh).astype(o_ref.dtype)


def _gdn_conv(main, conv_w, bsz, s):
    ncol = 3 * HA * DK_A // GDN_COLS
    hpc = GDN_COLS // DK_A
    return pl.pallas_call(
        _gdn_conv_kernel,
        out_shape=jax.ShapeDtypeStruct((3, bsz, HA, s, DK_A), BF16),
        grid=(bsz, ncol),
        in_specs=[pl.BlockSpec((1, s, GDN_COLS), lambda b, c: (b, 0, c)),
                  pl.BlockSpec((CONV_W, GDN_COLS), lambda b, c: (0, c))],
        out_specs=pl.BlockSpec((1, 1, hpc, s, DK_A), lambda b, c: (c // (HA // hpc), b, c % (HA // hpc), 0, 0)),
        compiler_params=_cparams(("parallel", "parallel")),
        name="gdn_conv_silu_l2norm",
    )(main, conv_w)


def _col_rep(row, n):
    return jnp.broadcast_to(row, (LANES, n)).T


def _dot_hilo(x, m):
    hi = x.astype(BF16)
    lo = (x - hi.astype(F32)).astype(BF16)
    return jnp.dot(hi, m, preferred_element_type=F32) + jnp.dot(lo, m, preferred_element_type=F32)


def _gdn_prep_kernel(q_ref, k_ref, v_ref, g_ref, b_ref, u_ref, w_ref, qg_ref, kd_ref, a_ref, egl_ref):
    n = GDN_GROUP
    ri = lax.broadcasted_iota(jnp.int32, (n, n), 0)
    ci = lax.broadcasted_iota(jnp.int32, (n, n), 1)
    same = (ri // CHUNK) == (ci // CHUNK)
    incl = same & (ri >= ci)
    strict = same & (ri > ci)
    one_if = lambda m: jnp.where(m, 1.0, 0.0).astype(BF16)
    cum_m, tot_m = one_if(same & (ri <= ci)), one_if(same)
    eye = jnp.where(ri == ci, 1.0, 0.0)
    wide = lambda c: jnp.concatenate([c] * (n // LANES), axis=1)
    nt = (((1,), (1,)), ((), ()))
    heads = range(GDN_PH)
    ts, ps, rhs = [], [], []
    for h in heads:
        q, k, v = q_ref[0, 0, h], k_ref[0, 0, h], v_ref[0, 0, h]
        g8 = jnp.broadcast_to(g_ref[0, h], (8, n))
        gc_row = _dot_hilo(g8, cum_m)[0:1]
        gl_row = _dot_hilo(g8, tot_m)[0:1]
        gc_c, gl_c, b_c = _col_rep(gc_row, n), _col_rep(gl_row, n), _col_rep(b_ref[0, h], n)
        decay = jnp.exp(jnp.where(incl, wide(gc_c) - gc_row, NEG))
        kk = lax.dot_general(k, k, nt, preferred_element_type=F32)
        qk = lax.dot_general(q, k, nt, preferred_element_type=F32)
        xb = jnp.where(strict, -(kk * wide(b_c) * decay), 0.0).astype(BF16)
        a = qk * decay
        kf = k.astype(F32)
        egc = jnp.exp(gc_c)
        qg_ref[0, h] = (q.astype(F32) * egc).astype(qg_ref.dtype)
        kd_ref[0, h] = (kf * jnp.exp(gl_c - gc_c)).astype(kd_ref.dtype)
        for c in range(n // CHUNK):
            blk = slice(c * CHUNK, (c + 1) * CHUNK)
            a_ref[0, h, blk, :] = a[blk, blk].astype(a_ref.dtype)
        egl = jnp.exp(gl_c)
        egl_ref[0, h, 0] = jnp.concatenate([egl[c * CHUNK:c * CHUNK + 1] for c in range(n // CHUNK)], axis=0)
        rhs.append(jnp.concatenate([(v.astype(F32) * b_c).astype(BF16), (kf * b_c * egc).astype(BF16)], axis=1))
        ts.append(eye + xb.astype(F32))
        ps.append(xb)
    ps = [jnp.dot(p, p, preferred_element_type=F32).astype(BF16) for p in ps]
    for step in range(5):
        for h in heads:
            if step < 4:
                tp = jnp.dot(jnp.concatenate([ts[h].astype(BF16), ps[h]], axis=0), ps[h], preferred_element_type=F32)
                ts[h] = ts[h] + tp[:n]
                ps[h] = tp[n:].astype(BF16)
            else:
                ts[h] = ts[h] + jnp.dot(ts[h].astype(BF16), ps[h], preferred_element_type=F32)
    for h in heads:
        uw = jnp.dot(ts[h].astype(BF16), rhs[h], preferred_element_type=F32)
        u_ref[0, h] = uw[:, :DV_A].astype(u_ref.dtype)
        w_ref[0, h] = uw[:, DV_A:].astype(w_ref.dtype)


def _gdn_prep(qkv, g_t, beta_t, bsz, s):
    n, ph = GDN_GROUP, GDN_PH
    tok = lambda width, dt: jax.ShapeDtypeStruct((bsz, HA, s, width), dt)
    tspec = lambda width: pl.BlockSpec((1, ph, n, width), lambda b, h, i: (b, h, i, 0))
    qspec = lambda sec: pl.BlockSpec((1, 1, ph, n, DK_A), lambda b, h, i: (sec, b, h, i, 0))
    rspec = pl.BlockSpec((1, ph, 1, n), lambda b, h, i: (b, h, 0, i))
    return pl.pallas_call(
        _gdn_prep_kernel,
        out_shape=(tok(DV_A, BF16), tok(DK_A, BF16), tok(DK_A, BF16), tok(DK_A, BF16), tok(CHUNK, BF16),
                   jax.ShapeDtypeStruct((bsz, HA, s // n, n // CHUNK, LANES), F32)),
        grid=(bsz, HA // ph, s // n),
        in_specs=[qspec(0), qspec(1), qspec(2), rspec, rspec],
        out_specs=(tspec(DV_A), tspec(DK_A), tspec(DK_A), tspec(DK_A), tspec(CHUNK),
                   pl.BlockSpec((1, ph, 1, n // CHUNK, LANES), lambda b, h, i: (b, h, i, 0, 0))),
        compiler_params=_cparams(("parallel", "parallel", "parallel")),
        name="gdn_chunk_prep",
    )(qkv, qkv, qkv, g_t, beta_t)


def _gdn_scan_kernel(u_ref, w_ref, qg_ref, kd_ref, a_ref, egl_ref, z_ref, ng_ref, o_ref, state_ref, *, nchunk):
    @pl.when(pl.program_id(1) == 0)
    def _():
        state_ref[...] = jnp.zeros_like(state_ref)

    tn = (((0,), (0,)), ((), ()))
    heads = range(HA)

    def body(c, carry):
        rows = pl.ds(pl.multiple_of(c * CHUNK, CHUNK), CHUNK)
        st = [state_ref[h] for h in heads]
        sb = [x.astype(BF16) for x in st]
        vb = [(u_ref[0, h, rows, :].astype(F32)
               - jnp.dot(w_ref[0, h, rows, :], sb[h], preferred_element_type=F32)).astype(BF16) for h in heads]
        o = [jnp.dot(qg_ref[0, h, rows, :], sb[h], preferred_element_type=F32)
             + jnp.dot(a_ref[0, h, rows, :], vb[h], preferred_element_type=F32) for h in heads]
        for h in heads:
            state_ref[h] = (st[h] * egl_ref[0, h, pl.ds(c, 1), :]
                            + lax.dot_general(kd_ref[0, h, rows, :], vb[h], tn, preferred_element_type=F32))
        for h in heads:
            cols = slice(h * DV_A, (h + 1) * DV_A)
            z = z_ref[0, rows, cols].astype(F32)
            on = o[h] * lax.rsqrt(jnp.mean(o[h] * o[h], axis=-1, keepdims=True) + EPS) * ng_ref[...]
            o_ref[0, rows, cols] = (on * (z * jax.nn.sigmoid(z))).astype(o_ref.dtype)
        return carry

    lax.fori_loop(0, nchunk, body, 0)


def _gdn_scan(u, w, qg, kd, a, egl, main, norm_gdn, bsz, s):
    ts = GDN_TS
    z_blk0 = 3 * HA * DK_A // (HA * DV_A)
    hspec = lambda width: pl.BlockSpec((1, HA, ts, width), lambda b, i: (b, 0, i, 0))
    return pl.pallas_call(
        functools.partial(_gdn_scan_kernel, nchunk=ts // CHUNK),
        out_shape=jax.ShapeDtypeStruct((bsz, s, HA * DV_A), BF16),
        grid=(bsz, s // ts),
        in_specs=[hspec(DV_A), hspec(DK_A), hspec(DK_A), hspec(DK_A), hspec(CHUNK),
                  pl.BlockSpec((1, HA, ts // CHUNK, LANES), lambda b, i: (b, 0, i, 0)),
                  pl.BlockSpec((1, ts, HA * DV_A), lambda b, i: (b, i, z_blk0)),
                  pl.BlockSpec((1, DV_A), lambda b, i: (0, 0))],
        out_specs=pl.BlockSpec((1, ts, HA * DV_A), lambda b, i: (b, i, 0)),
        scratch_shapes=[pltpu.VMEM((HA, DK_A, DV_A), F32)],
        compiler_params=_cparams(("parallel", "arbitrary")),
        name="gdn_delta_scan",
    )(u, w, qg, kd, a, egl, main, norm_gdn.reshape(1, DV_A))


def _gated_deltanet(main, a_in, b_in, conv_w, a_log, dt_bias, norm_gdn, bsz, s):
    qkv = _gdn_conv(main, conv_w, bsz, s)
    g = -jnp.exp(a_log.astype(F32)) * jax.nn.softplus(a_in.astype(F32) + dt_bias.astype(F32))
    beta = jax.nn.sigmoid(b_in.astype(F32))
    g_t = g.transpose(0, 2, 1).reshape(bsz, HA, 1, s)
    beta_t = beta.transpose(0, 2, 1).reshape(bsz, HA, 1, s)
    u, w, qg, kd, a, egl = _gdn_prep(qkv, g_t, beta_t, bsz, s)
    egl = egl.reshape(bsz, HA, s // CHUNK, LANES)
    return _gdn_scan(u, w, qg, kd, a, egl, main, norm_gdn, bsz, s)


def _stack_heads(q, extra, scale):
    parts = []
    for r in range(R_GRP):
        qr = (q[:, r * DH_B:(r + 1) * DH_B].astype(F32) * scale).astype(BF16)
        parts.append(qr if extra is None else jnp.concatenate([qr, extra], axis=1))
    return jnp.concatenate(parts, axis=0)


def _unstack_heads(o_ref, o, tq):
    for r in range(R_GRP):
        o_ref[0, :, r * DH_B:(r + 1) * DH_B] = o[r * tq:(r + 1) * tq].astype(o_ref.dtype)


def _slc_kernel(q_ref, mb_ref, k_ref, v_ref, o_ref, *, tq):
    i = pl.program_id(2)
    rows = R_GRP * tq
    qa = _stack_heads(q_ref[0], mb_ref[0, 0], DH_B ** -0.5)

    def scores(j):
        k = k_ref[0, 0, pl.ds(pl.multiple_of(j * tq, tq), tq), :]
        return lax.dot_general(qa, k, (((1,), (1,)), ((), ())), preferred_element_type=F32)

    def update(j, s, carry):
        m, l, acc = carry
        v = v_ref[0, pl.ds(pl.multiple_of(j * tq, tq), tq), :]
        m_new = jnp.maximum(m, jnp.max(s, axis=1, keepdims=True))
        alpha = jnp.exp(m - m_new)
        p = jnp.exp(s - m_new)
        l = alpha * l + jnp.sum(p, axis=1, keepdims=True)
        acc = alpha * acc + jnp.dot(p.astype(BF16), v, preferred_element_type=F32)
        return m_new, l, acc

    init = (jnp.full((rows, 1), -jnp.inf, F32), jnp.zeros((rows, 1), F32), jnp.zeros((rows, DH_B), F32))
    carry = lax.fori_loop(0, i, lambda j, c: update(j, scores(j), c), init)
    s = scores(i)
    t_loc = lax.broadcasted_iota(jnp.int32, (rows, tq), 0) & (tq - 1)
    k_loc = lax.broadcasted_iota(jnp.int32, (rows, tq), 1)
    s = jnp.where(k_loc <= t_loc, s, NEG)
    m, l, acc = update(i, s, carry)
    _unstack_heads(o_ref, acc / l, tq)


def _slc_attention(q, maskbias, k_aug, v, bsz, s):
    tq = ATT_TQ
    return pl.pallas_call(
        functools.partial(_slc_kernel, tq=tq),
        out_shape=jax.ShapeDtypeStruct((bsz, s, HB * DH_B), BF16),
        grid=(bsz, G_KV, s // tq),
        in_specs=[pl.BlockSpec((1, tq, R_GRP * DH_B), lambda b, g, i: (b, i, g)),
                  pl.BlockSpec((1, 1, tq, LANES), lambda b, g, i: (b, g, i, 0)),
                  pl.BlockSpec((1, 1, s, 2 * DH_B), lambda b, g, i: (b, g, 0, 0)),
                  pl.BlockSpec((1, s, DH_B), lambda b, g, i: (b, 0, g))],
        out_specs=pl.BlockSpec((1, tq, R_GRP * DH_B), lambda b, g, i: (b, i, g)),
        compiler_params=_cparams(("parallel", "parallel", "arbitrary")),
        name="nsa_selected_attention",
    )(q, maskbias, k_aug, v)


def _win_kernel(q_ref, k0_ref, k1_ref, k2_ref, v0_ref, v1_ref, v2_ref, o_ref, *, tq):
    i = pl.program_id(2)
    rows = R_GRP * tq
    nk = 3 * tq
    qa = _stack_heads(q_ref[0], None, DH_B ** -0.5)
    k = jnp.concatenate([k0_ref[0], k1_ref[0], k2_ref[0]], axis=0)
    v = jnp.concatenate([v0_ref[0], v1_ref[0], v2_ref[0]], axis=0)
    s = lax.dot_general(qa, k, (((1,), (1,)), ((), ())), preferred_element_type=F32)
    t_loc = lax.broadcasted_iota(jnp.int32, (rows, nk), 0) & (tq - 1)
    k_loc = lax.broadcasted_iota(jnp.int32, (rows, nk), 1)
    dist = t_loc + 2 * tq - k_loc
    t_abs = (lax.broadcasted_iota(jnp.int32, (rows, 1), 0) & (tq - 1)) + i * tq
    bound = jnp.minimum(t_abs + 1, WINDOW)
    valid = dist.astype(jnp.uint32) < bound.astype(jnp.uint32)
    s = jnp.where(valid, s, NEG)
    m = jnp.max(s, axis=1, keepdims=True)
    p = jnp.exp(s - m)
    l = jnp.sum(p, axis=1, keepdims=True)
    o = jnp.dot(p.astype(BF16), v, preferred_element_type=F32) / l
    _unstack_heads(o_ref, o, tq)


def _win_attention(q, k, v, bsz, s):
    tq = ATT_TQ
    assert 2 * tq >= WINDOW
    qspec = pl.BlockSpec((1, tq, R_GRP * DH_B), lambda b, g, i: (b, i, g))

    def kv_spec(back):
        return pl.BlockSpec((1, tq, DH_B), lambda b, g, i: (b, jnp.maximum(i - back, 0), g))

    return pl.pallas_call(
        functools.partial(_win_kernel, tq=tq),
        out_shape=jax.ShapeDtypeStruct((bsz, s, HB * DH_B), BF16),
        grid=(bsz, G_KV, s // tq),
        in_specs=[qspec, kv_spec(2), kv_spec(1), kv_spec(0), kv_spec(2), kv_spec(1), kv_spec(0)],
        out_specs=qspec,
        compiler_params=_cparams(("parallel", "parallel", "arbitrary")),
        name="nsa_window_attention",
    )(q, k, k, k, v, v, v)


def _moe_kernel(be_ref, nv_ref, x_ref, wg_ref, wu_ref, wd_ref, bg_ref, bu_ref, bd_ref, o_ref, acc_ref):
    i = pl.program_id(0)
    f = pl.program_id(1)
    nv = nv_ref[i]

    @pl.when(f == 0)
    def _():
        acc_ref[...] = jnp.broadcast_to(bd_ref[0], acc_ref.shape)

    @pl.when(nv > 0)
    def _():
        wgb = wg_ref[0].astype(BF16)
        wub = wu_ref[0].astype(BF16)
        wdb = wd_ref[0].astype(BF16)

        def sub(sb, carry):
            rows = pl.ds(pl.multiple_of(sb * MOE_SUB, MOE_SUB), MOE_SUB)
            x = x_ref[rows, :]
            gate = jnp.dot(x, wgb, preferred_element_type=F32) + bg_ref[0]
            up = jnp.dot(x, wub, preferred_element_type=F32) + bu_ref[0]
            gate = jnp.minimum(gate, SWIGLU_LIMIT)
            up = jnp.clip(up, -SWIGLU_LIMIT, SWIGLU_LIMIT)
            act = (up + 1.0) * gate * jax.nn.sigmoid(SWIGLU_ALPHA * gate)
            acc_ref[rows, :] += jnp.dot(act.astype(BF16), wdb, preferred_element_type=F32)
            return carry

        lax.fori_loop(0, (nv + MOE_SUB - 1) // MOE_SUB, sub, 0)

    @pl.when(f == pl.num_programs(1) - 1)
    def _():
        o_ref[...] = acc_ref[...].astype(o_ref.dtype)


def _moe_experts(rows, block_e, n_valid, w_gate, b_gate, w_up, b_up, w_down, b_down):
    n_rows, d = rows.shape
    n_blocks = n_rows // MOE_TM
    nf = D_FF // MOE_TF

    def f_eff(i, f, nv):
        return jnp.where(nv[i] > 0, f, nf - 1)

    return pl.pallas_call(
        _moe_kernel,
        out_shape=jax.ShapeDtypeStruct((n_rows, d), BF16),
        grid_spec=pltpu.PrefetchScalarGridSpec(
            num_scalar_prefetch=2,
            grid=(n_blocks, nf),
            in_specs=[
                pl.BlockSpec((MOE_TM, d), lambda i, f, be, nv: (i, 0)),
                pl.BlockSpec((1, d, MOE_TF), lambda i, f, be, nv: (be[i], 0, f_eff(i, f, nv))),
                pl.BlockSpec((1, d, MOE_TF), lambda i, f, be, nv: (be[i], 0, f_eff(i, f, nv))),
                pl.BlockSpec((1, MOE_TF, d), lambda i, f, be, nv: (be[i], f_eff(i, f, nv), 0)),
                pl.BlockSpec((1, 1, MOE_TF), lambda i, f, be, nv: (be[i], 0, f_eff(i, f, nv))),
                pl.BlockSpec((1, 1, MOE_TF), lambda i, f, be, nv: (be[i], 0, f_eff(i, f, nv))),
                pl.BlockSpec((1, 1, d), lambda i, f, be, nv: (be[i], 0, 0)),
            ],
            out_specs=pl.BlockSpec((MOE_TM, d), lambda i, f, be, nv: (i, 0)),
            scratch_shapes=[pltpu.VMEM((MOE_TM, d), F32)],
        ),
        compiler_params=_cparams(("arbitrary", "arbitrary")),
        name="moe_expert_ffn",
    )(block_e, n_valid, rows, w_gate, w_up, w_down,
      b_gate.reshape(N_EXP, 1, D_FF), b_up.reshape(N_EXP, 1, D_FF), b_down.reshape(N_EXP, 1, d))


def _partial_rotary(x, positions):
    half = ROT_DIM // 2
    inv_freq = ROPE_THETA ** (-jnp.arange(half, dtype=F32) * 2.0 / ROT_DIM)
    ang = positions.astype(F32)[..., None] * inv_freq
    cos = jnp.cos(ang)[:, :, None, :]
    sin = jnp.sin(ang)[:, :, None, :]
    xf = x.astype(F32)
    x1, x2 = xf[..., :half], xf[..., half:ROT_DIM]
    return jnp.concatenate([x1 * cos - x2 * sin, x2 * cos + x1 * sin, xf[..., ROT_DIM:]], axis=-1)


def _compress_blocks(kv, pe, w1, w2):
    bsz, s, g, dh = kv.shape
    n_cmp = (s - L_CMP) // STRIDE_CMP + 1
    idx = np.arange(n_cmp)[:, None] * STRIDE_CMP + np.arange(L_CMP)[None, :]
    blk = kv[:, idx] + pe[None, None, :, None, :]
    flat = blk.transpose(0, 1, 3, 2, 4).reshape(bsz, n_cmp, g, L_CMP * dh)
    return jax.nn.gelu(flat @ w1) @ w2


def _native_sparse_attention(qb, kc, vc, ks, vs, kw, vw, gb, positions, pe_ck, w1_ck, w2_ck, pe_cv, w1_cv, w2_cv):
    bsz, s, _ = qb.shape
    scale = DH_B ** -0.5
    q4 = _partial_rotary(qb.reshape(bsz, s, HB, DH_B), positions)
    q = q4.reshape(bsz, s, G_KV, R_GRP, DH_B)
    kc = _partial_rotary(kc.reshape(bsz, s, G_KV, DH_B), positions)
    ks = _partial_rotary(ks.reshape(bsz, s, G_KV, DH_B), positions)
    kw = _partial_rotary(kw.reshape(bsz, s, G_KV, DH_B), positions)
    vc = vc.reshape(bsz, s, G_KV, DH_B).astype(F32)
    k_cmp = _compress_blocks(kc, pe_ck, w1_ck, w2_ck)
    v_cmp = _compress_blocks(vc, pe_cv, w1_cv, w2_cv)
    n_cmp = k_cmp.shape[1]
    t = jnp.arange(s)
    cmp_valid = (jnp.arange(n_cmp) * STRIDE_CMP + L_CMP - 1)[None, :] <= t[:, None]
    s_cmp = jnp.einsum('bsgrd,bcgd->bgrsc', q, k_cmp).astype(F32) * scale
    p_cmp = jax.nn.softmax(jnp.where(cmp_valid, s_cmp, NEG), axis=-1)
    p_cmp = jnp.where(jnp.any(cmp_valid, axis=-1)[:, None], p_cmp, 0.0)
    o_cmp = jnp.einsum('bgrsc,bcgd->bsgrd', p_cmp, v_cmp)
    n_slc = s // L_SLC
    c_start = np.arange(n_cmp) * STRIDE_CMP
    j_start = np.arange(n_slc) * L_SLC
    overlap = ((c_start[:, None] < j_start[None, :] + L_SLC)
               & (c_start[:, None] + L_CMP > j_start[None, :])).astype(np.float32)
    imp = jnp.einsum('bgrsc,cj->bgsj', p_cmp, jnp.asarray(overlap))
    blk = jnp.arange(n_slc)
    cur = t // L_SLC
    forced = (blk[None, :] == 0) | (blk[None, :] == cur[:, None])
    imp = jnp.where(forced, jnp.inf, jnp.where(blk[None, :] <= cur[:, None], imp, -jnp.inf))
    _, sel = lax.top_k(imp, min(T_SEL, n_slc))
    sel_mask = jnp.sum(jax.nn.one_hot(sel, n_slc, dtype=F32), axis=-2) > 0.5
    maskbias = jnp.where(sel_mask, 0.0, NEG).astype(BF16)
    maskbias = jnp.pad(maskbias, ((0, 0), (0, 0), (0, 0), (0, LANES - n_slc)))
    onehot = (t[:, None] // L_SLC == jnp.arange(LANES)[None, :]).astype(BF16)
    k_aug = jnp.concatenate([ks.transpose(0, 2, 1, 3).astype(BF16),
                             jnp.broadcast_to(onehot, (bsz, G_KV, s, LANES))], axis=-1)
    q_bf = q4.reshape(bsz, s, HB * DH_B).astype(BF16)
    o_slc = _slc_attention(q_bf, maskbias, k_aug, vs.astype(BF16), bsz, s)
    o_win = _win_attention(q_bf, kw.reshape(bsz, s, G_KV * DH_B).astype(BF16), vw.astype(BF16), bsz, s)
    o_slc = o_slc.reshape(bsz, s, G_KV, R_GRP, DH_B).astype(F32)
    o_win = o_win.reshape(bsz, s, G_KV, R_GRP, DH_B).astype(F32)
    gates = jax.nn.sigmoid(gb.astype(F32)).reshape(bsz, s, G_KV, R_GRP, 3)
    o = gates[..., 0:1] * o_cmp + gates[..., 1:2] * o_slc + gates[..., 2:3] * o_win
    return o.reshape(bsz, s, HB * DH_B)


def _moe_ffn(h, w_router, b_router, w_gate, b_gate, w_up, b_up, w_down, b_down):
    n_tok, d = h.shape
    logits = (h @ w_router + b_router).astype(F32)
    top_val, top_idx = lax.top_k(logits, TOP_K)
    weights = jax.nn.softmax(top_val, axis=-1)
    n_asg = n_tok * TOP_K
    e_flat = top_idx.reshape(-1)
    w_flat = weights.reshape(-1)
    tok = jnp.arange(n_asg, dtype=jnp.int32) // TOP_K
    order = jnp.argsort(e_flat)
    e_sorted = e_flat[order]
    tok_sorted = tok[order]
    counts = jnp.zeros((N_EXP,), jnp.int32).at[e_flat].add(1)
    padded = ((counts + MOE_TM - 1) // MOE_TM) * MOE_TM
    pad_end = jnp.cumsum(padded)
    pad_start = pad_end - padded
    start = jnp.cumsum(counts) - counts
    dest = pad_start[e_sorted] + jnp.arange(n_asg, dtype=jnp.int32) - start[e_sorted]
    n_blocks = (n_asg + N_EXP * (MOE_TM - 1) + MOE_TM - 1) // MOE_TM
    src_row = jnp.zeros((n_blocks * MOE_TM,), jnp.int32).at[dest].set(tok_sorted)
    rows = h.astype(BF16)[src_row]
    tile0 = jnp.arange(n_blocks, dtype=jnp.int32) * MOE_TM
    block_e = jnp.minimum(jnp.searchsorted(pad_end, tile0, side='right'), N_EXP - 1).astype(jnp.int32)
    n_valid = jnp.clip(pad_start[block_e] + counts[block_e] - tile0, 0, MOE_TM).astype(jnp.int32)
    yb = _moe_experts(rows, block_e, n_valid, w_gate, b_gate, w_up, b_up, w_down, b_down)
    dest_tk = jnp.zeros((n_asg,), jnp.int32).at[order].set(dest)
    y_asg = yb[dest_tk].reshape(n_tok, TOP_K, d).astype(F32) * weights[:, :, None]
    return jnp.sum(y_asg, axis=1)


def _layer(x, positions, norm_mix, w_in, conv_w, a_log, dt_bias, norm_gdn, pe_ck, w1_ck, w2_ck,
           pe_cv, w1_cv, w2_cv, w_proj_a, w_proj_b, w_out, norm_ffn, w_router, b_router,
           w_gate, b_gate, w_up, b_up, w_down, b_down):
    bsz, s, d = x.shape
    n_tok = bsz * s
    x2 = x.reshape(n_tok, d)
    h = _rmsnorm(x2, norm_mix, BF16)
    n_small = 2 * HA + 3 * HB
    sp = SPLIT_POINTS
    w_main = jnp.concatenate([w_in[:, :sp[3]], w_in[:, sp[5]:sp[12]], w_in[:, sp[13]:]], axis=1).astype(BF16)
    w_small = jnp.concatenate([w_in[:, sp[3]:sp[5]], w_in[:, sp[12]:sp[13]]], axis=1)
    w_small = jnp.pad(w_small, ((0, 0), (0, LANES - n_small))).astype(BF16)
    main = _matmul(h, w_main, BF16).reshape(bsz, s, -1)
    small = _matmul(h, w_small, F32).reshape(bsz, s, -1)
    sizes = (HA * DK_A, HA * DK_A, HA * DV_A, HA * DV_A,
             HB * DH_B, G_KV * DH_B, G_KV * DH_B, G_KV * DH_B, G_KV * DH_B, G_KV * DH_B, G_KV * DH_B, 2 * D_MODEL)
    pts = tuple(sum(sizes[:i + 1]) for i in range(len(sizes) - 1))
    qa, ka, va, za, qb, kc, vc, ks, vs, kw, vw, gm = jnp.split(main, pts, axis=-1)
    aa, ba, gb = small[..., :HA], small[..., HA:2 * HA], small[..., 2 * HA:n_small]

    o_a = _gated_deltanet(main, aa, ba, conv_w, a_log, dt_bias, norm_gdn, bsz, s)
    o_b = _native_sparse_attention(qb, kc, vc, ks, vs, kw, vw, gb, positions,
                                   pe_ck, w1_ck, w2_ck, pe_cv, w1_cv, w2_cv)
    merged = _merge(o_a.reshape(n_tok, d), o_b.reshape(n_tok, d).astype(BF16), w_proj_a.astype(BF16),
                    w_proj_b.astype(BF16), main.reshape(n_tok, -1), pts[-1])
    x1, hf = _outproj(merged, w_out.astype(BF16), x2, norm_ffn)
    y_moe = _moe_ffn(hf, w_router, b_router, w_gate, b_gate, w_up, b_up, w_down, b_down)
    return x1, y_moe


def kernel(x, positions, norm_mix, w_in, conv_w, a_log, dt_bias, norm_gdn, pe_ck, w1_ck, w2_ck, pe_cv, w1_cv, w2_cv, w_proj_a, w_proj_b, w_out, norm_ffn, w_router, b_router, w_gate, b_gate, w_up, b_up, w_down, b_down, norm_final):
    depth = norm_mix.shape[0]
    bsz, s, d = x.shape
    for l in range(depth):
        x1, y_moe = _layer(x, positions, norm_mix[l], w_in[l], conv_w[l], a_log[l], dt_bias[l], norm_gdn[l],
                           pe_ck[l], w1_ck[l], w2_ck[l], pe_cv[l], w1_cv[l], w2_cv[l],
                           w_proj_a[l], w_proj_b[l], w_out[l], norm_ffn[l], w_router[l], b_router[l],
                           w_gate[l], b_gate[l], w_up[l], b_up[l], w_down[l], b_down[l])
        if l + 1 < depth:
            x = (x1 + y_moe).reshape(bsz, s, d)
    return _add_rmsnorm(x1, y_moe, norm_final, F32).reshape(bsz, s, d)
```

```python
import functools
import math

import jax
import jax.numpy as jnp
import numpy as np
from jax import lax
from jax.experimental import pallas as pl
from jax.experimental.pallas import tpu as pltpu

F32 = jnp.float32
BF16 = jnp.bfloat16

D_MODEL = 2048
EPS = 1e-6
NEG = -1e30
HA = D_MODEL // 128
DK_A = 128
DV_A = 128
CONV_W = 4
CHUNK = 64
HB = D_MODEL // 128
G_KV = 4
R_GRP = HB // G_KV
DH_B = 128
ROT_DIM = DH_B // 4
ROPE_THETA = 500000.0
L_CMP = 32
STRIDE_CMP = 16
L_SLC = 64
T_SEL = 8
WINDOW = 512
N_EXP = 32
TOP_K = 4
D_FF = D_MODEL
SWIGLU_LIMIT = 7.0
SWIGLU_ALPHA = 1.702
SPLIT_SIZES = (HA * DK_A, HA * DK_A, HA * DV_A, HA * DV_A, HA, HA,
               HB * DH_B, G_KV * DH_B, G_KV * DH_B, G_KV * DH_B, G_KV * DH_B, G_KV * DH_B, G_KV * DH_B,
               3 * HB, 2 * D_MODEL)
SPLIT_POINTS = tuple(sum(SPLIT_SIZES[:i + 1]) for i in range(len(SPLIT_SIZES) - 1))

V7X_VMEM_LIMIT_BYTES = 56 * 1024 * 1024
LANES = 128
MOE_TM = 1024
MOE_SUB = 256
MOE_TN = 512
ATT_TQ = 256


def _cparams(sem):
    return pltpu.CompilerParams(dimension_semantics=sem, vmem_limit_bytes=V7X_VMEM_LIMIT_BYTES)


def _mm_kernel(x_ref, w_ref, o_ref):
    o_ref[...] = jnp.dot(x_ref[...], w_ref[...], preferred_element_type=F32).astype(o_ref.dtype)


def _matmul(x, w, out_dtype, tm=1024, tn=1024):
    m, k = x.shape
    n = w.shape[1]
    tm, tn = min(tm, m), min(tn, n)
    assert m % tm == 0 and n % tn == 0
    return pl.pallas_call(
        _mm_kernel,
        out_shape=jax.ShapeDtypeStruct((m, n), out_dtype),
        grid=(n // tn, m // tm),
        in_specs=[pl.BlockSpec((tm, k), lambda j, i: (i, 0)),
                  pl.BlockSpec((k, tn), lambda j, i: (0, j))],
        out_specs=pl.BlockSpec((tm, tn), lambda j, i: (i, j)),
        compiler_params=_cparams(("parallel", "parallel")),
        name="dense_matmul",
    )(x, w)


def _rmsnorm_kernel(x_ref, g_ref, o_ref):
    x = x_ref[...]
    y = x * lax.rsqrt(jnp.mean(x * x, axis=-1, keepdims=True) + EPS)
    o_ref[...] = (y * g_ref[...]).astype(o_ref.dtype)


def _rmsnorm(x, gain, out_dtype, tm=512):
    m, d = x.shape
    return pl.pallas_call(
        _rmsnorm_kernel,
        out_shape=jax.ShapeDtypeStruct((m, d), out_dtype),
        grid=(m // tm,),
        in_specs=[pl.BlockSpec((tm, d), lambda i: (i, 0)),
                  pl.BlockSpec((1, d), lambda i: (0, 0))],
        out_specs=pl.BlockSpec((tm, d), lambda i: (i, 0)),
        compiler_params=_cparams(("parallel",)),
        name="rmsnorm",
    )(x, gain.reshape(1, d))


def _add_rmsnorm_kernel(x_ref, y_ref, g_ref, o_ref):
    x = x_ref[...] + y_ref[...]
    y = x * lax.rsqrt(jnp.mean(x * x, axis=-1, keepdims=True) + EPS)
    o_ref[...] = (y * g_ref[...]).astype(o_ref.dtype)


def _add_rmsnorm(x, y, gain, out_dtype, tm=512):
    m, d = x.shape
    row = pl.BlockSpec((tm, d), lambda i: (i, 0))
    return pl.pallas_call(
        _add_rmsnorm_kernel,
        out_shape=jax.ShapeDtypeStruct((m, d), out_dtype),
        grid=(m // tm,),
        in_specs=[row, row, pl.BlockSpec((1, d), lambda i: (0, 0))],
        out_specs=row,
        compiler_params=_cparams(("parallel",)),
        name="add_rmsnorm",
    )(x, y, gain.reshape(1, d))


def _merge_kernel(oa_ref, ob_ref, wa_ref, wb_ref, ga_ref, gb_ref, o_ref):
    ya = jnp.dot(oa_ref[...], wa_ref[...], preferred_element_type=F32)
    yb = jnp.dot(ob_ref[...], wb_ref[...], preferred_element_type=F32)
    o_ref[...] = (jax.nn.sigmoid(ga_ref[...].astype(F32)) * ya
                  + jax.nn.sigmoid(gb_ref[...].astype(F32)) * yb).astype(o_ref.dtype)


def _merge(o_a, o_b, w_a, w_b, main2d, gm_col0, tm=512, tn=1024):
    m, d = o_a.shape
    assert gm_col0 % tn == 0 and d % tn == 0
    g0 = gm_col0 // tn
    lhs = pl.BlockSpec((tm, d), lambda j, i: (i, 0))
    rhs = pl.BlockSpec((d, tn), lambda j, i: (0, j))
    return pl.pallas_call(
        _merge_kernel,
        out_shape=jax.ShapeDtypeStruct((m, d), BF16),
        grid=(d // tn, m // tm),
        in_specs=[lhs, lhs, rhs, rhs,
                  pl.BlockSpec((tm, tn), lambda j, i: (i, g0 + j)),
                  pl.BlockSpec((tm, tn), lambda j, i: (i, g0 + d // tn + j))],
        out_specs=pl.BlockSpec((tm, tn), lambda j, i: (i, j)),
        compiler_params=_cparams(("parallel", "parallel")),
        name="mixer_merge",
    )(o_a, o_b, w_a, w_b, main2d, main2d)


def _outproj_kernel(m_ref, w_ref, x_ref, g_ref, x1_ref, h_ref):
    x1 = x_ref[...] + jnp.dot(m_ref[...], w_ref[...], preferred_element_type=F32)
    x1_ref[...] = x1
    y = x1 * lax.rsqrt(jnp.mean(x1 * x1, axis=-1, keepdims=True) + EPS)
    h_ref[...] = (y * g_ref[...]).astype(h_ref.dtype)


def _outproj(merged, w_out, x, gain, tm=512):
    m, d = x.shape
    row = lambda: pl.BlockSpec((tm, d), lambda i: (i, 0))
    return pl.pallas_call(
        _outproj_kernel,
        out_shape=(jax.ShapeDtypeStruct((m, d), F32), jax.ShapeDtypeStruct((m, d), BF16)),
        grid=(m // tm,),
        in_specs=[row(), pl.BlockSpec((d, d), lambda i: (0, 0)), row(), pl.BlockSpec((1, d), lambda i: (0, 0))],
        out_specs=(row(), row()),
        compiler_params=_cparams(("parallel",)),
        name="out_proj_residual_norm",
    )(merged, w_out, x, gain.reshape(1, d))


GDN_COLS = 512
GDN_GROUP = 256
GDN_PH = 4
GDN_TS = 512


def _gdn_conv_kernel(x_ref, w_ref, o_ref):
    sec = pl.program_id(1) // (HA * DK_A // GDN_COLS)
    x = x_ref[0].astype(F32)
    w = w_ref[...]
    row = lax.broadcasted_iota(jnp.int32, x.shape, 0)
    y = x * w[CONV_W - 1:CONV_W]
    for i in range(CONV_W - 1):
        sh = CONV_W - 1 - i
        y = y + jnp.where(row >= sh, pltpu.roll(x, sh, axis=0), 0.0) * w[i:i + 1]
    y = y * jax.nn.sigmoid(y)
    qscale = jnp.where(sec == 0, DK_A ** -0.5, 1.0)
    for h in range(GDN_COLS // DK_A):
        yh = y[:, h * DK_A:(h + 1) * DK_A]
        nrm = yh * (lax.rsqrt(jnp.sum(yh * yh, axis=-1, keepdims=True) + EPS) * qscale)
        o_ref[0, 0, h] = jnp.where(sec < 2, nrm, yh).astype(o_ref.dtype)


def _gdn_conv(main, conv_w, bsz, s):
    ncol = 3 * HA * DK_A // GDN_COLS
    hpc = GDN_COLS // DK_A
    return pl.pallas_call(
        _gdn_conv_kernel,
        out_shape=jax.ShapeDtypeStruct((3, bsz, HA, s, DK_A), BF16),
        grid=(bsz, ncol),
        in_specs=[pl.BlockSpec((1, s, GDN_COLS), lambda b, c: (b, 0, c)),
                  pl.BlockSpec((CONV_W, GDN_COLS), lambda b, c: (0, c))],
        out_specs=pl.BlockSpec((1, 1, hpc, s, DK_A), lambda b, c: (c // (HA // hpc), b, c % (HA // hpc), 0, 0)),
        compiler_params=_cparams(("parallel", "parallel")),
        name="gdn_conv_silu_l2norm",
    )(main, conv_w)


def _col_rep(row, n):
    return jnp.broadcast_to(row, (LANES, n)).T


def _dot_hilo(x, m):
    hi = x.astype(BF16)
    lo = (x - hi.astype(F32)).astype(BF16)
    return jnp.dot(hi, m, preferred_element_type=F32) + jnp.dot(lo, m, preferred_element_type=F32)


def _gdn_prep_kernel(q_ref, k_ref, v_ref, g_ref, b_ref, u_ref, w_ref, qg_ref, kd_ref, a_ref, egl_ref):
    n = GDN_GROUP
    ri = lax.broadcasted_iota(jnp.int32, (n, n), 0)
    ci = lax.broadcasted_iota(jnp.int32, (n, n), 1)
    same = (ri // CHUNK) == (ci // CHUNK)
    incl = same & (ri >= ci)
    strict = same & (ri > ci)
    one_if = lambda m: jnp.where(m, 1.0, 0.0).astype(BF16)
    cum_m, tot_m = one_if(same & (ri <= ci)), one_if(same)
    eye = jnp.where(ri == ci, 1.0, 0.0)
    wide = lambda c: jnp.concatenate([c] * (n // LANES), axis=1)
    nt = (((1,), (1,)), ((), ()))
    heads = range(GDN_PH)
    ts, ps, rhs = [], [], []
    for h in heads:
        q, k, v = q_ref[0, 0, h], k_ref[0, 0, h], v_ref[0, 0, h]
        g8 = jnp.broadcast_to(g_ref[0, h], (8, n))
        gc_row = _dot_hilo(g8, cum_m)[0:1]
        gl_row = _dot_hilo(g8, tot_m)[0:1]
        gc_c, gl_c, b_c = _col_rep(gc_row, n), _col_rep(gl_row, n), _col_rep(b_ref[0, h], n)
        decay = jnp.exp(jnp.where(incl, wide(gc_c) - gc_row, NEG))
        kk = lax.dot_general(k, k, nt, preferred_element_type=F32)
        qk = lax.dot_general(q, k, nt, preferred_element_type=F32)
        xb = jnp.where(strict, -(kk * wide(b_c) * decay), 0.0).astype(BF16)
        a = qk * decay
        kf = k.astype(F32)
        egc = jnp.exp(gc_c)
        qg_ref[0, h] = (q.astype(F32) * egc).astype(qg_ref.dtype)
        kd_ref[0, h] = (kf * jnp.exp(gl_c - gc_c)).astype(kd_ref.dtype)
        for c in range(n // CHUNK):
            blk = slice(c * CHUNK, (c + 1) * CHUNK)
            a_ref[0, h, blk, :] = a[blk, blk].astype(a_ref.dtype)
        egl = jnp.exp(gl_c)
        egl_ref[0, h, 0] = jnp.concatenate([egl[c * CHUNK:c * CHUNK + 1] for c in range(n // CHUNK)], axis=0)
        rhs.append(jnp.concatenate([(v.astype(F32) * b_c).astype(BF16), (kf * b_c * egc).astype(BF16)], axis=1))
        ts.append(eye + xb.astype(F32))
        ps.append(xb)
    ps = [jnp.dot(p, p, preferred_element_type=F32).astype(BF16) for p in ps]
    for step in range(5):
        for h in heads:
            if step < 4:
                tp = jnp.dot(jnp.concatenate([ts[h].astype(BF16), ps[h]], axis=0), ps[h], preferred_element_type=F32)
                ts[h] = ts[h] + tp[:n]
                ps[h] = tp[n:].astype(BF16)
            else:
                ts[h] = ts[h] + jnp.dot(ts[h].astype(BF16), ps[h], preferred_element_type=F32)
    for h in heads:
        uw = jnp.dot(ts[h].astype(BF16), rhs[h], preferred_element_type=F32)
        u_ref[0, h] = uw[:, :DV_A].astype(u_ref.dtype)
        w_ref[0, h] = uw[:, DV_A:].astype(w_ref.dtype)


def _gdn_prep(qkv, g_t, beta_t, bsz, s):
    n, ph = GDN_GROUP, GDN_PH
    tok = lambda width, dt: jax.ShapeDtypeStruct((bsz, HA, s, width), dt)
    tspec = lambda width: pl.BlockSpec((1, ph, n, width), lambda b, h, i: (b, h, i, 0))
    qspec = lambda sec: pl.BlockSpec((1, 1, ph, n, DK_A), lambda b, h, i: (sec, b, h, i, 0))
    rspec = pl.BlockSpec((1, ph, 1, n), lambda b, h, i: (b, h, 0, i))
    return pl.pallas_call(
        _gdn_prep_kernel,
        out_shape=(tok(DV_A, BF16), tok(DK_A, BF16), tok(DK_A, BF16), tok(DK_A, BF16), tok(CHUNK, BF16),
                   jax.ShapeDtypeStruct((bsz, HA, s // n, n // CHUNK, LANES), F32)),
        grid=(bsz, HA // ph, s // n),
        in_specs=[qspec(0), qspec(1), qspec(2), rspec, rspec],
        out_specs=(tspec(DV_A), tspec(DK_A), tspec(DK_A), tspec(DK_A), tspec(CHUNK),
                   pl.BlockSpec((1, ph, 1, n // CHUNK, LANES), lambda b, h, i: (b, h, i, 0, 0))),
        compiler_params=_cparams(("parallel", "parallel", "parallel")),
        name="gdn_chunk_prep",
    )(qkv, qkv, qkv, g_t, beta_t)


def _gdn_scan_kernel(u_ref, w_ref, qg_ref, kd_ref, a_ref, egl_ref, z_ref, ng_ref, o_ref, state_ref, *, nchunk):
    @pl.when(pl.program_id(1) == 0)
    def _():
        state_ref[...] = jnp.zeros_like(state_ref)

    tn = (((0,), (0,)), ((), ()))
    heads = range(HA)

    def body(c, carry):
        rows = pl.ds(pl.multiple_of(c * CHUNK, CHUNK), CHUNK)
        st = [state_ref[h] for h in heads]
        sb = [x.astype(BF16) for x in st]
        vb = [(u_ref[0, h, rows, :].astype(F32)
               - jnp.dot(w_ref[0, h, rows, :], sb[h], preferred_element_type=F32)).astype(BF16) for h in heads]
        o = [jnp.dot(qg_ref[0, h, rows, :], sb[h], preferred_element_type=F32)
             + jnp.dot(a_ref[0, h, rows, :], vb[h], preferred_element_type=F32) for h in heads]
        for h in heads:
            state_ref[h] = (st[h] * egl_ref[0, h, pl.ds(c, 1), :]
                            + lax.dot_general(kd_ref[0, h, rows, :], vb[h], tn, preferred_element_type=F32))
        for h in heads:
            cols = slice(h * DV_A, (h + 1) * DV_A)
            z = z_ref[0, rows, cols].astype(F32)
            on = o[h] * lax.rsqrt(jnp.mean(o[h] * o[h], axis=-1, keepdims=True) + EPS) * ng_ref[...]
            o_ref[0, rows, cols] = (on * (z * jax.nn.sigmoid(z))).astype(o_ref.dtype)
        return carry

    lax.fori_loop(0, nchunk, body, 0)


def _gdn_scan(u, w, qg, kd, a, egl, main, norm_gdn, bsz, s):
    ts = GDN_TS
    z_blk0 = 3 * HA * DK_A // (HA * DV_A)
    hspec = lambda width: pl.BlockSpec((1, HA, ts, width), lambda b, i: (b, 0, i, 0))
    return pl.pallas_call(
        functools.partial(_gdn_scan_kernel, nchunk=ts // CHUNK),
        out_shape=jax.ShapeDtypeStruct((bsz, s, HA * DV_A), BF16),
        grid=(bsz, s // ts),
        in_specs=[hspec(DV_A), hspec(DK_A), hspec(DK_A), hspec(DK_A), hspec(CHUNK),
                  pl.BlockSpec((1, HA, ts // CHUNK, LANES), lambda b, i: (b, 0, i, 0)),
                  pl.BlockSpec((1, ts, HA * DV_A), lambda b, i: (b, i, z_blk0)),
                  pl.BlockSpec((1, DV_A), lambda b, i: (0, 0))],
        out_specs=pl.BlockSpec((1, ts, HA * DV_A), lambda b, i: (b, i, 0)),
        scratch_shapes=[pltpu.VMEM((HA, DK_A, DV_A), F32)],
        compiler_params=_cparams(("parallel", "arbitrary")),
        name="gdn_delta_scan",
    )(u, w, qg, kd, a, egl, main, norm_gdn.reshape(1, DV_A))


def _gated_deltanet(main, a_in, b_in, conv_w, a_log, dt_bias, norm_gdn, bsz, s):
    qkv = _gdn_conv(main, conv_w, bsz, s)
    g = -jnp.exp(a_log.astype(F32)) * jax.nn.softplus(a_in.astype(F32) + dt_bias.astype(F32))
    beta = jax.nn.sigmoid(b_in.astype(F32))
    g_t = g.transpose(0, 2, 1).reshape(bsz, HA, 1, s)
    beta_t = beta.transpose(0, 2, 1).reshape(bsz, HA, 1, s)
    u, w, qg, kd, a, egl = _gdn_prep(qkv, g_t, beta_t, bsz, s)
    egl = egl.reshape(bsz, HA, s // CHUNK, LANES)
    return _gdn_scan(u, w, qg, kd, a, egl, main, norm_gdn, bsz, s)


NSA_TS = 512
BIG = 1e30
DROPPED = -3e38


def _rope(x, cos, sin, lane):
    half = ROT_DIM // 2
    partner = jnp.where(lane < half, pltpu.roll(x, DH_B - half, axis=1), pltpu.roll(x, half, axis=1))
    return x * cos + partner * sin


def _nsa_rope_kernel(q_ref, kc_ref, vc_ref, ks_ref, kw_ref, cos_ref, sin_ref,
                     qo_ref, kco_ref, vco_ref, kso_ref, kwo_ref):
    cos, sin = cos_ref[0], sin_ref[0]
    lane = lax.broadcasted_iota(jnp.int32, cos.shape, 1)
    head = lambda ref, h: ref[0, :, h * DH_B:(h + 1) * DH_B].astype(F32)
    for h in range(HB):
        qo_ref[0, :, h * DH_B:(h + 1) * DH_B] = (_rope(head(q_ref, h), cos, sin, lane) * DH_B ** -0.5).astype(qo_ref.dtype)
    for g in range(G_KV):
        cols = slice(g * DH_B, (g + 1) * DH_B)
        kco_ref[0, g] = _rope(head(kc_ref, g), cos, sin, lane).astype(kco_ref.dtype)
        vco_ref[0, g] = vc_ref[0, :, cols]
        kso_ref[0, :, cols] = _rope(head(ks_ref, g), cos, sin, lane).astype(kso_ref.dtype)
        kwo_ref[0, :, cols] = _rope(head(kw_ref, g), cos, sin, lane).astype(kwo_ref.dtype)


def _nsa_rope(main, cos, sin, col, bsz, s):
    ts = NSA_TS
    kvw = G_KV * DH_B
    tok = lambda width, c0: pl.BlockSpec((1, ts, width), lambda b, i: (b, i, c0 // width))
    tab = pl.BlockSpec((1, ts, DH_B), lambda b, i: (b, i, 0))
    grp = pl.BlockSpec((1, G_KV, ts, DH_B), lambda b, i: (b, 0, i, 0))
    flat = lambda width: pl.BlockSpec((1, ts, width), lambda b, i: (b, i, 0))
    return pl.pallas_call(
        _nsa_rope_kernel,
        out_shape=(jax.ShapeDtypeStruct((bsz, s, HB * DH_B), BF16),
                   jax.ShapeDtypeStruct((bsz, G_KV, s, DH_B), BF16), jax.ShapeDtypeStruct((bsz, G_KV, s, DH_B), BF16),
                   jax.ShapeDtypeStruct((bsz, s, kvw), BF16), jax.ShapeDtypeStruct((bsz, s, kvw), BF16)),
        grid=(bsz, s // ts),
        in_specs=[tok(HB * DH_B, col["qb"]), tok(kvw, col["kc"]), tok(kvw, col["vc"]), tok(kvw, col["ks"]),
                  tok(kvw, col["kw"]), tab, tab],
        out_specs=(flat(HB * DH_B), grp, grp, flat(kvw), flat(kvw)),
        compiler_params=_cparams(("parallel", "parallel")),
        name="nsa_rotary",
    )(main, main, main, main, main, cos, sin)


def _gelu_tanh(x):
    return 0.5 * x * (1.0 + jnp.tanh(math.sqrt(2.0 / math.pi) * (x + 0.044715 * x * x * x)))


def _nsa_compress_kernel(k_ref, v_ref, pek_ref, pev_ref, w1k_ref, w1v_ref, w2k_ref, w2v_ref, ko_ref, vo_ref):
    nseg = k_ref.shape[2]
    for x_ref, pe_ref, w1_ref, w2_ref, o_ref in ((k_ref, pek_ref, w1k_ref, w2k_ref, ko_ref),
                                                 (v_ref, pev_ref, w1v_ref, w2v_ref, vo_ref)):
        for g in range(G_KV):
            x = x_ref[0, g].astype(F32)
            lo = jnp.dot((x + pe_ref[0:1]).astype(BF16), w1_ref[0], preferred_element_type=F32)
            hi = jnp.dot((x + pe_ref[1:2]).astype(BF16), w1_ref[1], preferred_element_type=F32)
            pre = lo + pltpu.roll(hi, nseg - 1, axis=0)
            o_ref[0, g] = jnp.dot(_gelu_tanh(pre).astype(BF16), w2_ref[...],
                                  preferred_element_type=F32).astype(o_ref.dtype)


def _nsa_compress(kc_t, vc_t, pe_ck, w1_ck, w2_ck, pe_cv, w1_cv, w2_cv, bsz, s):
    assert L_CMP == 2 * STRIDE_CMP
    nseg = s // STRIDE_CMP
    width = STRIDE_CMP * DH_B
    seg = lambda t: t.reshape(bsz, G_KV, nseg, width)
    pe2 = lambda pe: pe.reshape(2, width)
    w1h = lambda w: w.reshape(2, width, DH_B).astype(BF16)
    xspec = pl.BlockSpec((1, G_KV, nseg, width), lambda b: (b, 0, 0, 0))
    pspec = pl.BlockSpec((2, width), lambda b: (0, 0))
    w1spec = pl.BlockSpec((2, width, DH_B), lambda b: (0, 0, 0))
    w2spec = pl.BlockSpec((DH_B, DH_B), lambda b: (0, 0))
    ospec = pl.BlockSpec((1, G_KV, nseg, DH_B), lambda b: (b, 0, 0, 0))
    oshape = jax.ShapeDtypeStruct((bsz, G_KV, nseg, DH_B), BF16)
    return pl.pallas_call(
        _nsa_compress_kernel,
        out_shape=(oshape, oshape),
        grid=(bsz,),
        in_specs=[xspec, xspec, pspec, pspec, w1spec, w1spec, w2spec, w2spec],
        out_specs=(ospec, ospec),
        compiler_params=_cparams(("parallel",)),
        name="nsa_compress",
    )(seg(kc_t), seg(vc_t), pe2(pe_ck), pe2(pe_cv), w1h(w1_ck), w1h(w1_cv), w2_ck.astype(BF16), w2_cv.astype(BF16))


def _nsa_cmp_kernel(q_ref, kc_ref, vc_ref, ov_ref, o_ref, mb_ref, *, tq, n_slc):
    i = pl.program_id(2)
    rows = R_GRP * tq
    ncmp = kc_ref.shape[2]
    qa = _stack_heads(q_ref[0], None)
    s = lax.dot_general(qa, kc_ref[0, 0], (((1,), (1,)), ((), ())), preferred_element_type=F32)
    t_row = (lax.broadcasted_iota(jnp.int32, (rows, ncmp), 0) & (tq - 1)) + i * tq
    c_end = lax.broadcasted_iota(jnp.int32, (rows, ncmp), 1) * STRIDE_CMP + (L_CMP - 1)
    valid = c_end <= t_row
    sm = jnp.where(valid, s, NEG)
    p = jnp.where(valid, jnp.exp(sm - jnp.max(sm, axis=1, keepdims=True)), 0.0)
    l = jnp.sum(p, axis=1, keepdims=True)
    p = p * (1.0 / jnp.where(l > 0.0, l, 1.0))
    _unstack_heads(o_ref, jnp.dot(p.astype(BF16), vc_ref[0, 0], preferred_element_type=F32), tq)
    psum = p[0:tq]
    for r in range(1, R_GRP):
        psum = psum + p[r * tq:(r + 1) * tq]
    imp = _dot_hilo(psum, ov_ref[...])
    lane = lax.broadcasted_iota(jnp.int32, (tq, LANES), 1)
    cur = (lax.broadcasted_iota(jnp.int32, (tq, LANES), 0) + i * tq) // L_SLC
    forced = (lane == 0) | (lane == cur)
    v = jnp.where(forced, BIG, jnp.where(lane <= cur, imp, -BIG))
    sel = jnp.zeros((tq, LANES), F32)
    for _ in range(T_SEL):
        m = jnp.max(v, axis=1, keepdims=True)
        first = jnp.min(jnp.where(v == m, lane, LANES), axis=1, keepdims=True)
        hit = lane == first
        sel = jnp.where(hit, 1.0, sel)
        v = jnp.where(hit, DROPPED, v)
    visible = jnp.where(lane <= cur, sel, 0.0)
    mb_ref[0, 0] = jnp.where(lane < n_slc, (visible - 1.0) * BIG, 0.0).astype(mb_ref.dtype)


def _nsa_cmp_select(q, k_cmp, v_cmp, bsz, s):
    tq = ATT_TQ
    nseg = s // STRIDE_CMP
    n_slc = s // L_SLC
    assert nseg <= LANES or nseg % LANES == 0
    c_start = np.arange(nseg) * STRIDE_CMP
    j_start = np.arange(n_slc) * L_SLC
    overlap = ((c_start[:, None] < j_start[None, :] + L_SLC) & (c_start[:, None] + L_CMP > j_start[None, :]))
    overlap = jnp.asarray(np.pad(overlap.astype(np.float32), ((0, 0), (0, LANES - n_slc))), BF16)
    qspec = pl.BlockSpec((1, tq, R_GRP * DH_B), lambda b, g, i: (b, i, g))
    cspec = pl.BlockSpec((1, 1, nseg, DH_B), lambda b, g, i: (b, g, 0, 0))
    return pl.pallas_call(
        functools.partial(_nsa_cmp_kernel, tq=tq, n_slc=n_slc),
        out_shape=(jax.ShapeDtypeStruct((bsz, s, HB * DH_B), BF16), jax.ShapeDtypeStruct((bsz, G_KV, s, LANES), BF16)),
        grid=(bsz, G_KV, s // tq),
        in_specs=[qspec, cspec, cspec, pl.BlockSpec((nseg, LANES), lambda b, g, i: (0, 0))],
        out_specs=(qspec, pl.BlockSpec((1, 1, tq, LANES), lambda b, g, i: (b, g, i, 0))),
        compiler_params=_cparams(("parallel", "parallel", "parallel")),
        name="nsa_compressed_select",
    )(q, k_cmp, v_cmp, overlap)


def _stack_heads(q, extra):
    parts = []
    for r in range(R_GRP):
        qr = q[:, r * DH_B:(r + 1) * DH_B]
        parts.append(qr if extra is None else jnp.concatenate([qr, extra], axis=1))
    return jnp.concatenate(parts, axis=0)


def _unstack_heads(o_ref, o, tq):
    for r in range(R_GRP):
        o_ref[0, :, r * DH_B:(r + 1) * DH_B] = o[r * tq:(r + 1) * tq].astype(o_ref.dtype)


def _selected_branch(i, q, mb, ks_ref, oh_ref, vs_ref, tq):
    rows = R_GRP * tq
    qa = _stack_heads(q, mb)

    def scores(j):
        keys = pl.ds(pl.multiple_of(j * tq, tq), tq)
        k = jnp.concatenate([ks_ref[0, keys, :], oh_ref[keys, :]], axis=1)
        return lax.dot_general(qa, k, (((1,), (1,)), ((), ())), preferred_element_type=F32)

    def update(j, s, carry):
        m, l, acc = carry
        v = vs_ref[0, pl.ds(pl.multiple_of(j * tq, tq), tq), :]
        m_new = jnp.maximum(m, jnp.max(s, axis=1, keepdims=True))
        alpha = jnp.exp(m - m_new)
        p = jnp.exp(s - m_new)
        l = alpha * l + jnp.sum(p, axis=1, keepdims=True)
        acc = alpha * acc + jnp.dot(p.astype(BF16), v, preferred_element_type=F32)
        return m_new, l, acc

    init = (jnp.full((rows, 1), -jnp.inf, F32), jnp.zeros((rows, 1), F32), jnp.zeros((rows, DH_B), F32))
    carry = lax.fori_loop(0, i, lambda j, c: update(j, scores(j), c), init)
    s = scores(i)
    t_loc = lax.broadcasted_iota(jnp.int32, (rows, tq), 0) & (tq - 1)
    k_loc = lax.broadcasted_iota(jnp.int32, (rows, tq), 1)
    m, l, acc = update(i, jnp.where(k_loc <= t_loc, s, NEG), carry)
    return acc / l


def _window_branch(i, q, kw_refs, vw_refs, tq):
    rows = R_GRP * tq
    nk = 3 * tq
    qa = _stack_heads(q, None)
    k = jnp.concatenate([r[0] for r in kw_refs], axis=0)
    v = jnp.concatenate([r[0] for r in vw_refs], axis=0)
    s = lax.dot_general(qa, k, (((1,), (1,)), ((), ())), preferred_element_type=F32)
    t_loc = lax.broadcasted_iota(jnp.int32, (rows, nk), 0) & (tq - 1)
    k_loc = lax.broadcasted_iota(jnp.int32, (rows, nk), 1)
    dist = t_loc + 2 * tq - k_loc
    t_abs = (lax.broadcasted_iota(jnp.int32, (rows, 1), 0) & (tq - 1)) + i * tq
    bound = jnp.minimum(t_abs + 1, WINDOW)
    valid = dist.astype(jnp.uint32) < bound.astype(jnp.uint32)
    s = jnp.where(valid, s, NEG)
    p = jnp.exp(s - jnp.max(s, axis=1, keepdims=True))
    l = jnp.sum(p, axis=1, keepdims=True)
    return jnp.dot(p.astype(BF16), v, preferred_element_type=F32) / l


def _nsa_local_kernel(q_ref, mb_ref, ks_ref, oh_ref, vs_ref, kw0, kw1, kw2, vw0, vw1, vw2, oc_ref, gate_ref, o_ref,
                      *, tq, gate_lane0):
    g, i = pl.program_id(1), pl.program_id(2)
    q = q_ref[0]
    o_slc = _selected_branch(i, q, mb_ref[0, 0], ks_ref, oh_ref, vs_ref, tq)
    o_win = _window_branch(i, q, (kw0, kw1, kw2), (vw0, vw1, vw2), tq)
    gates = jax.nn.sigmoid(gate_ref[0])
    lane = lax.broadcasted_iota(jnp.int32, gates.shape, 1)
    pick = lambda idx: jnp.sum(jnp.where(lane == idx, gates, 0.0), axis=1, keepdims=True)
    for r in range(R_GRP):
        base = gate_lane0 + (g * R_GRP + r) * 3
        rows = slice(r * tq, (r + 1) * tq)
        cols = slice(r * DH_B, (r + 1) * DH_B)
        o = (pick(base) * oc_ref[0, :, cols].astype(F32) + pick(base + 1) * o_slc[rows] + pick(base + 2) * o_win[rows])
        o_ref[0, :, cols] = o.astype(o_ref.dtype)


def _nsa_local(q, maskbias, ks, main, kw, o_cmp, small, col, gate_lane0, bsz, s):
    tq = ATT_TQ
    assert 2 * tq >= WINDOW
    onehot = jnp.asarray(np.arange(s)[:, None] // L_SLC == np.arange(LANES)[None, :], BF16)
    qspec = pl.BlockSpec((1, tq, R_GRP * DH_B), lambda b, g, i: (b, i, g))
    seq = lambda c0: pl.BlockSpec((1, s, DH_B), lambda b, g, i: (b, 0, c0 // DH_B + g))
    back = lambda c0, n: pl.BlockSpec((1, tq, DH_B), lambda b, g, i: (b, jnp.maximum(i - n, 0), c0 // DH_B + g))
    return pl.pallas_call(
        functools.partial(_nsa_local_kernel, tq=tq, gate_lane0=gate_lane0),
        out_shape=jax.ShapeDtypeStruct((bsz, s, HB * DH_B), BF16),
        grid=(bsz, G_KV, s // tq),
        in_specs=[qspec,
                  pl.BlockSpec((1, 1, tq, LANES), lambda b, g, i: (b, g, i, 0)),
                  seq(0), pl.BlockSpec((s, LANES), lambda b, g, i: (0, 0)), seq(col["vs"]),
                  back(0, 2), back(0, 1), back(0, 0),
                  back(col["vw"], 2), back(col["vw"], 1), back(col["vw"], 0),
                  qspec, pl.BlockSpec((1, tq, LANES), lambda b, g, i: (b, i, 0))],
        out_specs=qspec,
        compiler_params=_cparams(("parallel", "parallel", "arbitrary")),
        name="nsa_selected_window_combine",
    )(q, maskbias, ks, onehot, main, kw, kw, kw, main, main, main, o_cmp, small)


def _moe_sub_blocks(nv_ref, out_ref, compute):
    nsub = (nv_ref[pl.program_id(0)] + MOE_SUB - 1) // MOE_SUB
    rows_of = lambda sb: pl.ds(pl.multiple_of(sb * MOE_SUB, MOE_SUB), MOE_SUB)

    def live(sb, carry):
        out_ref[rows_of(sb), :] = compute(rows_of(sb)).astype(out_ref.dtype)
        return carry

    def dead(sb, carry):
        out_ref[rows_of(sb), :] = jnp.zeros((MOE_SUB, out_ref.shape[1]), out_ref.dtype)
        return carry

    lax.fori_loop(0, nsub, live, 0)
    lax.fori_loop(nsub, MOE_TM // MOE_SUB, dead, 0)


def _moe_up_kernel(be_ref, nv_ref, x_ref, wg_ref, wu_ref, bg_ref, bu_ref, h_ref):
    wgb = wg_ref[0].astype(BF16)
    wub = wu_ref[0].astype(BF16)

    def compute(rows):
        x = x_ref[rows, :]
        gate = jnp.dot(x, wgb, preferred_element_type=F32) + bg_ref[0]
        up = jnp.dot(x, wub, preferred_element_type=F32) + bu_ref[0]
        gate = jnp.minimum(gate, SWIGLU_LIMIT)
        up = jnp.clip(up, -SWIGLU_LIMIT, SWIGLU_LIMIT)
        return (up + 1.0) * gate * jax.nn.sigmoid(SWIGLU_ALPHA * gate)

    _moe_sub_blocks(nv_ref, h_ref, compute)


def _moe_down_kernel(be_ref, nv_ref, h_ref, wd_ref, bd_ref, y_ref):
    wdb = wd_ref[0].astype(BF16)
    _moe_sub_blocks(nv_ref, y_ref,
                    lambda rows: jnp.dot(h_ref[rows, :], wdb, preferred_element_type=F32) + bd_ref[0])


def _moe_experts(rows, block_e, n_valid, w_gate, b_gate, w_up, b_up, w_down, b_down):
    n_rows, d = rows.shape
    n_blocks = n_rows // MOE_TM
    tn = MOE_TN
    last = D_FF // tn - 1

    def col(i, n, nv):
        return jnp.where(nv[i] > 0, n, last)

    tile = lambda width: pl.BlockSpec((MOE_TM, width), lambda i, n, be, nv: (i, 0))
    wcol = lambda k: pl.BlockSpec((1, k, tn), lambda i, n, be, nv: (be[i], 0, col(i, n, nv)))
    bcol = pl.BlockSpec((1, 1, tn), lambda i, n, be, nv: (be[i], 0, col(i, n, nv)))
    ocol = pl.BlockSpec((MOE_TM, tn), lambda i, n, be, nv: (i, n))
    params = _cparams(("arbitrary", "arbitrary"))
    h = pl.pallas_call(
        _moe_up_kernel,
        out_shape=jax.ShapeDtypeStruct((n_rows, D_FF), BF16),
        grid_spec=pltpu.PrefetchScalarGridSpec(
            num_scalar_prefetch=2, grid=(n_blocks, D_FF // tn),
            in_specs=[tile(d), wcol(d), wcol(d), bcol, bcol], out_specs=ocol),
        compiler_params=params, name="moe_expert_up",
    )(block_e, n_valid, rows, w_gate, w_up, b_gate.reshape(N_EXP, 1, D_FF), b_up.reshape(N_EXP, 1, D_FF))
    return pl.pallas_call(
        _moe_down_kernel,
        out_shape=jax.ShapeDtypeStruct((n_rows, d), BF16),
        grid_spec=pltpu.PrefetchScalarGridSpec(
            num_scalar_prefetch=2, grid=(n_blocks, d // tn),
            in_specs=[tile(D_FF), wcol(D_FF), bcol], out_specs=ocol),
        compiler_params=params, name="moe_expert_down",
    )(block_e, n_valid, h, w_down, b_down.reshape(N_EXP, 1, d))


def _native_sparse_attention(main, small, positions, pe_ck, w1_ck, w2_ck, pe_cv, w1_cv, w2_cv, col, gate_lane0, bsz, s):
    half = ROT_DIM // 2
    inv_freq = ROPE_THETA ** (-jnp.arange(half, dtype=F32) * 2.0 / ROT_DIM)
    ang = positions.astype(F32)[..., None] * inv_freq
    cos, sin = jnp.cos(ang), jnp.sin(ang)
    rest = (bsz, s, DH_B - ROT_DIM)
    cos_t = jnp.concatenate([cos, cos, jnp.ones(rest, F32)], axis=-1)
    sin_t = jnp.concatenate([-sin, sin, jnp.zeros(rest, F32)], axis=-1)
    q, kc_t, vc_t, ks, kw = _nsa_rope(main, cos_t, sin_t, col, bsz, s)
    k_cmp, v_cmp = _nsa_compress(kc_t, vc_t, pe_ck, w1_ck, w2_ck, pe_cv, w1_cv, w2_cv, bsz, s)
    o_cmp, maskbias = _nsa_cmp_select(q, k_cmp, v_cmp, bsz, s)
    return _nsa_local(q, maskbias, ks, main, kw, o_cmp, small, col, gate_lane0, bsz, s)


def _moe_ffn(h, w_router, b_router, w_gate, b_gate, w_up, b_up, w_down, b_down):
    n_tok, d = h.shape
    logits = (h @ w_router + b_router).astype(F32)
    top_val, top_idx = lax.top_k(logits, TOP_K)
    weights = jax.nn.softmax(top_val, axis=-1)
    n_asg = n_tok * TOP_K
    e_flat = top_idx.reshape(-1)
    tok = jnp.arange(n_asg, dtype=jnp.int32) // TOP_K
    order = jnp.argsort(e_flat)
    e_sorted = e_flat[order]
    tok_sorted = tok[order]
    counts = jnp.zeros((N_EXP,), jnp.int32).at[e_flat].add(1)
    padded = ((counts + MOE_TM - 1) // MOE_TM) * MOE_TM
    pad_end = jnp.cumsum(padded)
    pad_start = pad_end - padded
    start = jnp.cumsum(counts) - counts
    dest = pad_start[e_sorted] + jnp.arange(n_asg, dtype=jnp.int32) - start[e_sorted]
    n_blocks = (n_asg + N_EXP * (MOE_TM - 1) + MOE_TM - 1) // MOE_TM
    src_row = (jnp.arange(n_blocks * MOE_TM, dtype=jnp.int32) % n_tok).at[dest].set(tok_sorted)
    rows = h.astype(BF16)[src_row]
    tile0 = jnp.arange(n_blocks, dtype=jnp.int32) * MOE_TM
    block_e = jnp.minimum(jnp.searchsorted(pad_end, tile0, side='right'), N_EXP - 1).astype(jnp.int32)
    n_valid = jnp.clip(pad_start[block_e] + counts[block_e] - tile0, 0, MOE_TM).astype(jnp.int32)
    yb = _moe_experts(rows, block_e, n_valid, w_gate, b_gate, w_up, b_up, w_down, b_down)
    dest_tk = jnp.zeros((n_asg,), jnp.int32).at[order].set(dest)
    y_asg = yb[dest_tk].reshape(n_tok, TOP_K, d).astype(F32) * weights[:, :, None]
    return jnp.sum(y_asg, axis=1)


def _layer(x, positions, norm_mix, w_in, conv_w, a_log, dt_bias, norm_gdn, pe_ck, w1_ck, w2_ck,
           pe_cv, w1_cv, w2_cv, w_proj_a, w_proj_b, w_out, norm_ffn, w_router, b_router,
           w_gate, b_gate, w_up, b_up, w_down, b_down):
    bsz, s, d = x.shape
    n_tok = bsz * s
    x2 = x.reshape(n_tok, d)
    h = _rmsnorm(x2, norm_mix, BF16)
    n_small = 2 * HA + 3 * HB
    sp = SPLIT_POINTS
    w_main = jnp.concatenate([w_in[:, :sp[3]], w_in[:, sp[5]:sp[12]], w_in[:, sp[13]:]], axis=1).astype(BF16)
    w_small = jnp.concatenate([w_in[:, sp[3]:sp[5]], w_in[:, sp[12]:sp[13]]], axis=1)
    w_small = jnp.pad(w_small, ((0, 0), (0, LANES - n_small))).astype(BF16)
    main = _matmul(h, w_main, BF16).reshape(bsz, s, -1)
    small = _matmul(h, w_small, F32).reshape(bsz, s, -1)
    names = ("qa", "ka", "va", "za", "qb", "kc", "vc", "ks", "vs", "kw", "vw", "gm")
    sizes = (HA * DK_A, HA * DK_A, HA * DV_A, HA * DV_A,
             HB * DH_B, G_KV * DH_B, G_KV * DH_B, G_KV * DH_B, G_KV * DH_B, G_KV * DH_B, G_KV * DH_B, 2 * D_MODEL)
    col = {nm: sum(sizes[:i]) for i, nm in enumerate(names)}
    aa, ba = small[..., :HA], small[..., HA:2 * HA]

    o_a = _gated_deltanet(main, aa, ba, conv_w, a_log, dt_bias, norm_gdn, bsz, s)
    o_b = _native_sparse_attention(main, small, positions, pe_ck, w1_ck, w2_ck, pe_cv, w1_cv, w2_cv,
                                   col, 2 * HA, bsz, s)
    merged = _merge(o_a.reshape(n_tok, d), o_b.reshape(n_tok, d), w_proj_a.astype(BF16),
                    w_proj_b.astype(BF16), main.reshape(n_tok, -1), col["gm"])
    x1, hf = _outproj(merged, w_out.astype(BF16), x2, norm_ffn)
    y_moe = _moe_ffn(hf, w_router, b_router, w_gate, b_gate, w_up, b_up, w_down, b_down)
    return x1, y_moe


def kernel(x, positions, norm_mix, w_in, conv_w, a_log, dt_bias, norm_gdn, pe_ck, w1_ck, w2_ck, pe_cv, w1_cv, w2_cv, w_proj_a, w_proj_b, w_out, norm_ffn, w_router, b_router, w_gate, b_gate, w_up, b_up, w_down, b_down, norm_final):
    depth = norm_mix.shape[0]
    bsz, s, d = x.shape
    for l in range(depth):
        x1, y_moe = _layer(x, positions, norm_mix[l], w_in[l], conv_w[l], a_log[l], dt_bias[l], norm_gdn[l],
                           pe_ck[l], w1_ck[l], w2_ck[l], pe_cv[l], w1_cv[l], w2_cv[l],
                           w_proj_a[l], w_proj_b[l], w_out[l], norm_ffn[l], w_router[l], b_router[l],
                           w_gate[l], b_gate[l], w_up[l], b_up[l], w_down[l], b_down[l])
        if l + 1 < depth:
            x = (x1 + y_moe).reshape(bsz, s, d)
    return _add_rmsnorm(x1, y_moe, norm_final, F32).reshape(bsz, s, d)
```

```python
import functools
import math

import jax
import jax.numpy as jnp
import numpy as np
from jax import lax
from jax.experimental import pallas as pl
from jax.experimental.pallas import tpu as pltpu

F32 = jnp.float32
BF16 = jnp.bfloat16

D_MODEL = 2048
EPS = 1e-6
NEG = -1e30
HA = D_MODEL // 128
DK_A = 128
DV_A = 128
CONV_W = 4
CHUNK = 64
HB = D_MODEL // 128
G_KV = 4
R_GRP = HB // G_KV
DH_B = 128
ROT_DIM = DH_B // 4
ROPE_THETA = 500000.0
L_CMP = 32
STRIDE_CMP = 16
L_SLC = 64
T_SEL = 8
WINDOW = 512
N_EXP = 32
TOP_K = 4
D_FF = D_MODEL
SWIGLU_LIMIT = 7.0
SWIGLU_ALPHA = 1.702
SPLIT_SIZES = (HA * DK_A, HA * DK_A, HA * DV_A, HA * DV_A, HA, HA,
               HB * DH_B, G_KV * DH_B, G_KV * DH_B, G_KV * DH_B, G_KV * DH_B, G_KV * DH_B, G_KV * DH_B,
               3 * HB, 2 * D_MODEL)
SPLIT_POINTS = tuple(sum(SPLIT_SIZES[:i + 1]) for i in range(len(SPLIT_SIZES) - 1))

V7X_VMEM_LIMIT_BYTES = 56 * 1024 * 1024
LANES = 128
MOE_TM = 1024
MOE_SUB = 256
MOE_TN = 512
ATT_TQ = 256


def _cparams(sem):
    return pltpu.CompilerParams(dimension_semantics=sem, vmem_limit_bytes=V7X_VMEM_LIMIT_BYTES)


def _mm_kernel(x_ref, w_ref, o_ref):
    o_ref[...] = jnp.dot(x_ref[...], w_ref[...], preferred_element_type=F32).astype(o_ref.dtype)


def _matmul(x, w, out_dtype, tm=1024, tn=1024):
    m, k = x.shape
    n = w.shape[1]
    tm, tn = min(tm, m), min(tn, n)
    assert m % tm == 0 and n % tn == 0
    return pl.pallas_call(
        _mm_kernel,
        out_shape=jax.ShapeDtypeStruct((m, n), out_dtype),
        grid=(n // tn, m // tm),
        in_specs=[pl.BlockSpec((tm, k), lambda j, i: (i, 0)),
                  pl.BlockSpec((k, tn), lambda j, i: (0, j))],
        out_specs=pl.BlockSpec((tm, tn), lambda j, i: (i, j)),
        compiler_params=_cparams(("parallel", "parallel")),
        name="dense_matmul",
    )(x, w)


def _rmsnorm_kernel(x_ref, g_ref, o_ref):
    x = x_ref[...]
    y = x * lax.rsqrt(jnp.mean(x * x, axis=-1, keepdims=True) + EPS)
    o_ref[...] = (y * g_ref[...]).astype(o_ref.dtype)


def _rmsnorm(x, gain, out_dtype, tm=512):
    m, d = x.shape
    return pl.pallas_call(
        _rmsnorm_kernel,
        out_shape=jax.ShapeDtypeStruct((m, d), out_dtype),
        grid=(m // tm,),
        in_specs=[pl.BlockSpec((tm, d), lambda i: (i, 0)),
                  pl.BlockSpec((1, d), lambda i: (0, 0))],
        out_specs=pl.BlockSpec((tm, d), lambda i: (i, 0)),
        compiler_params=_cparams(("parallel",)),
        name="rmsnorm",
    )(x, gain.reshape(1, d))


def _add_rmsnorm_kernel(x_ref, y_ref, g_ref, o_ref):
    x = x_ref[...] + y_ref[...]
    y = x * lax.rsqrt(jnp.mean(x * x, axis=-1, keepdims=True) + EPS)
    o_ref[...] = (y * g_ref[...]).astype(o_ref.dtype)


def _add_rmsnorm(x, y, gain, out_dtype, tm=512):
    m, d = x.shape
    row = pl.BlockSpec((tm, d), lambda i: (i, 0))
    return pl.pallas_call(
        _add_rmsnorm_kernel,
        out_shape=jax.ShapeDtypeStruct((m, d), out_dtype),
        grid=(m // tm,),
        in_specs=[row, row, pl.BlockSpec((1, d), lambda i: (0, 0))],
        out_specs=row,
        compiler_params=_cparams(("parallel",)),
        name="add_rmsnorm",
    )(x, y, gain.reshape(1, d))


def _merge_kernel(oa_ref, ob_ref, wa_ref, wb_ref, ga_ref, gb_ref, o_ref):
    ya = jnp.dot(oa_ref[...], wa_ref[...], preferred_element_type=F32)
    yb = jnp.dot(ob_ref[...], wb_ref[...], preferred_element_type=F32)
    o_ref[...] = (jax.nn.sigmoid(ga_ref[...].astype(F32)) * ya
                  + jax.nn.sigmoid(gb_ref[...].astype(F32)) * yb).astype(o_ref.dtype)


def _merge(o_a, o_b, w_a, w_b, main2d, gm_col0, tm=512, tn=1024):
    m, d = o_a.shape
    assert gm_col0 % tn == 0 and d % tn == 0
    g0 = gm_col0 // tn
    lhs = pl.BlockSpec((tm, d), lambda j, i: (i, 0))
    rhs = pl.BlockSpec((d, tn), lambda j, i: (0, j))
    return pl.pallas_call(
        _merge_kernel,
        out_shape=jax.ShapeDtypeStruct((m, d), BF16),
        grid=(d // tn, m // tm),
        in_specs=[lhs, lhs, rhs, rhs,
                  pl.BlockSpec((tm, tn), lambda j, i: (i, g0 + j)),
                  pl.BlockSpec((tm, tn), lambda j, i: (i, g0 + d // tn + j))],
        out_specs=pl.BlockSpec((tm, tn), lambda j, i: (i, j)),
        compiler_params=_cparams(("parallel", "parallel")),
        name="mixer_merge",
    )(o_a, o_b, w_a, w_b, main2d, main2d)


def _outproj_kernel(m_ref, w_ref, x_ref, g_ref, x1_ref, h_ref, hpk_ref):
    x1 = x_ref[...] + jnp.dot(m_ref[...], w_ref[...], preferred_element_type=F32)
    x1_ref[...] = x1
    y = x1 * lax.rsqrt(jnp.mean(x1 * x1, axis=-1, keepdims=True) + EPS) * g_ref[...]
    h_ref[...] = y.astype(h_ref.dtype)
    half = y.shape[1] // 2
    hpk_ref[...] = _pack_bf16_pairs(y[:, :half], y[:, half:])


def _outproj(merged, w_out, x, gain, tm=512):
    m, d = x.shape
    row = lambda width=d: pl.BlockSpec((tm, width), lambda i: (i, 0))
    return pl.pallas_call(
        _outproj_kernel,
        out_shape=(jax.ShapeDtypeStruct((m, d), F32), jax.ShapeDtypeStruct((m, d), BF16),
                   jax.ShapeDtypeStruct((m, d // 2), jnp.uint32)),
        grid=(m // tm,),
        in_specs=[row(), pl.BlockSpec((d, d), lambda i: (0, 0)), row(), pl.BlockSpec((1, d), lambda i: (0, 0))],
        out_specs=(row(), row(), row(d // 2)),
        compiler_params=_cparams(("parallel",)),
        name="out_proj_residual_norm",
    )(merged, w_out, x, gain.reshape(1, d))


GDN_COLS = 512
GDN_GROUP = 256
GDN_PH = 4
GDN_TS = 512


def _gdn_conv_kernel(x_ref, w_ref, o_ref):
    sec = pl.program_id(1) // (HA * DK_A // GDN_COLS)
    x = x_ref[0].astype(F32)
    w = w_ref[...]
    row = lax.broadcasted_iota(jnp.int32, x.shape, 0)
    y = x * w[CONV_W - 1:CONV_W]
    for i in range(CONV_W - 1):
        sh = CONV_W - 1 - i
        y = y + jnp.where(row >= sh, pltpu.roll(x, sh, axis=0), 0.0) * w[i:i + 1]
    y = y * jax.nn.sigmoid(y)
    qscale = jnp.where(sec == 0, DK_A ** -0.5, 1.0)
    for h in range(GDN_COLS // DK_A):
        yh = y[:, h * DK_A:(h + 1) * DK_A]
        nrm = yh * (lax.rsqrt(jnp.sum(yh * yh, axis=-1, keepdims=True) + EPS) * qscale)
        o_ref[0, 0, h] = jnp.where(sec < 2, nrm, yh).astype(o_ref.dtype)


def _gdn_conv(main, conv_w, bsz, s):
    ncol = 3 * HA * DK_A // GDN_COLS
    hpc = GDN_COLS // DK_A
    return pl.pallas_call(
        _gdn_conv_kernel,
        out_shape=jax.ShapeDtypeStruct((3, bsz, HA, s, DK_A), BF16),
        grid=(bsz, ncol),
        in_specs=[pl.BlockSpec((1, s, GDN_COLS), lambda b, c: (b, 0, c)),
                  pl.BlockSpec((CONV_W, GDN_COLS), lambda b, c: (0, c))],
        out_specs=pl.BlockSpec((1, 1, hpc, s, DK_A), lambda b, c: (c // (HA // hpc), b, c % (HA // hpc), 0, 0)),
        compiler_params=_cparams(("parallel", "parallel")),
        name="gdn_conv_silu_l2norm",
    )(main, conv_w)


def _col_rep(row, n):
    return jnp.broadcast_to(row, (LANES, n)).T


def _dot_hilo(x, m):
    hi = x.astype(BF16)
    lo = (x - hi.astype(F32)).astype(BF16)
    return jnp.dot(hi, m, preferred_element_type=F32) + jnp.dot(lo, m, preferred_element_type=F32)


def _gdn_prep_kernel(q_ref, k_ref, v_ref, g_ref, b_ref, u_ref, w_ref, qg_ref, kd_ref, a_ref, egl_ref):
    n = GDN_GROUP
    ri = lax.broadcasted_iota(jnp.int32, (n, n), 0)
    ci = lax.broadcasted_iota(jnp.int32, (n, n), 1)
    same = (ri // CHUNK) == (ci // CHUNK)
    incl = same & (ri >= ci)
    strict = same & (ri > ci)
    one_if = lambda m: jnp.where(m, 1.0, 0.0).astype(BF16)
    cum_m, tot_m = one_if(same & (ri <= ci)), one_if(same)
    eye = jnp.where(ri == ci, 1.0, 0.0)
    wide = lambda c: jnp.concatenate([c] * (n // LANES), axis=1)
    nt = (((1,), (1,)), ((), ()))
    heads = range(GDN_PH)
    ts, ps, rhs = [], [], []
    for h in heads:
        q, k, v = q_ref[0, 0, h], k_ref[0, 0, h], v_ref[0, 0, h]
        g8 = jnp.broadcast_to(g_ref[0, h], (8, n))
        gc_row = _dot_hilo(g8, cum_m)[0:1]
        gl_row = _dot_hilo(g8, tot_m)[0:1]
        gc_c, gl_c, b_c = _col_rep(gc_row, n), _col_rep(gl_row, n), _col_rep(b_ref[0, h], n)
        decay = jnp.exp(jnp.where(incl, wide(gc_c) - gc_row, NEG))
        kk = lax.dot_general(k, k, nt, preferred_element_type=F32)
        qk = lax.dot_general(q, k, nt, preferred_element_type=F32)
        xb = jnp.where(strict, -(kk * wide(b_c) * decay), 0.0).astype(BF16)
        a = qk * decay
        kf = k.astype(F32)
        egc = jnp.exp(gc_c)
        qg_ref[0, h] = (q.astype(F32) * egc).astype(qg_ref.dtype)
        kd_ref[0, h] = (kf * jnp.exp(gl_c - gc_c)).astype(kd_ref.dtype)
        for c in range(n // CHUNK):
            blk = slice(c * CHUNK, (c + 1) * CHUNK)
            a_ref[0, h, blk, :] = a[blk, blk].astype(a_ref.dtype)
        egl = jnp.exp(gl_c)
        egl_ref[0, h, 0] = jnp.concatenate([egl[c * CHUNK:c * CHUNK + 1] for c in range(n // CHUNK)], axis=0)
        rhs.append(jnp.concatenate([(v.astype(F32) * b_c).astype(BF16), (kf * b_c * egc).astype(BF16)], axis=1))
        ts.append(eye + xb.astype(F32))
        ps.append(xb)
    ps = [jnp.dot(p, p, preferred_element_type=F32).astype(BF16) for p in ps]
    for step in range(5):
        for h in heads:
            if step < 4:
                tp = jnp.dot(jnp.concatenate([ts[h].astype(BF16), ps[h]], axis=0), ps[h], preferred_element_type=F32)
                ts[h] = ts[h] + tp[:n]
                ps[h] = tp[n:].astype(BF16)
            else:
                ts[h] = ts[h] + jnp.dot(ts[h].astype(BF16), ps[h], preferred_element_type=F32)
    for h in heads:
        uw = jnp.dot(ts[h].astype(BF16), rhs[h], preferred_element_type=F32)
        u_ref[0, h] = uw[:, :DV_A].astype(u_ref.dtype)
        w_ref[0, h] = uw[:, DV_A:].astype(w_ref.dtype)


def _gdn_prep(qkv, g_t, beta_t, bsz, s):
    n, ph = GDN_GROUP, GDN_PH
    tok = lambda width, dt: jax.ShapeDtypeStruct((bsz, HA, s, width), dt)
    tspec = lambda width: pl.BlockSpec((1, ph, n, width), lambda b, h, i: (b, h, i, 0))
    qspec = lambda sec: pl.BlockSpec((1, 1, ph, n, DK_A), lambda b, h, i: (sec, b, h, i, 0))
    rspec = pl.BlockSpec((1, ph, 1, n), lambda b, h, i: (b, h, 0, i))
    return pl.pallas_call(
        _gdn_prep_kernel,
        out_shape=(tok(DV_A, BF16), tok(DK_A, BF16), tok(DK_A, BF16), tok(DK_A, BF16), tok(CHUNK, BF16),
                   jax.ShapeDtypeStruct((bsz, HA, s // n, n // CHUNK, LANES), F32)),
        grid=(bsz, HA // ph, s // n),
        in_specs=[qspec(0), qspec(1), qspec(2), rspec, rspec],
        out_specs=(tspec(DV_A), tspec(DK_A), tspec(DK_A), tspec(DK_A), tspec(CHUNK),
                   pl.BlockSpec((1, ph, 1, n // CHUNK, LANES), lambda b, h, i: (b, h, i, 0, 0))),
        compiler_params=_cparams(("parallel", "parallel", "parallel")),
        name="gdn_chunk_prep",
    )(qkv, qkv, qkv, g_t, beta_t)


def _gdn_scan_kernel(u_ref, w_ref, qg_ref, kd_ref, a_ref, egl_ref, z_ref, ng_ref, o_ref, state_ref, *, nchunk):
    @pl.when(pl.program_id(1) == 0)
    def _():
        state_ref[...] = jnp.zeros_like(state_ref)

    tn = (((0,), (0,)), ((), ()))
    heads = range(HA)

    def body(c, carry):
        rows = pl.ds(pl.multiple_of(c * CHUNK, CHUNK), CHUNK)
        st = [state_ref[h] for h in heads]
        sb = [x.astype(BF16) for x in st]
        vb = [(u_ref[0, h, rows, :].astype(F32)
               - jnp.dot(w_ref[0, h, rows, :], sb[h], preferred_element_type=F32)).astype(BF16) for h in heads]
        o = [jnp.dot(qg_ref[0, h, rows, :], sb[h], preferred_element_type=F32)
             + jnp.dot(a_ref[0, h, rows, :], vb[h], preferred_element_type=F32) for h in heads]
        for h in heads:
            state_ref[h] = (st[h] * egl_ref[0, h, pl.ds(c, 1), :]
                            + lax.dot_general(kd_ref[0, h, rows, :], vb[h], tn, preferred_element_type=F32))
        for h in heads:
            cols = slice(h * DV_A, (h + 1) * DV_A)
            z = z_ref[0, rows, cols].astype(F32)
            on = o[h] * lax.rsqrt(jnp.mean(o[h] * o[h], axis=-1, keepdims=True) + EPS) * ng_ref[...]
            o_ref[0, rows, cols] = (on * (z * jax.nn.sigmoid(z))).astype(o_ref.dtype)
        return carry

    lax.fori_loop(0, nchunk, body, 0)


def _gdn_scan(u, w, qg, kd, a, egl, main, norm_gdn, bsz, s):
    ts = GDN_TS
    z_blk0 = 3 * HA * DK_A // (HA * DV_A)
    hspec = lambda width: pl.BlockSpec((1, HA, ts, width), lambda b, i: (b, 0, i, 0))
    return pl.pallas_call(
        functools.partial(_gdn_scan_kernel, nchunk=ts // CHUNK),
        out_shape=jax.ShapeDtypeStruct((bsz, s, HA * DV_A), BF16),
        grid=(bsz, s // ts),
        in_specs=[hspec(DV_A), hspec(DK_A), hspec(DK_A), hspec(DK_A), hspec(CHUNK),
                  pl.BlockSpec((1, HA, ts // CHUNK, LANES), lambda b, i: (b, 0, i, 0)),
                  pl.BlockSpec((1, ts, HA * DV_A), lambda b, i: (b, i, z_blk0)),
                  pl.BlockSpec((1, DV_A), lambda b, i: (0, 0))],
        out_specs=pl.BlockSpec((1, ts, HA * DV_A), lambda b, i: (b, i, 0)),
        scratch_shapes=[pltpu.VMEM((HA, DK_A, DV_A), F32)],
        compiler_params=_cparams(("parallel", "arbitrary")),
        name="gdn_delta_scan",
    )(u, w, qg, kd, a, egl, main, norm_gdn.reshape(1, DV_A))


def _gated_deltanet(main, a_in, b_in, conv_w, a_log, dt_bias, norm_gdn, bsz, s):
    qkv = _gdn_conv(main, conv_w, bsz, s)
    g = -jnp.exp(a_log.astype(F32)) * jax.nn.softplus(a_in.astype(F32) + dt_bias.astype(F32))
    beta = jax.nn.sigmoid(b_in.astype(F32))
    g_t = g.transpose(0, 2, 1).reshape(bsz, HA, 1, s)
    beta_t = beta.transpose(0, 2, 1).reshape(bsz, HA, 1, s)
    u, w, qg, kd, a, egl = _gdn_prep(qkv, g_t, beta_t, bsz, s)
    egl = egl.reshape(bsz, HA, s // CHUNK, LANES)
    return _gdn_scan(u, w, qg, kd, a, egl, main, norm_gdn, bsz, s)


NSA_TS = 512
BIG = 1e30
DROPPED = -3e38


def _rope(x, cos, sin, lane):
    half = ROT_DIM // 2
    partner = jnp.where(lane < half, pltpu.roll(x, DH_B - half, axis=1), pltpu.roll(x, half, axis=1))
    return x * cos + partner * sin


def _nsa_rope_kernel(q_ref, kc_ref, vc_ref, ks_ref, kw_ref, cos_ref, sin_ref,
                     qo_ref, kco_ref, vco_ref, kso_ref, kwo_ref):
    cos, sin = cos_ref[0], sin_ref[0]
    lane = lax.broadcasted_iota(jnp.int32, cos.shape, 1)
    head = lambda ref, h: ref[0, :, h * DH_B:(h + 1) * DH_B].astype(F32)
    for h in range(HB):
        qo_ref[0, :, h * DH_B:(h + 1) * DH_B] = (_rope(head(q_ref, h), cos, sin, lane) * DH_B ** -0.5).astype(qo_ref.dtype)
    for g in range(G_KV):
        cols = slice(g * DH_B, (g + 1) * DH_B)
        kco_ref[0, g] = _rope(head(kc_ref, g), cos, sin, lane).astype(kco_ref.dtype)
        vco_ref[0, g] = vc_ref[0, :, cols]
        kso_ref[0, :, cols] = _rope(head(ks_ref, g), cos, sin, lane).astype(kso_ref.dtype)
        kwo_ref[0, :, cols] = _rope(head(kw_ref, g), cos, sin, lane).astype(kwo_ref.dtype)


def _nsa_rope(main, cos, sin, col, bsz, s):
    ts = NSA_TS
    kvw = G_KV * DH_B
    tok = lambda width, c0: pl.BlockSpec((1, ts, width), lambda b, i: (b, i, c0 // width))
    tab = pl.BlockSpec((1, ts, DH_B), lambda b, i: (b, i, 0))
    grp = pl.BlockSpec((1, G_KV, ts, DH_B), lambda b, i: (b, 0, i, 0))
    flat = lambda width: pl.BlockSpec((1, ts, width), lambda b, i: (b, i, 0))
    return pl.pallas_call(
        _nsa_rope_kernel,
        out_shape=(jax.ShapeDtypeStruct((bsz, s, HB * DH_B), BF16),
                   jax.ShapeDtypeStruct((bsz, G_KV, s, DH_B), BF16), jax.ShapeDtypeStruct((bsz, G_KV, s, DH_B), BF16),
                   jax.ShapeDtypeStruct((bsz, s, kvw), BF16), jax.ShapeDtypeStruct((bsz, s, kvw), BF16)),
        grid=(bsz, s // ts),
        in_specs=[tok(HB * DH_B, col["qb"]), tok(kvw, col["kc"]), tok(kvw, col["vc"]), tok(kvw, col["ks"]),
                  tok(kvw, col["kw"]), tab, tab],
        out_specs=(flat(HB * DH_B), grp, grp, flat(kvw), flat(kvw)),
        compiler_params=_cparams(("parallel", "parallel")),
        name="nsa_rotary",
    )(main, main, main, main, main, cos, sin)


def _gelu_tanh(x):
    return 0.5 * x * (1.0 + jnp.tanh(math.sqrt(2.0 / math.pi) * (x + 0.044715 * x * x * x)))


def _nsa_compress_kernel(k_ref, v_ref, pek_ref, pev_ref, w1k_ref, w1v_ref, w2k_ref, w2v_ref, ko_ref, vo_ref):
    nseg = k_ref.shape[2]
    for x_ref, pe_ref, w1_ref, w2_ref, o_ref in ((k_ref, pek_ref, w1k_ref, w2k_ref, ko_ref),
                                                 (v_ref, pev_ref, w1v_ref, w2v_ref, vo_ref)):
        for g in range(G_KV):
            x = x_ref[0, g].astype(F32)
            lo = jnp.dot((x + pe_ref[0:1]).astype(BF16), w1_ref[0], preferred_element_type=F32)
            hi = jnp.dot((x + pe_ref[1:2]).astype(BF16), w1_ref[1], preferred_element_type=F32)
            pre = lo + pltpu.roll(hi, nseg - 1, axis=0)
            o_ref[0, g] = jnp.dot(_gelu_tanh(pre).astype(BF16), w2_ref[...],
                                  preferred_element_type=F32).astype(o_ref.dtype)


def _nsa_compress(kc_t, vc_t, pe_ck, w1_ck, w2_ck, pe_cv, w1_cv, w2_cv, bsz, s):
    assert L_CMP == 2 * STRIDE_CMP
    nseg = s // STRIDE_CMP
    width = STRIDE_CMP * DH_B
    seg = lambda t: t.reshape(bsz, G_KV, nseg, width)
    pe2 = lambda pe: pe.reshape(2, width)
    w1h = lambda w: w.reshape(2, width, DH_B).astype(BF16)
    xspec = pl.BlockSpec((1, G_KV, nseg, width), lambda b: (b, 0, 0, 0))
    pspec = pl.BlockSpec((2, width), lambda b: (0, 0))
    w1spec = pl.BlockSpec((2, width, DH_B), lambda b: (0, 0, 0))
    w2spec = pl.BlockSpec((DH_B, DH_B), lambda b: (0, 0))
    ospec = pl.BlockSpec((1, G_KV, nseg, DH_B), lambda b: (b, 0, 0, 0))
    oshape = jax.ShapeDtypeStruct((bsz, G_KV, nseg, DH_B), BF16)
    return pl.pallas_call(
        _nsa_compress_kernel,
        out_shape=(oshape, oshape),
        grid=(bsz,),
        in_specs=[xspec, xspec, pspec, pspec, w1spec, w1spec, w2spec, w2spec],
        out_specs=(ospec, ospec),
        compiler_params=_cparams(("parallel",)),
        name="nsa_compress",
    )(seg(kc_t), seg(vc_t), pe2(pe_ck), pe2(pe_cv), w1h(w1_ck), w1h(w1_cv), w2_ck.astype(BF16), w2_cv.astype(BF16))


def _nsa_cmp_kernel(q_ref, kc_ref, vc_ref, ov_ref, o_ref, mb_ref, *, tq, n_slc):
    i = pl.program_id(2)
    rows = R_GRP * tq
    ncmp = kc_ref.shape[2]
    qa = _stack_heads(q_ref[0], None)
    s = lax.dot_general(qa, kc_ref[0, 0], (((1,), (1,)), ((), ())), preferred_element_type=F32)
    t_row = (lax.broadcasted_iota(jnp.int32, (rows, ncmp), 0) & (tq - 1)) + i * tq
    c_end = lax.broadcasted_iota(jnp.int32, (rows, ncmp), 1) * STRIDE_CMP + (L_CMP - 1)
    valid = c_end <= t_row
    sm = jnp.where(valid, s, NEG)
    p = jnp.where(valid, jnp.exp(sm - jnp.max(sm, axis=1, keepdims=True)), 0.0)
    l = jnp.sum(p, axis=1, keepdims=True)
    p = p * (1.0 / jnp.where(l > 0.0, l, 1.0))
    _unstack_heads(o_ref, jnp.dot(p.astype(BF16), vc_ref[0, 0], preferred_element_type=F32), tq)
    psum = p[0:tq]
    for r in range(1, R_GRP):
        psum = psum + p[r * tq:(r + 1) * tq]
    imp = _dot_hilo(psum, ov_ref[...])
    lane = lax.broadcasted_iota(jnp.int32, (tq, LANES), 1)
    cur = (lax.broadcasted_iota(jnp.int32, (tq, LANES), 0) + i * tq) // L_SLC
    forced = (lane == 0) | (lane == cur)
    v = jnp.where(forced, BIG, jnp.where(lane <= cur, imp, -BIG))
    sel = jnp.zeros((tq, LANES), F32)
    for _ in range(T_SEL):
        m = jnp.max(v, axis=1, keepdims=True)
        first = jnp.min(jnp.where(v == m, lane, LANES), axis=1, keepdims=True)
        hit = lane == first
        sel = jnp.where(hit, 1.0, sel)
        v = jnp.where(hit, DROPPED, v)
    visible = jnp.where(lane <= cur, sel, 0.0)
    mb_ref[0, 0] = jnp.where(lane < n_slc, (visible - 1.0) * BIG, 0.0).astype(mb_ref.dtype)


def _nsa_cmp_select(q, k_cmp, v_cmp, bsz, s):
    tq = ATT_TQ
    nseg = s // STRIDE_CMP
    n_slc = s // L_SLC
    assert nseg <= LANES or nseg % LANES == 0
    c_start = np.arange(nseg) * STRIDE_CMP
    j_start = np.arange(n_slc) * L_SLC
    overlap = ((c_start[:, None] < j_start[None, :] + L_SLC) & (c_start[:, None] + L_CMP > j_start[None, :]))
    overlap = jnp.asarray(np.pad(overlap.astype(np.float32), ((0, 0), (0, LANES - n_slc))), BF16)
    qspec = pl.BlockSpec((1, tq, R_GRP * DH_B), lambda b, g, i: (b, i, g))
    cspec = pl.BlockSpec((1, 1, nseg, DH_B), lambda b, g, i: (b, g, 0, 0))
    return pl.pallas_call(
        functools.partial(_nsa_cmp_kernel, tq=tq, n_slc=n_slc),
        out_shape=(jax.ShapeDtypeStruct((bsz, s, HB * DH_B), BF16), jax.ShapeDtypeStruct((bsz, G_KV, s, LANES), BF16)),
        grid=(bsz, G_KV, s // tq),
        in_specs=[qspec, cspec, cspec, pl.BlockSpec((nseg, LANES), lambda b, g, i: (0, 0))],
        out_specs=(qspec, pl.BlockSpec((1, 1, tq, LANES), lambda b, g, i: (b, g, i, 0))),
        compiler_params=_cparams(("parallel", "parallel", "parallel")),
        name="nsa_compressed_select",
    )(q, k_cmp, v_cmp, overlap)


def _stack_heads(q, extra):
    parts = []
    for r in range(R_GRP):
        qr = q[:, r * DH_B:(r + 1) * DH_B]
        parts.append(qr if extra is None else jnp.concatenate([qr, extra], axis=1))
    return jnp.concatenate(parts, axis=0)


def _unstack_heads(o_ref, o, tq):
    for r in range(R_GRP):
        o_ref[0, :, r * DH_B:(r + 1) * DH_B] = o[r * tq:(r + 1) * tq].astype(o_ref.dtype)


def _selected_branch(i, q, mb, ks_ref, oh_ref, vs_ref, tq):
    rows = R_GRP * tq
    qa = _stack_heads(q, mb)

    def scores(j):
        keys = pl.ds(pl.multiple_of(j * tq, tq), tq)
        k = jnp.concatenate([ks_ref[0, keys, :], oh_ref[keys, :]], axis=1)
        return lax.dot_general(qa, k, (((1,), (1,)), ((), ())), preferred_element_type=F32)

    def update(j, s, carry):
        m, l, acc = carry
        v = vs_ref[0, pl.ds(pl.multiple_of(j * tq, tq), tq), :]
        m_new = jnp.maximum(m, jnp.max(s, axis=1, keepdims=True))
        alpha = jnp.exp(m - m_new)
        p = jnp.exp(s - m_new)
        l = alpha * l + jnp.sum(p, axis=1, keepdims=True)
        acc = alpha * acc + jnp.dot(p.astype(BF16), v, preferred_element_type=F32)
        return m_new, l, acc

    init = (jnp.full((rows, 1), -jnp.inf, F32), jnp.zeros((rows, 1), F32), jnp.zeros((rows, DH_B), F32))
    carry = lax.fori_loop(0, i, lambda j, c: update(j, scores(j), c), init)
    s = scores(i)
    t_loc = lax.broadcasted_iota(jnp.int32, (rows, tq), 0) & (tq - 1)
    k_loc = lax.broadcasted_iota(jnp.int32, (rows, tq), 1)
    m, l, acc = update(i, jnp.where(k_loc <= t_loc, s, NEG), carry)
    return acc / l


def _window_branch(i, q, kw_refs, vw_refs, tq):
    rows = R_GRP * tq
    nk = 3 * tq
    qa = _stack_heads(q, None)
    k = jnp.concatenate([r[0] for r in kw_refs], axis=0)
    v = jnp.concatenate([r[0] for r in vw_refs], axis=0)
    s = lax.dot_general(qa, k, (((1,), (1,)), ((), ())), preferred_element_type=F32)
    t_loc = lax.broadcasted_iota(jnp.int32, (rows, nk), 0) & (tq - 1)
    k_loc = lax.broadcasted_iota(jnp.int32, (rows, nk), 1)
    dist = t_loc + 2 * tq - k_loc
    t_abs = (lax.broadcasted_iota(jnp.int32, (rows, 1), 0) & (tq - 1)) + i * tq
    bound = jnp.minimum(t_abs + 1, WINDOW)
    valid = dist.astype(jnp.uint32) < bound.astype(jnp.uint32)
    s = jnp.where(valid, s, NEG)
    p = jnp.exp(s - jnp.max(s, axis=1, keepdims=True))
    l = jnp.sum(p, axis=1, keepdims=True)
    return jnp.dot(p.astype(BF16), v, preferred_element_type=F32) / l


def _nsa_local_kernel(q_ref, mb_ref, ks_ref, oh_ref, vs_ref, kw0, kw1, kw2, vw0, vw1, vw2, oc_ref, gate_ref, o_ref,
                      *, tq, gate_lane0):
    g, i = pl.program_id(1), pl.program_id(2)
    q = q_ref[0]
    o_slc = _selected_branch(i, q, mb_ref[0, 0], ks_ref, oh_ref, vs_ref, tq)
    o_win = _window_branch(i, q, (kw0, kw1, kw2), (vw0, vw1, vw2), tq)
    gates = jax.nn.sigmoid(gate_ref[0])
    lane = lax.broadcasted_iota(jnp.int32, gates.shape, 1)
    pick = lambda idx: jnp.sum(jnp.where(lane == idx, gates, 0.0), axis=1, keepdims=True)
    for r in range(R_GRP):
        base = gate_lane0 + (g * R_GRP + r) * 3
        rows = slice(r * tq, (r + 1) * tq)
        cols = slice(r * DH_B, (r + 1) * DH_B)
        o = (pick(base) * oc_ref[0, :, cols].astype(F32) + pick(base + 1) * o_slc[rows] + pick(base + 2) * o_win[rows])
        o_ref[0, :, cols] = o.astype(o_ref.dtype)


def _nsa_local(q, maskbias, ks, main, kw, o_cmp, small, col, gate_lane0, bsz, s):
    tq = ATT_TQ
    assert 2 * tq >= WINDOW
    onehot = jnp.asarray(np.arange(s)[:, None] // L_SLC == np.arange(LANES)[None, :], BF16)
    qspec = pl.BlockSpec((1, tq, R_GRP * DH_B), lambda b, g, i: (b, i, g))
    seq = lambda c0: pl.BlockSpec((1, s, DH_B), lambda b, g, i: (b, 0, c0 // DH_B + g))
    back = lambda c0, n: pl.BlockSpec((1, tq, DH_B), lambda b, g, i: (b, jnp.maximum(i - n, 0), c0 // DH_B + g))
    return pl.pallas_call(
        functools.partial(_nsa_local_kernel, tq=tq, gate_lane0=gate_lane0),
        out_shape=jax.ShapeDtypeStruct((bsz, s, HB * DH_B), BF16),
        grid=(bsz, G_KV, s // tq),
        in_specs=[qspec,
                  pl.BlockSpec((1, 1, tq, LANES), lambda b, g, i: (b, g, i, 0)),
                  seq(0), pl.BlockSpec((s, LANES), lambda b, g, i: (0, 0)), seq(col["vs"]),
                  back(0, 2), back(0, 1), back(0, 0),
                  back(col["vw"], 2), back(col["vw"], 1), back(col["vw"], 0),
                  qspec, pl.BlockSpec((1, tq, LANES), lambda b, g, i: (b, i, 0))],
        out_specs=qspec,
        compiler_params=_cparams(("parallel", "parallel", "arbitrary")),
        name="nsa_selected_window_combine",
    )(q, maskbias, ks, onehot, main, kw, kw, kw, main, main, main, o_cmp, small)


def _moe_sub_blocks(nv_ref, out_ref, compute):
    nsub = (nv_ref[pl.program_id(0)] + MOE_SUB - 1) // MOE_SUB
    rows_of = lambda sb: pl.ds(pl.multiple_of(sb * MOE_SUB, MOE_SUB), MOE_SUB)

    def live(sb, carry):
        out_ref[rows_of(sb), :] = compute(rows_of(sb))
        return carry

    def dead(sb, carry):
        out_ref[rows_of(sb), :] = jnp.zeros((MOE_SUB, out_ref.shape[1]), out_ref.dtype)
        return carry

    lax.fori_loop(0, nsub, live, 0)
    lax.fori_loop(nsub, MOE_TM // MOE_SUB, dead, 0)


def _moe_up_kernel(be_ref, nv_ref, x_ref, wg_ref, wu_ref, bg_ref, bu_ref, h_ref):
    wgb = wg_ref[0].astype(BF16)
    wub = wu_ref[0].astype(BF16)

    def compute(rows):
        x = jnp.concatenate(_unpack_bf16_pairs(x_ref[rows, :]), axis=1).astype(BF16)
        gate = jnp.dot(x, wgb, preferred_element_type=F32) + bg_ref[0]
        up = jnp.dot(x, wub, preferred_element_type=F32) + bu_ref[0]
        gate = jnp.minimum(gate, SWIGLU_LIMIT)
        up = jnp.clip(up, -SWIGLU_LIMIT, SWIGLU_LIMIT)
        return ((up + 1.0) * gate * jax.nn.sigmoid(SWIGLU_ALPHA * gate)).astype(h_ref.dtype)

    _moe_sub_blocks(nv_ref, h_ref, compute)


def _moe_down_kernel(be_ref, nv_ref, h_ref, wd_ref, bd_ref, y_ref):
    wdb = wd_ref[0].astype(BF16)
    half = wdb.shape[1] // 2

    def compute(rows):
        y = jnp.dot(h_ref[rows, :], wdb, preferred_element_type=F32) + bd_ref[0]
        return _pack_bf16_pairs(y[:, :half], y[:, half:])

    _moe_sub_blocks(nv_ref, y_ref, compute)


def _moe_experts(rows, block_e, n_valid, w_gate, b_gate, w_up, b_up, w_down, b_down):
    n_rows, d = rows.shape[0], 2 * rows.shape[1]
    n_blocks = n_rows // MOE_TM
    tn = MOE_TN
    last = D_FF // tn - 1

    def col(i, n, nv):
        return jnp.where(nv[i] > 0, n, last)

    tile = lambda width: pl.BlockSpec((MOE_TM, width), lambda i, n, be, nv: (i, 0))
    wcol = lambda k: pl.BlockSpec((1, k, tn), lambda i, n, be, nv: (be[i], 0, col(i, n, nv)))
    bcol = pl.BlockSpec((1, 1, tn), lambda i, n, be, nv: (be[i], 0, col(i, n, nv)))
    ocol = pl.BlockSpec((MOE_TM, tn), lambda i, n, be, nv: (i, n))
    params = _cparams(("arbitrary", "arbitrary"))
    h = pl.pallas_call(
        _moe_up_kernel,
        out_shape=jax.ShapeDtypeStruct((n_rows, D_FF), BF16),
        grid_spec=pltpu.PrefetchScalarGridSpec(
            num_scalar_prefetch=2, grid=(n_blocks, D_FF // tn),
            in_specs=[tile(d // 2), wcol(d), wcol(d), bcol, bcol], out_specs=ocol),
        compiler_params=params, name="moe_expert_up",
    )(block_e, n_valid, rows, w_gate, w_up, b_gate.reshape(N_EXP, 1, D_FF), b_up.reshape(N_EXP, 1, D_FF))
    return pl.pallas_call(
        _moe_down_kernel,
        out_shape=jax.ShapeDtypeStruct((n_rows, d // 2), jnp.uint32),
        grid_spec=pltpu.PrefetchScalarGridSpec(
            num_scalar_prefetch=2, grid=(n_blocks, d // tn),
            in_specs=[tile(D_FF), wcol(D_FF), bcol],
            out_specs=pl.BlockSpec((MOE_TM, tn // 2), lambda i, n, be, nv: (i, n))),
        compiler_params=params, name="moe_expert_down",
    )(block_e, n_valid, h, w_down, b_down.reshape(N_EXP, 1, d))


ROUTE_TT = 512
MOVE_TT = 256
HALF = D_MODEL // 2


def _pack_bf16_pairs(lo, hi):
    as_bits = lambda v: pltpu.bitcast(v.astype(BF16).astype(F32), jnp.uint32)
    return (as_bits(lo) >> 16) | (as_bits(hi) & jnp.uint32(0xFFFF0000))


def _unpack_bf16_pairs(w):
    return pltpu.bitcast(w << 16, F32), pltpu.bitcast(w & jnp.uint32(0xFFFF0000), F32)


def _route_kernel(h_ref, wr_ref, br_ref, e_ref, w_ref, p_ref, cnt_ref, run_ref, *, tt):
    @pl.when(pl.program_id(0) == 0)
    def _():
        run_ref[...] = jnp.zeros_like(run_ref)

    h = h_ref[...]
    logits = (jnp.dot(h, wr_ref[0], preferred_element_type=F32) + jnp.dot(h, wr_ref[1], preferred_element_type=F32)
              + br_ref[...])
    lane = lax.broadcasted_iota(jnp.int32, (tt, LANES), 1)
    v = jnp.where(lane < N_EXP, logits, -BIG)
    tops, hits, firsts = [], [], []
    for _ in range(TOP_K):
        m = jnp.max(v, axis=1, keepdims=True)
        first = jnp.min(jnp.where(v == m, lane, LANES), axis=1, keepdims=True)
        hit = lane == first
        v = jnp.where(hit, DROPPED, v)
        tops.append(m), hits.append(hit), firsts.append(first)
    ex = [jnp.exp(m - tops[0]) for m in tops]
    inv = 1.0 / functools.reduce(lambda a, b: a + b, ex)
    onehot = functools.reduce(lambda a, b: a + b, [jnp.where(hh, 1.0, 0.0) for hh in hits]).astype(BF16)
    ri = lax.broadcasted_iota(jnp.int32, (tt, tt), 0)
    ci = lax.broadcasted_iota(jnp.int32, (tt, tt), 1)
    before = jnp.where(ci < ri, 1.0, 0.0).astype(BF16)
    rank = jnp.dot(before, onehot, preferred_element_type=F32) + run_ref[0:1]
    run_ref[...] = run_ref[...] + jnp.dot(jnp.ones((8, tt), BF16), onehot, preferred_element_type=F32)
    cnt_ref[...] = run_ref[...].astype(jnp.int32)
    place = lambda cols, zero: functools.reduce(
        lambda acc, kc: jnp.where(lane == kc[0], kc[1], acc), list(enumerate(cols)), zero)
    e_ref[...] = place(firsts, jnp.zeros((tt, LANES), jnp.int32))
    w_ref[...] = place([e * inv for e in ex], jnp.zeros((tt, LANES), F32))
    pos = [jnp.sum(jnp.where(hh, rank, 0.0), axis=1, keepdims=True).astype(jnp.int32) for hh in hits]
    p_ref[...] = place(pos, jnp.zeros((tt, LANES), jnp.int32))


def _route(h, w_router, b_router):
    n, d = h.shape
    tt = ROUTE_TT
    wr = jnp.pad(w_router, ((0, 0), (0, LANES - N_EXP)))
    hi = wr.astype(BF16)
    wr2 = jnp.stack([hi, (wr - hi.astype(F32)).astype(BF16)])
    br = jnp.pad(b_router, (0, LANES - N_EXP)).reshape(1, LANES)
    tok = lambda dt: jax.ShapeDtypeStruct((n, LANES), dt)
    tspec = pl.BlockSpec((tt, LANES), lambda i: (i, 0))
    return pl.pallas_call(
        functools.partial(_route_kernel, tt=tt),
        out_shape=(tok(jnp.int32), tok(F32), tok(jnp.int32), jax.ShapeDtypeStruct((8, LANES), jnp.int32)),
        grid=(n // tt,),
        in_specs=[pl.BlockSpec((tt, d), lambda i: (i, 0)), pl.BlockSpec((2, d, LANES), lambda i: (0, 0, 0)),
                  pl.BlockSpec((1, LANES), lambda i: (0, 0))],
        out_specs=(tspec, tspec, tspec, pl.BlockSpec((8, LANES), lambda i: (0, 0))),
        scratch_shapes=[pltpu.VMEM((8, LANES), F32)],
        compiler_params=_cparams(("arbitrary",)),
        name="moe_route",
    )(h, wr2, br)


def _dispatch_kernel(dest_ref, src_ref, init_ref, rows_ref, sem, *, tt):
    del init_ref
    t0 = pl.program_id(0) * tt

    def copy(t, k):
        return pltpu.make_async_copy(src_ref.at[pl.ds(t, 1)], rows_ref.at[pl.ds(dest_ref[t * TOP_K + k], 1)], sem)

    def start(j, carry):
        for k in range(TOP_K):
            copy(t0 + j, k).start()
        return carry

    def wait(j, carry):
        for k in range(TOP_K):
            copy(t0 + j, k).wait()
        return carry

    lax.fori_loop(0, tt, start, 0)
    lax.fori_loop(0, tt, wait, 0)


def _dispatch(dest, hpk, n_rows):
    n, width = hpk.shape
    tt = MOVE_TT
    return pl.pallas_call(
        functools.partial(_dispatch_kernel, tt=tt),
        out_shape=jax.ShapeDtypeStruct((n_rows, width), jnp.uint32),
        grid_spec=pltpu.PrefetchScalarGridSpec(
            num_scalar_prefetch=1, grid=(n // tt,),
            in_specs=[pl.BlockSpec(memory_space=pl.ANY), pl.BlockSpec(memory_space=pl.ANY)],
            out_specs=pl.BlockSpec(memory_space=pl.ANY),
            scratch_shapes=[pltpu.SemaphoreType.DMA]),
        input_output_aliases={2: 0},
        compiler_params=_cparams(("arbitrary",)),
        name="moe_dispatch",
    )(dest, hpk, jnp.zeros((n_rows, width), jnp.uint32))


def _combine_kernel(dest_ref, y_ref, w_ref, x1_ref, g_ref, o_ref, buf_ref, sem, *, tt, norm):
    t0 = pl.program_id(0) * tt

    def copy(j, k):
        return pltpu.make_async_copy(y_ref.at[pl.ds(dest_ref[(t0 + j) * TOP_K + k], 1)],
                                     buf_ref.at[k, pl.ds(j, 1)], sem)

    def start(j, carry):
        for k in range(TOP_K):
            copy(j, k).start()
        return carry

    def wait(j, carry):
        for k in range(TOP_K):
            copy(j, k).wait()
        return carry

    lax.fori_loop(0, tt, start, 0)
    lax.fori_loop(0, tt, wait, 0)
    wts = w_ref[...]
    lane = lax.broadcasted_iota(jnp.int32, wts.shape, 1)
    x = x1_ref[...]
    nq = MOE_TN // 2
    for k in range(TOP_K):
        wk = jnp.sum(jnp.where(lane == k, wts, 0.0), axis=1, keepdims=True)
        parts = []
        for n in range(D_MODEL // MOE_TN):
            lo, hi = _unpack_bf16_pairs(buf_ref[k, :, n * nq:(n + 1) * nq])
            parts += [lo, hi]
        x = x + wk * jnp.concatenate(parts, axis=1)
    if norm:
        x = x * lax.rsqrt(jnp.mean(x * x, axis=-1, keepdims=True) + EPS) * g_ref[...]
    o_ref[...] = x.astype(o_ref.dtype)


def _combine(dest, ypk, wts, x1, gain):
    n, d = x1.shape
    tt = MOVE_TT
    norm = gain is not None
    gain = gain if norm else jnp.ones((d,), F32)
    return pl.pallas_call(
        functools.partial(_combine_kernel, tt=tt, norm=norm),
        out_shape=jax.ShapeDtypeStruct((n, d), F32),
        grid_spec=pltpu.PrefetchScalarGridSpec(
            num_scalar_prefetch=1, grid=(n // tt,),
            in_specs=[pl.BlockSpec(memory_space=pl.ANY),
                      pl.BlockSpec((tt, LANES), lambda i, dest: (i, 0)),
                      pl.BlockSpec((tt, d), lambda i, dest: (i, 0)),
                      pl.BlockSpec((1, d), lambda i, dest: (0, 0))],
            out_specs=pl.BlockSpec((tt, d), lambda i, dest: (i, 0)),
            scratch_shapes=[pltpu.VMEM((TOP_K, tt, d // 2), jnp.uint32), pltpu.SemaphoreType.DMA]),
        compiler_params=_cparams(("arbitrary",)),
        name="moe_combine_norm",
    )(dest, ypk, wts, x1, gain.reshape(1, d))


def _native_sparse_attention(main, small, positions, pe_ck, w1_ck, w2_ck, pe_cv, w1_cv, w2_cv, col, gate_lane0, bsz, s):
    half = ROT_DIM // 2
    inv_freq = ROPE_THETA ** (-jnp.arange(half, dtype=F32) * 2.0 / ROT_DIM)
    ang = positions.astype(F32)[..., None] * inv_freq
    cos, sin = jnp.cos(ang), jnp.sin(ang)
    rest = (bsz, s, DH_B - ROT_DIM)
    cos_t = jnp.concatenate([cos, cos, jnp.ones(rest, F32)], axis=-1)
    sin_t = jnp.concatenate([-sin, sin, jnp.zeros(rest, F32)], axis=-1)
    q, kc_t, vc_t, ks, kw = _nsa_rope(main, cos_t, sin_t, col, bsz, s)
    k_cmp, v_cmp = _nsa_compress(kc_t, vc_t, pe_ck, w1_ck, w2_ck, pe_cv, w1_cv, w2_cv, bsz, s)
    o_cmp, maskbias = _nsa_cmp_select(q, k_cmp, v_cmp, bsz, s)
    return _nsa_local(q, maskbias, ks, main, kw, o_cmp, small, col, gate_lane0, bsz, s)


def _moe_ffn(hf, hpk, x1, final_gain, w_router, b_router, w_gate, b_gate, w_up, b_up, w_down, b_down):
    n_tok, d = x1.shape
    eidx, wts, pos, cnt = _route(hf, w_router, b_router)
    counts = cnt[0, :N_EXP]
    padded = ((counts + MOE_TM - 1) // MOE_TM) * MOE_TM
    pad_end = jnp.cumsum(padded)
    pad_start = pad_end - padded
    group0 = jnp.sum(jnp.where(eidx[:, :TOP_K, None] == jnp.arange(N_EXP), pad_start, 0), axis=-1)
    dest = (group0 + pos[:, :TOP_K]).reshape(-1).astype(jnp.int32)
    n_blocks = (n_tok * TOP_K + N_EXP * (MOE_TM - 1) + MOE_TM - 1) // MOE_TM
    rows = _dispatch(dest, hpk, n_blocks * MOE_TM)
    tile0 = jnp.arange(n_blocks, dtype=jnp.int32) * MOE_TM
    block_e = jnp.minimum(jnp.searchsorted(pad_end, tile0, side='right'), N_EXP - 1).astype(jnp.int32)
    n_valid = jnp.clip(pad_start[block_e] + counts[block_e] - tile0, 0, MOE_TM).astype(jnp.int32)
    ypk = _moe_experts(rows, block_e, n_valid, w_gate, b_gate, w_up, b_up, w_down, b_down)
    return _combine(dest, ypk, wts, x1, final_gain)


def _layer(x, positions, norm_mix, w_in, conv_w, a_log, dt_bias, norm_gdn, pe_ck, w1_ck, w2_ck,
           pe_cv, w1_cv, w2_cv, w_proj_a, w_proj_b, w_out, norm_ffn, w_router, b_router,
           w_gate, b_gate, w_up, b_up, w_down, b_down, final_gain):
    bsz, s, d = x.shape
    n_tok = bsz * s
    x2 = x.reshape(n_tok, d)
    h = _rmsnorm(x2, norm_mix, BF16)
    n_small = 2 * HA + 3 * HB
    sp = SPLIT_POINTS
    w_main = jnp.concatenate([w_in[:, :sp[3]], w_in[:, sp[5]:sp[12]], w_in[:, sp[13]:]], axis=1).astype(BF16)
    w_small = jnp.concatenate([w_in[:, sp[3]:sp[5]], w_in[:, sp[12]:sp[13]]], axis=1)
    w_small = jnp.pad(w_small, ((0, 0), (0, LANES - n_small))).astype(BF16)
    main = _matmul(h, w_main, BF16).reshape(bsz, s, -1)
    small = _matmul(h, w_small, F32).reshape(bsz, s, -1)
    names = ("qa", "ka", "va", "za", "qb", "kc", "vc", "ks", "vs", "kw", "vw", "gm")
    sizes = (HA * DK_A, HA * DK_A, HA * DV_A, HA * DV_A,
             HB * DH_B, G_KV * DH_B, G_KV * DH_B, G_KV * DH_B, G_KV * DH_B, G_KV * DH_B, G_KV * DH_B, 2 * D_MODEL)
    col = {nm: sum(sizes[:i]) for i, nm in enumerate(names)}
    aa, ba = small[..., :HA], small[..., HA:2 * HA]

    o_a = _gated_deltanet(main, aa, ba, conv_w, a_log, dt_bias, norm_gdn, bsz, s)
    o_b = _native_sparse_attention(main, small, positions, pe_ck, w1_ck, w2_ck, pe_cv, w1_cv, w2_cv,
                                   col, 2 * HA, bsz, s)
    merged = _merge(o_a.reshape(n_tok, d), o_b.reshape(n_tok, d), w_proj_a.astype(BF16),
                    w_proj_b.astype(BF16), main.reshape(n_tok, -1), col["gm"])
    x1, hf, hpk = _outproj(merged, w_out.astype(BF16), x2, norm_ffn)
    out = _moe_ffn(hf, hpk, x1, final_gain, w_router, b_router, w_gate, b_gate, w_up, b_up, w_down, b_down)
    return out.reshape(bsz, s, d)


def kernel(x, positions, norm_mix, w_in, conv_w, a_log, dt_bias, norm_gdn, pe_ck, w1_ck, w2_ck, pe_cv, w1_cv, w2_cv, w_proj_a, w_proj_b, w_out, norm_ffn, w_router, b_router, w_gate, b_gate, w_up, b_up, w_down, b_down, norm_final):
    depth = norm_mix.shape[0]
    for l in range(depth):
        x = _layer(x, positions, norm_mix[l], w_in[l], conv_w[l], a_log[l], dt_bias[l], norm_gdn[l],
                   pe_ck[l], w1_ck[l], w2_ck[l], pe_cv[l], w1_cv[l], w2_cv[l],
                   w_proj_a[l], w_proj_b[l], w_out[l], norm_ffn[l], w_router[l], b_router[l],
                   w_gate[l], b_gate[l], w_up[l], b_up[l], w_down[l], b_down[l],
                   norm_final if l + 1 == depth else None)
    return x
```

```python
import functools
import math

import jax
import jax.numpy as jnp
import numpy as np
from jax import lax
from jax.experimental import pallas as pl
from jax.experimental.pallas import tpu as pltpu

F32 = jnp.float32
BF16 = jnp.bfloat16

D_MODEL = 2048
EPS = 1e-6
NEG = -1e30
HA = D_MODEL // 128
DK_A = 128
DV_A = 128
CONV_W = 4
CHUNK = 64
HB = D_MODEL // 128
G_KV = 4
R_GRP = HB // G_KV
DH_B = 128
ROT_DIM = DH_B // 4
ROPE_THETA = 500000.0
L_CMP = 32
STRIDE_CMP = 16
L_SLC = 64
T_SEL = 8
WINDOW = 512
N_EXP = 32
TOP_K = 4
D_FF = D_MODEL
SWIGLU_LIMIT = 7.0
SWIGLU_ALPHA = 1.702
SPLIT_SIZES = (HA * DK_A, HA * DK_A, HA * DV_A, HA * DV_A, HA, HA,
               HB * DH_B, G_KV * DH_B, G_KV * DH_B, G_KV * DH_B, G_KV * DH_B, G_KV * DH_B, G_KV * DH_B,
               3 * HB, 2 * D_MODEL)
SPLIT_POINTS = tuple(sum(SPLIT_SIZES[:i + 1]) for i in range(len(SPLIT_SIZES) - 1))

V7X_VMEM_LIMIT_BYTES = 56 * 1024 * 1024
LANES = 128
MOE_TM = 1024
MOE_SUB = 256
MOE_TN = 512
ATT_TQ = 256


def _cparams(sem):
    return pltpu.CompilerParams(dimension_semantics=sem, vmem_limit_bytes=V7X_VMEM_LIMIT_BYTES)


def _mm_kernel(x_ref, w_ref, o_ref):
    o_ref[...] = jnp.dot(x_ref[...], w_ref[...], preferred_element_type=F32).astype(o_ref.dtype)


def _matmul(x, w, out_dtype, tm=1024, tn=1024):
    m, k = x.shape
    n = w.shape[1]
    tm, tn = min(tm, m), min(tn, n)
    assert m % tm == 0 and n % tn == 0
    return pl.pallas_call(
        _mm_kernel,
        out_shape=jax.ShapeDtypeStruct((m, n), out_dtype),
        grid=(n // tn, m // tm),
        in_specs=[pl.BlockSpec((tm, k), lambda j, i: (i, 0)),
                  pl.BlockSpec((k, tn), lambda j, i: (0, j))],
        out_specs=pl.BlockSpec((tm, tn), lambda j, i: (i, j)),
        compiler_params=_cparams(("parallel", "parallel")),
        name="dense_matmul",
    )(x, w)


def _rmsnorm_kernel(x_ref, g_ref, o_ref):
    x = x_ref[...]
    y = x * lax.rsqrt(jnp.mean(x * x, axis=-1, keepdims=True) + EPS)
    o_ref[...] = (y * g_ref[...]).astype(o_ref.dtype)


def _rmsnorm(x, gain, out_dtype, tm=512):
    m, d = x.shape
    return pl.pallas_call(
        _rmsnorm_kernel,
        out_shape=jax.ShapeDtypeStruct((m, d), out_dtype),
        grid=(m // tm,),
        in_specs=[pl.BlockSpec((tm, d), lambda i: (i, 0)),
                  pl.BlockSpec((1, d), lambda i: (0, 0))],
        out_specs=pl.BlockSpec((tm, d), lambda i: (i, 0)),
        compiler_params=_cparams(("parallel",)),
        name="rmsnorm",
    )(x, gain.reshape(1, d))


def _add_rmsnorm_kernel(x_ref, y_ref, g_ref, o_ref):
    x = x_ref[...] + y_ref[...]
    y = x * lax.rsqrt(jnp.mean(x * x, axis=-1, keepdims=True) + EPS)
    o_ref[...] = (y * g_ref[...]).astype(o_ref.dtype)


def _add_rmsnorm(x, y, gain, out_dtype, tm=512):
    m, d = x.shape
    row = pl.BlockSpec((tm, d), lambda i: (i, 0))
    return pl.pallas_call(
        _add_rmsnorm_kernel,
        out_shape=jax.ShapeDtypeStruct((m, d), out_dtype),
        grid=(m // tm,),
        in_specs=[row, row, pl.BlockSpec((1, d), lambda i: (0, 0))],
        out_specs=row,
        compiler_params=_cparams(("parallel",)),
        name="add_rmsnorm",
    )(x, y, gain.reshape(1, d))


def _merge_kernel(oa_ref, ob_ref, wa_ref, wb_ref, ga_ref, gb_ref, o_ref):
    ya = jnp.dot(oa_ref[...], wa_ref[...], preferred_element_type=F32)
    yb = jnp.dot(ob_ref[...], wb_ref[...], preferred_element_type=F32)
    o_ref[...] = (jax.nn.sigmoid(ga_ref[...].astype(F32)) * ya
                  + jax.nn.sigmoid(gb_ref[...].astype(F32)) * yb).astype(o_ref.dtype)


def _merge(o_a, o_b, w_a, w_b, main2d, gm_col0, tm=512, tn=1024):
    m, d = o_a.shape
    assert gm_col0 % tn == 0 and d % tn == 0
    g0 = gm_col0 // tn
    lhs = pl.BlockSpec((tm, d), lambda j, i: (i, 0))
    rhs = pl.BlockSpec((d, tn), lambda j, i: (0, j))
    return pl.pallas_call(
        _merge_kernel,
        out_shape=jax.ShapeDtypeStruct((m, d), BF16),
        grid=(d // tn, m // tm),
        in_specs=[lhs, lhs, rhs, rhs,
                  pl.BlockSpec((tm, tn), lambda j, i: (i, g0 + j)),
                  pl.BlockSpec((tm, tn), lambda j, i: (i, g0 + d // tn + j))],
        out_specs=pl.BlockSpec((tm, tn), lambda j, i: (i, j)),
        compiler_params=_cparams(("parallel", "parallel")),
        name="mixer_merge",
    )(o_a, o_b, w_a, w_b, main2d, main2d)


def _outproj_kernel(m_ref, w_ref, x_ref, g_ref, x1_ref, h_ref, hpk_ref):
    x1 = x_ref[...] + jnp.dot(m_ref[...], w_ref[...], preferred_element_type=F32)
    x1_ref[...] = x1
    y = x1 * lax.rsqrt(jnp.mean(x1 * x1, axis=-1, keepdims=True) + EPS) * g_ref[...]
    h_ref[...] = y.astype(h_ref.dtype)
    half = y.shape[1] // 2
    hpk_ref[...] = _pack_bf16_pairs(y[:, :half], y[:, half:])


def _outproj(merged, w_out, x, gain, tm=512):
    m, d = x.shape
    row = lambda width=d: pl.BlockSpec((tm, width), lambda i: (i, 0))
    return pl.pallas_call(
        _outproj_kernel,
        out_shape=(jax.ShapeDtypeStruct((m, d), F32), jax.ShapeDtypeStruct((m, d), BF16),
                   jax.ShapeDtypeStruct((m, d // 2), jnp.uint32)),
        grid=(m // tm,),
        in_specs=[row(), pl.BlockSpec((d, d), lambda i: (0, 0)), row(), pl.BlockSpec((1, d), lambda i: (0, 0))],
        out_specs=(row(), row(), row(d // 2)),
        compiler_params=_cparams(("parallel",)),
        name="out_proj_residual_norm",
    )(merged, w_out, x, gain.reshape(1, d))


GDN_COLS = 512
GDN_GROUP = 256
GDN_PH = 4
GDN_TS = 512


def _gdn_conv_kernel(x_ref, w_ref, o_ref):
    sec = pl.program_id(1) // (HA * DK_A // GDN_COLS)
    x = x_ref[0].astype(F32)
    w = w_ref[...]
    row = lax.broadcasted_iota(jnp.int32, x.shape, 0)
    y = x * w[CONV_W - 1:CONV_W]
    for i in range(CONV_W - 1):
        sh = CONV_W - 1 - i
        y = y + jnp.where(row >= sh, pltpu.roll(x, sh, axis=0), 0.0) * w[i:i + 1]
    y = y * jax.nn.sigmoid(y)
    qscale = jnp.where(sec == 0, DK_A ** -0.5, 1.0)
    for h in range(GDN_COLS // DK_A):
        yh = y[:, h * DK_A:(h + 1) * DK_A]
        nrm = yh * (lax.rsqrt(jnp.sum(yh * yh, axis=-1, keepdims=True) + EPS) * qscale)
        o_ref[0, 0, h] = jnp.where(sec < 2, nrm, yh).astype(o_ref.dtype)


def _gdn_conv(main, conv_w, bsz, s):
    ncol = 3 * HA * DK_A // GDN_COLS
    hpc = GDN_COLS // DK_A
    return pl.pallas_call(
        _gdn_conv_kernel,
        out_shape=jax.ShapeDtypeStruct((3, bsz, HA, s, DK_A), BF16),
        grid=(bsz, ncol),
        in_specs=[pl.BlockSpec((1, s, GDN_COLS), lambda b, c: (b, 0, c)),
                  pl.BlockSpec((CONV_W, GDN_COLS), lambda b, c: (0, c))],
        out_specs=pl.BlockSpec((1, 1, hpc, s, DK_A), lambda b, c: (c // (HA // hpc), b, c % (HA // hpc), 0, 0)),
        compiler_params=_cparams(("parallel", "parallel")),
        name="gdn_conv_silu_l2norm",
    )(main, conv_w)


def _col_rep(row, n):
    return jnp.broadcast_to(row, (LANES, n)).T


def _dot_hilo(x, m):
    hi = x.astype(BF16)
    lo = (x - hi.astype(F32)).astype(BF16)
    return jnp.dot(hi, m, preferred_element_type=F32) + jnp.dot(lo, m, preferred_element_type=F32)


def _gdn_prep_kernel(q_ref, k_ref, v_ref, g_ref, b_ref, u_ref, w_ref, qg_ref, kd_ref, a_ref, egl_ref):
    n = GDN_GROUP
    ri = lax.broadcasted_iota(jnp.int32, (n, n), 0)
    ci = lax.broadcasted_iota(jnp.int32, (n, n), 1)
    same = (ri // CHUNK) == (ci // CHUNK)
    incl = same & (ri >= ci)
    strict = same & (ri > ci)
    one_if = lambda m: jnp.where(m, 1.0, 0.0).astype(BF16)
    cum_m, tot_m = one_if(same & (ri <= ci)), one_if(same)
    eye = jnp.where(ri == ci, 1.0, 0.0)
    wide = lambda c: jnp.concatenate([c] * (n // LANES), axis=1)
    nt = (((1,), (1,)), ((), ()))
    heads = range(GDN_PH)
    ts, ps, rhs = [], [], []
    for h in heads:
        q, k, v = q_ref[0, 0, h], k_ref[0, 0, h], v_ref[0, 0, h]
        g8 = jnp.broadcast_to(g_ref[0, h], (8, n))
        gc_row = _dot_hilo(g8, cum_m)[0:1]
        gl_row = _dot_hilo(g8, tot_m)[0:1]
        gc_c, gl_c, b_c = _col_rep(gc_row, n), _col_rep(gl_row, n), _col_rep(b_ref[0, h], n)
        decay = jnp.exp(jnp.where(incl, wide(gc_c) - gc_row, NEG))
        kk = lax.dot_general(k, k, nt, preferred_element_type=F32)
        qk = lax.dot_general(q, k, nt, preferred_element_type=F32)
        xb = jnp.where(strict, -(kk * wide(b_c) * decay), 0.0).astype(BF16)
        a = qk * decay
        kf = k.astype(F32)
        egc = jnp.exp(gc_c)
        qg_ref[0, h] = (q.astype(F32) * egc).astype(qg_ref.dtype)
        kd_ref[0, h] = (kf * jnp.exp(gl_c - gc_c)).astype(kd_ref.dtype)
        for c in range(n // CHUNK):
            blk = slice(c * CHUNK, (c + 1) * CHUNK)
            a_ref[0, h, blk, :] = a[blk, blk].astype(a_ref.dtype)
        egl = jnp.exp(gl_c)
        egl_ref[0, h, 0] = jnp.concatenate([egl[c * CHUNK:c * CHUNK + 1] for c in range(n // CHUNK)], axis=0)
        rhs.append(jnp.concatenate([(v.astype(F32) * b_c).astype(BF16), (kf * b_c * egc).astype(BF16)], axis=1))
        ts.append(eye + xb.astype(F32))
        ps.append(xb)
    ps = [jnp.dot(p, p, preferred_element_type=F32).astype(BF16) for p in ps]
    for step in range(5):
        for h in heads:
            if step < 4:
                tp = jnp.dot(jnp.concatenate([ts[h].astype(BF16), ps[h]], axis=0), ps[h], preferred_element_type=F32)
                ts[h] = ts[h] + tp[:n]
                ps[h] = tp[n:].astype(BF16)
            else:
                ts[h] = ts[h] + jnp.dot(ts[h].astype(BF16), ps[h], preferred_element_type=F32)
    for h in heads:
        uw = jnp.dot(ts[h].astype(BF16), rhs[h], preferred_element_type=F32)
        u_ref[0, h] = uw[:, :DV_A].astype(u_ref.dtype)
        w_ref[0, h] = uw[:, DV_A:].astype(w_ref.dtype)


def _gdn_prep(qkv, g_t, beta_t, bsz, s):
    n, ph = GDN_GROUP, GDN_PH
    tok = lambda width, dt: jax.ShapeDtypeStruct((bsz, HA, s, width), dt)
    tspec = lambda width: pl.BlockSpec((1, ph, n, width), lambda b, h, i: (b, h, i, 0))
    qspec = lambda sec: pl.BlockSpec((1, 1, ph, n, DK_A), lambda b, h, i: (sec, b, h, i, 0))
    rspec = pl.BlockSpec((1, ph, 1, n), lambda b, h, i: (b, h, 0, i))
    return pl.pallas_call(
        _gdn_prep_kernel,
        out_shape=(tok(DV_A, BF16), tok(DK_A, BF16), tok(DK_A, BF16), tok(DK_A, BF16), tok(CHUNK, BF16),
                   jax.ShapeDtypeStruct((bsz, HA, s // n, n // CHUNK, LANES), F32)),
        grid=(bsz, HA // ph, s // n),
        in_specs=[qspec(0), qspec(1), qspec(2), rspec, rspec],
        out_specs=(tspec(DV_A), tspec(DK_A), tspec(DK_A), tspec(DK_A), tspec(CHUNK),
                   pl.BlockSpec((1, ph, 1, n // CHUNK, LANES), lambda b, h, i: (b, h, i, 0, 0))),
        compiler_params=_cparams(("parallel", "parallel", "parallel")),
        name="gdn_chunk_prep",
    )(qkv, qkv, qkv, g_t, beta_t)


def _gdn_scan_kernel(u_ref, w_ref, qg_ref, kd_ref, a_ref, egl_ref, z_ref, ng_ref, o_ref, state_ref, *, nchunk):
    @pl.when(pl.program_id(1) == 0)
    def _():
        state_ref[...] = jnp.zeros_like(state_ref)

    tn = (((0,), (0,)), ((), ()))
    heads = range(HA)

    def body(c, carry):
        rows = pl.ds(pl.multiple_of(c * CHUNK, CHUNK), CHUNK)
        st = [state_ref[h] for h in heads]
        sb = [x.astype(BF16) for x in st]
        vb = [(u_ref[0, h, rows, :].astype(F32)
               - jnp.dot(w_ref[0, h, rows, :], sb[h], preferred_element_type=F32)).astype(BF16) for h in heads]
        o = [jnp.dot(qg_ref[0, h, rows, :], sb[h], preferred_element_type=F32)
             + jnp.dot(a_ref[0, h, rows, :], vb[h], preferred_element_type=F32) for h in heads]
        for h in heads:
            state_ref[h] = (st[h] * egl_ref[0, h, pl.ds(c, 1), :]
                            + lax.dot_general(kd_ref[0, h, rows, :], vb[h], tn, preferred_element_type=F32))
        for h in heads:
            cols = slice(h * DV_A, (h + 1) * DV_A)
            z = z_ref[0, rows, cols].astype(F32)
            on = o[h] * lax.rsqrt(jnp.mean(o[h] * o[h], axis=-1, keepdims=True) + EPS) * ng_ref[...]
            o_ref[0, rows, cols] = (on * (z * jax.nn.sigmoid(z))).astype(o_ref.dtype)
        return carry

    lax.fori_loop(0, nchunk, body, 0)


def _gdn_scan(u, w, qg, kd, a, egl, main, norm_gdn, bsz, s):
    ts = GDN_TS
    z_blk0 = 3 * HA * DK_A // (HA * DV_A)
    hspec = lambda width: pl.BlockSpec((1, HA, ts, width), lambda b, i: (b, 0, i, 0))
    return pl.pallas_call(
        functools.partial(_gdn_scan_kernel, nchunk=ts // CHUNK),
        out_shape=jax.ShapeDtypeStruct((bsz, s, HA * DV_A), BF16),
        grid=(bsz, s // ts),
        in_specs=[hspec(DV_A), hspec(DK_A), hspec(DK_A), hspec(DK_A), hspec(CHUNK),
                  pl.BlockSpec((1, HA, ts // CHUNK, LANES), lambda b, i: (b, 0, i, 0)),
                  pl.BlockSpec((1, ts, HA * DV_A), lambda b, i: (b, i, z_blk0)),
                  pl.BlockSpec((1, DV_A), lambda b, i: (0, 0))],
        out_specs=pl.BlockSpec((1, ts, HA * DV_A), lambda b, i: (b, i, 0)),
        scratch_shapes=[pltpu.VMEM((HA, DK_A, DV_A), F32)],
        compiler_params=_cparams(("parallel", "arbitrary")),
        name="gdn_delta_scan",
    )(u, w, qg, kd, a, egl, main, norm_gdn.reshape(1, DV_A))


def _gated_deltanet(main, a_in, b_in, conv_w, a_log, dt_bias, norm_gdn, bsz, s):
    qkv = _gdn_conv(main, conv_w, bsz, s)
    g = -jnp.exp(a_log.astype(F32)) * jax.nn.softplus(a_in.astype(F32) + dt_bias.astype(F32))
    beta = jax.nn.sigmoid(b_in.astype(F32))
    g_t = g.transpose(0, 2, 1).reshape(bsz, HA, 1, s)
    beta_t = beta.transpose(0, 2, 1).reshape(bsz, HA, 1, s)
    u, w, qg, kd, a, egl = _gdn_prep(qkv, g_t, beta_t, bsz, s)
    egl = egl.reshape(bsz, HA, s // CHUNK, LANES)
    return _gdn_scan(u, w, qg, kd, a, egl, main, norm_gdn, bsz, s)


NSA_TS = 512
BIG = 1e30
DROPPED = -3e38


def _rope(x, cos, sin, lane):
    half = ROT_DIM // 2
    partner = jnp.where(lane < half, pltpu.roll(x, DH_B - half, axis=1), pltpu.roll(x, half, axis=1))
    return x * cos + partner * sin


def _nsa_rope_kernel(q_ref, kc_ref, vc_ref, ks_ref, kw_ref, cos_ref, sin_ref,
                     qo_ref, kco_ref, vco_ref, kso_ref, kwo_ref):
    cos, sin = cos_ref[0], sin_ref[0]
    lane = lax.broadcasted_iota(jnp.int32, cos.shape, 1)
    head = lambda ref, h: ref[0, :, h * DH_B:(h + 1) * DH_B].astype(F32)
    for h in range(HB):
        qo_ref[0, :, h * DH_B:(h + 1) * DH_B] = (_rope(head(q_ref, h), cos, sin, lane) * DH_B ** -0.5).astype(qo_ref.dtype)
    for g in range(G_KV):
        cols = slice(g * DH_B, (g + 1) * DH_B)
        kco_ref[0, g] = _rope(head(kc_ref, g), cos, sin, lane).astype(kco_ref.dtype)
        vco_ref[0, g] = vc_ref[0, :, cols]
        kso_ref[0, :, cols] = _rope(head(ks_ref, g), cos, sin, lane).astype(kso_ref.dtype)
        kwo_ref[0, :, cols] = _rope(head(kw_ref, g), cos, sin, lane).astype(kwo_ref.dtype)


def _nsa_rope(main, cos, sin, col, bsz, s):
    ts = NSA_TS
    kvw = G_KV * DH_B
    tok = lambda width, c0: pl.BlockSpec((1, ts, width), lambda b, i: (b, i, c0 // width))
    tab = pl.BlockSpec((1, ts, DH_B), lambda b, i: (b, i, 0))
    grp = pl.BlockSpec((1, G_KV, ts, DH_B), lambda b, i: (b, 0, i, 0))
    flat = lambda width: pl.BlockSpec((1, ts, width), lambda b, i: (b, i, 0))
    return pl.pallas_call(
        _nsa_rope_kernel,
        out_shape=(jax.ShapeDtypeStruct((bsz, s, HB * DH_B), BF16),
                   jax.ShapeDtypeStruct((bsz, G_KV, s, DH_B), BF16), jax.ShapeDtypeStruct((bsz, G_KV, s, DH_B), BF16),
                   jax.ShapeDtypeStruct((bsz, s, kvw), BF16), jax.ShapeDtypeStruct((bsz, s, kvw), BF16)),
        grid=(bsz, s // ts),
        in_specs=[tok(HB * DH_B, col["qb"]), tok(kvw, col["kc"]), tok(kvw, col["vc"]), tok(kvw, col["ks"]),
                  tok(kvw, col["kw"]), tab, tab],
        out_specs=(flat(HB * DH_B), grp, grp, flat(kvw), flat(kvw)),
        compiler_params=_cparams(("parallel", "parallel")),
        name="nsa_rotary",
    )(main, main, main, main, main, cos, sin)


def _gelu_tanh(x):
    return 0.5 * x * (1.0 + jnp.tanh(math.sqrt(2.0 / math.pi) * (x + 0.044715 * x * x * x)))


def _nsa_compress_kernel(k_ref, v_ref, pek_ref, pev_ref, w1k_ref, w1v_ref, w2k_ref, w2v_ref, ko_ref, vo_ref):
    nseg = k_ref.shape[2]
    for x_ref, pe_ref, w1_ref, w2_ref, o_ref in ((k_ref, pek_ref, w1k_ref, w2k_ref, ko_ref),
                                                 (v_ref, pev_ref, w1v_ref, w2v_ref, vo_ref)):
        for g in range(G_KV):
            x = x_ref[0, g].astype(F32)
            lo = jnp.dot((x + pe_ref[0:1]).astype(BF16), w1_ref[0], preferred_element_type=F32)
            hi = jnp.dot((x + pe_ref[1:2]).astype(BF16), w1_ref[1], preferred_element_type=F32)
            pre = lo + pltpu.roll(hi, nseg - 1, axis=0)
            o_ref[0, g] = jnp.dot(_gelu_tanh(pre).astype(BF16), w2_ref[...],
                                  preferred_element_type=F32).astype(o_ref.dtype)


def _nsa_compress(kc_t, vc_t, pe_ck, w1_ck, w2_ck, pe_cv, w1_cv, w2_cv, bsz, s):
    assert L_CMP == 2 * STRIDE_CMP
    nseg = s // STRIDE_CMP
    width = STRIDE_CMP * DH_B
    seg = lambda t: t.reshape(bsz, G_KV, nseg, width)
    pe2 = lambda pe: pe.reshape(2, width)
    w1h = lambda w: w.reshape(2, width, DH_B).astype(BF16)
    xspec = pl.BlockSpec((1, G_KV, nseg, width), lambda b: (b, 0, 0, 0))
    pspec = pl.BlockSpec((2, width), lambda b: (0, 0))
    w1spec = pl.BlockSpec((2, width, DH_B), lambda b: (0, 0, 0))
    w2spec = pl.BlockSpec((DH_B, DH_B), lambda b: (0, 0))
    ospec = pl.BlockSpec((1, G_KV, nseg, DH_B), lambda b: (b, 0, 0, 0))
    oshape = jax.ShapeDtypeStruct((bsz, G_KV, nseg, DH_B), BF16)
    return pl.pallas_call(
        _nsa_compress_kernel,
        out_shape=(oshape, oshape),
        grid=(bsz,),
        in_specs=[xspec, xspec, pspec, pspec, w1spec, w1spec, w2spec, w2spec],
        out_specs=(ospec, ospec),
        compiler_params=_cparams(("parallel",)),
        name="nsa_compress",
    )(seg(kc_t), seg(vc_t), pe2(pe_ck), pe2(pe_cv), w1h(w1_ck), w1h(w1_cv), w2_ck.astype(BF16), w2_cv.astype(BF16))


def _nsa_cmp_kernel(q_ref, kc_ref, vc_ref, ov_ref, o_ref, mb_ref, *, tq, n_slc):
    i = pl.program_id(2)
    rows = R_GRP * tq
    ncmp = kc_ref.shape[2]
    qa = _stack_heads(q_ref[0], None)
    s = lax.dot_general(qa, kc_ref[0, 0], (((1,), (1,)), ((), ())), preferred_element_type=F32)
    t_row = (lax.broadcasted_iota(jnp.int32, (rows, ncmp), 0) & (tq - 1)) + i * tq
    c_end = lax.broadcasted_iota(jnp.int32, (rows, ncmp), 1) * STRIDE_CMP + (L_CMP - 1)
    valid = c_end <= t_row
    sm = jnp.where(valid, s, NEG)
    p = jnp.where(valid, jnp.exp(sm - jnp.max(sm, axis=1, keepdims=True)), 0.0)
    l = jnp.sum(p, axis=1, keepdims=True)
    p = p * (1.0 / jnp.where(l > 0.0, l, 1.0))
    _unstack_heads(o_ref, jnp.dot(p.astype(BF16), vc_ref[0, 0], preferred_element_type=F32), tq)
    psum = p[0:tq]
    for r in range(1, R_GRP):
        psum = psum + p[r * tq:(r + 1) * tq]
    imp = _dot_hilo(psum, ov_ref[...])
    lane = lax.broadcasted_iota(jnp.int32, (tq, LANES), 1)
    cur = (lax.broadcasted_iota(jnp.int32, (tq, LANES), 0) + i * tq) // L_SLC
    forced = (lane == 0) | (lane == cur)
    v = jnp.where(forced, BIG, jnp.where(lane <= cur, imp, -BIG))
    sel = jnp.zeros((tq, LANES), F32)
    for _ in range(T_SEL):
        m = jnp.max(v, axis=1, keepdims=True)
        first = jnp.min(jnp.where(v == m, lane, LANES), axis=1, keepdims=True)
        hit = lane == first
        sel = jnp.where(hit, 1.0, sel)
        v = jnp.where(hit, DROPPED, v)
    visible = jnp.where(lane <= cur, sel, 0.0)
    mb_ref[0, 0] = jnp.where(lane < n_slc, (visible - 1.0) * BIG, 0.0).astype(mb_ref.dtype)


def _nsa_cmp_select(q, k_cmp, v_cmp, bsz, s):
    tq = ATT_TQ
    nseg = s // STRIDE_CMP
    n_slc = s // L_SLC
    assert nseg <= LANES or nseg % LANES == 0
    c_start = np.arange(nseg) * STRIDE_CMP
    j_start = np.arange(n_slc) * L_SLC
    overlap = ((c_start[:, None] < j_start[None, :] + L_SLC) & (c_start[:, None] + L_CMP > j_start[None, :]))
    overlap = jnp.asarray(np.pad(overlap.astype(np.float32), ((0, 0), (0, LANES - n_slc))), BF16)
    qspec = pl.BlockSpec((1, tq, R_GRP * DH_B), lambda b, g, i: (b, i, g))
    cspec = pl.BlockSpec((1, 1, nseg, DH_B), lambda b, g, i: (b, g, 0, 0))
    return pl.pallas_call(
        functools.partial(_nsa_cmp_kernel, tq=tq, n_slc=n_slc),
        out_shape=(jax.ShapeDtypeStruct((bsz, s, HB * DH_B), BF16), jax.ShapeDtypeStruct((bsz, G_KV, s, LANES), BF16)),
        grid=(bsz, G_KV, s // tq),
        in_specs=[qspec, cspec, cspec, pl.BlockSpec((nseg, LANES), lambda b, g, i: (0, 0))],
        out_specs=(qspec, pl.BlockSpec((1, 1, tq, LANES), lambda b, g, i: (b, g, i, 0))),
        compiler_params=_cparams(("parallel", "parallel", "parallel")),
        name="nsa_compressed_select",
    )(q, k_cmp, v_cmp, overlap)


def _stack_heads(q, extra):
    parts = []
    for r in range(R_GRP):
        qr = q[:, r * DH_B:(r + 1) * DH_B]
        parts.append(qr if extra is None else jnp.concatenate([qr, extra], axis=1))
    return jnp.concatenate(parts, axis=0)


def _unstack_heads(o_ref, o, tq):
    for r in range(R_GRP):
        o_ref[0, :, r * DH_B:(r + 1) * DH_B] = o[r * tq:(r + 1) * tq].astype(o_ref.dtype)


def _selected_branch(i, q, mb, ks_ref, oh_ref, vs_ref, tq):
    rows = R_GRP * tq
    qa = _stack_heads(q, mb)

    def scores(j):
        keys = pl.ds(pl.multiple_of(j * tq, tq), tq)
        k = jnp.concatenate([ks_ref[0, keys, :], oh_ref[keys, :]], axis=1)
        return lax.dot_general(qa, k, (((1,), (1,)), ((), ())), preferred_element_type=F32)

    def update(j, s, carry):
        m, l, acc = carry
        v = vs_ref[0, pl.ds(pl.multiple_of(j * tq, tq), tq), :]
        m_new = jnp.maximum(m, jnp.max(s, axis=1, keepdims=True))
        alpha = jnp.exp(m - m_new)
        p = jnp.exp(s - m_new)
        l = alpha * l + jnp.sum(p, axis=1, keepdims=True)
        acc = alpha * acc + jnp.dot(p.astype(BF16), v, preferred_element_type=F32)
        return m_new, l, acc

    init = (jnp.full((rows, 1), -jnp.inf, F32), jnp.zeros((rows, 1), F32), jnp.zeros((rows, DH_B), F32))
    carry = lax.fori_loop(0, i, lambda j, c: update(j, scores(j), c), init)
    s = scores(i)
    t_loc = lax.broadcasted_iota(jnp.int32, (tq, tq), 0)
    k_loc = lax.broadcasted_iota(jnp.int32, (tq, tq), 1)
    causal = jnp.where(k_loc <= t_loc, 0.0, NEG)
    m, l, acc = update(i, s + jnp.concatenate([causal] * R_GRP, axis=0), carry)
    return acc / l


def _window_branch(i, q, kw_refs, vw_refs, tq):
    rows = R_GRP * tq
    nk = 3 * tq
    qa = _stack_heads(q, None)
    k = jnp.concatenate([r[0] for r in kw_refs], axis=0)
    v = jnp.concatenate([r[0] for r in vw_refs], axis=0)
    s = lax.dot_general(qa, k, (((1,), (1,)), ((), ())), preferred_element_type=F32)
    t_loc = lax.broadcasted_iota(jnp.int32, (tq, nk), 0)
    k_loc = lax.broadcasted_iota(jnp.int32, (tq, nk), 1)
    dist = t_loc + 2 * tq - k_loc
    bound = jnp.minimum(t_loc[:, 0:1] + (i * tq + 1), WINDOW)
    bias = jnp.where(dist.astype(jnp.uint32) < bound.astype(jnp.uint32), 0.0, NEG)
    s = s + jnp.concatenate([bias] * R_GRP, axis=0)
    p = jnp.exp(s - jnp.max(s, axis=1, keepdims=True))
    l = jnp.sum(p, axis=1, keepdims=True)
    return jnp.dot(p.astype(BF16), v, preferred_element_type=F32) / l


def _nsa_local_kernel(q_ref, mb_ref, ks_ref, oh_ref, vs_ref, kw0, kw1, kw2, vw0, vw1, vw2, oc_ref, gate_ref, o_ref,
                      *, tq, gate_lane0):
    g, i = pl.program_id(1), pl.program_id(2)
    q = q_ref[0]
    o_slc = _selected_branch(i, q, mb_ref[0, 0], ks_ref, oh_ref, vs_ref, tq)
    o_win = _window_branch(i, q, (kw0, kw1, kw2), (vw0, vw1, vw2), tq)
    gates = jax.nn.sigmoid(gate_ref[0])
    lane = lax.broadcasted_iota(jnp.int32, gates.shape, 1)
    pick = lambda idx: jnp.sum(jnp.where(lane == idx, gates, 0.0), axis=1, keepdims=True)
    for r in range(R_GRP):
        base = gate_lane0 + (g * R_GRP + r) * 3
        rows = slice(r * tq, (r + 1) * tq)
        cols = slice(r * DH_B, (r + 1) * DH_B)
        o = (pick(base) * oc_ref[0, :, cols].astype(F32) + pick(base + 1) * o_slc[rows] + pick(base + 2) * o_win[rows])
        o_ref[0, :, cols] = o.astype(o_ref.dtype)


def _nsa_local(q, maskbias, ks, main, kw, o_cmp, small, col, gate_lane0, bsz, s):
    tq = ATT_TQ
    assert 2 * tq >= WINDOW
    onehot = jnp.asarray(np.arange(s)[:, None] // L_SLC == np.arange(LANES)[None, :], BF16)
    qspec = pl.BlockSpec((1, tq, R_GRP * DH_B), lambda b, g, i: (b, i, g))
    seq = lambda c0: pl.BlockSpec((1, s, DH_B), lambda b, g, i: (b, 0, c0 // DH_B + g))
    back = lambda c0, n: pl.BlockSpec((1, tq, DH_B), lambda b, g, i: (b, jnp.maximum(i - n, 0), c0 // DH_B + g))
    return pl.pallas_call(
        functools.partial(_nsa_local_kernel, tq=tq, gate_lane0=gate_lane0),
        out_shape=jax.ShapeDtypeStruct((bsz, s, HB * DH_B), BF16),
        grid=(bsz, G_KV, s // tq),
        in_specs=[qspec,
                  pl.BlockSpec((1, 1, tq, LANES), lambda b, g, i: (b, g, i, 0)),
                  seq(0), pl.BlockSpec((s, LANES), lambda b, g, i: (0, 0)), seq(col["vs"]),
                  back(0, 2), back(0, 1), back(0, 0),
                  back(col["vw"], 2), back(col["vw"], 1), back(col["vw"], 0),
                  qspec, pl.BlockSpec((1, tq, LANES), lambda b, g, i: (b, i, 0))],
        out_specs=qspec,
        compiler_params=_cparams(("parallel", "parallel", "arbitrary")),
        name="nsa_selected_window_combine",
    )(q, maskbias, ks, onehot, main, kw, kw, kw, main, main, main, o_cmp, small)


def _moe_sub_blocks(nv_ref, out_ref, compute):
    nsub = (nv_ref[pl.program_id(0)] + MOE_SUB - 1) // MOE_SUB
    rows_of = lambda sb: pl.ds(pl.multiple_of(sb * MOE_SUB, MOE_SUB), MOE_SUB)

    def live(sb, carry):
        out_ref[rows_of(sb), :] = compute(rows_of(sb))
        return carry

    def dead(sb, carry):
        out_ref[rows_of(sb), :] = jnp.zeros((MOE_SUB, out_ref.shape[1]), out_ref.dtype)
        return carry

    lax.fori_loop(0, nsub, live, 0)
    lax.fori_loop(nsub, MOE_TM // MOE_SUB, dead, 0)


def _moe_up_kernel(be_ref, nv_ref, x_ref, wg_ref, wu_ref, bg_ref, bu_ref, h_ref):
    wgb = wg_ref[0].astype(BF16)
    wub = wu_ref[0].astype(BF16)

    def compute(rows):
        x = jnp.concatenate(_unpack_bf16_pairs(x_ref[rows, :]), axis=1).astype(BF16)
        gate = jnp.dot(x, wgb, preferred_element_type=F32) + bg_ref[0]
        up = jnp.dot(x, wub, preferred_element_type=F32) + bu_ref[0]
        gate = jnp.minimum(gate, SWIGLU_LIMIT)
        up = jnp.clip(up, -SWIGLU_LIMIT, SWIGLU_LIMIT)
        return ((up + 1.0) * gate * jax.nn.sigmoid(SWIGLU_ALPHA * gate)).astype(h_ref.dtype)

    _moe_sub_blocks(nv_ref, h_ref, compute)


def _moe_down_kernel(be_ref, nv_ref, h_ref, wd_ref, bd_ref, y_ref):
    wdb = wd_ref[0].astype(BF16)
    half = wdb.shape[1] // 2

    def compute(rows):
        y = jnp.dot(h_ref[rows, :], wdb, preferred_element_type=F32) + bd_ref[0]
        return _pack_bf16_pairs(y[:, :half], y[:, half:])

    _moe_sub_blocks(nv_ref, y_ref, compute)


def _moe_experts(rows, block_e, n_valid, w_gate, b_gate, w_up, b_up, w_down, b_down):
    n_rows, d = rows.shape[0], 2 * rows.shape[1]
    n_blocks = n_rows // MOE_TM
    tn = MOE_TN
    last = D_FF // tn - 1

    def col(i, n, nv):
        return jnp.where(nv[i] > 0, n, last)

    tile = lambda width: pl.BlockSpec((MOE_TM, width), lambda i, n, be, nv: (i, 0))
    wcol = lambda k: pl.BlockSpec((1, k, tn), lambda i, n, be, nv: (be[i], 0, col(i, n, nv)))
    bcol = pl.BlockSpec((1, 1, tn), lambda i, n, be, nv: (be[i], 0, col(i, n, nv)))
    ocol = pl.BlockSpec((MOE_TM, tn), lambda i, n, be, nv: (i, n))
    params = _cparams(("arbitrary", "arbitrary"))
    h = pl.pallas_call(
        _moe_up_kernel,
        out_shape=jax.ShapeDtypeStruct((n_rows, D_FF), BF16),
        grid_spec=pltpu.PrefetchScalarGridSpec(
            num_scalar_prefetch=2, grid=(n_blocks, D_FF // tn),
            in_specs=[tile(d // 2), wcol(d), wcol(d), bcol, bcol], out_specs=ocol),
        compiler_params=params, name="moe_expert_up",
    )(block_e, n_valid, rows, w_gate, w_up, b_gate.reshape(N_EXP, 1, D_FF), b_up.reshape(N_EXP, 1, D_FF))
    return pl.pallas_call(
        _moe_down_kernel,
        out_shape=jax.ShapeDtypeStruct((n_rows, d // 2), jnp.uint32),
        grid_spec=pltpu.PrefetchScalarGridSpec(
            num_scalar_prefetch=2, grid=(n_blocks, d // tn),
            in_specs=[tile(D_FF), wcol(D_FF), bcol],
            out_specs=pl.BlockSpec((MOE_TM, tn // 2), lambda i, n, be, nv: (i, n))),
        compiler_params=params, name="moe_expert_down",
    )(block_e, n_valid, h, w_down, b_down.reshape(N_EXP, 1, d))


ROUTE_TT = 512
MOVE_TT = 256
HALF = D_MODEL // 2


def _pack_bf16_pairs(lo, hi):
    as_bits = lambda v: pltpu.bitcast(v.astype(BF16).astype(F32), jnp.uint32)
    return (as_bits(lo) >> 16) | (as_bits(hi) & jnp.uint32(0xFFFF0000))


def _unpack_bf16_pairs(w):
    return pltpu.bitcast(w << 16, F32), pltpu.bitcast(w & jnp.uint32(0xFFFF0000), F32)


def _route_kernel(h_ref, wr_ref, br_ref, e_ref, w_ref, p_ref, cnt_ref, run_ref, *, tt):
    @pl.when(pl.program_id(0) == 0)
    def _():
        run_ref[...] = jnp.zeros_like(run_ref)

    h = h_ref[...]
    logits = (jnp.dot(h, wr_ref[0], preferred_element_type=F32) + jnp.dot(h, wr_ref[1], preferred_element_type=F32)
              + br_ref[...])
    lane = lax.broadcasted_iota(jnp.int32, (tt, LANES), 1)
    v = jnp.where(lane < N_EXP, logits, -BIG)
    tops, hits, firsts = [], [], []
    for _ in range(TOP_K):
        m = jnp.max(v, axis=1, keepdims=True)
        first = jnp.min(jnp.where(v == m, lane, LANES), axis=1, keepdims=True)
        hit = lane == first
        v = jnp.where(hit, DROPPED, v)
        tops.append(m), hits.append(hit), firsts.append(first)
    ex = [jnp.exp(m - tops[0]) for m in tops]
    inv = 1.0 / functools.reduce(lambda a, b: a + b, ex)
    onehot = functools.reduce(lambda a, b: a + b, [jnp.where(hh, 1.0, 0.0) for hh in hits]).astype(BF16)
    ri = lax.broadcasted_iota(jnp.int32, (tt, tt), 0)
    ci = lax.broadcasted_iota(jnp.int32, (tt, tt), 1)
    before = jnp.where(ci < ri, 1.0, 0.0).astype(BF16)
    rank = jnp.dot(before, onehot, preferred_element_type=F32) + run_ref[0:1]
    run_ref[...] = run_ref[...] + jnp.dot(jnp.ones((8, tt), BF16), onehot, preferred_element_type=F32)
    cnt_ref[...] = run_ref[...].astype(jnp.int32)
    place = lambda cols, zero: functools.reduce(
        lambda acc, kc: jnp.where(lane == kc[0], kc[1], acc), list(enumerate(cols)), zero)
    e_ref[...] = place(firsts, jnp.zeros((tt, LANES), jnp.int32))
    w_ref[...] = place([e * inv for e in ex], jnp.zeros((tt, LANES), F32))
    pos = [jnp.sum(jnp.where(hh, rank, 0.0), axis=1, keepdims=True).astype(jnp.int32) for hh in hits]
    p_ref[...] = place(pos, jnp.zeros((tt, LANES), jnp.int32))


def _route(h, w_router, b_router):
    n, d = h.shape
    tt = ROUTE_TT
    wr = jnp.pad(w_router, ((0, 0), (0, LANES - N_EXP)))
    hi = wr.astype(BF16)
    wr2 = jnp.stack([hi, (wr - hi.astype(F32)).astype(BF16)])
    br = jnp.pad(b_router, (0, LANES - N_EXP)).reshape(1, LANES)
    tok = lambda dt: jax.ShapeDtypeStruct((n, LANES), dt)
    tspec = pl.BlockSpec((tt, LANES), lambda i: (i, 0))
    return pl.pallas_call(
        functools.partial(_route_kernel, tt=tt),
        out_shape=(tok(jnp.int32), tok(F32), tok(jnp.int32), jax.ShapeDtypeStruct((8, LANES), jnp.int32)),
        grid=(n // tt,),
        in_specs=[pl.BlockSpec((tt, d), lambda i: (i, 0)), pl.BlockSpec((2, d, LANES), lambda i: (0, 0, 0)),
                  pl.BlockSpec((1, LANES), lambda i: (0, 0))],
        out_specs=(tspec, tspec, tspec, pl.BlockSpec((8, LANES), lambda i: (0, 0))),
        scratch_shapes=[pltpu.VMEM((8, LANES), F32)],
        compiler_params=_cparams(("arbitrary",)),
        name="moe_route",
    )(h, wr2, br)


def _dispatch_kernel(dest_ref, src_ref, init_ref, rows_ref, sem, *, tt):
    del init_ref
    t0 = pl.program_id(0) * tt

    def copy(j, k):
        return pltpu.make_async_copy(src_ref.at[pl.ds(j, 1)],
                                     rows_ref.at[pl.ds(dest_ref[(t0 + j) * TOP_K + k], 1)], sem)

    def start(j, carry):
        for k in range(TOP_K):
            copy(j, k).start()
        return carry

    def wait(j, carry):
        for k in range(TOP_K):
            copy(j, k).wait()
        return carry

    lax.fori_loop(0, tt, start, 0)
    lax.fori_loop(0, tt, wait, 0)


def _dispatch(dest, hpk, n_rows):
    n, width = hpk.shape
    tt = MOVE_TT
    return pl.pallas_call(
        functools.partial(_dispatch_kernel, tt=tt),
        out_shape=jax.ShapeDtypeStruct((n_rows, width), jnp.uint32),
        grid_spec=pltpu.PrefetchScalarGridSpec(
            num_scalar_prefetch=1, grid=(n // tt,),
            in_specs=[pl.BlockSpec((tt, width), lambda i, dest: (i, 0)), pl.BlockSpec(memory_space=pl.ANY)],
            out_specs=pl.BlockSpec(memory_space=pl.ANY),
            scratch_shapes=[pltpu.SemaphoreType.DMA]),
        input_output_aliases={2: 0},
        compiler_params=_cparams(("arbitrary",)),
        name="moe_dispatch",
    )(dest, hpk, jnp.zeros((n_rows, width), jnp.uint32))


def _combine_kernel(dest_ref, y_ref, w_ref, x1_ref, g_ref, o_ref, buf_ref, sem, *, tt, norm):
    t0 = pl.program_id(0) * tt

    def copy(j, k):
        return pltpu.make_async_copy(y_ref.at[pl.ds(dest_ref[(t0 + j) * TOP_K + k], 1)],
                                     buf_ref.at[k, pl.ds(j, 1)], sem)

    def start(j, carry):
        for k in range(TOP_K):
            copy(j, k).start()
        return carry

    def wait(j, carry):
        for k in range(TOP_K):
            copy(j, k).wait()
        return carry

    lax.fori_loop(0, tt, start, 0)
    lax.fori_loop(0, tt, wait, 0)
    wts = w_ref[...]
    lane = lax.broadcasted_iota(jnp.int32, wts.shape, 1)
    x = x1_ref[...]
    nq = MOE_TN // 2
    for k in range(TOP_K):
        wk = jnp.sum(jnp.where(lane == k, wts, 0.0), axis=1, keepdims=True)
        parts = []
        for n in range(D_MODEL // MOE_TN):
            lo, hi = _unpack_bf16_pairs(buf_ref[k, :, n * nq:(n + 1) * nq])
            parts += [lo, hi]
        x = x + wk * jnp.concatenate(parts, axis=1)
    if norm:
        x = x * lax.rsqrt(jnp.mean(x * x, axis=-1, keepdims=True) + EPS) * g_ref[...]
    o_ref[...] = x.astype(o_ref.dtype)


def _combine(dest, ypk, wts, x1, gain):
    n, d = x1.shape
    tt = MOVE_TT
    norm = gain is not None
    gain = gain if norm else jnp.ones((d,), F32)
    return pl.pallas_call(
        functools.partial(_combine_kernel, tt=tt, norm=norm),
        out_shape=jax.ShapeDtypeStruct((n, d), F32),
        grid_spec=pltpu.PrefetchScalarGridSpec(
            num_scalar_prefetch=1, grid=(n // tt,),
            in_specs=[pl.BlockSpec(memory_space=pl.ANY),
                      pl.BlockSpec((tt, LANES), lambda i, dest: (i, 0)),
                      pl.BlockSpec((tt, d), lambda i, dest: (i, 0)),
                      pl.BlockSpec((1, d), lambda i, dest: (0, 0))],
            out_specs=pl.BlockSpec((tt, d), lambda i, dest: (i, 0)),
            scratch_shapes=[pltpu.VMEM((TOP_K, tt, d // 2), jnp.uint32), pltpu.SemaphoreType.DMA]),
        compiler_params=_cparams(("arbitrary",)),
        name="moe_combine_norm",
    )(dest, ypk, wts, x1, gain.reshape(1, d))


def _native_sparse_attention(main, small, positions, pe_ck, w1_ck, w2_ck, pe_cv, w1_cv, w2_cv, col, gate_lane0, bsz, s):
    half = ROT_DIM // 2
    inv_freq = ROPE_THETA ** (-jnp.arange(half, dtype=F32) * 2.0 / ROT_DIM)
    ang = positions.astype(F32)[..., None] * inv_freq
    cos, sin = jnp.cos(ang), jnp.sin(ang)
    rest = (bsz, s, DH_B - ROT_DIM)
    cos_t = jnp.concatenate([cos, cos, jnp.ones(rest, F32)], axis=-1)
    sin_t = jnp.concatenate([-sin, sin, jnp.zeros(rest, F32)], axis=-1)
    q, kc_t, vc_t, ks, kw = _nsa_rope(main, cos_t, sin_t, col, bsz, s)
    k_cmp, v_cmp = _nsa_compress(kc_t, vc_t, pe_ck, w1_ck, w2_ck, pe_cv, w1_cv, w2_cv, bsz, s)
    o_cmp, maskbias = _nsa_cmp_select(q, k_cmp, v_cmp, bsz, s)
    return _nsa_local(q, maskbias, ks, main, kw, o_cmp, small, col, gate_lane0, bsz, s)


def _moe_ffn(hf, hpk, x1, final_gain, w_router, b_router, w_gate, b_gate, w_up, b_up, w_down, b_down):
    n_tok, d = x1.shape
    eidx, wts, pos, cnt = _route(hf, w_router, b_router)
    counts = cnt[0, :N_EXP]
    padded = ((counts + MOE_TM - 1) // MOE_TM) * MOE_TM
    pad_end = jnp.cumsum(padded)
    pad_start = pad_end - padded
    group0 = jnp.sum(jnp.where(eidx[:, :TOP_K, None] == jnp.arange(N_EXP), pad_start, 0), axis=-1)
    dest = (group0 + pos[:, :TOP_K]).reshape(-1).astype(jnp.int32)
    n_blocks = (n_tok * TOP_K + N_EXP * (MOE_TM - 1) + MOE_TM - 1) // MOE_TM
    rows = _dispatch(dest, hpk, n_blocks * MOE_TM)
    tile0 = jnp.arange(n_blocks, dtype=jnp.int32) * MOE_TM
    block_e = jnp.minimum(jnp.searchsorted(pad_end, tile0, side='right'), N_EXP - 1).astype(jnp.int32)
    n_valid = jnp.clip(pad_start[block_e] + counts[block_e] - tile0, 0, MOE_TM).astype(jnp.int32)
    ypk = _moe_experts(rows, block_e, n_valid, w_gate, b_gate, w_up, b_up, w_down, b_down)
    return _combine(dest, ypk, wts, x1, final_gain)


def _layer(x, positions, norm_mix, w_in, conv_w, a_log, dt_bias, norm_gdn, pe_ck, w1_ck, w2_ck,
           pe_cv, w1_cv, w2_cv, w_proj_a, w_proj_b, w_out, norm_ffn, w_router, b_router,
           w_gate, b_gate, w_up, b_up, w_down, b_down, final_gain):
    bsz, s, d = x.shape
    n_tok = bsz * s
    x2 = x.reshape(n_tok, d)
    h = _rmsnorm(x2, norm_mix, BF16)
    n_small = 2 * HA + 3 * HB
    sp = SPLIT_POINTS
    w_main = jnp.concatenate([w_in[:, :sp[3]], w_in[:, sp[5]:sp[12]], w_in[:, sp[13]:]], axis=1).astype(BF16)
    w_small = jnp.concatenate([w_in[:, sp[3]:sp[5]], w_in[:, sp[12]:sp[13]]], axis=1)
    w_small = jnp.pad(w_small, ((0, 0), (0, LANES - n_small))).astype(BF16)
    main = _matmul(h, w_main, BF16).reshape(bsz, s, -1)
    small = _matmul(h, w_small, F32).reshape(bsz, s, -1)
    names = ("qa", "ka", "va", "za", "qb", "kc", "vc", "ks", "vs", "kw", "vw", "gm")
    sizes = (HA * DK_A, HA * DK_A, HA * DV_A, HA * DV_A,
             HB * DH_B, G_KV * DH_B, G_KV * DH_B, G_KV * DH_B, G_KV * DH_B, G_KV * DH_B, G_KV * DH_B, 2 * D_MODEL)
    col = {nm: sum(sizes[:i]) for i, nm in enumerate(names)}
    aa, ba = small[..., :HA], small[..., HA:2 * HA]

    o_a = _gated_deltanet(main, aa, ba, conv_w, a_log, dt_bias, norm_gdn, bsz, s)
    o_b = _native_sparse_attention(main, small, positions, pe_ck, w1_ck, w2_ck, pe_cv, w1_cv, w2_cv,
                                   col, 2 * HA, bsz, s)
    merged = _merge(o_a.reshape(n_tok, d), o_b.reshape(n_tok, d), w_proj_a.astype(BF16),
                    w_proj_b.astype(BF16), main.reshape(n_tok, -1), col["gm"])
    x1, hf, hpk = _outproj(merged, w_out.astype(BF16), x2, norm_ffn)
    out = _moe_ffn(hf, hpk, x1, final_gain, w_router, b_router, w_gate, b_gate, w_up, b_up, w_down, b_down)
    return out.reshape(bsz, s, d)


def kernel(x, positions, norm_mix, w_in, conv_w, a_log, dt_bias, norm_gdn, pe_ck, w1_ck, w2_ck, pe_cv, w1_cv, w2_cv, w_proj_a, w_proj_b, w_out, norm_ffn, w_router, b_router, w_gate, b_gate, w_up, b_up, w_down, b_down, norm_final):
    depth = norm_mix.shape[0]
    for l in range(depth):
        x = _layer(x, positions, norm_mix[l], w_in[l], conv_w[l], a_log[l], dt_bias[l], norm_gdn[l],
                   pe_ck[l], w1_ck[l], w2_ck[l], pe_cv[l], w1_cv[l], w2_cv[l],
                   w_proj_a[l], w_proj_b[l], w_out[l], norm_ffn[l], w_router[l], b_router[l],
                   w_gate[l], b_gate[l], w_up[l], b_up[l], w_down[l], b_down[l],
                   norm_final if l + 1 == depth else None)
    return x
```

```python
import functools
import math

import jax
import jax.numpy as jnp
import numpy as np
from jax import lax
from jax.experimental import pallas as pl
from jax.experimental.pallas import tpu as pltpu

F32 = jnp.float32
BF16 = jnp.bfloat16

D_MODEL = 2048
EPS = 1e-6
NEG = -1e30
HA = D_MODEL // 128
DK_A = 128
DV_A = 128
CONV_W = 4
CHUNK = 64
HB = D_MODEL // 128
G_KV = 4
R_GRP = HB // G_KV
DH_B = 128
ROT_DIM = DH_B // 4
ROPE_THETA = 500000.0
L_CMP = 32
STRIDE_CMP = 16
L_SLC = 64
T_SEL = 8
WINDOW = 512
N_EXP = 32
TOP_K = 4
D_FF = D_MODEL
SWIGLU_LIMIT = 7.0
SWIGLU_ALPHA = 1.702
SPLIT_SIZES = (HA * DK_A, HA * DK_A, HA * DV_A, HA * DV_A, HA, HA,
               HB * DH_B, G_KV * DH_B, G_KV * DH_B, G_KV * DH_B, G_KV * DH_B, G_KV * DH_B, G_KV * DH_B,
               3 * HB, 2 * D_MODEL)
SPLIT_POINTS = tuple(sum(SPLIT_SIZES[:i + 1]) for i in range(len(SPLIT_SIZES) - 1))

V7X_VMEM_LIMIT_BYTES = 56 * 1024 * 1024
LANES = 128
MOE_TM = 1024
MOE_SUB = 256
MOE_TN = 512
ATT_TQ = 256


def _cparams(sem):
    return pltpu.CompilerParams(dimension_semantics=sem, vmem_limit_bytes=V7X_VMEM_LIMIT_BYTES)


def _mm_kernel(x_ref, w_ref, o_ref):
    o_ref[...] = jnp.dot(x_ref[...], w_ref[...], preferred_element_type=F32).astype(o_ref.dtype)


def _matmul(x, w, out_dtype, tm=1024, tn=1024):
    m, k = x.shape
    n = w.shape[1]
    tm, tn = min(tm, m), min(tn, n)
    assert m % tm == 0 and n % tn == 0
    return pl.pallas_call(
        _mm_kernel,
        out_shape=jax.ShapeDtypeStruct((m, n), out_dtype),
        grid=(n // tn, m // tm),
        in_specs=[pl.BlockSpec((tm, k), lambda j, i: (i, 0)),
                  pl.BlockSpec((k, tn), lambda j, i: (0, j))],
        out_specs=pl.BlockSpec((tm, tn), lambda j, i: (i, j)),
        compiler_params=_cparams(("parallel", "parallel")),
        name="dense_matmul",
    )(x, w)


def _rmsnorm_kernel(x_ref, g_ref, o_ref):
    x = x_ref[...]
    y = x * lax.rsqrt(jnp.mean(x * x, axis=-1, keepdims=True) + EPS)
    o_ref[...] = (y * g_ref[...]).astype(o_ref.dtype)


def _rmsnorm(x, gain, out_dtype, tm=512):
    m, d = x.shape
    return pl.pallas_call(
        _rmsnorm_kernel,
        out_shape=jax.ShapeDtypeStruct((m, d), out_dtype),
        grid=(m // tm,),
        in_specs=[pl.BlockSpec((tm, d), lambda i: (i, 0)),
                  pl.BlockSpec((1, d), lambda i: (0, 0))],
        out_specs=pl.BlockSpec((tm, d), lambda i: (i, 0)),
        compiler_params=_cparams(("parallel",)),
        name="rmsnorm",
    )(x, gain.reshape(1, d))


def _add_rmsnorm_kernel(x_ref, y_ref, g_ref, o_ref):
    x = x_ref[...] + y_ref[...]
    y = x * lax.rsqrt(jnp.mean(x * x, axis=-1, keepdims=True) + EPS)
    o_ref[...] = (y * g_ref[...]).astype(o_ref.dtype)


def _add_rmsnorm(x, y, gain, out_dtype, tm=512):
    m, d = x.shape
    row = pl.BlockSpec((tm, d), lambda i: (i, 0))
    return pl.pallas_call(
        _add_rmsnorm_kernel,
        out_shape=jax.ShapeDtypeStruct((m, d), out_dtype),
        grid=(m // tm,),
        in_specs=[row, row, pl.BlockSpec((1, d), lambda i: (0, 0))],
        out_specs=row,
        compiler_params=_cparams(("parallel",)),
        name="add_rmsnorm",
    )(x, y, gain.reshape(1, d))


def _merge_kernel(oa_ref, ob_ref, wa_ref, wb_ref, ga_ref, gb_ref, o_ref):
    ya = jnp.dot(oa_ref[...], wa_ref[...], preferred_element_type=F32)
    yb = jnp.dot(ob_ref[...], wb_ref[...], preferred_element_type=F32)
    o_ref[...] = (jax.nn.sigmoid(ga_ref[...].astype(F32)) * ya
                  + jax.nn.sigmoid(gb_ref[...].astype(F32)) * yb).astype(o_ref.dtype)


def _merge(o_a, o_b, w_a, w_b, main2d, gm_col0, tm=512, tn=1024):
    m, d = o_a.shape
    assert gm_col0 % tn == 0 and d % tn == 0
    g0 = gm_col0 // tn
    lhs = pl.BlockSpec((tm, d), lambda j, i: (i, 0))
    rhs = pl.BlockSpec((d, tn), lambda j, i: (0, j))
    return pl.pallas_call(
        _merge_kernel,
        out_shape=jax.ShapeDtypeStruct((m, d), BF16),
        grid=(d // tn, m // tm),
        in_specs=[lhs, lhs, rhs, rhs,
                  pl.BlockSpec((tm, tn), lambda j, i: (i, g0 + j)),
                  pl.BlockSpec((tm, tn), lambda j, i: (i, g0 + d // tn + j))],
        out_specs=pl.BlockSpec((tm, tn), lambda j, i: (i, j)),
        compiler_params=_cparams(("parallel", "parallel")),
        name="mixer_merge",
    )(o_a, o_b, w_a, w_b, main2d, main2d)


def _outproj_kernel(m_ref, w_ref, x_ref, g_ref, x1_ref, h_ref, hpk_ref):
    x1 = x_ref[...] + jnp.dot(m_ref[...], w_ref[...], preferred_element_type=F32)
    x1_ref[...] = x1
    y = x1 * lax.rsqrt(jnp.mean(x1 * x1, axis=-1, keepdims=True) + EPS) * g_ref[...]
    h_ref[...] = y.astype(h_ref.dtype)
    half = y.shape[1] // 2
    hpk_ref[...] = _pack_bf16_pairs(y[:, :half], y[:, half:])


def _outproj(merged, w_out, x, gain, tm=512):
    m, d = x.shape
    row = lambda width=d: pl.BlockSpec((tm, width), lambda i: (i, 0))
    return pl.pallas_call(
        _outproj_kernel,
        out_shape=(jax.ShapeDtypeStruct((m, d), F32), jax.ShapeDtypeStruct((m, d), BF16),
                   jax.ShapeDtypeStruct((m, d // 2), jnp.uint32)),
        grid=(m // tm,),
        in_specs=[row(), pl.BlockSpec((d, d), lambda i: (0, 0)), row(), pl.BlockSpec((1, d), lambda i: (0, 0))],
        out_specs=(row(), row(), row(d // 2)),
        compiler_params=_cparams(("parallel",)),
        name="out_proj_residual_norm",
    )(merged, w_out, x, gain.reshape(1, d))


GDN_COLS = 512
GDN_GROUP = 256
GDN_PH = 4
GDN_TS = 512


def _gdn_conv_kernel(x_ref, w_ref, o_ref):
    sec = pl.program_id(1) // (HA * DK_A // GDN_COLS)
    x = x_ref[0].astype(F32)
    w = w_ref[...]
    row = lax.broadcasted_iota(jnp.int32, x.shape, 0)
    y = x * w[CONV_W - 1:CONV_W]
    for i in range(CONV_W - 1):
        sh = CONV_W - 1 - i
        y = y + jnp.where(row >= sh, pltpu.roll(x, sh, axis=0), 0.0) * w[i:i + 1]
    y = y * jax.nn.sigmoid(y)
    qscale = jnp.where(sec == 0, DK_A ** -0.5, 1.0)
    for h in range(GDN_COLS // DK_A):
        yh = y[:, h * DK_A:(h + 1) * DK_A]
        nrm = yh * (lax.rsqrt(jnp.sum(yh * yh, axis=-1, keepdims=True) + EPS) * qscale)
        o_ref[0, 0, h] = jnp.where(sec < 2, nrm, yh).astype(o_ref.dtype)


def _gdn_conv(main, conv_w, bsz, s):
    ncol = 3 * HA * DK_A // GDN_COLS
    hpc = GDN_COLS // DK_A
    return pl.pallas_call(
        _gdn_conv_kernel,
        out_shape=jax.ShapeDtypeStruct((3, bsz, HA, s, DK_A), BF16),
        grid=(bsz, ncol),
        in_specs=[pl.BlockSpec((1, s, GDN_COLS), lambda b, c: (b, 0, c)),
                  pl.BlockSpec((CONV_W, GDN_COLS), lambda b, c: (0, c))],
        out_specs=pl.BlockSpec((1, 1, hpc, s, DK_A), lambda b, c: (c // (HA // hpc), b, c % (HA // hpc), 0, 0)),
        compiler_params=_cparams(("parallel", "parallel")),
        name="gdn_conv_silu_l2norm",
    )(main, conv_w)


def _col_rep(row, n):
    return jnp.broadcast_to(row, (LANES, n)).T


def _dot_hilo(x, m):
    hi = x.astype(BF16)
    lo = (x - hi.astype(F32)).astype(BF16)
    return jnp.dot(hi, m, preferred_element_type=F32) + jnp.dot(lo, m, preferred_element_type=F32)


def _gdn_prep_kernel(q_ref, k_ref, v_ref, g_ref, b_ref, u_ref, w_ref, qg_ref, kd_ref, a_ref, egl_ref):
    n = GDN_GROUP
    ri = lax.broadcasted_iota(jnp.int32, (n, n), 0)
    ci = lax.broadcasted_iota(jnp.int32, (n, n), 1)
    same = (ri // CHUNK) == (ci // CHUNK)
    incl = same & (ri >= ci)
    strict = same & (ri > ci)
    one_if = lambda m: jnp.where(m, 1.0, 0.0).astype(BF16)
    cum_m, tot_m = one_if(same & (ri <= ci)), one_if(same)
    eye = jnp.where(ri == ci, 1.0, 0.0)
    wide = lambda c: jnp.concatenate([c] * (n // LANES), axis=1)
    nt = (((1,), (1,)), ((), ()))
    heads = range(GDN_PH)
    ts, ps, rhs = [], [], []
    for h in heads:
        q, k, v = q_ref[0, 0, h], k_ref[0, 0, h], v_ref[0, 0, h]
        g8 = jnp.broadcast_to(g_ref[0, h], (8, n))
        gc_row = _dot_hilo(g8, cum_m)[0:1]
        gl_row = _dot_hilo(g8, tot_m)[0:1]
        gc_c, gl_c, b_c = _col_rep(gc_row, n), _col_rep(gl_row, n), _col_rep(b_ref[0, h], n)
        decay = jnp.exp(jnp.where(incl, wide(gc_c) - gc_row, NEG))
        kk = lax.dot_general(k, k, nt, preferred_element_type=F32)
        qk = lax.dot_general(q, k, nt, preferred_element_type=F32)
        xb = jnp.where(strict, -(kk * wide(b_c) * decay), 0.0).astype(BF16)
        a = qk * decay
        kf = k.astype(F32)
        egc = jnp.exp(gc_c)
        qg_ref[0, h] = (q.astype(F32) * egc).astype(qg_ref.dtype)
        kd_ref[0, h] = (kf * jnp.exp(gl_c - gc_c)).astype(kd_ref.dtype)
        for c in range(n // CHUNK):
            blk = slice(c * CHUNK, (c + 1) * CHUNK)
            a_ref[0, h, blk, :] = a[blk, blk].astype(a_ref.dtype)
        egl = jnp.exp(gl_c)
        egl_ref[0, h, 0] = jnp.concatenate([egl[c * CHUNK:c * CHUNK + 1] for c in range(n // CHUNK)], axis=0)
        rhs.append(jnp.concatenate([(v.astype(F32) * b_c).astype(BF16), (kf * b_c * egc).astype(BF16)], axis=1))
        ts.append(eye + xb.astype(F32))
        ps.append(xb)
    ps = [jnp.dot(p, p, preferred_element_type=F32).astype(BF16) for p in ps]
    for step in range(5):
        for h in heads:
            if step < 4:
                tp = jnp.dot(jnp.concatenate([ts[h].astype(BF16), ps[h]], axis=0), ps[h], preferred_element_type=F32)
                ts[h] = ts[h] + tp[:n]
                ps[h] = tp[n:].astype(BF16)
            else:
                ts[h] = ts[h] + jnp.dot(ts[h].astype(BF16), ps[h], preferred_element_type=F32)
    for h in heads:
        uw = jnp.dot(ts[h].astype(BF16), rhs[h], preferred_element_type=F32)
        u_ref[0, h] = uw[:, :DV_A].astype(u_ref.dtype)
        w_ref[0, h] = uw[:, DV_A:].astype(w_ref.dtype)


def _gdn_prep(qkv, g_t, beta_t, bsz, s):
    n, ph = GDN_GROUP, GDN_PH
    tok = lambda width, dt: jax.ShapeDtypeStruct((bsz, HA, s, width), dt)
    tspec = lambda width: pl.BlockSpec((1, ph, n, width), lambda b, h, i: (b, h, i, 0))
    qspec = lambda sec: pl.BlockSpec((1, 1, ph, n, DK_A), lambda b, h, i: (sec, b, h, i, 0))
    rspec = pl.BlockSpec((1, ph, 1, n), lambda b, h, i: (b, h, 0, i))
    return pl.pallas_call(
        _gdn_prep_kernel,
        out_shape=(tok(DV_A, BF16), tok(DK_A, BF16), tok(DK_A, BF16), tok(DK_A, BF16), tok(CHUNK, BF16),
                   jax.ShapeDtypeStruct((bsz, HA, s // n, n // CHUNK, LANES), F32)),
        grid=(bsz, HA // ph, s // n),
        in_specs=[qspec(0), qspec(1), qspec(2), rspec, rspec],
        out_specs=(tspec(DV_A), tspec(DK_A), tspec(DK_A), tspec(DK_A), tspec(CHUNK),
                   pl.BlockSpec((1, ph, 1, n // CHUNK, LANES), lambda b, h, i: (b, h, i, 0, 0))),
        compiler_params=_cparams(("parallel", "parallel", "parallel")),
        name="gdn_chunk_prep",
    )(qkv, qkv, qkv, g_t, beta_t)


def _gdn_scan_kernel(u_ref, w_ref, qg_ref, kd_ref, a_ref, egl_ref, z_ref, ng_ref, o_ref, state_ref, *, nchunk):
    @pl.when(pl.program_id(1) == 0)
    def _():
        state_ref[...] = jnp.zeros_like(state_ref)

    tn = (((0,), (0,)), ((), ()))
    heads = range(HA)

    def body(c, carry):
        rows = pl.ds(pl.multiple_of(c * CHUNK, CHUNK), CHUNK)
        st = [state_ref[h] for h in heads]
        sb = [x.astype(BF16) for x in st]
        vb = [(u_ref[0, h, rows, :].astype(F32)
               - jnp.dot(w_ref[0, h, rows, :], sb[h], preferred_element_type=F32)).astype(BF16) for h in heads]
        o = [jnp.dot(qg_ref[0, h, rows, :], sb[h], preferred_element_type=F32)
             + jnp.dot(a_ref[0, h, rows, :], vb[h], preferred_element_type=F32) for h in heads]
        for h in heads:
            state_ref[h] = (st[h] * egl_ref[0, h, pl.ds(c, 1), :]
                            + lax.dot_general(kd_ref[0, h, rows, :], vb[h], tn, preferred_element_type=F32))
        for h in heads:
            cols = slice(h * DV_A, (h + 1) * DV_A)
            z = z_ref[0, rows, cols].astype(F32)
            on = o[h] * lax.rsqrt(jnp.mean(o[h] * o[h], axis=-1, keepdims=True) + EPS) * ng_ref[...]
            o_ref[0, rows, cols] = (on * (z * jax.nn.sigmoid(z))).astype(o_ref.dtype)
        return carry

    lax.fori_loop(0, nchunk, body, 0)


def _gdn_scan(u, w, qg, kd, a, egl, main, norm_gdn, bsz, s):
    ts = GDN_TS
    z_blk0 = 3 * HA * DK_A // (HA * DV_A)
    hspec = lambda width: pl.BlockSpec((1, HA, ts, width), lambda b, i: (b, 0, i, 0))
    return pl.pallas_call(
        functools.partial(_gdn_scan_kernel, nchunk=ts // CHUNK),
        out_shape=jax.ShapeDtypeStruct((bsz, s, HA * DV_A), BF16),
        grid=(bsz, s // ts),
        in_specs=[hspec(DV_A), hspec(DK_A), hspec(DK_A), hspec(DK_A), hspec(CHUNK),
                  pl.BlockSpec((1, HA, ts // CHUNK, LANES), lambda b, i: (b, 0, i, 0)),
                  pl.BlockSpec((1, ts, HA * DV_A), lambda b, i: (b, i, z_blk0)),
                  pl.BlockSpec((1, DV_A), lambda b, i: (0, 0))],
        out_specs=pl.BlockSpec((1, ts, HA * DV_A), lambda b, i: (b, i, 0)),
        scratch_shapes=[pltpu.VMEM((HA, DK_A, DV_A), F32)],
        compiler_params=_cparams(("parallel", "arbitrary")),
        name="gdn_delta_scan",
    )(u, w, qg, kd, a, egl, main, norm_gdn.reshape(1, DV_A))


def _gated_deltanet(main, a_in, b_in, conv_w, a_log, dt_bias, norm_gdn, bsz, s):
    qkv = _gdn_conv(main, conv_w, bsz, s)
    g = -jnp.exp(a_log.astype(F32)) * jax.nn.softplus(a_in.astype(F32) + dt_bias.astype(F32))
    beta = jax.nn.sigmoid(b_in.astype(F32))
    g_t = g.transpose(0, 2, 1).reshape(bsz, HA, 1, s)
    beta_t = beta.transpose(0, 2, 1).reshape(bsz, HA, 1, s)
    u, w, qg, kd, a, egl = _gdn_prep(qkv, g_t, beta_t, bsz, s)
    egl = egl.reshape(bsz, HA, s // CHUNK, LANES)
    return _gdn_scan(u, w, qg, kd, a, egl, main, norm_gdn, bsz, s)


NSA_TS = 512
BIG = 1e30
DROPPED = -3e38


def _rope(x, cos, sin, lane):
    half = ROT_DIM // 2
    partner = jnp.where(lane < half, pltpu.roll(x, DH_B - half, axis=1), pltpu.roll(x, half, axis=1))
    return x * cos + partner * sin


def _nsa_rope_kernel(q_ref, kc_ref, vc_ref, ks_ref, kw_ref, cos_ref, sin_ref,
                     qo_ref, kco_ref, vco_ref, kso_ref, kwo_ref):
    cos, sin = cos_ref[0], sin_ref[0]
    lane = lax.broadcasted_iota(jnp.int32, cos.shape, 1)
    head = lambda ref, h: ref[0, :, h * DH_B:(h + 1) * DH_B].astype(F32)
    for h in range(HB):
        qo_ref[0, :, h * DH_B:(h + 1) * DH_B] = (_rope(head(q_ref, h), cos, sin, lane) * DH_B ** -0.5).astype(qo_ref.dtype)
    for g in range(G_KV):
        cols = slice(g * DH_B, (g + 1) * DH_B)
        kco_ref[0, g] = _rope(head(kc_ref, g), cos, sin, lane).astype(kco_ref.dtype)
        vco_ref[0, g] = vc_ref[0, :, cols]
        kso_ref[0, :, cols] = _rope(head(ks_ref, g), cos, sin, lane).astype(kso_ref.dtype)
        kwo_ref[0, :, cols] = _rope(head(kw_ref, g), cos, sin, lane).astype(kwo_ref.dtype)


def _nsa_rope(main, cos, sin, col, bsz, s):
    ts = NSA_TS
    kvw = G_KV * DH_B
    tok = lambda width, c0: pl.BlockSpec((1, ts, width), lambda b, i: (b, i, c0 // width))
    tab = pl.BlockSpec((1, ts, DH_B), lambda b, i: (b, i, 0))
    grp = pl.BlockSpec((1, G_KV, ts, DH_B), lambda b, i: (b, 0, i, 0))
    flat = lambda width: pl.BlockSpec((1, ts, width), lambda b, i: (b, i, 0))
    return pl.pallas_call(
        _nsa_rope_kernel,
        out_shape=(jax.ShapeDtypeStruct((bsz, s, HB * DH_B), BF16),
                   jax.ShapeDtypeStruct((bsz, G_KV, s, DH_B), BF16), jax.ShapeDtypeStruct((bsz, G_KV, s, DH_B), BF16),
                   jax.ShapeDtypeStruct((bsz, s, kvw), BF16), jax.ShapeDtypeStruct((bsz, s, kvw), BF16)),
        grid=(bsz, s // ts),
        in_specs=[tok(HB * DH_B, col["qb"]), tok(kvw, col["kc"]), tok(kvw, col["vc"]), tok(kvw, col["ks"]),
                  tok(kvw, col["kw"]), tab, tab],
        out_specs=(flat(HB * DH_B), grp, grp, flat(kvw), flat(kvw)),
        compiler_params=_cparams(("parallel", "parallel")),
        name="nsa_rotary",
    )(main, main, main, main, main, cos, sin)


def _gelu_tanh(x):
    return 0.5 * x * (1.0 + jnp.tanh(math.sqrt(2.0 / math.pi) * (x + 0.044715 * x * x * x)))


def _nsa_compress_kernel(k_ref, v_ref, pek_ref, pev_ref, w1k_ref, w1v_ref, w2k_ref, w2v_ref, ko_ref, vo_ref):
    nseg = k_ref.shape[2]
    for x_ref, pe_ref, w1_ref, w2_ref, o_ref in ((k_ref, pek_ref, w1k_ref, w2k_ref, ko_ref),
                                                 (v_ref, pev_ref, w1v_ref, w2v_ref, vo_ref)):
        for g in range(G_KV):
            x = x_ref[0, g].astype(F32)
            lo = jnp.dot((x + pe_ref[0:1]).astype(BF16), w1_ref[0], preferred_element_type=F32)
            hi = jnp.dot((x + pe_ref[1:2]).astype(BF16), w1_ref[1], preferred_element_type=F32)
            pre = lo + pltpu.roll(hi, nseg - 1, axis=0)
            o_ref[0, g] = jnp.dot(_gelu_tanh(pre).astype(BF16), w2_ref[...],
                                  preferred_element_type=F32).astype(o_ref.dtype)


def _nsa_compress(kc_t, vc_t, pe_ck, w1_ck, w2_ck, pe_cv, w1_cv, w2_cv, bsz, s):
    assert L_CMP == 2 * STRIDE_CMP
    nseg = s // STRIDE_CMP
    width = STRIDE_CMP * DH_B
    seg = lambda t: t.reshape(bsz, G_KV, nseg, width)
    pe2 = lambda pe: pe.reshape(2, width)
    w1h = lambda w: w.reshape(2, width, DH_B).astype(BF16)
    xspec = pl.BlockSpec((1, G_KV, nseg, width), lambda b: (b, 0, 0, 0))
    pspec = pl.BlockSpec((2, width), lambda b: (0, 0))
    w1spec = pl.BlockSpec((2, width, DH_B), lambda b: (0, 0, 0))
    w2spec = pl.BlockSpec((DH_B, DH_B), lambda b: (0, 0))
    ospec = pl.BlockSpec((1, G_KV, nseg, DH_B), lambda b: (b, 0, 0, 0))
    oshape = jax.ShapeDtypeStruct((bsz, G_KV, nseg, DH_B), BF16)
    return pl.pallas_call(
        _nsa_compress_kernel,
        out_shape=(oshape, oshape),
        grid=(bsz,),
        in_specs=[xspec, xspec, pspec, pspec, w1spec, w1spec, w2spec, w2spec],
        out_specs=(ospec, ospec),
        compiler_params=_cparams(("parallel",)),
        name="nsa_compress",
    )(seg(kc_t), seg(vc_t), pe2(pe_ck), pe2(pe_cv), w1h(w1_ck), w1h(w1_cv), w2_ck.astype(BF16), w2_cv.astype(BF16))


def _nsa_cmp_kernel(q_ref, kc_ref, vc_ref, ov_ref, o_ref, mb_ref, *, tq, n_slc):
    i = pl.program_id(2)
    rows = R_GRP * tq
    ncmp = kc_ref.shape[2]
    qa = _stack_heads(q_ref[0], None)
    s = lax.dot_general(qa, kc_ref[0, 0], (((1,), (1,)), ((), ())), preferred_element_type=F32)
    t_row = (lax.broadcasted_iota(jnp.int32, (rows, ncmp), 0) & (tq - 1)) + i * tq
    c_end = lax.broadcasted_iota(jnp.int32, (rows, ncmp), 1) * STRIDE_CMP + (L_CMP - 1)
    valid = c_end <= t_row
    sm = jnp.where(valid, s, NEG)
    p = jnp.where(valid, jnp.exp(sm - jnp.max(sm, axis=1, keepdims=True)), 0.0)
    l = jnp.sum(p, axis=1, keepdims=True)
    p = p * (1.0 / jnp.where(l > 0.0, l, 1.0))
    _unstack_heads(o_ref, jnp.dot(p.astype(BF16), vc_ref[0, 0], preferred_element_type=F32), tq)
    psum = p[0:tq]
    for r in range(1, R_GRP):
        psum = psum + p[r * tq:(r + 1) * tq]
    imp = _dot_hilo(psum, ov_ref[...])
    nb = -(-n_slc // 8) * 8
    v = imp.T[:nb]
    blk = lax.broadcasted_iota(jnp.int32, (nb, tq), 0)
    cur = (lax.broadcasted_iota(jnp.int32, (nb, tq), 1) + i * tq) // L_SLC
    forced = (blk == 0) | (blk == cur)
    v = jnp.where(forced, BIG, jnp.where(blk <= cur, v, -BIG))
    sel = jnp.zeros((nb, tq), F32)
    for _ in range(T_SEL):
        m = jnp.max(v, axis=0, keepdims=True)
        first = jnp.min(jnp.where(v == m, blk, LANES), axis=0, keepdims=True)
        hit = blk == first
        sel = jnp.where(hit, 1.0, sel)
        v = jnp.where(hit, DROPPED, v)
    bias = (jnp.where(blk <= cur, sel, 0.0) - 1.0) * BIG
    bias = jnp.concatenate([bias, jnp.zeros((LANES - nb, tq), F32)], axis=0)
    mb_ref[0, 0] = bias.T.astype(mb_ref.dtype)


def _nsa_cmp_select(q, k_cmp, v_cmp, bsz, s):
    tq = ATT_TQ
    nseg = s // STRIDE_CMP
    n_slc = s // L_SLC
    assert nseg <= LANES or nseg % LANES == 0
    c_start = np.arange(nseg) * STRIDE_CMP
    j_start = np.arange(n_slc) * L_SLC
    overlap = ((c_start[:, None] < j_start[None, :] + L_SLC) & (c_start[:, None] + L_CMP > j_start[None, :]))
    overlap = jnp.asarray(np.pad(overlap.astype(np.float32), ((0, 0), (0, LANES - n_slc))), BF16)
    qspec = pl.BlockSpec((1, tq, R_GRP * DH_B), lambda b, g, i: (b, i, g))
    cspec = pl.BlockSpec((1, 1, nseg, DH_B), lambda b, g, i: (b, g, 0, 0))
    return pl.pallas_call(
        functools.partial(_nsa_cmp_kernel, tq=tq, n_slc=n_slc),
        out_shape=(jax.ShapeDtypeStruct((bsz, s, HB * DH_B), BF16), jax.ShapeDtypeStruct((bsz, G_KV, s, LANES), BF16)),
        grid=(bsz, G_KV, s // tq),
        in_specs=[qspec, cspec, cspec, pl.BlockSpec((nseg, LANES), lambda b, g, i: (0, 0))],
        out_specs=(qspec, pl.BlockSpec((1, 1, tq, LANES), lambda b, g, i: (b, g, i, 0))),
        compiler_params=_cparams(("parallel", "parallel", "parallel")),
        name="nsa_compressed_select",
    )(q, k_cmp, v_cmp, overlap)


def _stack_heads(q, extra):
    parts = []
    for r in range(R_GRP):
        qr = q[:, r * DH_B:(r + 1) * DH_B]
        parts.append(qr if extra is None else jnp.concatenate([qr, extra], axis=1))
    return jnp.concatenate(parts, axis=0)


def _unstack_heads(o_ref, o, tq):
    for r in range(R_GRP):
        o_ref[0, :, r * DH_B:(r + 1) * DH_B] = o[r * tq:(r + 1) * tq].astype(o_ref.dtype)


def _selected_branch(i, q, mb, ks_ref, oh_ref, vs_ref, tq):
    rows = R_GRP * tq
    qa = _stack_heads(q, mb)

    def scores(j):
        keys = pl.ds(pl.multiple_of(j * tq, tq), tq)
        k = jnp.concatenate([ks_ref[0, keys, :], oh_ref[keys, :]], axis=1)
        return lax.dot_general(qa, k, (((1,), (1,)), ((), ())), preferred_element_type=F32)

    def update(j, s, carry):
        m, l, acc = carry
        v = vs_ref[0, pl.ds(pl.multiple_of(j * tq, tq), tq), :]
        m_new = jnp.maximum(m, jnp.max(s, axis=1, keepdims=True))
        alpha = jnp.exp(m - m_new)
        p = jnp.exp(s - m_new)
        l = alpha * l + jnp.sum(p, axis=1, keepdims=True)
        acc = alpha * acc + jnp.dot(p.astype(BF16), v, preferred_element_type=F32)
        return m_new, l, acc

    init = (jnp.full((rows, 1), -jnp.inf, F32), jnp.zeros((rows, 1), F32), jnp.zeros((rows, DH_B), F32))
    carry = lax.fori_loop(0, i, lambda j, c: update(j, scores(j), c), init)
    s = scores(i)
    t_loc = lax.broadcasted_iota(jnp.int32, (tq, tq), 0)
    k_loc = lax.broadcasted_iota(jnp.int32, (tq, tq), 1)
    causal = jnp.where(k_loc <= t_loc, 0.0, NEG)
    m, l, acc = update(i, s + jnp.concatenate([causal] * R_GRP, axis=0), carry)
    return acc / l


def _window_branch(i, q, kw_refs, vw_refs, tq):
    rows = R_GRP * tq
    nk = 3 * tq
    qa = _stack_heads(q, None)
    k = jnp.concatenate([r[0] for r in kw_refs], axis=0)
    v = jnp.concatenate([r[0] for r in vw_refs], axis=0)
    s = lax.dot_general(qa, k, (((1,), (1,)), ((), ())), preferred_element_type=F32)
    t_loc = lax.broadcasted_iota(jnp.int32, (tq, nk), 0)
    k_loc = lax.broadcasted_iota(jnp.int32, (tq, nk), 1)
    dist = t_loc + 2 * tq - k_loc
    bound = jnp.minimum(t_loc[:, 0:1] + (i * tq + 1), WINDOW)
    bias = jnp.where(dist.astype(jnp.uint32) < bound.astype(jnp.uint32), 0.0, NEG)
    s = s + jnp.concatenate([bias] * R_GRP, axis=0)
    p = jnp.exp(s - jnp.max(s, axis=1, keepdims=True))
    l = jnp.sum(p, axis=1, keepdims=True)
    return jnp.dot(p.astype(BF16), v, preferred_element_type=F32) / l


def _nsa_local_kernel(q_ref, mb_ref, ks_ref, oh_ref, vs_ref, kw0, kw1, kw2, vw0, vw1, vw2, oc_ref, gate_ref, o_ref,
                      *, tq, gate_lane0):
    g, i = pl.program_id(1), pl.program_id(2)
    q = q_ref[0]
    o_slc = _selected_branch(i, q, mb_ref[0, 0], ks_ref, oh_ref, vs_ref, tq)
    o_win = _window_branch(i, q, (kw0, kw1, kw2), (vw0, vw1, vw2), tq)
    gates = jax.nn.sigmoid(gate_ref[0])
    lane = lax.broadcasted_iota(jnp.int32, gates.shape, 1)
    pick = lambda idx: jnp.sum(jnp.where(lane == idx, gates, 0.0), axis=1, keepdims=True)
    for r in range(R_GRP):
        base = gate_lane0 + (g * R_GRP + r) * 3
        rows = slice(r * tq, (r + 1) * tq)
        cols = slice(r * DH_B, (r + 1) * DH_B)
        o = (pick(base) * oc_ref[0, :, cols].astype(F32) + pick(base + 1) * o_slc[rows] + pick(base + 2) * o_win[rows])
        o_ref[0, :, cols] = o.astype(o_ref.dtype)


def _nsa_local(q, maskbias, ks, main, kw, o_cmp, small, col, gate_lane0, bsz, s):
    tq = ATT_TQ
    assert 2 * tq >= WINDOW
    onehot = jnp.asarray(np.arange(s)[:, None] // L_SLC == np.arange(LANES)[None, :], BF16)
    qspec = pl.BlockSpec((1, tq, R_GRP * DH_B), lambda b, g, i: (b, i, g))
    seq = lambda c0: pl.BlockSpec((1, s, DH_B), lambda b, g, i: (b, 0, c0 // DH_B + g))
    back = lambda c0, n: pl.BlockSpec((1, tq, DH_B), lambda b, g, i: (b, jnp.maximum(i - n, 0), c0 // DH_B + g))
    return pl.pallas_call(
        functools.partial(_nsa_local_kernel, tq=tq, gate_lane0=gate_lane0),
        out_shape=jax.ShapeDtypeStruct((bsz, s, HB * DH_B), BF16),
        grid=(bsz, G_KV, s // tq),
        in_specs=[qspec,
                  pl.BlockSpec((1, 1, tq, LANES), lambda b, g, i: (b, g, i, 0)),
                  seq(0), pl.BlockSpec((s, LANES), lambda b, g, i: (0, 0)), seq(col["vs"]),
                  back(0, 2), back(0, 1), back(0, 0),
                  back(col["vw"], 2), back(col["vw"], 1), back(col["vw"], 0),
                  qspec, pl.BlockSpec((1, tq, LANES), lambda b, g, i: (b, i, 0))],
        out_specs=qspec,
        compiler_params=_cparams(("parallel", "parallel", "arbitrary")),
        name="nsa_selected_window_combine",
    )(q, maskbias, ks, onehot, main, kw, kw, kw, main, main, main, o_cmp, small)


def _moe_sub_blocks(nv_ref, out_ref, compute):
    nsub = (nv_ref[pl.program_id(1)] + MOE_SUB - 1) // MOE_SUB
    rows_of = lambda sb: pl.ds(pl.multiple_of(sb * MOE_SUB, MOE_SUB), MOE_SUB)

    def pair(p, carry):
        rows = [rows_of(2 * p), rows_of(2 * p + 1)]
        for r, val in zip(rows, compute(rows)):
            out_ref[r, :] = val
        return carry

    def dead(sb, carry):
        out_ref[rows_of(sb), :] = jnp.zeros((MOE_SUB, out_ref.shape[1]), out_ref.dtype)
        return carry

    lax.fori_loop(0, nsub // 2, pair, 0)

    @pl.when(nsub % 2 == 1)
    def _():
        out_ref[rows_of(nsub - 1), :] = compute([rows_of(nsub - 1)])[0]

    lax.fori_loop(nsub, MOE_TM // MOE_SUB, dead, 0)


def _moe_new_expert(be_ref):
    i = pl.program_id(1)
    return (i == 0) | (be_ref[i] != be_ref[jnp.maximum(i - 1, 0)])


def _moe_up_kernel(be_ref, nv_ref, x_ref, wg_ref, wu_ref, bg_ref, bu_ref, h_ref, wgb_ref, wub_ref):
    @pl.when(_moe_new_expert(be_ref))
    def _():
        wgb_ref[...] = wg_ref[0].astype(BF16)
        wub_ref[...] = wu_ref[0].astype(BF16)

    def compute(rows):
        xs = [jnp.concatenate(_unpack_bf16_pairs(x_ref[r, :]), axis=1).astype(BF16) for r in rows]
        gates = [jnp.dot(x, wgb_ref[...], preferred_element_type=F32) + bg_ref[0] for x in xs]
        ups = [jnp.dot(x, wub_ref[...], preferred_element_type=F32) + bu_ref[0] for x in xs]
        outs = []
        for gate, up in zip(gates, ups):
            gate = jnp.minimum(gate, SWIGLU_LIMIT)
            up = jnp.clip(up, -SWIGLU_LIMIT, SWIGLU_LIMIT)
            outs.append(((up + 1.0) * gate * jax.nn.sigmoid(SWIGLU_ALPHA * gate)).astype(h_ref.dtype))
        return outs

    _moe_sub_blocks(nv_ref, h_ref, compute)


def _moe_down_kernel(be_ref, nv_ref, h_ref, wd_ref, bd_ref, y_ref, wdb_ref):
    @pl.when(_moe_new_expert(be_ref))
    def _():
        wdb_ref[...] = wd_ref[0].astype(BF16)

    half = wdb_ref.shape[1] // 2

    def compute(rows):
        ys = [jnp.dot(h_ref[r, :], wdb_ref[...], preferred_element_type=F32) + bd_ref[0] for r in rows]
        return [_pack_bf16_pairs(y[:, :half], y[:, half:]) for y in ys]

    _moe_sub_blocks(nv_ref, y_ref, compute)


def _moe_experts(rows, block_e, n_valid, w_gate, b_gate, w_up, b_up, w_down, b_down):
    n_rows, d = rows.shape[0], 2 * rows.shape[1]
    n_blocks = n_rows // MOE_TM
    tn = MOE_TN
    tile = lambda width: pl.BlockSpec((MOE_TM, width), lambda n, i, be, nv: (i, 0))
    wcol = lambda k: pl.BlockSpec((1, k, tn), lambda n, i, be, nv: (be[i], 0, n))
    bcol = pl.BlockSpec((1, 1, tn), lambda n, i, be, nv: (be[i], 0, n))
    params = _cparams(("arbitrary", "arbitrary"))
    h = pl.pallas_call(
        _moe_up_kernel,
        out_shape=jax.ShapeDtypeStruct((n_rows, D_FF), BF16),
        grid_spec=pltpu.PrefetchScalarGridSpec(
            num_scalar_prefetch=2, grid=(D_FF // tn, n_blocks),
            in_specs=[tile(d // 2), wcol(d), wcol(d), bcol, bcol],
            out_specs=pl.BlockSpec((MOE_TM, tn), lambda n, i, be, nv: (i, n)),
            scratch_shapes=[pltpu.VMEM((d, tn), BF16), pltpu.VMEM((d, tn), BF16)]),
        compiler_params=params, name="moe_expert_up",
    )(block_e, n_valid, rows, w_gate, w_up, b_gate.reshape(N_EXP, 1, D_FF), b_up.reshape(N_EXP, 1, D_FF))
    return pl.pallas_call(
        _moe_down_kernel,
        out_shape=jax.ShapeDtypeStruct((n_rows, d // 2), jnp.uint32),
        grid_spec=pltpu.PrefetchScalarGridSpec(
            num_scalar_prefetch=2, grid=(d // tn, n_blocks),
            in_specs=[tile(D_FF), wcol(D_FF), bcol],
            out_specs=pl.BlockSpec((MOE_TM, tn // 2), lambda n, i, be, nv: (i, n)),
            scratch_shapes=[pltpu.VMEM((D_FF, tn), BF16)]),
        compiler_params=params, name="moe_expert_down",
    )(block_e, n_valid, h, w_down, b_down.reshape(N_EXP, 1, d))


ROUTE_TT = 512
MOVE_TT = 256
HALF = D_MODEL // 2


def _pack_bf16_pairs(lo, hi):
    as_bits = lambda v: pltpu.bitcast(v.astype(BF16).astype(F32), jnp.uint32)
    return (as_bits(lo) >> 16) | (as_bits(hi) & jnp.uint32(0xFFFF0000))


def _unpack_bf16_pairs(w):
    return pltpu.bitcast(w << 16, F32), pltpu.bitcast(w & jnp.uint32(0xFFFF0000), F32)


def _route_kernel(h_ref, wr_ref, br_ref, e_ref, w_ref, p_ref, cnt_ref, run_ref, *, tt):
    @pl.when(pl.program_id(0) == 0)
    def _():
        run_ref[...] = jnp.zeros_like(run_ref)

    h = h_ref[...]
    logits = (jnp.dot(h, wr_ref[0], preferred_element_type=F32) + jnp.dot(h, wr_ref[1], preferred_element_type=F32)
              + br_ref[...])
    lane = lax.broadcasted_iota(jnp.int32, (tt, LANES), 1)
    v = jnp.where(lane < N_EXP, logits, -BIG)
    tops, hits, firsts = [], [], []
    for _ in range(TOP_K):
        m = jnp.max(v, axis=1, keepdims=True)
        first = jnp.min(jnp.where(v == m, lane, LANES), axis=1, keepdims=True)
        hit = lane == first
        v = jnp.where(hit, DROPPED, v)
        tops.append(m), hits.append(hit), firsts.append(first)
    ex = [jnp.exp(m - tops[0]) for m in tops]
    inv = 1.0 / functools.reduce(lambda a, b: a + b, ex)
    onehot = functools.reduce(lambda a, b: a + b, [jnp.where(hh, 1.0, 0.0) for hh in hits]).astype(BF16)
    ri = lax.broadcasted_iota(jnp.int32, (tt, tt), 0)
    ci = lax.broadcasted_iota(jnp.int32, (tt, tt), 1)
    before = jnp.where(ci < ri, 1.0, 0.0).astype(BF16)
    rank = jnp.dot(before, onehot, preferred_element_type=F32) + run_ref[0:1]
    run_ref[...] = run_ref[...] + jnp.dot(jnp.ones((8, tt), BF16), onehot, preferred_element_type=F32)
    cnt_ref[...] = run_ref[...].astype(jnp.int32)
    place = lambda cols, zero: functools.reduce(
        lambda acc, kc: jnp.where(lane == kc[0], kc[1], acc), list(enumerate(cols)), zero)
    e_ref[...] = place(firsts, jnp.zeros((tt, LANES), jnp.int32))
    w_ref[...] = place([e * inv for e in ex], jnp.zeros((tt, LANES), F32))
    pos = [jnp.sum(jnp.where(hh, rank, 0.0), axis=1, keepdims=True).astype(jnp.int32) for hh in hits]
    p_ref[...] = place(pos, jnp.zeros((tt, LANES), jnp.int32))


def _route(h, w_router, b_router):
    n, d = h.shape
    tt = ROUTE_TT
    wr = jnp.pad(w_router, ((0, 0), (0, LANES - N_EXP)))
    hi = wr.astype(BF16)
    wr2 = jnp.stack([hi, (wr - hi.astype(F32)).astype(BF16)])
    br = jnp.pad(b_router, (0, LANES - N_EXP)).reshape(1, LANES)
    tok = lambda dt: jax.ShapeDtypeStruct((n, LANES), dt)
    tspec = pl.BlockSpec((tt, LANES), lambda i: (i, 0))
    return pl.pallas_call(
        functools.partial(_route_kernel, tt=tt),
        out_shape=(tok(jnp.int32), tok(F32), tok(jnp.int32), jax.ShapeDtypeStruct((8, LANES), jnp.int32)),
        grid=(n // tt,),
        in_specs=[pl.BlockSpec((tt, d), lambda i: (i, 0)), pl.BlockSpec((2, d, LANES), lambda i: (0, 0, 0)),
                  pl.BlockSpec((1, LANES), lambda i: (0, 0))],
        out_specs=(tspec, tspec, tspec, pl.BlockSpec((8, LANES), lambda i: (0, 0))),
        scratch_shapes=[pltpu.VMEM((8, LANES), F32)],
        compiler_params=_cparams(("arbitrary",)),
        name="moe_route",
    )(h, wr2, br)


def _dispatch_kernel(dest_ref, src_ref, init_ref, rows_ref, sem, *, tt):
    del init_ref
    t0 = pl.program_id(0) * tt

    def copy(j, k):
        return pltpu.make_async_copy(src_ref.at[pl.ds(j, 1)],
                                     rows_ref.at[pl.ds(dest_ref[(t0 + j) * TOP_K + k], 1)], sem)

    def start(j, carry):
        for k in range(TOP_K):
            copy(j, k).start(priority=k % 2)
        return carry

    def wait(j, carry):
        for k in range(TOP_K):
            copy(j, k).wait()
        return carry

    lax.fori_loop(0, tt, start, 0)
    lax.fori_loop(0, tt, wait, 0)


def _dispatch(dest, hpk, n_rows):
    n, width = hpk.shape
    tt = MOVE_TT
    return pl.pallas_call(
        functools.partial(_dispatch_kernel, tt=tt),
        out_shape=jax.ShapeDtypeStruct((n_rows, width), jnp.uint32),
        grid_spec=pltpu.PrefetchScalarGridSpec(
            num_scalar_prefetch=1, grid=(n // tt,),
            in_specs=[pl.BlockSpec((tt, width), lambda i, dest: (i, 0)), pl.BlockSpec(memory_space=pl.ANY)],
            out_specs=pl.BlockSpec(memory_space=pl.ANY),
            scratch_shapes=[pltpu.SemaphoreType.DMA]),
        input_output_aliases={2: 0},
        compiler_params=_cparams(("arbitrary",)),
        name="moe_dispatch",
    )(dest, hpk, jnp.zeros((n_rows, width), jnp.uint32))


def _combine_kernel(dest_ref, y_ref, w_ref, x1_ref, g_ref, o_ref, buf_ref, sem, *, tt, norm):
    t0 = pl.program_id(0) * tt

    def copy(j, k):
        return pltpu.make_async_copy(y_ref.at[pl.ds(dest_ref[(t0 + j) * TOP_K + k], 1)],
                                     buf_ref.at[k, pl.ds(j, 1)], sem)

    def start(j, carry):
        for k in range(TOP_K):
            copy(j, k).start(priority=k % 2)
        return carry

    def wait(j, carry):
        for k in range(TOP_K):
            copy(j, k).wait()
        return carry

    lax.fori_loop(0, tt, start, 0)
    lax.fori_loop(0, tt, wait, 0)
    wts = w_ref[...]
    lane = lax.broadcasted_iota(jnp.int32, wts.shape, 1)
    x = x1_ref[...]
    nq = MOE_TN // 2
    for k in range(TOP_K):
        wk = jnp.sum(jnp.where(lane == k, wts, 0.0), axis=1, keepdims=True)
        parts = []
        for n in range(D_MODEL // MOE_TN):
            lo, hi = _unpack_bf16_pairs(buf_ref[k, :, n * nq:(n + 1) * nq])
            parts += [lo, hi]
        x = x + wk * jnp.concatenate(parts, axis=1)
    if norm:
        x = x * lax.rsqrt(jnp.mean(x * x, axis=-1, keepdims=True) + EPS) * g_ref[...]
    o_ref[...] = x.astype(o_ref.dtype)


def _combine(dest, ypk, wts, x1, gain):
    n, d = x1.shape
    tt = MOVE_TT
    norm = gain is not None
    gain = gain if norm else jnp.ones((d,), F32)
    return pl.pallas_call(
        functools.partial(_combine_kernel, tt=tt, norm=norm),
        out_shape=jax.ShapeDtypeStruct((n, d), F32),
        grid_spec=pltpu.PrefetchScalarGridSpec(
            num_scalar_prefetch=1, grid=(n // tt,),
            in_specs=[pl.BlockSpec(memory_space=pl.ANY),
                      pl.BlockSpec((tt, LANES), lambda i, dest: (i, 0)),
                      pl.BlockSpec((tt, d), lambda i, dest: (i, 0)),
                      pl.BlockSpec((1, d), lambda i, dest: (0, 0))],
            out_specs=pl.BlockSpec((tt, d), lambda i, dest: (i, 0)),
            scratch_shapes=[pltpu.VMEM((TOP_K, tt, d // 2), jnp.uint32), pltpu.SemaphoreType.DMA]),
        compiler_params=_cparams(("arbitrary",)),
        name="moe_combine_norm",
    )(dest, ypk, wts, x1, gain.reshape(1, d))


def _native_sparse_attention(main, small, positions, pe_ck, w1_ck, w2_ck, pe_cv, w1_cv, w2_cv, col, gate_lane0, bsz, s):
    half = ROT_DIM // 2
    inv_freq = ROPE_THETA ** (-jnp.arange(half, dtype=F32) * 2.0 / ROT_DIM)
    ang = positions.astype(F32)[..., None] * inv_freq
    cos, sin = jnp.cos(ang), jnp.sin(ang)
    rest = (bsz, s, DH_B - ROT_DIM)
    cos_t = jnp.concatenate([cos, cos, jnp.ones(rest, F32)], axis=-1)
    sin_t = jnp.concatenate([-sin, sin, jnp.zeros(rest, F32)], axis=-1)
    q, kc_t, vc_t, ks, kw = _nsa_rope(main, cos_t, sin_t, col, bsz, s)
    k_cmp, v_cmp = _nsa_compress(kc_t, vc_t, pe_ck, w1_ck, w2_ck, pe_cv, w1_cv, w2_cv, bsz, s)
    o_cmp, maskbias = _nsa_cmp_select(q, k_cmp, v_cmp, bsz, s)
    return _nsa_local(q, maskbias, ks, main, kw, o_cmp, small, col, gate_lane0, bsz, s)


def _moe_ffn(hf, hpk, x1, final_gain, w_router, b_router, w_gate, b_gate, w_up, b_up, w_down, b_down):
    n_tok, d = x1.shape
    eidx, wts, pos, cnt = _route(hf, w_router, b_router)
    counts = cnt[0, :N_EXP]
    padded = ((counts + MOE_TM - 1) // MOE_TM) * MOE_TM
    pad_end = jnp.cumsum(padded)
    pad_start = pad_end - padded
    group0 = jnp.sum(jnp.where(eidx[:, :TOP_K, None] == jnp.arange(N_EXP), pad_start, 0), axis=-1)
    dest = (group0 + pos[:, :TOP_K]).reshape(-1).astype(jnp.int32)
    n_blocks = (n_tok * TOP_K + N_EXP * (MOE_TM - 1) + MOE_TM - 1) // MOE_TM
    rows = _dispatch(dest, hpk, n_blocks * MOE_TM)
    tile0 = jnp.arange(n_blocks, dtype=jnp.int32) * MOE_TM
    block_e = jnp.minimum(jnp.searchsorted(pad_end, tile0, side='right'), N_EXP - 1).astype(jnp.int32)
    n_valid = jnp.clip(pad_start[block_e] + counts[block_e] - tile0, 0, MOE_TM).astype(jnp.int32)
    ypk = _moe_experts(rows, block_e, n_valid, w_gate, b_gate, w_up, b_up, w_down, b_down)
    return _combine(dest, ypk, wts, x1, final_gain)


def _layer(x, positions, norm_mix, w_in, conv_w, a_log, dt_bias, norm_gdn, pe_ck, w1_ck, w2_ck,
           pe_cv, w1_cv, w2_cv, w_proj_a, w_proj_b, w_out, norm_ffn, w_router, b_router,
           w_gate, b_gate, w_up, b_up, w_down, b_down, final_gain):
    bsz, s, d = x.shape
    n_tok = bsz * s
    x2 = x.reshape(n_tok, d)
    h = _rmsnorm(x2, norm_mix, BF16)
    n_small = 2 * HA + 3 * HB
    sp = SPLIT_POINTS
    w_main = jnp.concatenate([w_in[:, :sp[3]], w_in[:, sp[5]:sp[12]], w_in[:, sp[13]:]], axis=1).astype(BF16)
    w_small = jnp.concatenate([w_in[:, sp[3]:sp[5]], w_in[:, sp[12]:sp[13]]], axis=1)
    w_small = jnp.pad(w_small, ((0, 0), (0, LANES - n_small))).astype(BF16)
    main = _matmul(h, w_main, BF16).reshape(bsz, s, -1)
    small = _matmul(h, w_small, F32).reshape(bsz, s, -1)
    names = ("qa", "ka", "va", "za", "qb", "kc", "vc", "ks", "vs", "kw", "vw", "gm")
    sizes = (HA * DK_A, HA * DK_A, HA * DV_A, HA * DV_A,
             HB * DH_B, G_KV * DH_B, G_KV * DH_B, G_KV * DH_B, G_KV * DH_B, G_KV * DH_B, G_KV * DH_B, 2 * D_MODEL)
    col = {nm: sum(sizes[:i]) for i, nm in enumerate(names)}
    aa, ba = small[..., :HA], small[..., HA:2 * HA]

    o_a = _gated_deltanet(main, aa, ba, conv_w, a_log, dt_bias, norm_gdn, bsz, s)
    o_b = _native_sparse_attention(main, small, positions, pe_ck, w1_ck, w2_ck, pe_cv, w1_cv, w2_cv,
                                   col, 2 * HA, bsz, s)
    merged = _merge(o_a.reshape(n_tok, d), o_b.reshape(n_tok, d), w_proj_a.astype(BF16),
                    w_proj_b.astype(BF16), main.reshape(n_tok, -1), col["gm"])
    x1, hf, hpk = _outproj(merged, w_out.astype(BF16), x2, norm_ffn)
    out = _moe_ffn(hf, hpk, x1, final_gain, w_router, b_router, w_gate, b_gate, w_up, b_up, w_down, b_down)
    return out.reshape(bsz, s, d)


def kernel(x, positions, norm_mix, w_in, conv_w, a_log, dt_bias, norm_gdn, pe_ck, w1_ck, w2_ck, pe_cv, w1_cv, w2_cv, w_proj_a, w_proj_b, w_out, norm_ffn, w_router, b_router, w_gate, b_gate, w_up, b_up, w_down, b_down, norm_final):
    depth = norm_mix.shape[0]
    for l in range(depth):
        x = _layer(x, positions, norm_mix[l], w_in[l], conv_w[l], a_log[l], dt_bias[l], norm_gdn[l],
                   pe_ck[l], w1_ck[l], w2_ck[l], pe_cv[l], w1_cv[l], w2_cv[l],
                   w_proj_a[l], w_proj_b[l], w_out[l], norm_ffn[l], w_router[l], b_router[l],
                   w_gate[l], b_gate[l], w_up[l], b_up[l], w_down[l], b_down[l],
                   norm_final if l + 1 == depth else None)
    return x
```

```python
import functools
import math

import jax
import jax.numpy as jnp
import numpy as np
from jax import lax
from jax.experimental import pallas as pl
from jax.experimental.pallas import tpu as pltpu

F32 = jnp.float32
BF16 = jnp.bfloat16

D_MODEL = 2048
EPS = 1e-6
NEG = -1e30
HA = D_MODEL // 128
DK_A = 128
DV_A = 128
CONV_W = 4
CHUNK = 64
HB = D_MODEL // 128
G_KV = 4
R_GRP = HB // G_KV
DH_B = 128
ROT_DIM = DH_B // 4
ROPE_THETA = 500000.0
L_CMP = 32
STRIDE_CMP = 16
L_SLC = 64
T_SEL = 8
WINDOW = 512
N_EXP = 32
TOP_K = 4
D_FF = D_MODEL
SWIGLU_LIMIT = 7.0
SWIGLU_ALPHA = 1.702
SPLIT_SIZES = (HA * DK_A, HA * DK_A, HA * DV_A, HA * DV_A, HA, HA,
               HB * DH_B, G_KV * DH_B, G_KV * DH_B, G_KV * DH_B, G_KV * DH_B, G_KV * DH_B, G_KV * DH_B,
               3 * HB, 2 * D_MODEL)
SPLIT_POINTS = tuple(sum(SPLIT_SIZES[:i + 1]) for i in range(len(SPLIT_SIZES) - 1))

V7X_VMEM_LIMIT_BYTES = 56 * 1024 * 1024
LANES = 128
MOE_TM = 1024
MOE_SUB = 256
MOE_TN = 512
ATT_TQ = 256


def _cparams(sem):
    return pltpu.CompilerParams(dimension_semantics=sem, vmem_limit_bytes=V7X_VMEM_LIMIT_BYTES)


def _mm_kernel(x_ref, w_ref, o_ref):
    o_ref[...] = jnp.dot(x_ref[...], w_ref[...], preferred_element_type=F32).astype(o_ref.dtype)


def _matmul(x, w, out_dtype, tm=1024, tn=1024):
    m, k = x.shape
    n = w.shape[1]
    tm, tn = min(tm, m), min(tn, n)
    assert m % tm == 0 and n % tn == 0
    return pl.pallas_call(
        _mm_kernel,
        out_shape=jax.ShapeDtypeStruct((m, n), out_dtype),
        grid=(n // tn, m // tm),
        in_specs=[pl.BlockSpec((tm, k), lambda j, i: (i, 0)),
                  pl.BlockSpec((k, tn), lambda j, i: (0, j))],
        out_specs=pl.BlockSpec((tm, tn), lambda j, i: (i, j)),
        compiler_params=_cparams(("parallel", "parallel")),
        name="dense_matmul",
    )(x, w)


def _rmsnorm_kernel(x_ref, g_ref, o_ref):
    x = x_ref[...]
    y = x * lax.rsqrt(jnp.mean(x * x, axis=-1, keepdims=True) + EPS)
    o_ref[...] = (y * g_ref[...]).astype(o_ref.dtype)


def _rmsnorm(x, gain, out_dtype, tm=512):
    m, d = x.shape
    return pl.pallas_call(
        _rmsnorm_kernel,
        out_shape=jax.ShapeDtypeStruct((m, d), out_dtype),
        grid=(m // tm,),
        in_specs=[pl.BlockSpec((tm, d), lambda i: (i, 0)),
                  pl.BlockSpec((1, d), lambda i: (0, 0))],
        out_specs=pl.BlockSpec((tm, d), lambda i: (i, 0)),
        compiler_params=_cparams(("parallel",)),
        name="rmsnorm",
    )(x, gain.reshape(1, d))


def _add_rmsnorm_kernel(x_ref, y_ref, g_ref, o_ref):
    x = x_ref[...] + y_ref[...]
    y = x * lax.rsqrt(jnp.mean(x * x, axis=-1, keepdims=True) + EPS)
    o_ref[...] = (y * g_ref[...]).astype(o_ref.dtype)


def _add_rmsnorm(x, y, gain, out_dtype, tm=512):
    m, d = x.shape
    row = pl.BlockSpec((tm, d), lambda i: (i, 0))
    return pl.pallas_call(
        _add_rmsnorm_kernel,
        out_shape=jax.ShapeDtypeStruct((m, d), out_dtype),
        grid=(m // tm,),
        in_specs=[row, row, pl.BlockSpec((1, d), lambda i: (0, 0))],
        out_specs=row,
        compiler_params=_cparams(("parallel",)),
        name="add_rmsnorm",
    )(x, y, gain.reshape(1, d))


def _merge_kernel(oa_ref, ob_ref, wa_ref, wb_ref, ga_ref, gb_ref, o_ref):
    ya = jnp.dot(oa_ref[...], wa_ref[...], preferred_element_type=F32)
    yb = jnp.dot(ob_ref[...], wb_ref[...], preferred_element_type=F32)
    o_ref[...] = (jax.nn.sigmoid(ga_ref[...].astype(F32)) * ya
                  + jax.nn.sigmoid(gb_ref[...].astype(F32)) * yb).astype(o_ref.dtype)


def _merge(o_a, o_b, w_a, w_b, main2d, gm_col0, tm=512, tn=1024):
    m, d = o_a.shape
    assert gm_col0 % tn == 0 and d % tn == 0
    g0 = gm_col0 // tn
    lhs = pl.BlockSpec((tm, d), lambda j, i: (i, 0))
    rhs = pl.BlockSpec((d, tn), lambda j, i: (0, j))
    return pl.pallas_call(
        _merge_kernel,
        out_shape=jax.ShapeDtypeStruct((m, d), BF16),
        grid=(d // tn, m // tm),
        in_specs=[lhs, lhs, rhs, rhs,
                  pl.BlockSpec((tm, tn), lambda j, i: (i, g0 + j)),
                  pl.BlockSpec((tm, tn), lambda j, i: (i, g0 + d // tn + j))],
        out_specs=pl.BlockSpec((tm, tn), lambda j, i: (i, j)),
        compiler_params=_cparams(("parallel", "parallel")),
        name="mixer_merge",
    )(o_a, o_b, w_a, w_b, main2d, main2d)


def _outproj_kernel(m_ref, w_ref, x_ref, g_ref, x1_ref, h_ref, hpk_ref):
    x1 = x_ref[...] + jnp.dot(m_ref[...], w_ref[...], preferred_element_type=F32)
    x1_ref[...] = x1
    y = x1 * lax.rsqrt(jnp.mean(x1 * x1, axis=-1, keepdims=True) + EPS) * g_ref[...]
    h_ref[...] = y.astype(h_ref.dtype)
    half = y.shape[1] // 2
    hpk_ref[...] = _pack_bf16_pairs(y[:, :half], y[:, half:])


def _outproj(merged, w_out, x, gain, tm=512):
    m, d = x.shape
    row = lambda width=d: pl.BlockSpec((tm, width), lambda i: (i, 0))
    return pl.pallas_call(
        _outproj_kernel,
        out_shape=(jax.ShapeDtypeStruct((m, d), F32), jax.ShapeDtypeStruct((m, d), BF16),
                   jax.ShapeDtypeStruct((m, d // 2), jnp.uint32)),
        grid=(m // tm,),
        in_specs=[row(), pl.BlockSpec((d, d), lambda i: (0, 0)), row(), pl.BlockSpec((1, d), lambda i: (0, 0))],
        out_specs=(row(), row(), row(d // 2)),
        compiler_params=_cparams(("parallel",)),
        name="out_proj_residual_norm",
    )(merged, w_out, x, gain.reshape(1, d))


GDN_COLS = 512
GDN_GROUP = 256
GDN_PH = 4
GDN_TS = 512


def _gdn_conv_kernel(x_ref, w_ref, o_ref):
    sec = pl.program_id(1) // (HA * DK_A // GDN_COLS)
    x = x_ref[0].astype(F32)
    w = w_ref[...]
    row = lax.broadcasted_iota(jnp.int32, x.shape, 0)
    y = x * w[CONV_W - 1:CONV_W]
    for i in range(CONV_W - 1):
        sh = CONV_W - 1 - i
        y = y + jnp.where(row >= sh, pltpu.roll(x, sh, axis=0), 0.0) * w[i:i + 1]
    y = y * jax.nn.sigmoid(y)
    qscale = jnp.where(sec == 0, DK_A ** -0.5, 1.0)
    for h in range(GDN_COLS // DK_A):
        yh = y[:, h * DK_A:(h + 1) * DK_A]
        nrm = yh * (lax.rsqrt(jnp.sum(yh * yh, axis=-1, keepdims=True) + EPS) * qscale)
        o_ref[0, 0, h] = jnp.where(sec < 2, nrm, yh).astype(o_ref.dtype)


def _gdn_conv(main, conv_w, bsz, s):
    ncol = 3 * HA * DK_A // GDN_COLS
    hpc = GDN_COLS // DK_A
    return pl.pallas_call(
        _gdn_conv_kernel,
        out_shape=jax.ShapeDtypeStruct((3, bsz, HA, s, DK_A), BF16),
        grid=(bsz, ncol),
        in_specs=[pl.BlockSpec((1, s, GDN_COLS), lambda b, c: (b, 0, c)),
                  pl.BlockSpec((CONV_W, GDN_COLS), lambda b, c: (0, c))],
        out_specs=pl.BlockSpec((1, 1, hpc, s, DK_A), lambda b, c: (c // (HA // hpc), b, c % (HA // hpc), 0, 0)),
        compiler_params=_cparams(("parallel", "parallel")),
        name="gdn_conv_silu_l2norm",
    )(main, conv_w)


def _col_rep(row, n):
    return jnp.broadcast_to(row, (LANES, n)).T


def _dot_hilo(x, m):
    hi = x.astype(BF16)
    lo = (x - hi.astype(F32)).astype(BF16)
    return jnp.dot(hi, m, preferred_element_type=F32) + jnp.dot(lo, m, preferred_element_type=F32)


def _gdn_prep_kernel(q_ref, k_ref, v_ref, g_ref, b_ref, u_ref, w_ref, qg_ref, kd_ref, a_ref, egl_ref):
    n = GDN_GROUP
    ri = lax.broadcasted_iota(jnp.int32, (n, n), 0)
    ci = lax.broadcasted_iota(jnp.int32, (n, n), 1)
    same = (ri // CHUNK) == (ci // CHUNK)
    incl = same & (ri >= ci)
    strict = same & (ri > ci)
    one_if = lambda m: jnp.where(m, 1.0, 0.0).astype(BF16)
    cum_m, tot_m = one_if(same & (ri <= ci)), one_if(same)
    eye = jnp.where(ri == ci, 1.0, 0.0)
    wide = lambda c: jnp.concatenate([c] * (n // LANES), axis=1)
    nt = (((1,), (1,)), ((), ()))
    heads = range(GDN_PH)
    ts, ps, rhs = [], [], []
    for h in heads:
        q, k, v = q_ref[0, 0, h], k_ref[0, 0, h], v_ref[0, 0, h]
        g8 = jnp.broadcast_to(g_ref[0, h], (8, n))
        gc_row = _dot_hilo(g8, cum_m)[0:1]
        gl_row = _dot_hilo(g8, tot_m)[0:1]
        gc_c, gl_c, b_c = _col_rep(gc_row, n), _col_rep(gl_row, n), _col_rep(b_ref[0, h], n)
        decay = jnp.exp(jnp.where(incl, wide(gc_c) - gc_row, NEG))
        kk = lax.dot_general(k, k, nt, preferred_element_type=F32)
        qk = lax.dot_general(q, k, nt, preferred_element_type=F32)
        xb = jnp.where(strict, -(kk * wide(b_c) * decay), 0.0).astype(BF16)
        a = qk * decay
        kf = k.astype(F32)
        egc = jnp.exp(gc_c)
        qg_ref[0, h] = (q.astype(F32) * egc).astype(qg_ref.dtype)
        kd_ref[0, h] = (kf * jnp.exp(gl_c - gc_c)).astype(kd_ref.dtype)
        for c in range(n // CHUNK):
            blk = slice(c * CHUNK, (c + 1) * CHUNK)
            a_ref[0, h, blk, :] = a[blk, blk].astype(a_ref.dtype)
        egl = jnp.exp(gl_c)
        egl_ref[0, h, 0] = jnp.concatenate([egl[c * CHUNK:c * CHUNK + 1] for c in range(n // CHUNK)], axis=0)
        rhs.append(jnp.concatenate([(v.astype(F32) * b_c).astype(BF16), (kf * b_c * egc).astype(BF16)], axis=1))
        ts.append(eye + xb.astype(F32))
        ps.append(xb)
    ps = [jnp.dot(p, p, preferred_element_type=F32).astype(BF16) for p in ps]
    for step in range(5):
        for h in heads:
            if step < 4:
                tp = jnp.dot(jnp.concatenate([ts[h].astype(BF16), ps[h]], axis=0), ps[h], preferred_element_type=F32)
                ts[h] = ts[h] + tp[:n]
                ps[h] = tp[n:].astype(BF16)
            else:
                ts[h] = ts[h] + jnp.dot(ts[h].astype(BF16), ps[h], preferred_element_type=F32)
    for h in heads:
        uw = jnp.dot(ts[h].astype(BF16), rhs[h], preferred_element_type=F32)
        u_ref[0, h] = uw[:, :DV_A].astype(u_ref.dtype)
        w_ref[0, h] = uw[:, DV_A:].astype(w_ref.dtype)


def _gdn_prep(qkv, g_t, beta_t, bsz, s):
    n, ph = GDN_GROUP, GDN_PH
    tok = lambda width, dt: jax.ShapeDtypeStruct((bsz, HA, s, width), dt)
    tspec = lambda width: pl.BlockSpec((1, ph, n, width), lambda b, h, i: (b, h, i, 0))
    qspec = lambda sec: pl.BlockSpec((1, 1, ph, n, DK_A), lambda b, h, i: (sec, b, h, i, 0))
    rspec = pl.BlockSpec((1, ph, 1, n), lambda b, h, i: (b, h, 0, i))
    return pl.pallas_call(
        _gdn_prep_kernel,
        out_shape=(tok(DV_A, BF16), tok(DK_A, BF16), tok(DK_A, BF16), tok(DK_A, BF16), tok(CHUNK, BF16),
                   jax.ShapeDtypeStruct((bsz, HA, s // n, n // CHUNK, LANES), F32)),
        grid=(bsz, HA // ph, s // n),
        in_specs=[qspec(0), qspec(1), qspec(2), rspec, rspec],
        out_specs=(tspec(DV_A), tspec(DK_A), tspec(DK_A), tspec(DK_A), tspec(CHUNK),
                   pl.BlockSpec((1, ph, 1, n // CHUNK, LANES), lambda b, h, i: (b, h, i, 0, 0))),
        compiler_params=_cparams(("parallel", "parallel", "parallel")),
        name="gdn_chunk_prep",
    )(qkv, qkv, qkv, g_t, beta_t)


def _gdn_scan_kernel(u_ref, w_ref, qg_ref, kd_ref, a_ref, egl_ref, z_ref, ng_ref, o_ref, state_ref, *, nchunk):
    @pl.when(pl.program_id(1) == 0)
    def _():
        state_ref[...] = jnp.zeros_like(state_ref)

    tn = (((0,), (0,)), ((), ()))
    heads = range(HA)

    def body(c, carry):
        rows = pl.ds(pl.multiple_of(c * CHUNK, CHUNK), CHUNK)
        st = [state_ref[h] for h in heads]
        sb = [x.astype(BF16) for x in st]
        vb = [(u_ref[0, h, rows, :].astype(F32)
               - jnp.dot(w_ref[0, h, rows, :], sb[h], preferred_element_type=F32)).astype(BF16) for h in heads]
        o = [jnp.dot(qg_ref[0, h, rows, :], sb[h], preferred_element_type=F32)
             + jnp.dot(a_ref[0, h, rows, :], vb[h], preferred_element_type=F32) for h in heads]
        for h in heads:
            state_ref[h] = (st[h] * egl_ref[0, h, pl.ds(c, 1), :]
                            + lax.dot_general(kd_ref[0, h, rows, :], vb[h], tn, preferred_element_type=F32))
        for h in heads:
            cols = slice(h * DV_A, (h + 1) * DV_A)
            z = z_ref[0, rows, cols].astype(F32)
            on = o[h] * lax.rsqrt(jnp.mean(o[h] * o[h], axis=-1, keepdims=True) + EPS) * ng_ref[...]
            o_ref[0, rows, cols] = (on * (z * jax.nn.sigmoid(z))).astype(o_ref.dtype)
        return carry

    lax.fori_loop(0, nchunk, body, 0)


def _gdn_scan(u, w, qg, kd, a, egl, main, norm_gdn, bsz, s):
    ts = GDN_TS
    z_blk0 = 3 * HA * DK_A // (HA * DV_A)
    hspec = lambda width: pl.BlockSpec((1, HA, ts, width), lambda b, i: (b, 0, i, 0))
    return pl.pallas_call(
        functools.partial(_gdn_scan_kernel, nchunk=ts // CHUNK),
        out_shape=jax.ShapeDtypeStruct((bsz, s, HA * DV_A), BF16),
        grid=(bsz, s // ts),
        in_specs=[hspec(DV_A), hspec(DK_A), hspec(DK_A), hspec(DK_A), hspec(CHUNK),
                  pl.BlockSpec((1, HA, ts // CHUNK, LANES), lambda b, i: (b, 0, i, 0)),
                  pl.BlockSpec((1, ts, HA * DV_A), lambda b, i: (b, i, z_blk0)),
                  pl.BlockSpec((1, DV_A), lambda b, i: (0, 0))],
        out_specs=pl.BlockSpec((1, ts, HA * DV_A), lambda b, i: (b, i, 0)),
        scratch_shapes=[pltpu.VMEM((HA, DK_A, DV_A), F32)],
        compiler_params=_cparams(("parallel", "arbitrary")),
        name="gdn_delta_scan",
    )(u, w, qg, kd, a, egl, main, norm_gdn.reshape(1, DV_A))


def _gated_deltanet(main, a_in, b_in, conv_w, a_log, dt_bias, norm_gdn, bsz, s):
    qkv = _gdn_conv(main, conv_w, bsz, s)
    g = -jnp.exp(a_log.astype(F32)) * jax.nn.softplus(a_in.astype(F32) + dt_bias.astype(F32))
    beta = jax.nn.sigmoid(b_in.astype(F32))
    g_t = g.transpose(0, 2, 1).reshape(bsz, HA, 1, s)
    beta_t = beta.transpose(0, 2, 1).reshape(bsz, HA, 1, s)
    u, w, qg, kd, a, egl = _gdn_prep(qkv, g_t, beta_t, bsz, s)
    egl = egl.reshape(bsz, HA, s // CHUNK, LANES)
    return _gdn_scan(u, w, qg, kd, a, egl, main, norm_gdn, bsz, s)


NSA_TS = 512
BIG = 1e30
DROPPED = -3e38


def _rope(x, cos, sin, lane):
    half = ROT_DIM // 2
    partner = jnp.where(lane < half, pltpu.roll(x, DH_B - half, axis=1), pltpu.roll(x, half, axis=1))
    return x * cos + partner * sin


def _nsa_rope_kernel(q_ref, kc_ref, vc_ref, ks_ref, kw_ref, cos_ref, sin_ref,
                     qo_ref, kco_ref, vco_ref, kso_ref, kwo_ref):
    cos, sin = cos_ref[0], sin_ref[0]
    lane = lax.broadcasted_iota(jnp.int32, cos.shape, 1)
    head = lambda ref, h: ref[0, :, h * DH_B:(h + 1) * DH_B].astype(F32)
    for h in range(HB):
        qo_ref[0, :, h * DH_B:(h + 1) * DH_B] = (_rope(head(q_ref, h), cos, sin, lane) * DH_B ** -0.5).astype(qo_ref.dtype)
    for g in range(G_KV):
        cols = slice(g * DH_B, (g + 1) * DH_B)
        kco_ref[0, g] = _rope(head(kc_ref, g), cos, sin, lane).astype(kco_ref.dtype)
        vco_ref[0, g] = vc_ref[0, :, cols]
        kso_ref[0, :, cols] = _rope(head(ks_ref, g), cos, sin, lane).astype(kso_ref.dtype)
        kwo_ref[0, :, cols] = _rope(head(kw_ref, g), cos, sin, lane).astype(kwo_ref.dtype)


def _nsa_rope(main, cos, sin, col, bsz, s):
    ts = NSA_TS
    kvw = G_KV * DH_B
    tok = lambda width, c0: pl.BlockSpec((1, ts, width), lambda b, i: (b, i, c0 // width))
    tab = pl.BlockSpec((1, ts, DH_B), lambda b, i: (b, i, 0))
    grp = pl.BlockSpec((1, G_KV, ts, DH_B), lambda b, i: (b, 0, i, 0))
    flat = lambda width: pl.BlockSpec((1, ts, width), lambda b, i: (b, i, 0))
    return pl.pallas_call(
        _nsa_rope_kernel,
        out_shape=(jax.ShapeDtypeStruct((bsz, s, HB * DH_B), BF16),
                   jax.ShapeDtypeStruct((bsz, G_KV, s, DH_B), BF16), jax.ShapeDtypeStruct((bsz, G_KV, s, DH_B), BF16),
                   jax.ShapeDtypeStruct((bsz, s, kvw), BF16), jax.ShapeDtypeStruct((bsz, s, kvw), BF16)),
        grid=(bsz, s // ts),
        in_specs=[tok(HB * DH_B, col["qb"]), tok(kvw, col["kc"]), tok(kvw, col["vc"]), tok(kvw, col["ks"]),
                  tok(kvw, col["kw"]), tab, tab],
        out_specs=(flat(HB * DH_B), grp, grp, flat(kvw), flat(kvw)),
        compiler_params=_cparams(("parallel", "parallel")),
        name="nsa_rotary",
    )(main, main, main, main, main, cos, sin)


def _gelu_tanh(x):
    return 0.5 * x * (1.0 + jnp.tanh(math.sqrt(2.0 / math.pi) * (x + 0.044715 * x * x * x)))


def _nsa_compress_kernel(k_ref, v_ref, pek_ref, pev_ref, w1k_ref, w1v_ref, w2k_ref, w2v_ref, ko_ref, vo_ref):
    nseg = k_ref.shape[2]
    for x_ref, pe_ref, w1_ref, w2_ref, o_ref in ((k_ref, pek_ref, w1k_ref, w2k_ref, ko_ref),
                                                 (v_ref, pev_ref, w1v_ref, w2v_ref, vo_ref)):
        for g in range(G_KV):
            x = x_ref[0, g].astype(F32)
            lo = jnp.dot((x + pe_ref[0:1]).astype(BF16), w1_ref[0], preferred_element_type=F32)
            hi = jnp.dot((x + pe_ref[1:2]).astype(BF16), w1_ref[1], preferred_element_type=F32)
            pre = lo + pltpu.roll(hi, nseg - 1, axis=0)
            o_ref[0, g] = jnp.dot(_gelu_tanh(pre).astype(BF16), w2_ref[...],
                                  preferred_element_type=F32).astype(o_ref.dtype)


def _nsa_compress(kc_t, vc_t, pe_ck, w1_ck, w2_ck, pe_cv, w1_cv, w2_cv, bsz, s):
    assert L_CMP == 2 * STRIDE_CMP
    nseg = s // STRIDE_CMP
    width = STRIDE_CMP * DH_B
    seg = lambda t: t.reshape(bsz, G_KV, nseg, width)
    pe2 = lambda pe: pe.reshape(2, width)
    w1h = lambda w: w.reshape(2, width, DH_B).astype(BF16)
    xspec = pl.BlockSpec((1, G_KV, nseg, width), lambda b: (b, 0, 0, 0))
    pspec = pl.BlockSpec((2, width), lambda b: (0, 0))
    w1spec = pl.BlockSpec((2, width, DH_B), lambda b: (0, 0, 0))
    w2spec = pl.BlockSpec((DH_B, DH_B), lambda b: (0, 0))
    ospec = pl.BlockSpec((1, G_KV, nseg, DH_B), lambda b: (b, 0, 0, 0))
    oshape = jax.ShapeDtypeStruct((bsz, G_KV, nseg, DH_B), BF16)
    return pl.pallas_call(
        _nsa_compress_kernel,
        out_shape=(oshape, oshape),
        grid=(bsz,),
        in_specs=[xspec, xspec, pspec, pspec, w1spec, w1spec, w2spec, w2spec],
        out_specs=(ospec, ospec),
        compiler_params=_cparams(("parallel",)),
        name="nsa_compress",
    )(seg(kc_t), seg(vc_t), pe2(pe_ck), pe2(pe_cv), w1h(w1_ck), w1h(w1_cv), w2_ck.astype(BF16), w2_cv.astype(BF16))


def _nsa_cmp_kernel(q_ref, kc_ref, vc_ref, ov_ref, o_ref, mb_ref, *, tq, n_slc):
    i = pl.program_id(2)
    rows = R_GRP * tq
    ncmp = kc_ref.shape[2]
    qa = _stack_heads(q_ref[0], None)
    s = lax.dot_general(qa, kc_ref[0, 0], (((1,), (1,)), ((), ())), preferred_element_type=F32)
    t_row = (lax.broadcasted_iota(jnp.int32, (rows, ncmp), 0) & (tq - 1)) + i * tq
    c_end = lax.broadcasted_iota(jnp.int32, (rows, ncmp), 1) * STRIDE_CMP + (L_CMP - 1)
    valid = c_end <= t_row
    sm = jnp.where(valid, s, NEG)
    p = jnp.where(valid, jnp.exp(sm - jnp.max(sm, axis=1, keepdims=True)), 0.0)
    l = jnp.sum(p, axis=1, keepdims=True)
    p = p * (1.0 / jnp.where(l > 0.0, l, 1.0))
    _unstack_heads(o_ref, jnp.dot(p.astype(BF16), vc_ref[0, 0], preferred_element_type=F32), tq)
    psum = p[0:tq]
    for r in range(1, R_GRP):
        psum = psum + p[r * tq:(r + 1) * tq]
    imp = _dot_hilo(psum, ov_ref[...])
    nb = -(-n_slc // 8) * 8
    v = imp.T[:nb]
    blk = lax.broadcasted_iota(jnp.int32, (nb, tq), 0)
    cur = (lax.broadcasted_iota(jnp.int32, (nb, tq), 1) + i * tq) // L_SLC
    forced = (blk == 0) | (blk == cur)
    v = jnp.where(forced, BIG, jnp.where(blk <= cur, v, -BIG))
    sel = jnp.zeros((nb, tq), F32)
    for _ in range(T_SEL):
        m = jnp.max(v, axis=0, keepdims=True)
        first = jnp.min(jnp.where(v == m, blk, LANES), axis=0, keepdims=True)
        hit = blk == first
        sel = jnp.where(hit, 1.0, sel)
        v = jnp.where(hit, DROPPED, v)
    bias = (jnp.where(blk <= cur, sel, 0.0) - 1.0) * BIG
    bias = jnp.concatenate([bias, jnp.zeros((LANES - nb, tq), F32)], axis=0)
    mb_ref[0, 0] = bias.T.astype(mb_ref.dtype)


def _nsa_cmp_select(q, k_cmp, v_cmp, bsz, s):
    tq = ATT_TQ
    nseg = s // STRIDE_CMP
    n_slc = s // L_SLC
    assert nseg <= LANES or nseg % LANES == 0
    c_start = np.arange(nseg) * STRIDE_CMP
    j_start = np.arange(n_slc) * L_SLC
    overlap = ((c_start[:, None] < j_start[None, :] + L_SLC) & (c_start[:, None] + L_CMP > j_start[None, :]))
    overlap = jnp.asarray(np.pad(overlap.astype(np.float32), ((0, 0), (0, LANES - n_slc))), BF16)
    qspec = pl.BlockSpec((1, tq, R_GRP * DH_B), lambda b, g, i: (b, i, g))
    cspec = pl.BlockSpec((1, 1, nseg, DH_B), lambda b, g, i: (b, g, 0, 0))
    return pl.pallas_call(
        functools.partial(_nsa_cmp_kernel, tq=tq, n_slc=n_slc),
        out_shape=(jax.ShapeDtypeStruct((bsz, s, HB * DH_B), BF16), jax.ShapeDtypeStruct((bsz, G_KV, s, LANES), BF16)),
        grid=(bsz, G_KV, s // tq),
        in_specs=[qspec, cspec, cspec, pl.BlockSpec((nseg, LANES), lambda b, g, i: (0, 0))],
        out_specs=(qspec, pl.BlockSpec((1, 1, tq, LANES), lambda b, g, i: (b, g, i, 0))),
        compiler_params=_cparams(("parallel", "parallel", "parallel")),
        name="nsa_compressed_select",
    )(q, k_cmp, v_cmp, overlap)


def _stack_heads(q, extra):
    parts = []
    for r in range(R_GRP):
        qr = q[:, r * DH_B:(r + 1) * DH_B]
        parts.append(qr if extra is None else jnp.concatenate([qr, extra], axis=1))
    return jnp.concatenate(parts, axis=0)


def _unstack_heads(o_ref, o, tq):
    for r in range(R_GRP):
        o_ref[0, :, r * DH_B:(r + 1) * DH_B] = o[r * tq:(r + 1) * tq].astype(o_ref.dtype)


def _selected_branch(i, q, mb, ks_ref, oh_ref, vs_ref, tq):
    rows = R_GRP * tq
    qa = _stack_heads(q, mb)

    def scores(j):
        keys = pl.ds(pl.multiple_of(j * tq, tq), tq)
        k = jnp.concatenate([ks_ref[0, keys, :], oh_ref[keys, :]], axis=1)
        return lax.dot_general(qa, k, (((1,), (1,)), ((), ())), preferred_element_type=F32)

    def update(j, s, carry):
        m, l, acc = carry
        v = vs_ref[0, pl.ds(pl.multiple_of(j * tq, tq), tq), :]
        m_new = jnp.maximum(m, jnp.max(s, axis=1, keepdims=True))
        alpha = jnp.exp(m - m_new)
        p = jnp.exp(s - m_new)
        l = alpha * l + jnp.sum(p, axis=1, keepdims=True)
        acc = alpha * acc + jnp.dot(p.astype(BF16), v, preferred_element_type=F32)
        return m_new, l, acc

    init = (jnp.full((rows, 1), -jnp.inf, F32), jnp.zeros((rows, 1), F32), jnp.zeros((rows, DH_B), F32))
    carry = lax.fori_loop(0, i, lambda j, c: update(j, scores(j), c), init)
    s = scores(i)
    t_loc = lax.broadcasted_iota(jnp.int32, (tq, tq), 0)
    k_loc = lax.broadcasted_iota(jnp.int32, (tq, tq), 1)
    causal = jnp.where(k_loc <= t_loc, 0.0, NEG)
    m, l, acc = update(i, s + jnp.concatenate([causal] * R_GRP, axis=0), carry)
    return acc / l


def _window_branch(i, q, kw_refs, vw_refs, tq):
    rows = R_GRP * tq
    nk = 3 * tq
    qa = _stack_heads(q, None)
    k = jnp.concatenate([r[0] for r in kw_refs], axis=0)
    v = jnp.concatenate([r[0] for r in vw_refs], axis=0)
    s = lax.dot_general(qa, k, (((1,), (1,)), ((), ())), preferred_element_type=F32)
    t_loc = lax.broadcasted_iota(jnp.int32, (tq, nk), 0)
    k_loc = lax.broadcasted_iota(jnp.int32, (tq, nk), 1)
    dist = t_loc + 2 * tq - k_loc
    bound = jnp.minimum(t_loc[:, 0:1] + (i * tq + 1), WINDOW)
    bias = jnp.where(dist.astype(jnp.uint32) < bound.astype(jnp.uint32), 0.0, NEG)
    s = s + jnp.concatenate([bias] * R_GRP, axis=0)
    p = jnp.exp(s - jnp.max(s, axis=1, keepdims=True))
    l = jnp.sum(p, axis=1, keepdims=True)
    return jnp.dot(p.astype(BF16), v, preferred_element_type=F32) / l


def _nsa_local_kernel(q_ref, mb_ref, ks_ref, oh_ref, vs_ref, kw0, kw1, kw2, vw0, vw1, vw2, oc_ref, gate_ref, o_ref,
                      *, tq, gate_lane0):
    g, i = pl.program_id(1), pl.program_id(2)
    q = q_ref[0]
    o_slc = _selected_branch(i, q, mb_ref[0, 0], ks_ref, oh_ref, vs_ref, tq)
    o_win = _window_branch(i, q, (kw0, kw1, kw2), (vw0, vw1, vw2), tq)
    gates = jax.nn.sigmoid(gate_ref[0])
    lane = lax.broadcasted_iota(jnp.int32, gates.shape, 1)
    pick = lambda idx: jnp.sum(jnp.where(lane == idx, gates, 0.0), axis=1, keepdims=True)
    for r in range(R_GRP):
        base = gate_lane0 + (g * R_GRP + r) * 3
        rows = slice(r * tq, (r + 1) * tq)
        cols = slice(r * DH_B, (r + 1) * DH_B)
        o = (pick(base) * oc_ref[0, :, cols].astype(F32) + pick(base + 1) * o_slc[rows] + pick(base + 2) * o_win[rows])
        o_ref[0, :, cols] = o.astype(o_ref.dtype)


def _nsa_local(q, maskbias, ks, main, kw, o_cmp, small, col, gate_lane0, bsz, s):
    tq = ATT_TQ
    assert 2 * tq >= WINDOW
    onehot = jnp.asarray(np.arange(s)[:, None] // L_SLC == np.arange(LANES)[None, :], BF16)
    qspec = pl.BlockSpec((1, tq, R_GRP * DH_B), lambda b, g, i: (b, i, g))
    seq = lambda c0: pl.BlockSpec((1, s, DH_B), lambda b, g, i: (b, 0, c0 // DH_B + g))
    back = lambda c0, n: pl.BlockSpec((1, tq, DH_B), lambda b, g, i: (b, jnp.maximum(i - n, 0), c0 // DH_B + g))
    return pl.pallas_call(
        functools.partial(_nsa_local_kernel, tq=tq, gate_lane0=gate_lane0),
        out_shape=jax.ShapeDtypeStruct((bsz, s, HB * DH_B), BF16),
        grid=(bsz, G_KV, s // tq),
        in_specs=[qspec,
                  pl.BlockSpec((1, 1, tq, LANES), lambda b, g, i: (b, g, i, 0)),
                  seq(0), pl.BlockSpec((s, LANES), lambda b, g, i: (0, 0)), seq(col["vs"]),
                  back(0, 2), back(0, 1), back(0, 0),
                  back(col["vw"], 2), back(col["vw"], 1), back(col["vw"], 0),
                  qspec, pl.BlockSpec((1, tq, LANES), lambda b, g, i: (b, i, 0))],
        out_specs=qspec,
        compiler_params=_cparams(("parallel", "parallel", "arbitrary")),
        name="nsa_selected_window_combine",
    )(q, maskbias, ks, onehot, main, kw, kw, kw, main, main, main, o_cmp, small)


def _moe_sub_blocks(nv_ref, out_ref, compute):
    nsub = (nv_ref[pl.program_id(1)] + MOE_SUB - 1) // MOE_SUB
    rows_of = lambda sb: pl.ds(pl.multiple_of(sb * MOE_SUB, MOE_SUB), MOE_SUB)

    def pair(p, carry):
        rows = [rows_of(2 * p), rows_of(2 * p + 1)]
        for r, val in zip(rows, compute(rows)):
            out_ref[r, :] = val
        return carry

    def dead(sb, carry):
        out_ref[rows_of(sb), :] = jnp.zeros((MOE_SUB, out_ref.shape[1]), out_ref.dtype)
        return carry

    lax.fori_loop(0, nsub // 2, pair, 0)

    @pl.when(nsub % 2 == 1)
    def _():
        out_ref[rows_of(nsub - 1), :] = compute([rows_of(nsub - 1)])[0]

    lax.fori_loop(nsub, MOE_TM // MOE_SUB, dead, 0)


def _moe_new_expert(be_ref):
    i = pl.program_id(1)
    return (i == 0) | (be_ref[i] != be_ref[jnp.maximum(i - 1, 0)])


def _moe_up_kernel(be_ref, nv_ref, x_ref, wg_ref, wu_ref, bg_ref, bu_ref, h_ref, wgb_ref, wub_ref):
    @pl.when(_moe_new_expert(be_ref))
    def _():
        wgb_ref[...] = wg_ref[0].astype(BF16)
        wub_ref[...] = wu_ref[0].astype(BF16)

    def compute(rows):
        xs = [jnp.concatenate(_unpack_bf16_pairs(x_ref[r, :]), axis=1).astype(BF16) for r in rows]
        gates = [jnp.dot(x, wgb_ref[...], preferred_element_type=F32) + bg_ref[0] for x in xs]
        ups = [jnp.dot(x, wub_ref[...], preferred_element_type=F32) + bu_ref[0] for x in xs]
        outs = []
        for gate, up in zip(gates, ups):
            gate = jnp.minimum(gate, SWIGLU_LIMIT)
            up = jnp.clip(up, -SWIGLU_LIMIT, SWIGLU_LIMIT)
            outs.append(((up + 1.0) * gate * jax.nn.sigmoid(SWIGLU_ALPHA * gate)).astype(h_ref.dtype))
        return outs

    _moe_sub_blocks(nv_ref, h_ref, compute)


def _moe_down_kernel(be_ref, nv_ref, h_ref, wd_ref, bd_ref, y_ref, wdb_ref):
    @pl.when(_moe_new_expert(be_ref))
    def _():
        wdb_ref[...] = wd_ref[0].astype(BF16)

    half = wdb_ref.shape[1] // 2

    def compute(rows):
        ys = [jnp.dot(h_ref[r, :], wdb_ref[...], preferred_element_type=F32) + bd_ref[0] for r in rows]
        return [_pack_bf16_pairs(y[:, :half], y[:, half:]) for y in ys]

    _moe_sub_blocks(nv_ref, y_ref, compute)


def _moe_experts(rows, block_e, n_valid, w_gate, b_gate, w_up, b_up, w_down, b_down):
    n_rows, d = rows.shape[0], 2 * rows.shape[1]
    n_blocks = n_rows // MOE_TM
    tn = MOE_TN
    tile = lambda width: pl.BlockSpec((MOE_TM, width), lambda n, i, be, nv: (i, 0))
    wcol = lambda k: pl.BlockSpec((1, k, tn), lambda n, i, be, nv: (be[i], 0, n))
    bcol = pl.BlockSpec((1, 1, tn), lambda n, i, be, nv: (be[i], 0, n))
    params = _cparams(("arbitrary", "arbitrary"))
    h = pl.pallas_call(
        _moe_up_kernel,
        out_shape=jax.ShapeDtypeStruct((n_rows, D_FF), BF16),
        grid_spec=pltpu.PrefetchScalarGridSpec(
            num_scalar_prefetch=2, grid=(D_FF // tn, n_blocks),
            in_specs=[tile(d // 2), wcol(d), wcol(d), bcol, bcol],
            out_specs=pl.BlockSpec((MOE_TM, tn), lambda n, i, be, nv: (i, n)),
            scratch_shapes=[pltpu.VMEM((d, tn), BF16), pltpu.VMEM((d, tn), BF16)]),
        compiler_params=params, name="moe_expert_up",
    )(block_e, n_valid, rows, w_gate, w_up, b_gate.reshape(N_EXP, 1, D_FF), b_up.reshape(N_EXP, 1, D_FF))
    return pl.pallas_call(
        _moe_down_kernel,
        out_shape=jax.ShapeDtypeStruct((n_rows, d // 2), jnp.uint32),
        grid_spec=pltpu.PrefetchScalarGridSpec(
            num_scalar_prefetch=2, grid=(d // tn, n_blocks),
            in_specs=[tile(D_FF), wcol(D_FF), bcol],
            out_specs=pl.BlockSpec((MOE_TM, tn // 2), lambda n, i, be, nv: (i, n)),
            scratch_shapes=[pltpu.VMEM((D_FF, tn), BF16)]),
        compiler_params=params, name="moe_expert_down",
    )(block_e, n_valid, h, w_down, b_down.reshape(N_EXP, 1, d))


ROUTE_TT = 512
MOVE_TT = 256
HALF = D_MODEL // 2


def _pack_bf16_pairs(lo, hi):
    as_bits = lambda v: pltpu.bitcast(v.astype(BF16).astype(F32), jnp.uint32)
    return (as_bits(lo) >> 16) | (as_bits(hi) & jnp.uint32(0xFFFF0000))


def _unpack_bf16_pairs(w):
    return pltpu.bitcast(w << 16, F32), pltpu.bitcast(w & jnp.uint32(0xFFFF0000), F32)


def _route_kernel(h_ref, wr_ref, br_ref, e_ref, w_ref, p_ref, cnt_ref, run_ref, *, tt):
    @pl.when(pl.program_id(0) == 0)
    def _():
        run_ref[...] = jnp.zeros_like(run_ref)

    h = h_ref[...]
    logits = (jnp.dot(h, wr_ref[0], preferred_element_type=F32) + jnp.dot(h, wr_ref[1], preferred_element_type=F32)
              + br_ref[...])
    lane = lax.broadcasted_iota(jnp.int32, (tt, LANES), 1)
    v = jnp.where(lane < N_EXP, logits, -BIG)
    tops, hits, firsts = [], [], []
    for _ in range(TOP_K):
        m = jnp.max(v, axis=1, keepdims=True)
        first = jnp.min(jnp.where(v == m, lane, LANES), axis=1, keepdims=True)
        hit = lane == first
        v = jnp.where(hit, DROPPED, v)
        tops.append(m), hits.append(hit), firsts.append(first)
    ex = [jnp.exp(m - tops[0]) for m in tops]
    inv = 1.0 / functools.reduce(lambda a, b: a + b, ex)
    onehot = functools.reduce(lambda a, b: a + b, [jnp.where(hh, 1.0, 0.0) for hh in hits]).astype(BF16)
    ri = lax.broadcasted_iota(jnp.int32, (tt, tt), 0)
    ci = lax.broadcasted_iota(jnp.int32, (tt, tt), 1)
    before = jnp.where(ci < ri, 1.0, 0.0).astype(BF16)
    rank = jnp.dot(before, onehot, preferred_element_type=F32) + run_ref[0:1]
    run_ref[...] = run_ref[...] + jnp.dot(jnp.ones((8, tt), BF16), onehot, preferred_element_type=F32)
    cnt_ref[...] = run_ref[...].astype(jnp.int32)
    place = lambda cols, zero: functools.reduce(
        lambda acc, kc: jnp.where(lane == kc[0], kc[1], acc), list(enumerate(cols)), zero)
    e_ref[...] = place(firsts, jnp.zeros((tt, LANES), jnp.int32))
    w_ref[...] = place([e * inv for e in ex], jnp.zeros((tt, LANES), F32))
    pos = [jnp.sum(jnp.where(hh, rank, 0.0), axis=1, keepdims=True).astype(jnp.int32) for hh in hits]
    p_ref[...] = place(pos, jnp.zeros((tt, LANES), jnp.int32))


def _route(h, w_router, b_router):
    n, d = h.shape
    tt = ROUTE_TT
    wr = jnp.pad(w_router, ((0, 0), (0, LANES - N_EXP)))
    hi = wr.astype(BF16)
    wr2 = jnp.stack([hi, (wr - hi.astype(F32)).astype(BF16)])
    br = jnp.pad(b_router, (0, LANES - N_EXP)).reshape(1, LANES)
    tok = lambda dt: jax.ShapeDtypeStruct((n, LANES), dt)
    tspec = pl.BlockSpec((tt, LANES), lambda i: (i, 0))
    return pl.pallas_call(
        functools.partial(_route_kernel, tt=tt),
        out_shape=(tok(jnp.int32), tok(F32), tok(jnp.int32), jax.ShapeDtypeStruct((8, LANES), jnp.int32)),
        grid=(n // tt,),
        in_specs=[pl.BlockSpec((tt, d), lambda i: (i, 0)), pl.BlockSpec((2, d, LANES), lambda i: (0, 0, 0)),
                  pl.BlockSpec((1, LANES), lambda i: (0, 0))],
        out_specs=(tspec, tspec, tspec, pl.BlockSpec((8, LANES), lambda i: (0, 0))),
        scratch_shapes=[pltpu.VMEM((8, LANES), F32)],
        compiler_params=_cparams(("arbitrary",)),
        name="moe_route",
    )(h, wr2, br)


def _dispatch_kernel(dest_ref, src_ref, init_ref, rows_ref, sem, *, tt):
    del init_ref
    t0 = pl.program_id(0) * tt

    def copy(j, k):
        return pltpu.make_async_copy(src_ref.at[pl.ds(j, 1)],
                                     rows_ref.at[pl.ds(dest_ref[(t0 + j) * TOP_K + k], 1)], sem)

    def start(j, carry):
        for k in range(TOP_K):
            copy(j, k).start(priority=k % 2)
        return carry

    def wait(j, carry):
        for k in range(TOP_K):
            copy(j, k).wait()
        return carry

    lax.fori_loop(0, tt, start, 0)
    lax.fori_loop(0, tt, wait, 0)


def _dispatch(dest, hpk, n_rows):
    n, width = hpk.shape
    tt = MOVE_TT
    return pl.pallas_call(
        functools.partial(_dispatch_kernel, tt=tt),
        out_shape=jax.ShapeDtypeStruct((n_rows, width), jnp.uint32),
        grid_spec=pltpu.PrefetchScalarGridSpec(
            num_scalar_prefetch=1, grid=(n // tt,),
            in_specs=[pl.BlockSpec((tt, width), lambda i, dest: (i, 0)), pl.BlockSpec(memory_space=pl.ANY)],
            out_specs=pl.BlockSpec(memory_space=pl.ANY),
            scratch_shapes=[pltpu.SemaphoreType.DMA]),
        input_output_aliases={2: 0},
        compiler_params=_cparams(("arbitrary",)),
        name="moe_dispatch",
    )(dest, hpk, jnp.zeros((n_rows, width), jnp.uint32))


def _combine_kernel(dest_ref, y_ref, w_ref, x1_ref, g_ref, o_ref, buf_ref, sem, *, tt, norm):
    t0 = pl.program_id(0) * tt

    def copy(j, k):
        return pltpu.make_async_copy(y_ref.at[pl.ds(dest_ref[(t0 + j) * TOP_K + k], 1)],
                                     buf_ref.at[k, pl.ds(j, 1)], sem)

    def start(j, carry):
        for k in range(TOP_K):
            copy(j, k).start(priority=k % 2)
        return carry

    def wait(j, carry):
        for k in range(TOP_K):
            copy(j, k).wait()
        return carry

    lax.fori_loop(0, tt, start, 0)
    lax.fori_loop(0, tt, wait, 0)
    wts = w_ref[...]
    lane = lax.broadcasted_iota(jnp.int32, wts.shape, 1)
    x = x1_ref[...]
    nq = MOE_TN // 2
    for k in range(TOP_K):
        wk = jnp.sum(jnp.where(lane == k, wts, 0.0), axis=1, keepdims=True)
        parts = []
        for n in range(D_MODEL // MOE_TN):
            lo, hi = _unpack_bf16_pairs(buf_ref[k, :, n * nq:(n + 1) * nq])
            parts += [lo, hi]
        x = x + wk * jnp.concatenate(parts, axis=1)
    if norm:
        x = x * lax.rsqrt(jnp.mean(x * x, axis=-1, keepdims=True) + EPS) * g_ref[...]
    o_ref[...] = x.astype(o_ref.dtype)


def _combine(dest, ypk, wts, x1, gain):
    n, d = x1.shape
    tt = MOVE_TT
    norm = gain is not None
    gain = gain if norm else jnp.ones((d,), F32)
    return pl.pallas_call(
        functools.partial(_combine_kernel, tt=tt, norm=norm),
        out_shape=jax.ShapeDtypeStruct((n, d), F32),
        grid_spec=pltpu.PrefetchScalarGridSpec(
            num_scalar_prefetch=1, grid=(n // tt,),
            in_specs=[pl.BlockSpec(memory_space=pl.ANY),
                      pl.BlockSpec((tt, LANES), lambda i, dest: (i, 0)),
                      pl.BlockSpec((tt, d), lambda i, dest: (i, 0)),
                      pl.BlockSpec((1, d), lambda i, dest: (0, 0))],
            out_specs=pl.BlockSpec((tt, d), lambda i, dest: (i, 0)),
            scratch_shapes=[pltpu.VMEM((TOP_K, tt, d // 2), jnp.uint32), pltpu.SemaphoreType.DMA]),
        compiler_params=_cparams(("arbitrary",)),
        name="moe_combine_norm",
    )(dest, ypk, wts, x1, gain.reshape(1, d))


def _native_sparse_attention(main, small, positions, pe_ck, w1_ck, w2_ck, pe_cv, w1_cv, w2_cv, col, gate_lane0, bsz, s):
    half = ROT_DIM // 2
    inv_freq = ROPE_THETA ** (-jnp.arange(half, dtype=F32) * 2.0 / ROT_DIM)
    ang = positions.astype(F32)[..., None] * inv_freq
    cos, sin = jnp.cos(ang), jnp.sin(ang)
    rest = (bsz, s, DH_B - ROT_DIM)
    cos_t = jnp.concatenate([cos, cos, jnp.ones(rest, F32)], axis=-1)
    sin_t = jnp.concatenate([-sin, sin, jnp.zeros(rest, F32)], axis=-1)
    q, kc_t, vc_t, ks, kw = _nsa_rope(main, cos_t, sin_t, col, bsz, s)
    k_cmp, v_cmp = _nsa_compress(kc_t, vc_t, pe_ck, w1_ck, w2_ck, pe_cv, w1_cv, w2_cv, bsz, s)
    o_cmp, maskbias = _nsa_cmp_select(q, k_cmp, v_cmp, bsz, s)
    return _nsa_local(q, maskbias, ks, main, kw, o_cmp, small, col, gate_lane0, bsz, s)


def _moe_ffn(hf, hpk, x1, final_gain, w_router, b_router, w_gate, b_gate, w_up, b_up, w_down, b_down):
    n_tok, d = x1.shape
    eidx, wts, pos, cnt = _route(hf, w_router, b_router)
    counts = cnt[0, :N_EXP]
    padded = ((counts + MOE_TM - 1) // MOE_TM) * MOE_TM
    pad_end = jnp.cumsum(padded)
    pad_start = pad_end - padded
    part = counts % MOE_TM
    gap = jnp.where(part > 0, MOE_TM - part, 0)
    pick = lambda table: jnp.sum(jnp.where(eidx[:, :TOP_K, None] == jnp.arange(N_EXP), table, 0), axis=-1)
    rank = pos[:, :TOP_K]
    dest = pick(pad_start) + rank + jnp.where(rank >= pick(part), pick(gap), 0)
    dest = dest.reshape(-1).astype(jnp.int32)
    n_blocks = (n_tok * TOP_K + N_EXP * (MOE_TM - 1) + MOE_TM - 1) // MOE_TM
    rows = _dispatch(dest, hpk, n_blocks * MOE_TM)
    tile0 = jnp.arange(n_blocks, dtype=jnp.int32) * MOE_TM
    block_e = jnp.minimum(jnp.searchsorted(pad_end, tile0, side='right'), N_EXP - 1).astype(jnp.int32)
    first = (tile0 == pad_start[block_e]) & (part[block_e] > 0)
    n_valid = jnp.where(tile0 < pad_end[-1], jnp.where(first, part[block_e], MOE_TM), 0).astype(jnp.int32)
    ypk = _moe_experts(rows, block_e, n_valid, w_gate, b_gate, w_up, b_up, w_down, b_down)
    return _combine(dest, ypk, wts, x1, final_gain)


def _layer(x, positions, norm_mix, w_in, conv_w, a_log, dt_bias, norm_gdn, pe_ck, w1_ck, w2_ck,
           pe_cv, w1_cv, w2_cv, w_proj_a, w_proj_b, w_out, norm_ffn, w_router, b_router,
           w_gate, b_gate, w_up, b_up, w_down, b_down, final_gain):
    bsz, s, d = x.shape
    n_tok = bsz * s
    x2 = x.reshape(n_tok, d)
    h = _rmsnorm(x2, norm_mix, BF16)
    n_small = 2 * HA + 3 * HB
    sp = SPLIT_POINTS
    w_main = jnp.concatenate([w_in[:, :sp[3]], w_in[:, sp[5]:sp[12]], w_in[:, sp[13]:]], axis=1).astype(BF16)
    w_small = jnp.concatenate([w_in[:, sp[3]:sp[5]], w_in[:, sp[12]:sp[13]]], axis=1)
    w_small = jnp.pad(w_small, ((0, 0), (0, LANES - n_small))).astype(BF16)
    main = _matmul(h, w_main, BF16).reshape(bsz, s, -1)
    small = _matmul(h, w_small, F32).reshape(bsz, s, -1)
    names = ("qa", "ka", "va", "za", "qb", "kc", "vc", "ks", "vs", "kw", "vw", "gm")
    sizes = (HA * DK_A, HA * DK_A, HA * DV_A, HA * DV_A,
             HB * DH_B, G_KV * DH_B, G_KV * DH_B, G_KV * DH_B, G_KV * DH_B, G_KV * DH_B, G_KV * DH_B, 2 * D_MODEL)
    col = {nm: sum(sizes[:i]) for i, nm in enumerate(names)}
    aa, ba = small[..., :HA], small[..., HA:2 * HA]

    o_a = _gated_deltanet(main, aa, ba, conv_w, a_log, dt_bias, norm_gdn, bsz, s)
    o_b = _native_sparse_attention(main, small, positions, pe_ck, w1_ck, w2_ck, pe_cv, w1_cv, w2_cv,
                                   col, 2 * HA, bsz, s)
    merged = _merge(o_a.reshape(n_tok, d), o_b.reshape(n_tok, d), w_proj_a.astype(BF16),
                    w_proj_b.astype(BF16), main.reshape(n_tok, -1), col["gm"])
    x1, hf, hpk = _outproj(merged, w_out.astype(BF16), x2, norm_ffn)
    out = _moe_ffn(hf, hpk, x1, final_gain, w_router, b_router, w_gate, b_gate, w_up, b_up, w_down, b_down)
    return out.reshape(bsz, s, d)


def kernel(x, positions, norm_mix, w_in, conv_w, a_log, dt_bias, norm_gdn, pe_ck, w1_ck, w2_ck, pe_cv, w1_cv, w2_cv, w_proj_a, w_proj_b, w_out, norm_ffn, w_router, b_router, w_gate, b_gate, w_up, b_up, w_down, b_down, norm_final):
    depth = norm_mix.shape[0]
    for l in range(depth):
        x = _layer(x, positions, norm_mix[l], w_in[l], conv_w[l], a_log[l], dt_bias[l], norm_gdn[l],
                   pe_ck[l], w1_ck[l], w2_ck[l], pe_cv[l], w1_cv[l], w2_cv[l],
                   w_proj_a[l], w_proj_b[l], w_out[l], norm_ffn[l], w_router[l], b_router[l],
                   w_gate[l], b_gate[l], w_up[l], b_up[l], w_down[l], b_down[l],
                   norm_final if l + 1 == depth else None)
    return x
```

```python
import functools
import math

import jax
import jax.numpy as jnp
import numpy as np
from jax import lax
from jax.experimental import pallas as pl
from jax.experimental.pallas import tpu as pltpu

F32 = jnp.float32
BF16 = jnp.bfloat16

D_MODEL = 2048
EPS = 1e-6
NEG = -1e30
HA = D_MODEL // 128
DK_A = 128
DV_A = 128
CONV_W = 4
CHUNK = 64
HB = D_MODEL // 128
G_KV = 4
R_GRP = HB // G_KV
DH_B = 128
ROT_DIM = DH_B // 4
ROPE_THETA = 500000.0
L_CMP = 32
STRIDE_CMP = 16
L_SLC = 64
T_SEL = 8
WINDOW = 512
N_EXP = 32
TOP_K = 4
D_FF = D_MODEL
SWIGLU_LIMIT = 7.0
SWIGLU_ALPHA = 1.702
SPLIT_SIZES = (HA * DK_A, HA * DK_A, HA * DV_A, HA * DV_A, HA, HA,
               HB * DH_B, G_KV * DH_B, G_KV * DH_B, G_KV * DH_B, G_KV * DH_B, G_KV * DH_B, G_KV * DH_B,
               3 * HB, 2 * D_MODEL)
SPLIT_POINTS = tuple(sum(SPLIT_SIZES[:i + 1]) for i in range(len(SPLIT_SIZES) - 1))

V7X_VMEM_LIMIT_BYTES = 56 * 1024 * 1024
LANES = 128
MOE_TM = 1024
MOE_SUB = 256
MOE_TN = 512
ATT_TQ = 256


def _cparams(sem):
    return pltpu.CompilerParams(dimension_semantics=sem, vmem_limit_bytes=V7X_VMEM_LIMIT_BYTES)


def _mm_kernel(x_ref, w_ref, o_ref):
    o_ref[...] = jnp.dot(x_ref[...], w_ref[...], preferred_element_type=F32).astype(o_ref.dtype)


def _matmul(x, w, out_dtype, tm=1024, tn=1024):
    m, k = x.shape
    n = w.shape[1]
    tm, tn = min(tm, m), min(tn, n)
    assert m % tm == 0 and n % tn == 0
    return pl.pallas_call(
        _mm_kernel,
        out_shape=jax.ShapeDtypeStruct((m, n), out_dtype),
        grid=(n // tn, m // tm),
        in_specs=[pl.BlockSpec((tm, k), lambda j, i: (i, 0)),
                  pl.BlockSpec((k, tn), lambda j, i: (0, j))],
        out_specs=pl.BlockSpec((tm, tn), lambda j, i: (i, j)),
        compiler_params=_cparams(("parallel", "parallel")),
        name="dense_matmul",
    )(x, w)


def _rmsnorm_kernel(x_ref, g_ref, o_ref):
    x = x_ref[...]
    y = x * lax.rsqrt(jnp.mean(x * x, axis=-1, keepdims=True) + EPS)
    o_ref[...] = (y * g_ref[...]).astype(o_ref.dtype)


def _rmsnorm(x, gain, out_dtype, tm=512):
    m, d = x.shape
    return pl.pallas_call(
        _rmsnorm_kernel,
        out_shape=jax.ShapeDtypeStruct((m, d), out_dtype),
        grid=(m // tm,),
        in_specs=[pl.BlockSpec((tm, d), lambda i: (i, 0)),
                  pl.BlockSpec((1, d), lambda i: (0, 0))],
        out_specs=pl.BlockSpec((tm, d), lambda i: (i, 0)),
        compiler_params=_cparams(("parallel",)),
        name="rmsnorm",
    )(x, gain.reshape(1, d))


def _add_rmsnorm_kernel(x_ref, y_ref, g_ref, o_ref):
    x = x_ref[...] + y_ref[...]
    y = x * lax.rsqrt(jnp.mean(x * x, axis=-1, keepdims=True) + EPS)
    o_ref[...] = (y * g_ref[...]).astype(o_ref.dtype)


def _add_rmsnorm(x, y, gain, out_dtype, tm=512):
    m, d = x.shape
    row = pl.BlockSpec((tm, d), lambda i: (i, 0))
    return pl.pallas_call(
        _add_rmsnorm_kernel,
        out_shape=jax.ShapeDtypeStruct((m, d), out_dtype),
        grid=(m // tm,),
        in_specs=[row, row, pl.BlockSpec((1, d), lambda i: (0, 0))],
        out_specs=row,
        compiler_params=_cparams(("parallel",)),
        name="add_rmsnorm",
    )(x, y, gain.reshape(1, d))


def _merge_kernel(oa_ref, ob_ref, wa_ref, wb_ref, ga_ref, gb_ref, o_ref):
    ya = jnp.dot(oa_ref[...], wa_ref[...], preferred_element_type=F32)
    yb = jnp.dot(ob_ref[...], wb_ref[...], preferred_element_type=F32)
    o_ref[...] = (jax.nn.sigmoid(ga_ref[...].astype(F32)) * ya
                  + jax.nn.sigmoid(gb_ref[...].astype(F32)) * yb).astype(o_ref.dtype)


def _merge(o_a, o_b, w_a, w_b, main2d, gm_col0, tm=512, tn=1024):
    m, d = o_a.shape
    assert gm_col0 % tn == 0 and d % tn == 0
    g0 = gm_col0 // tn
    lhs = pl.BlockSpec((tm, d), lambda j, i: (i, 0))
    rhs = pl.BlockSpec((d, tn), lambda j, i: (0, j))
    return pl.pallas_call(
        _merge_kernel,
        out_shape=jax.ShapeDtypeStruct((m, d), BF16),
        grid=(d // tn, m // tm),
        in_specs=[lhs, lhs, rhs, rhs,
                  pl.BlockSpec((tm, tn), lambda j, i: (i, g0 + j)),
                  pl.BlockSpec((tm, tn), lambda j, i: (i, g0 + d // tn + j))],
        out_specs=pl.BlockSpec((tm, tn), lambda j, i: (i, j)),
        compiler_params=_cparams(("parallel", "parallel")),
        name="mixer_merge",
    )(o_a, o_b, w_a, w_b, main2d, main2d)


def _outproj_kernel(m_ref, w_ref, x_ref, g_ref, x1_ref, h_ref, hpk_ref):
    x1 = x_ref[...] + jnp.dot(m_ref[...], w_ref[...], preferred_element_type=F32)
    x1_ref[...] = x1
    y = x1 * lax.rsqrt(jnp.mean(x1 * x1, axis=-1, keepdims=True) + EPS) * g_ref[...]
    h_ref[...] = y.astype(h_ref.dtype)
    half = y.shape[1] // 2
    hpk_ref[...] = _pack_bf16_pairs(y[:, :half], y[:, half:])


def _outproj(merged, w_out, x, gain, tm=512):
    m, d = x.shape
    row = lambda width=d: pl.BlockSpec((tm, width), lambda i: (i, 0))
    return pl.pallas_call(
        _outproj_kernel,
        out_shape=(jax.ShapeDtypeStruct((m, d), F32), jax.ShapeDtypeStruct((m, d), BF16),
                   jax.ShapeDtypeStruct((m, d // 2), jnp.uint32)),
        grid=(m // tm,),
        in_specs=[row(), pl.BlockSpec((d, d), lambda i: (0, 0)), row(), pl.BlockSpec((1, d), lambda i: (0, 0))],
        out_specs=(row(), row(), row(d // 2)),
        compiler_params=_cparams(("parallel",)),
        name="out_proj_residual_norm",
    )(merged, w_out, x, gain.reshape(1, d))


GDN_COLS = 512
GDN_GROUP = 256
GDN_PH = 4
GDN_TS = 512


def _gdn_conv_kernel(x_ref, w_ref, o_ref):
    sec = pl.program_id(1) // (HA * DK_A // GDN_COLS)
    x = x_ref[0].astype(F32)
    w = w_ref[...]
    row = lax.broadcasted_iota(jnp.int32, x.shape, 0)
    y = x * w[CONV_W - 1:CONV_W]
    for i in range(CONV_W - 1):
        sh = CONV_W - 1 - i
        y = y + jnp.where(row >= sh, pltpu.roll(x, sh, axis=0), 0.0) * w[i:i + 1]
    y = y * jax.nn.sigmoid(y)
    qscale = jnp.where(sec == 0, DK_A ** -0.5, 1.0)
    for h in range(GDN_COLS // DK_A):
        yh = y[:, h * DK_A:(h + 1) * DK_A]
        inv = lax.rsqrt(jnp.sum(yh * yh, axis=-1, keepdims=True) + EPS) * qscale
        o_ref[0, 0, h] = (yh * jnp.where(sec < 2, inv, 1.0)).astype(o_ref.dtype)


def _gdn_conv(main, conv_w, bsz, s):
    ncol = 3 * HA * DK_A // GDN_COLS
    hpc = GDN_COLS // DK_A
    return pl.pallas_call(
        _gdn_conv_kernel,
        out_shape=jax.ShapeDtypeStruct((3, bsz, HA, s, DK_A), BF16),
        grid=(bsz, ncol),
        in_specs=[pl.BlockSpec((1, s, GDN_COLS), lambda b, c: (b, 0, c)),
                  pl.BlockSpec((CONV_W, GDN_COLS), lambda b, c: (0, c))],
        out_specs=pl.BlockSpec((1, 1, hpc, s, DK_A), lambda b, c: (c // (HA // hpc), b, c % (HA // hpc), 0, 0)),
        compiler_params=_cparams(("parallel", "parallel")),
        name="gdn_conv_silu_l2norm",
    )(main, conv_w)


def _col_rep(row, n):
    return jnp.broadcast_to(row, (LANES, n)).T


def _dot_hilo(x, m):
    hi = x.astype(BF16)
    lo = (x - hi.astype(F32)).astype(BF16)
    return jnp.dot(hi, m, preferred_element_type=F32) + jnp.dot(lo, m, preferred_element_type=F32)


def _gdn_prep_kernel(q_ref, k_ref, v_ref, g_ref, b_ref, u_ref, w_ref, qg_ref, kd_ref, a_ref, egl_ref):
    n = GDN_GROUP
    ri = lax.broadcasted_iota(jnp.int32, (n, n), 0)
    ci = lax.broadcasted_iota(jnp.int32, (n, n), 1)
    same = (ri // CHUNK) == (ci // CHUNK)
    incl = same & (ri >= ci)
    strict = same & (ri > ci)
    one_if = lambda m: jnp.where(m, 1.0, 0.0).astype(BF16)
    cum_m, tot_m = one_if(same & (ri <= ci)), one_if(same)
    eye = jnp.where(ri == ci, 1.0, 0.0)
    wide = lambda c: jnp.concatenate([c] * (n // LANES), axis=1)
    nt = (((1,), (1,)), ((), ()))
    heads = range(GDN_PH)
    ts, ps, rhs = [], [], []
    for h in heads:
        q, k, v = q_ref[0, 0, h], k_ref[0, 0, h], v_ref[0, 0, h]
        g8 = jnp.broadcast_to(g_ref[0, h], (8, n))
        gc_row = _dot_hilo(g8, cum_m)[0:1]
        gl_row = _dot_hilo(g8, tot_m)[0:1]
        gc_c, gl_c, b_c = _col_rep(gc_row, n), _col_rep(gl_row, n), _col_rep(b_ref[0, h], n)
        decay = jnp.exp(jnp.where(incl, wide(gc_c) - gc_row, NEG))
        kk = lax.dot_general(k, k, nt, preferred_element_type=F32)
        qk = lax.dot_general(q, k, nt, preferred_element_type=F32)
        xb = jnp.where(strict, -(kk * wide(b_c) * decay), 0.0).astype(BF16)
        a = qk * decay
        kf = k.astype(F32)
        egc = jnp.exp(gc_c)
        qg_ref[0, h] = (q.astype(F32) * egc).astype(qg_ref.dtype)
        kd_ref[0, h] = (kf * jnp.exp(gl_c - gc_c)).astype(kd_ref.dtype)
        for c in range(n // CHUNK):
            blk = slice(c * CHUNK, (c + 1) * CHUNK)
            a_ref[0, h, blk, :] = a[blk, blk].astype(a_ref.dtype)
        egl = jnp.exp(gl_c)
        egl_ref[0, h, 0] = jnp.concatenate([egl[c * CHUNK:c * CHUNK + 1] for c in range(n // CHUNK)], axis=0)
        rhs.append(jnp.concatenate([(v.astype(F32) * b_c).astype(BF16), (kf * b_c * egc).astype(BF16)], axis=1))
        ts.append(eye + xb.astype(F32))
        ps.append(xb)
    ps = [jnp.dot(p, p, preferred_element_type=F32).astype(BF16) for p in ps]
    for step in range(5):
        for h in heads:
            if step < 4:
                tp = jnp.dot(jnp.concatenate([ts[h].astype(BF16), ps[h]], axis=0), ps[h], preferred_element_type=F32)
                ts[h] = ts[h] + tp[:n]
                ps[h] = tp[n:].astype(BF16)
            else:
                ts[h] = ts[h] + jnp.dot(ts[h].astype(BF16), ps[h], preferred_element_type=F32)
    for h in heads:
        uw = jnp.dot(ts[h].astype(BF16), rhs[h], preferred_element_type=F32)
        u_ref[0, h] = uw[:, :DV_A].astype(u_ref.dtype)
        w_ref[0, h] = uw[:, DV_A:].astype(w_ref.dtype)


def _gdn_prep(qkv, g_t, beta_t, bsz, s):
    n, ph = GDN_GROUP, GDN_PH
    tok = lambda width, dt: jax.ShapeDtypeStruct((bsz, HA, s, width), dt)
    tspec = lambda width: pl.BlockSpec((1, ph, n, width), lambda b, h, i: (b, h, i, 0))
    qspec = lambda sec: pl.BlockSpec((1, 1, ph, n, DK_A), lambda b, h, i: (sec, b, h, i, 0))
    rspec = pl.BlockSpec((1, ph, 1, n), lambda b, h, i: (b, h, 0, i))
    return pl.pallas_call(
        _gdn_prep_kernel,
        out_shape=(tok(DV_A, BF16), tok(DK_A, BF16), tok(DK_A, BF16), tok(DK_A, BF16), tok(CHUNK, BF16),
                   jax.ShapeDtypeStruct((bsz, HA, s // n, n // CHUNK, LANES), F32)),
        grid=(bsz, HA // ph, s // n),
        in_specs=[qspec(0), qspec(1), qspec(2), rspec, rspec],
        out_specs=(tspec(DV_A), tspec(DK_A), tspec(DK_A), tspec(DK_A), tspec(CHUNK),
                   pl.BlockSpec((1, ph, 1, n // CHUNK, LANES), lambda b, h, i: (b, h, i, 0, 0))),
        compiler_params=_cparams(("parallel", "parallel", "parallel")),
        name="gdn_chunk_prep",
    )(qkv, qkv, qkv, g_t, beta_t)


def _gdn_scan_kernel(u_ref, w_ref, qg_ref, kd_ref, a_ref, egl_ref, z_ref, ng_ref, o_ref, state_ref, *, nchunk):
    @pl.when(pl.program_id(1) == 0)
    def _():
        state_ref[...] = jnp.zeros_like(state_ref)

    tn = (((0,), (0,)), ((), ()))
    heads = range(HA)

    def body(c, carry):
        rows = pl.ds(pl.multiple_of(c * CHUNK, CHUNK), CHUNK)
        st = [state_ref[h] for h in heads]
        sb = [x.astype(BF16) for x in st]
        vb = [(u_ref[0, h, rows, :].astype(F32)
               - jnp.dot(w_ref[0, h, rows, :], sb[h], preferred_element_type=F32)).astype(BF16) for h in heads]
        o = [jnp.dot(qg_ref[0, h, rows, :], sb[h], preferred_element_type=F32)
             + jnp.dot(a_ref[0, h, rows, :], vb[h], preferred_element_type=F32) for h in heads]
        for h in heads:
            state_ref[h] = (st[h] * egl_ref[0, h, pl.ds(c, 1), :]
                            + lax.dot_general(kd_ref[0, h, rows, :], vb[h], tn, preferred_element_type=F32))
        for h in heads:
            cols = slice(h * DV_A, (h + 1) * DV_A)
            z = z_ref[0, rows, cols].astype(F32)
            on = o[h] * lax.rsqrt(jnp.mean(o[h] * o[h], axis=-1, keepdims=True) + EPS) * ng_ref[...]
            o_ref[0, rows, cols] = (on * (z * jax.nn.sigmoid(z))).astype(o_ref.dtype)
        return carry

    lax.fori_loop(0, nchunk, body, 0)


def _gdn_scan(u, w, qg, kd, a, egl, main, norm_gdn, bsz, s):
    ts = GDN_TS
    z_blk0 = 3 * HA * DK_A // (HA * DV_A)
    hspec = lambda width: pl.BlockSpec((1, HA, ts, width), lambda b, i: (b, 0, i, 0))
    return pl.pallas_call(
        functools.partial(_gdn_scan_kernel, nchunk=ts // CHUNK),
        out_shape=jax.ShapeDtypeStruct((bsz, s, HA * DV_A), BF16),
        grid=(bsz, s // ts),
        in_specs=[hspec(DV_A), hspec(DK_A), hspec(DK_A), hspec(DK_A), hspec(CHUNK),
                  pl.BlockSpec((1, HA, ts // CHUNK, LANES), lambda b, i: (b, 0, i, 0)),
                  pl.BlockSpec((1, ts, HA * DV_A), lambda b, i: (b, i, z_blk0)),
                  pl.BlockSpec((1, DV_A), lambda b, i: (0, 0))],
        out_specs=pl.BlockSpec((1, ts, HA * DV_A), lambda b, i: (b, i, 0)),
        scratch_shapes=[pltpu.VMEM((HA, DK_A, DV_A), F32)],
        compiler_params=_cparams(("parallel", "arbitrary")),
        name="gdn_delta_scan",
    )(u, w, qg, kd, a, egl, main, norm_gdn.reshape(1, DV_A))


def _gated_deltanet(main, a_in, b_in, conv_w, a_log, dt_bias, norm_gdn, bsz, s):
    qkv = _gdn_conv(main, conv_w, bsz, s)
    g = -jnp.exp(a_log.astype(F32)) * jax.nn.softplus(a_in.astype(F32) + dt_bias.astype(F32))
    beta = jax.nn.sigmoid(b_in.astype(F32))
    g_t = g.transpose(0, 2, 1).reshape(bsz, HA, 1, s)
    beta_t = beta.transpose(0, 2, 1).reshape(bsz, HA, 1, s)
    u, w, qg, kd, a, egl = _gdn_prep(qkv, g_t, beta_t, bsz, s)
    egl = egl.reshape(bsz, HA, s // CHUNK, LANES)
    return _gdn_scan(u, w, qg, kd, a, egl, main, norm_gdn, bsz, s)


NSA_TS = 512
BIG = 1e30
DROPPED = -3e38


def _rope(x, cos, sin, lane):
    half = ROT_DIM // 2
    partner = jnp.where(lane < half, pltpu.roll(x, DH_B - half, axis=1), pltpu.roll(x, half, axis=1))
    return x * cos + partner * sin


def _nsa_rope_kernel(q_ref, kc_ref, vc_ref, ks_ref, kw_ref, cos_ref, sin_ref,
                     qo_ref, kco_ref, vco_ref, kso_ref, kwo_ref):
    cos, sin = cos_ref[0], sin_ref[0]
    lane = lax.broadcasted_iota(jnp.int32, cos.shape, 1)
    head = lambda ref, h: ref[0, :, h * DH_B:(h + 1) * DH_B].astype(F32)
    for h in range(HB):
        qo_ref[0, :, h * DH_B:(h + 1) * DH_B] = (_rope(head(q_ref, h), cos, sin, lane) * DH_B ** -0.5).astype(qo_ref.dtype)
    for g in range(G_KV):
        cols = slice(g * DH_B, (g + 1) * DH_B)
        kco_ref[0, g] = _rope(head(kc_ref, g), cos, sin, lane).astype(kco_ref.dtype)
        vco_ref[0, g] = vc_ref[0, :, cols]
        kso_ref[0, :, cols] = _rope(head(ks_ref, g), cos, sin, lane).astype(kso_ref.dtype)
        kwo_ref[0, :, cols] = _rope(head(kw_ref, g), cos, sin, lane).astype(kwo_ref.dtype)


def _nsa_rope(main, cos, sin, col, bsz, s):
    ts = NSA_TS
    kvw = G_KV * DH_B
    tok = lambda width, c0: pl.BlockSpec((1, ts, width), lambda b, i: (b, i, c0 // width))
    tab = pl.BlockSpec((1, ts, DH_B), lambda b, i: (b, i, 0))
    grp = pl.BlockSpec((1, G_KV, ts, DH_B), lambda b, i: (b, 0, i, 0))
    flat = lambda width: pl.BlockSpec((1, ts, width), lambda b, i: (b, i, 0))
    return pl.pallas_call(
        _nsa_rope_kernel,
        out_shape=(jax.ShapeDtypeStruct((bsz, s, HB * DH_B), BF16),
                   jax.ShapeDtypeStruct((bsz, G_KV, s, DH_B), BF16), jax.ShapeDtypeStruct((bsz, G_KV, s, DH_B), BF16),
                   jax.ShapeDtypeStruct((bsz, s, kvw), BF16), jax.ShapeDtypeStruct((bsz, s, kvw), BF16)),
        grid=(bsz, s // ts),
        in_specs=[tok(HB * DH_B, col["qb"]), tok(kvw, col["kc"]), tok(kvw, col["vc"]), tok(kvw, col["ks"]),
                  tok(kvw, col["kw"]), tab, tab],
        out_specs=(flat(HB * DH_B), grp, grp, flat(kvw), flat(kvw)),
        compiler_params=_cparams(("parallel", "parallel")),
        name="nsa_rotary",
    )(main, main, main, main, main, cos, sin)


def _gelu_tanh(x):
    return 0.5 * x * (1.0 + jnp.tanh(math.sqrt(2.0 / math.pi) * (x + 0.044715 * x * x * x)))


def _nsa_compress_kernel(k_ref, v_ref, pek_ref, pev_ref, w1k_ref, w1v_ref, w2k_ref, w2v_ref, ko_ref, vo_ref):
    nseg = k_ref.shape[2]
    for x_ref, pe_ref, w1_ref, w2_ref, o_ref in ((k_ref, pek_ref, w1k_ref, w2k_ref, ko_ref),
                                                 (v_ref, pev_ref, w1v_ref, w2v_ref, vo_ref)):
        for g in range(G_KV):
            x = x_ref[0, g].astype(F32)
            lo = jnp.dot((x + pe_ref[0:1]).astype(BF16), w1_ref[0], preferred_element_type=F32)
            hi = jnp.dot((x + pe_ref[1:2]).astype(BF16), w1_ref[1], preferred_element_type=F32)
            pre = lo + pltpu.roll(hi, nseg - 1, axis=0)
            o_ref[0, g] = jnp.dot(_gelu_tanh(pre).astype(BF16), w2_ref[...],
                                  preferred_element_type=F32).astype(o_ref.dtype)


def _nsa_compress(kc_t, vc_t, pe_ck, w1_ck, w2_ck, pe_cv, w1_cv, w2_cv, bsz, s):
    assert L_CMP == 2 * STRIDE_CMP
    nseg = s // STRIDE_CMP
    width = STRIDE_CMP * DH_B
    seg = lambda t: t.reshape(bsz, G_KV, nseg, width)
    pe2 = lambda pe: pe.reshape(2, width)
    w1h = lambda w: w.reshape(2, width, DH_B).astype(BF16)
    xspec = pl.BlockSpec((1, G_KV, nseg, width), lambda b: (b, 0, 0, 0))
    pspec = pl.BlockSpec((2, width), lambda b: (0, 0))
    w1spec = pl.BlockSpec((2, width, DH_B), lambda b: (0, 0, 0))
    w2spec = pl.BlockSpec((DH_B, DH_B), lambda b: (0, 0))
    ospec = pl.BlockSpec((1, G_KV, nseg, DH_B), lambda b: (b, 0, 0, 0))
    oshape = jax.ShapeDtypeStruct((bsz, G_KV, nseg, DH_B), BF16)
    return pl.pallas_call(
        _nsa_compress_kernel,
        out_shape=(oshape, oshape),
        grid=(bsz,),
        in_specs=[xspec, xspec, pspec, pspec, w1spec, w1spec, w2spec, w2spec],
        out_specs=(ospec, ospec),
        compiler_params=_cparams(("parallel",)),
        name="nsa_compress",
    )(seg(kc_t), seg(vc_t), pe2(pe_ck), pe2(pe_cv), w1h(w1_ck), w1h(w1_cv), w2_ck.astype(BF16), w2_cv.astype(BF16))


def _nsa_cmp_kernel(q_ref, kc_ref, vc_ref, ov_ref, o_ref, mb_ref, *, tq, n_slc):
    i = pl.program_id(2)
    rows = R_GRP * tq
    ncmp = kc_ref.shape[2]
    qa = _stack_heads(q_ref[0], None)
    s = lax.dot_general(qa, kc_ref[0, 0], (((1,), (1,)), ((), ())), preferred_element_type=F32)
    t_row = (lax.broadcasted_iota(jnp.int32, (rows, ncmp), 0) & (tq - 1)) + i * tq
    c_end = lax.broadcasted_iota(jnp.int32, (rows, ncmp), 1) * STRIDE_CMP + (L_CMP - 1)
    valid = c_end <= t_row
    sm = jnp.where(valid, s, NEG)
    p = jnp.where(valid, jnp.exp(sm - jnp.max(sm, axis=1, keepdims=True)), 0.0)
    l = jnp.sum(p, axis=1, keepdims=True)
    p = p * (1.0 / jnp.where(l > 0.0, l, 1.0))
    _unstack_heads(o_ref, jnp.dot(p.astype(BF16), vc_ref[0, 0], preferred_element_type=F32), tq)
    psum = p[0:tq]
    for r in range(1, R_GRP):
        psum = psum + p[r * tq:(r + 1) * tq]
    imp = _dot_hilo(psum, ov_ref[...])
    nb = -(-n_slc // 8) * 8
    v = imp.T[:nb]
    blk = lax.broadcasted_iota(jnp.int32, (nb, tq), 0)
    cur = (lax.broadcasted_iota(jnp.int32, (nb, tq), 1) + i * tq) // L_SLC
    forced = (blk == 0) | (blk == cur)
    v = jnp.where(forced, BIG, jnp.where(blk <= cur, v, -BIG))
    sel = jnp.zeros((nb, tq), F32)
    for _ in range(T_SEL):
        m = jnp.max(v, axis=0, keepdims=True)
        first = jnp.min(jnp.where(v == m, blk, LANES), axis=0, keepdims=True)
        hit = blk == first
        sel = jnp.where(hit, 1.0, sel)
        v = jnp.where(hit, DROPPED, v)
    bias = (jnp.where(blk <= cur, sel, 0.0) - 1.0) * BIG
    bias = jnp.concatenate([bias, jnp.zeros((LANES - nb, tq), F32)], axis=0)
    mb_ref[0, 0] = bias.T.astype(mb_ref.dtype)


def _nsa_cmp_select(q, k_cmp, v_cmp, bsz, s):
    tq = ATT_TQ
    nseg = s // STRIDE_CMP
    n_slc = s // L_SLC
    assert nseg <= LANES or nseg % LANES == 0
    c_start = np.arange(nseg) * STRIDE_CMP
    j_start = np.arange(n_slc) * L_SLC
    overlap = ((c_start[:, None] < j_start[None, :] + L_SLC) & (c_start[:, None] + L_CMP > j_start[None, :]))
    overlap = jnp.asarray(np.pad(overlap.astype(np.float32), ((0, 0), (0, LANES - n_slc))), BF16)
    qspec = pl.BlockSpec((1, tq, R_GRP * DH_B), lambda b, g, i: (b, i, g))
    cspec = pl.BlockSpec((1, 1, nseg, DH_B), lambda b, g, i: (b, g, 0, 0))
    return pl.pallas_call(
        functools.partial(_nsa_cmp_kernel, tq=tq, n_slc=n_slc),
        out_shape=(jax.ShapeDtypeStruct((bsz, s, HB * DH_B), BF16), jax.ShapeDtypeStruct((bsz, G_KV, s, LANES), BF16)),
        grid=(bsz, G_KV, s // tq),
        in_specs=[qspec, cspec, cspec, pl.BlockSpec((nseg, LANES), lambda b, g, i: (0, 0))],
        out_specs=(qspec, pl.BlockSpec((1, 1, tq, LANES), lambda b, g, i: (b, g, i, 0))),
        compiler_params=_cparams(("parallel", "parallel", "parallel")),
        name="nsa_compressed_select",
    )(q, k_cmp, v_cmp, overlap)


def _stack_heads(q, extra):
    parts = []
    for r in range(R_GRP):
        qr = q[:, r * DH_B:(r + 1) * DH_B]
        parts.append(qr if extra is None else jnp.concatenate([qr, extra], axis=1))
    return jnp.concatenate(parts, axis=0)


def _unstack_heads(o_ref, o, tq):
    for r in range(R_GRP):
        o_ref[0, :, r * DH_B:(r + 1) * DH_B] = o[r * tq:(r + 1) * tq].astype(o_ref.dtype)


def _heads_t(q, extra_t):
    parts = []
    for r in range(R_GRP):
        qt = q[:, r * DH_B:(r + 1) * DH_B].astype(F32).T.astype(BF16)
        parts.append(qt if extra_t is None else jnp.concatenate([qt, extra_t], axis=0))
    return jnp.concatenate(parts, axis=1)


def _softmax_step_t(ss, vs, carry):
    m, l, acc = carry
    m_new = functools.reduce(jnp.maximum, [jnp.max(s, axis=0, keepdims=True) for s in ss], m)
    alpha = jnp.exp(m - m_new)
    ps = [jnp.exp(s - m_new) for s in ss]
    l = alpha * l + functools.reduce(lambda a, b: a + b, [jnp.sum(p, axis=0, keepdims=True) for p in ps])
    tn = (((0,), (0,)), ((), ()))
    pv = [lax.dot_general(v, p.astype(BF16), tn, preferred_element_type=F32) for v, p in zip(vs, ps)]
    return m_new, l, functools.reduce(lambda a, b: a + b, pv, alpha * acc)


def _softmax_init_t(rows):
    return (jnp.full((1, rows), -jnp.inf, F32), jnp.zeros((1, rows), F32), jnp.zeros((DH_B, rows), F32))


def _selected_branch(i, q, mb, ks_ref, oh_ref, vs_ref, tq):
    rows = R_GRP * tq
    qat = _heads_t(q, mb.astype(F32).T.astype(BF16))

    def scores(j):
        keys = pl.ds(pl.multiple_of(j * tq, tq), tq)
        k = jnp.concatenate([ks_ref[0, keys, :], oh_ref[keys, :]], axis=1)
        return jnp.dot(k, qat, preferred_element_type=F32)

    def values(j):
        return vs_ref[0, pl.ds(pl.multiple_of(j * tq, tq), tq), :]

    def pair(jj, c):
        j = 2 * jj
        return _softmax_step_t([scores(j), scores(j + 1)], [values(j), values(j + 1)], c)

    carry = lax.fori_loop(0, i // 2, pair, _softmax_init_t(rows))
    k_loc = lax.broadcasted_iota(jnp.int32, (tq, tq), 0)
    t_loc = lax.broadcasted_iota(jnp.int32, (tq, tq), 1)
    causal = jnp.concatenate([jnp.where(k_loc <= t_loc, 0.0, NEG)] * R_GRP, axis=1)
    diag = lambda c: _softmax_step_t([scores(i) + causal], [values(i)], c)
    both = lambda c: _softmax_step_t([scores(i) + causal, scores(i - 1)], [values(i), values(i - 1)], c)
    m, l, acc = lax.cond((i % 2) == 1, both, diag, carry)
    return acc / l


def _window_branch(i, q, kw_refs, vw_refs, tq):
    rows = R_GRP * tq
    qt = _heads_t(q, None)
    k_loc = lax.broadcasted_iota(jnp.int32, (tq, tq), 0)
    t_loc = lax.broadcasted_iota(jnp.int32, (tq, tq), 1)
    bound = jnp.minimum(t_loc[0:1, :] + (i * tq + 1), WINDOW)
    ss = []
    for n in range(3):
        dist = t_loc + (2 - n) * tq - k_loc
        bias = jnp.where(dist.astype(jnp.uint32) < bound.astype(jnp.uint32), 0.0, NEG)
        ss.append(jnp.dot(kw_refs[n][0], qt, preferred_element_type=F32) + jnp.concatenate([bias] * R_GRP, axis=1))
    m, l, acc = _softmax_step_t(ss, [r[0] for r in vw_refs], _softmax_init_t(rows))
    return acc / l


def _nsa_local_kernel(q_ref, mb_ref, ks_ref, oh_ref, vs_ref, kw0, kw1, kw2, vw0, vw1, vw2, oc_ref, gate_ref, o_ref,
                      *, tq, gate_lane0):
    g, i = pl.program_id(1), pl.program_id(2)
    q = q_ref[0]
    o_slc = _selected_branch(i, q, mb_ref[0, 0], ks_ref, oh_ref, vs_ref, tq)
    o_win = _window_branch(i, q, (kw0, kw1, kw2), (vw0, vw1, vw2), tq)
    gates = jax.nn.sigmoid(gate_ref[0])
    lane = lax.broadcasted_iota(jnp.int32, gates.shape, 1)
    pick = lambda idx: jnp.sum(jnp.where(lane == idx, gates, 0.0), axis=1, keepdims=True)
    for r in range(R_GRP):
        base = gate_lane0 + (g * R_GRP + r) * 3
        rows = slice(r * tq, (r + 1) * tq)
        cols = slice(r * DH_B, (r + 1) * DH_B)
        o = (pick(base) * oc_ref[0, :, cols].astype(F32) + pick(base + 1) * o_slc[:, rows].T
             + pick(base + 2) * o_win[:, rows].T)
        o_ref[0, :, cols] = o.astype(o_ref.dtype)


def _nsa_local(q, maskbias, ks, main, kw, o_cmp, small, col, gate_lane0, bsz, s):
    tq = ATT_TQ
    assert 2 * tq >= WINDOW
    onehot = jnp.asarray(np.arange(s)[:, None] // L_SLC == np.arange(LANES)[None, :], BF16)
    qspec = pl.BlockSpec((1, tq, R_GRP * DH_B), lambda b, g, i: (b, i, g))
    seq = lambda c0: pl.BlockSpec((1, s, DH_B), lambda b, g, i: (b, 0, c0 // DH_B + g))
    back = lambda c0, n: pl.BlockSpec((1, tq, DH_B), lambda b, g, i: (b, jnp.maximum(i - n, 0), c0 // DH_B + g))
    return pl.pallas_call(
        functools.partial(_nsa_local_kernel, tq=tq, gate_lane0=gate_lane0),
        out_shape=jax.ShapeDtypeStruct((bsz, s, HB * DH_B), BF16),
        grid=(bsz, G_KV, s // tq),
        in_specs=[qspec,
                  pl.BlockSpec((1, 1, tq, LANES), lambda b, g, i: (b, g, i, 0)),
                  seq(0), pl.BlockSpec((s, LANES), lambda b, g, i: (0, 0)), seq(col["vs"]),
                  back(0, 2), back(0, 1), back(0, 0),
                  back(col["vw"], 2), back(col["vw"], 1), back(col["vw"], 0),
                  qspec, pl.BlockSpec((1, tq, LANES), lambda b, g, i: (b, i, 0))],
        out_specs=qspec,
        compiler_params=_cparams(("parallel", "parallel", "arbitrary")),
        name="nsa_selected_window_combine",
    )(q, maskbias, ks, onehot, main, kw, kw, kw, main, main, main, o_cmp, small)


def _moe_sub_blocks(nv_ref, out_ref, compute):
    nsub = (nv_ref[pl.program_id(1)] + MOE_SUB - 1) // MOE_SUB
    rows_of = lambda sb: pl.ds(pl.multiple_of(sb * MOE_SUB, MOE_SUB), MOE_SUB)

    def pair(p, carry):
        rows = [rows_of(2 * p), rows_of(2 * p + 1)]
        for r, val in zip(rows, compute(rows)):
            out_ref[r, :] = val
        return carry

    def dead(sb, carry):
        out_ref[rows_of(sb), :] = jnp.zeros((MOE_SUB, out_ref.shape[1]), out_ref.dtype)
        return carry

    lax.fori_loop(0, nsub // 2, pair, 0)

    @pl.when(nsub % 2 == 1)
    def _():
        out_ref[rows_of(nsub - 1), :] = compute([rows_of(nsub - 1)])[0]

    lax.fori_loop(nsub, MOE_TM // MOE_SUB, dead, 0)


def _moe_new_expert(be_ref):
    i = pl.program_id(1)
    return (i == 0) | (be_ref[i] != be_ref[jnp.maximum(i - 1, 0)])


def _moe_up_kernel(be_ref, nv_ref, x_ref, wg_ref, wu_ref, bg_ref, bu_ref, h_ref, wgb_ref, wub_ref):
    @pl.when(_moe_new_expert(be_ref))
    def _():
        wgb_ref[...] = wg_ref[0].astype(BF16)
        wub_ref[...] = wu_ref[0].astype(BF16)

    def compute(rows):
        xs = [jnp.concatenate(_unpack_bf16_pairs(x_ref[r, :]), axis=1).astype(BF16) for r in rows]
        gates = [jnp.dot(x, wgb_ref[...], preferred_element_type=F32) + bg_ref[0] for x in xs]
        ups = [jnp.dot(x, wub_ref[...], preferred_element_type=F32) + bu_ref[0] for x in xs]
        outs = []
        for gate, up in zip(gates, ups):
            gate = jnp.minimum(gate, SWIGLU_LIMIT)
            up = jnp.clip(up, -SWIGLU_LIMIT, SWIGLU_LIMIT)
            outs.append(((up + 1.0) * gate * jax.nn.sigmoid(SWIGLU_ALPHA * gate)).astype(h_ref.dtype))
        return outs

    _moe_sub_blocks(nv_ref, h_ref, compute)


def _moe_down_kernel(be_ref, nv_ref, h_ref, wd_ref, bd_ref, y_ref, wdb_ref):
    @pl.when(_moe_new_expert(be_ref))
    def _():
        wdb_ref[...] = wd_ref[0].astype(BF16)

    half = wdb_ref.shape[1] // 2

    def compute(rows):
        ys = [jnp.dot(h_ref[r, :], wdb_ref[...], preferred_element_type=F32) + bd_ref[0] for r in rows]
        return [_pack_bf16_pairs(y[:, :half], y[:, half:]) for y in ys]

    _moe_sub_blocks(nv_ref, y_ref, compute)


def _moe_experts(rows, block_e, n_valid, w_gate, b_gate, w_up, b_up, w_down, b_down):
    n_rows, d = rows.shape[0], 2 * rows.shape[1]
    n_blocks = n_rows // MOE_TM
    tn = MOE_TN
    tile = lambda width: pl.BlockSpec((MOE_TM, width), lambda n, i, be, nv: (i, 0))
    wcol = lambda k: pl.BlockSpec((1, k, tn), lambda n, i, be, nv: (be[i], 0, n))
    bcol = pl.BlockSpec((1, 1, tn), lambda n, i, be, nv: (be[i], 0, n))
    params = _cparams(("arbitrary", "arbitrary"))
    h = pl.pallas_call(
        _moe_up_kernel,
        out_shape=jax.ShapeDtypeStruct((n_rows, D_FF), BF16),
        grid_spec=pltpu.PrefetchScalarGridSpec(
            num_scalar_prefetch=2, grid=(D_FF // tn, n_blocks),
            in_specs=[tile(d // 2), wcol(d), wcol(d), bcol, bcol],
            out_specs=pl.BlockSpec((MOE_TM, tn), lambda n, i, be, nv: (i, n)),
            scratch_shapes=[pltpu.VMEM((d, tn), BF16), pltpu.VMEM((d, tn), BF16)]),
        compiler_params=params, name="moe_expert_up",
    )(block_e, n_valid, rows, w_gate, w_up, b_gate.reshape(N_EXP, 1, D_FF), b_up.reshape(N_EXP, 1, D_FF))
    return pl.pallas_call(
        _moe_down_kernel,
        out_shape=jax.ShapeDtypeStruct((n_rows, d // 2), jnp.uint32),
        grid_spec=pltpu.PrefetchScalarGridSpec(
            num_scalar_prefetch=2, grid=(d // tn, n_blocks),
            in_specs=[tile(D_FF), wcol(D_FF), bcol],
            out_specs=pl.BlockSpec((MOE_TM, tn // 2), lambda n, i, be, nv: (i, n)),
            scratch_shapes=[pltpu.VMEM((D_FF, tn), BF16)]),
        compiler_params=params, name="moe_expert_down",
    )(block_e, n_valid, h, w_down, b_down.reshape(N_EXP, 1, d))


ROUTE_TT = 512
MOVE_TT = 256
HALF = D_MODEL // 2


def _pack_bf16_pairs(lo, hi):
    as_bits = lambda v: pltpu.bitcast(v.astype(BF16).astype(F32), jnp.uint32)
    return (as_bits(lo) >> 16) | (as_bits(hi) & jnp.uint32(0xFFFF0000))


def _unpack_bf16_pairs(w):
    return pltpu.bitcast(w << 16, F32), pltpu.bitcast(w & jnp.uint32(0xFFFF0000), F32)


def _route_kernel(h_ref, wr_ref, br_ref, e_ref, w_ref, p_ref, cnt_ref, run_ref, *, tt):
    @pl.when(pl.program_id(0) == 0)
    def _():
        run_ref[...] = jnp.zeros_like(run_ref)

    h = h_ref[...]
    logits = (jnp.dot(h, wr_ref[0], preferred_element_type=F32) + jnp.dot(h, wr_ref[1], preferred_element_type=F32)
              + br_ref[...])
    lane = lax.broadcasted_iota(jnp.int32, (tt, LANES), 1)
    v = jnp.where(lane < N_EXP, logits, -BIG)
    tops, hits, firsts = [], [], []
    for _ in range(TOP_K):
        m = jnp.max(v, axis=1, keepdims=True)
        first = jnp.min(jnp.where(v == m, lane, LANES), axis=1, keepdims=True)
        hit = lane == first
        v = jnp.where(hit, DROPPED, v)
        tops.append(m), hits.append(hit), firsts.append(first)
    ex = [jnp.exp(m - tops[0]) for m in tops]
    inv = 1.0 / functools.reduce(lambda a, b: a + b, ex)
    onehot = functools.reduce(lambda a, b: a + b, [jnp.where(hh, 1.0, 0.0) for hh in hits]).astype(BF16)
    ri = lax.broadcasted_iota(jnp.int32, (tt, tt), 0)
    ci = lax.broadcasted_iota(jnp.int32, (tt, tt), 1)
    before = jnp.where(ci < ri, 1.0, 0.0).astype(BF16)
    rank = jnp.dot(before, onehot, preferred_element_type=F32) + run_ref[0:1]
    run_ref[...] = run_ref[...] + jnp.dot(jnp.ones((8, tt), BF16), onehot, preferred_element_type=F32)
    cnt_ref[...] = run_ref[...].astype(jnp.int32)
    place = lambda cols, zero: functools.reduce(
        lambda acc, kc: jnp.where(lane == kc[0], kc[1], acc), list(enumerate(cols)), zero)
    e_ref[...] = place(firsts, jnp.zeros((tt, LANES), jnp.int32))
    w_ref[...] = place([e * inv for e in ex], jnp.zeros((tt, LANES), F32))
    pos = [jnp.sum(jnp.where(hh, rank, 0.0), axis=1, keepdims=True).astype(jnp.int32) for hh in hits]
    p_ref[...] = place(pos, jnp.zeros((tt, LANES), jnp.int32))


def _route(h, w_router, b_router):
    n, d = h.shape
    tt = ROUTE_TT
    wr = jnp.pad(w_router, ((0, 0), (0, LANES - N_EXP)))
    hi = wr.astype(BF16)
    wr2 = jnp.stack([hi, (wr - hi.astype(F32)).astype(BF16)])
    br = jnp.pad(b_router, (0, LANES - N_EXP)).reshape(1, LANES)
    tok = lambda dt: jax.ShapeDtypeStruct((n, LANES), dt)
    tspec = pl.BlockSpec((tt, LANES), lambda i: (i, 0))
    return pl.pallas_call(
        functools.partial(_route_kernel, tt=tt),
        out_shape=(tok(jnp.int32), tok(F32), tok(jnp.int32), jax.ShapeDtypeStruct((8, LANES), jnp.int32)),
        grid=(n // tt,),
        in_specs=[pl.BlockSpec((tt, d), lambda i: (i, 0)), pl.BlockSpec((2, d, LANES), lambda i: (0, 0, 0)),
                  pl.BlockSpec((1, LANES), lambda i: (0, 0))],
        out_specs=(tspec, tspec, tspec, pl.BlockSpec((8, LANES), lambda i: (0, 0))),
        scratch_shapes=[pltpu.VMEM((8, LANES), F32)],
        compiler_params=_cparams(("arbitrary",)),
        name="moe_route",
    )(h, wr2, br)


def _dispatch_kernel(dest_ref, src_ref, init_ref, rows_ref, sem, *, tt):
    del init_ref
    t0 = pl.program_id(0) * tt

    def copy(j, k):
        return pltpu.make_async_copy(src_ref.at[pl.ds(j, 1)],
                                     rows_ref.at[pl.ds(dest_ref[(t0 + j) * TOP_K + k], 1)], sem)

    def start(j, carry):
        for k in range(TOP_K):
            copy(j, k).start(priority=k % 2)
        return carry

    def wait(j, carry):
        for k in range(TOP_K):
            copy(j, k).wait()
        return carry

    lax.fori_loop(0, tt, start, 0)
    lax.fori_loop(0, tt, wait, 0)


def _dispatch(dest, hpk, n_rows):
    n, width = hpk.shape
    tt = MOVE_TT
    return pl.pallas_call(
        functools.partial(_dispatch_kernel, tt=tt),
        out_shape=jax.ShapeDtypeStruct((n_rows, width), jnp.uint32),
        grid_spec=pltpu.PrefetchScalarGridSpec(
            num_scalar_prefetch=1, grid=(n // tt,),
            in_specs=[pl.BlockSpec((tt, width), lambda i, dest: (i, 0)), pl.BlockSpec(memory_space=pl.ANY)],
            out_specs=pl.BlockSpec(memory_space=pl.ANY),
            scratch_shapes=[pltpu.SemaphoreType.DMA]),
        input_output_aliases={2: 0},
        compiler_params=_cparams(("arbitrary",)),
        name="moe_dispatch",
    )(dest, hpk, jnp.zeros((n_rows, width), jnp.uint32))


def _combine_kernel(dest_ref, y_ref, w_ref, x1_ref, g_ref, o_ref, buf_ref, sem, *, tt, norm):
    t0 = pl.program_id(0) * tt

    def copy(j, k):
        return pltpu.make_async_copy(y_ref.at[pl.ds(dest_ref[(t0 + j) * TOP_K + k], 1)],
                                     buf_ref.at[k, pl.ds(j, 1)], sem)

    def start(j, carry):
        for k in range(TOP_K):
            copy(j, k).start(priority=k % 2)
        return carry

    def wait(j, carry):
        for k in range(TOP_K):
            copy(j, k).wait()
        return carry

    lax.fori_loop(0, tt, start, 0)
    lax.fori_loop(0, tt, wait, 0)
    wts = w_ref[...]
    lane = lax.broadcasted_iota(jnp.int32, wts.shape, 1)
    x = x1_ref[...]
    nq = MOE_TN // 2
    for k in range(TOP_K):
        wk = jnp.sum(jnp.where(lane == k, wts, 0.0), axis=1, keepdims=True)
        parts = []
        for n in range(D_MODEL // MOE_TN):
            lo, hi = _unpack_bf16_pairs(buf_ref[k, :, n * nq:(n + 1) * nq])
            parts += [lo, hi]
        x = x + wk * jnp.concatenate(parts, axis=1)
    if norm:
        x = x * lax.rsqrt(jnp.mean(x * x, axis=-1, keepdims=True) + EPS) * g_ref[...]
    o_ref[...] = x.astype(o_ref.dtype)


def _combine(dest, ypk, wts, x1, gain):
    n, d = x1.shape
    tt = MOVE_TT
    norm = gain is not None
    gain = gain if norm else jnp.ones((d,), F32)
    return pl.pallas_call(
        functools.partial(_combine_kernel, tt=tt, norm=norm),
        out_shape=jax.ShapeDtypeStruct((n, d), F32),
        grid_spec=pltpu.PrefetchScalarGridSpec(
            num_scalar_prefetch=1, grid=(n // tt,),
            in_specs=[pl.BlockSpec(memory_space=pl.ANY),
                      pl.BlockSpec((tt, LANES), lambda i, dest: (i, 0)),
                      pl.BlockSpec((tt, d), lambda i, dest: (i, 0)),
                      pl.BlockSpec((1, d), lambda i, dest: (0, 0))],
            out_specs=pl.BlockSpec((tt, d), lambda i, dest: (i, 0)),
            scratch_shapes=[pltpu.VMEM((TOP_K, tt, d // 2), jnp.uint32), pltpu.SemaphoreType.DMA]),
        compiler_params=_cparams(("arbitrary",)),
        name="moe_combine_norm",
    )(dest, ypk, wts, x1, gain.reshape(1, d))


def _native_sparse_attention(main, small, positions, pe_ck, w1_ck, w2_ck, pe_cv, w1_cv, w2_cv, col, gate_lane0, bsz, s):
    half = ROT_DIM // 2
    inv_freq = ROPE_THETA ** (-jnp.arange(half, dtype=F32) * 2.0 / ROT_DIM)
    ang = positions.astype(F32)[..., None] * inv_freq
    cos, sin = jnp.cos(ang), jnp.sin(ang)
    rest = (bsz, s, DH_B - ROT_DIM)
    cos_t = jnp.concatenate([cos, cos, jnp.ones(rest, F32)], axis=-1)
    sin_t = jnp.concatenate([-sin, sin, jnp.zeros(rest, F32)], axis=-1)
    q, kc_t, vc_t, ks, kw = _nsa_rope(main, cos_t, sin_t, col, bsz, s)
    k_cmp, v_cmp = _nsa_compress(kc_t, vc_t, pe_ck, w1_ck, w2_ck, pe_cv, w1_cv, w2_cv, bsz, s)
    o_cmp, maskbias = _nsa_cmp_select(q, k_cmp, v_cmp, bsz, s)
    return _nsa_local(q, maskbias, ks, main, kw, o_cmp, small, col, gate_lane0, bsz, s)


def _moe_ffn(hf, hpk, x1, final_gain, w_router, b_router, w_gate, b_gate, w_up, b_up, w_down, b_down):
    n_tok, d = x1.shape
    eidx, wts, pos, cnt = _route(hf, w_router, b_router)
    counts = cnt[0, :N_EXP]
    padded = ((counts + MOE_TM - 1) // MOE_TM) * MOE_TM
    pad_end = jnp.cumsum(padded)
    pad_start = pad_end - padded
    part = counts % MOE_TM
    gap = jnp.where(part > 0, MOE_TM - part, 0)
    pick = lambda table: jnp.sum(jnp.where(eidx[:, :TOP_K, None] == jnp.arange(N_EXP), table, 0), axis=-1)
    rank = pos[:, :TOP_K]
    dest = pick(pad_start) + rank + jnp.where(rank >= pick(part), pick(gap), 0)
    dest = dest.reshape(-1).astype(jnp.int32)
    n_blocks = (n_tok * TOP_K + N_EXP * (MOE_TM - 1) + MOE_TM - 1) // MOE_TM
    rows = _dispatch(dest, hpk, n_blocks * MOE_TM)
    tile0 = jnp.arange(n_blocks, dtype=jnp.int32) * MOE_TM
    block_e = jnp.minimum(jnp.searchsorted(pad_end, tile0, side='right'), N_EXP - 1).astype(jnp.int32)
    first = (tile0 == pad_start[block_e]) & (part[block_e] > 0)
    n_valid = jnp.where(tile0 < pad_end[-1], jnp.where(first, part[block_e], MOE_TM), 0).astype(jnp.int32)
    ypk = _moe_experts(rows, block_e, n_valid, w_gate, b_gate, w_up, b_up, w_down, b_down)
    return _combine(dest, ypk, wts, x1, final_gain)


def _layer(x, positions, norm_mix, w_in, conv_w, a_log, dt_bias, norm_gdn, pe_ck, w1_ck, w2_ck,
           pe_cv, w1_cv, w2_cv, w_proj_a, w_proj_b, w_out, norm_ffn, w_router, b_router,
           w_gate, b_gate, w_up, b_up, w_down, b_down, final_gain):
    bsz, s, d = x.shape
    n_tok = bsz * s
    x2 = x.reshape(n_tok, d)
    h = _rmsnorm(x2, norm_mix, BF16)
    n_small = 2 * HA + 3 * HB
    sp = SPLIT_POINTS
    w_main = jnp.concatenate([w_in[:, :sp[3]], w_in[:, sp[5]:sp[12]], w_in[:, sp[13]:]], axis=1).astype(BF16)
    w_small = jnp.concatenate([w_in[:, sp[3]:sp[5]], w_in[:, sp[12]:sp[13]]], axis=1)
    w_small = jnp.pad(w_small, ((0, 0), (0, LANES - n_small))).astype(BF16)
    main = _matmul(h, w_main, BF16).reshape(bsz, s, -1)
    small = _matmul(h, w_small, F32).reshape(bsz, s, -1)
    names = ("qa", "ka", "va", "za", "qb", "kc", "vc", "ks", "vs", "kw", "vw", "gm")
    sizes = (HA * DK_A, HA * DK_A, HA * DV_A, HA * DV_A,
             HB * DH_B, G_KV * DH_B, G_KV * DH_B, G_KV * DH_B, G_KV * DH_B, G_KV * DH_B, G_KV * DH_B, 2 * D_MODEL)
    col = {nm: sum(sizes[:i]) for i, nm in enumerate(names)}
    aa, ba = small[..., :HA], small[..., HA:2 * HA]

    o_a = _gated_deltanet(main, aa, ba, conv_w, a_log, dt_bias, norm_gdn, bsz, s)
    o_b = _native_sparse_attention(main, small, positions, pe_ck, w1_ck, w2_ck, pe_cv, w1_cv, w2_cv,
                                   col, 2 * HA, bsz, s)
    merged = _merge(o_a.reshape(n_tok, d), o_b.reshape(n_tok, d), w_proj_a.astype(BF16),
                    w_proj_b.astype(BF16), main.reshape(n_tok, -1), col["gm"])
    x1, hf, hpk = _outproj(merged, w_out.astype(BF16), x2, norm_ffn)
    out = _moe_ffn(hf, hpk, x1, final_gain, w_router, b_router, w_gate, b_gate, w_up, b_up, w_down, b_down)
    return out.reshape(bsz, s, d)


def kernel(x, positions, norm_mix, w_in, conv_w, a_log, dt_bias, norm_gdn, pe_ck, w1_ck, w2_ck, pe_cv, w1_cv, w2_cv, w_proj_a, w_proj_b, w_out, norm_ffn, w_router, b_router, w_gate, b_gate, w_up, b_up, w_down, b_down, norm_final):
    depth = norm_mix.shape[0]
    for l in range(depth):
        x = _layer(x, positions, norm_mix[l], w_in[l], conv_w[l], a_log[l], dt_bias[l], norm_gdn[l],
                   pe_ck[l], w1_ck[l], w2_ck[l], pe_cv[l], w1_cv[l], w2_cv[l],
                   w_proj_a[l], w_proj_b[l], w_out[l], norm_ffn[l], w_router[l], b_router[l],
                   w_gate[l], b_gate[l], w_up[l], b_up[l], w_down[l], b_down[l],
                   norm_final if l + 1 == depth else None)
    return x
```

```python
import functools
import math

import jax
import jax.numpy as jnp
import numpy as np
from jax import lax
from jax.experimental import pallas as pl
from jax.experimental.pallas import tpu as pltpu

F32 = jnp.float32
BF16 = jnp.bfloat16

D_MODEL = 2048
EPS = 1e-6
NEG = -1e30
HA = D_MODEL // 128
DK_A = 128
DV_A = 128
CONV_W = 4
CHUNK = 64
HB = D_MODEL // 128
G_KV = 4
R_GRP = HB // G_KV
DH_B = 128
ROT_DIM = DH_B // 4
ROPE_THETA = 500000.0
L_CMP = 32
STRIDE_CMP = 16
L_SLC = 64
T_SEL = 8
WINDOW = 512
N_EXP = 32
TOP_K = 4
D_FF = D_MODEL
SWIGLU_LIMIT = 7.0
SWIGLU_ALPHA = 1.702
SPLIT_SIZES = (HA * DK_A, HA * DK_A, HA * DV_A, HA * DV_A, HA, HA,
               HB * DH_B, G_KV * DH_B, G_KV * DH_B, G_KV * DH_B, G_KV * DH_B, G_KV * DH_B, G_KV * DH_B,
               3 * HB, 2 * D_MODEL)
SPLIT_POINTS = tuple(sum(SPLIT_SIZES[:i + 1]) for i in range(len(SPLIT_SIZES) - 1))

V7X_VMEM_LIMIT_BYTES = 56 * 1024 * 1024
LANES = 128
MOE_TM = 1024
MOE_SUB = 256
MOE_TN = 512
ATT_TQ = 256


def _cparams(sem):
    return pltpu.CompilerParams(dimension_semantics=sem, vmem_limit_bytes=V7X_VMEM_LIMIT_BYTES)


def _mm_kernel(x_ref, w_ref, o_ref):
    o_ref[...] = jnp.dot(x_ref[...], w_ref[...], preferred_element_type=F32).astype(o_ref.dtype)


def _matmul(x, w, out_dtype, tm=1024, tn=1024):
    m, k = x.shape
    n = w.shape[1]
    tm, tn = min(tm, m), min(tn, n)
    assert m % tm == 0 and n % tn == 0
    return pl.pallas_call(
        _mm_kernel,
        out_shape=jax.ShapeDtypeStruct((m, n), out_dtype),
        grid=(n // tn, m // tm),
        in_specs=[pl.BlockSpec((tm, k), lambda j, i: (i, 0)),
                  pl.BlockSpec((k, tn), lambda j, i: (0, j))],
        out_specs=pl.BlockSpec((tm, tn), lambda j, i: (i, j)),
        compiler_params=_cparams(("parallel", "parallel")),
        name="dense_matmul",
    )(x, w)


def _rmsnorm_kernel(x_ref, g_ref, o_ref):
    x = x_ref[...]
    y = x * lax.rsqrt(jnp.mean(x * x, axis=-1, keepdims=True) + EPS)
    o_ref[...] = (y * g_ref[...]).astype(o_ref.dtype)


def _rmsnorm(x, gain, out_dtype, tm=512):
    m, d = x.shape
    return pl.pallas_call(
        _rmsnorm_kernel,
        out_shape=jax.ShapeDtypeStruct((m, d), out_dtype),
        grid=(m // tm,),
        in_specs=[pl.BlockSpec((tm, d), lambda i: (i, 0)),
                  pl.BlockSpec((1, d), lambda i: (0, 0))],
        out_specs=pl.BlockSpec((tm, d), lambda i: (i, 0)),
        compiler_params=_cparams(("parallel",)),
        name="rmsnorm",
    )(x, gain.reshape(1, d))


def _add_rmsnorm_kernel(x_ref, y_ref, g_ref, o_ref):
    x = x_ref[...] + y_ref[...]
    y = x * lax.rsqrt(jnp.mean(x * x, axis=-1, keepdims=True) + EPS)
    o_ref[...] = (y * g_ref[...]).astype(o_ref.dtype)


def _add_rmsnorm(x, y, gain, out_dtype, tm=512):
    m, d = x.shape
    row = pl.BlockSpec((tm, d), lambda i: (i, 0))
    return pl.pallas_call(
        _add_rmsnorm_kernel,
        out_shape=jax.ShapeDtypeStruct((m, d), out_dtype),
        grid=(m // tm,),
        in_specs=[row, row, pl.BlockSpec((1, d), lambda i: (0, 0))],
        out_specs=row,
        compiler_params=_cparams(("parallel",)),
        name="add_rmsnorm",
    )(x, y, gain.reshape(1, d))


def _merge_kernel(oa_ref, ob_ref, wa_ref, wb_ref, ga_ref, gb_ref, o_ref):
    ya = jnp.dot(oa_ref[...], wa_ref[...], preferred_element_type=F32)
    yb = jnp.dot(ob_ref[...], wb_ref[...], preferred_element_type=F32)
    o_ref[...] = (jax.nn.sigmoid(ga_ref[...].astype(F32)) * ya
                  + jax.nn.sigmoid(gb_ref[...].astype(F32)) * yb).astype(o_ref.dtype)


def _merge(o_a, o_b, w_a, w_b, main2d, gm_col0, tm=512, tn=1024):
    m, d = o_a.shape
    assert gm_col0 % tn == 0 and d % tn == 0
    g0 = gm_col0 // tn
    lhs = pl.BlockSpec((tm, d), lambda j, i: (i, 0))
    rhs = pl.BlockSpec((d, tn), lambda j, i: (0, j))
    return pl.pallas_call(
        _merge_kernel,
        out_shape=jax.ShapeDtypeStruct((m, d), BF16),
        grid=(d // tn, m // tm),
        in_specs=[lhs, lhs, rhs, rhs,
                  pl.BlockSpec((tm, tn), lambda j, i: (i, g0 + j)),
                  pl.BlockSpec((tm, tn), lambda j, i: (i, g0 + d // tn + j))],
        out_specs=pl.BlockSpec((tm, tn), lambda j, i: (i, j)),
        compiler_params=_cparams(("parallel", "parallel")),
        name="mixer_merge",
    )(o_a, o_b, w_a, w_b, main2d, main2d)


def _outproj_kernel(m_ref, w_ref, x_ref, g_ref, x1_ref, h_ref, hpk_ref):
    x1 = x_ref[...] + jnp.dot(m_ref[...], w_ref[...], preferred_element_type=F32)
    x1_ref[...] = x1
    y = x1 * lax.rsqrt(jnp.mean(x1 * x1, axis=-1, keepdims=True) + EPS) * g_ref[...]
    h_ref[...] = y.astype(h_ref.dtype)
    half = y.shape[1] // 2
    hpk_ref[...] = _pack_bf16_pairs(y[:, :half], y[:, half:])


def _outproj(merged, w_out, x, gain, tm=512):
    m, d = x.shape
    row = lambda width=d: pl.BlockSpec((tm, width), lambda i: (i, 0))
    return pl.pallas_call(
        _outproj_kernel,
        out_shape=(jax.ShapeDtypeStruct((m, d), F32), jax.ShapeDtypeStruct((m, d), BF16),
                   jax.ShapeDtypeStruct((m, d // 2), jnp.uint32)),
        grid=(m // tm,),
        in_specs=[row(), pl.BlockSpec((d, d), lambda i: (0, 0)), row(), pl.BlockSpec((1, d), lambda i: (0, 0))],
        out_specs=(row(), row(), row(d // 2)),
        compiler_params=_cparams(("parallel",)),
        name="out_proj_residual_norm",
    )(merged, w_out, x, gain.reshape(1, d))


GDN_COLS = 512
GDN_GROUP = 256
GDN_PH = 4
GDN_TS = 512


def _gdn_conv_kernel(x_ref, w_ref, o_ref):
    sec = pl.program_id(1) // (HA * DK_A // GDN_COLS)
    x = x_ref[0].astype(F32)
    w = w_ref[...]
    row = lax.broadcasted_iota(jnp.int32, x.shape, 0)
    y = x * w[CONV_W - 1:CONV_W]
    for i in range(CONV_W - 1):
        sh = CONV_W - 1 - i
        y = y + jnp.where(row >= sh, pltpu.roll(x, sh, axis=0), 0.0) * w[i:i + 1]
    y = y * jax.nn.sigmoid(y)
    qscale = jnp.where(sec == 0, DK_A ** -0.5, 1.0)
    for h in range(GDN_COLS // DK_A):
        yh = y[:, h * DK_A:(h + 1) * DK_A]
        inv = lax.rsqrt(jnp.sum(yh * yh, axis=-1, keepdims=True) + EPS) * qscale
        o_ref[0, 0, h] = (yh * jnp.where(sec < 2, inv, 1.0)).astype(o_ref.dtype)


def _gdn_conv(main, conv_w, bsz, s):
    ncol = 3 * HA * DK_A // GDN_COLS
    hpc = GDN_COLS // DK_A
    return pl.pallas_call(
        _gdn_conv_kernel,
        out_shape=jax.ShapeDtypeStruct((3, bsz, HA, s, DK_A), BF16),
        grid=(bsz, ncol),
        in_specs=[pl.BlockSpec((1, s, GDN_COLS), lambda b, c: (b, 0, c)),
                  pl.BlockSpec((CONV_W, GDN_COLS), lambda b, c: (0, c))],
        out_specs=pl.BlockSpec((1, 1, hpc, s, DK_A), lambda b, c: (c // (HA // hpc), b, c % (HA // hpc), 0, 0)),
        compiler_params=_cparams(("parallel", "parallel")),
        name="gdn_conv_silu_l2norm",
    )(main, conv_w)


def _col_rep(row, n):
    return jnp.broadcast_to(row, (LANES, n)).T


def _dot_hilo(x, m):
    hi = x.astype(BF16)
    lo = (x - hi.astype(F32)).astype(BF16)
    return jnp.dot(hi, m, preferred_element_type=F32) + jnp.dot(lo, m, preferred_element_type=F32)


def _gdn_prep_kernel(q_ref, k_ref, v_ref, g_ref, b_ref, u_ref, w_ref, qg_ref, kd_ref, a_ref, egl_ref):
    n = GDN_GROUP
    ri = lax.broadcasted_iota(jnp.int32, (n, n), 0)
    ci = lax.broadcasted_iota(jnp.int32, (n, n), 1)
    same = (ri // CHUNK) == (ci // CHUNK)
    incl = same & (ri >= ci)
    strict = same & (ri > ci)
    one_if = lambda m: jnp.where(m, 1.0, 0.0).astype(BF16)
    cum_m, tot_m = one_if(same & (ri <= ci)), one_if(same)
    eye = jnp.where(ri == ci, 1.0, 0.0)
    wide = lambda c: jnp.concatenate([c] * (n // LANES), axis=1)
    nt = (((1,), (1,)), ((), ()))
    heads = range(GDN_PH)
    ts, ps, rhs = [], [], []
    for h in heads:
        q, k, v = q_ref[0, 0, h], k_ref[0, 0, h], v_ref[0, 0, h]
        g8 = jnp.broadcast_to(g_ref[0, h], (8, n))
        gc_row = _dot_hilo(g8, cum_m)[0:1]
        gl_row = _dot_hilo(g8, tot_m)[0:1]
        gc_c, gl_c, b_c = _col_rep(gc_row, n), _col_rep(gl_row, n), _col_rep(b_ref[0, h], n)
        decay = jnp.exp(jnp.where(incl, wide(gc_c) - gc_row, NEG))
        kk = lax.dot_general(k, k, nt, preferred_element_type=F32)
        qk = lax.dot_general(q, k, nt, preferred_element_type=F32)
        xb = jnp.where(strict, -(kk * wide(b_c) * decay), 0.0).astype(BF16)
        a = qk * decay
        kf = k.astype(F32)
        egc = jnp.exp(gc_c)
        qg_ref[0, h] = (q.astype(F32) * egc).astype(qg_ref.dtype)
        kd_ref[0, h] = (kf * jnp.exp(gl_c - gc_c)).astype(kd_ref.dtype)
        for c in range(n // CHUNK):
            blk = slice(c * CHUNK, (c + 1) * CHUNK)
            a_ref[0, h, blk, :] = a[blk, blk].astype(a_ref.dtype)
        egl = jnp.exp(gl_c)
        egl_ref[0, h, 0] = jnp.concatenate([egl[c * CHUNK:c * CHUNK + 1] for c in range(n // CHUNK)], axis=0)
        rhs.append(jnp.concatenate([(v.astype(F32) * b_c).astype(BF16), (kf * b_c * egc).astype(BF16)], axis=1))
        ts.append(eye + xb.astype(F32))
        ps.append(xb)
    ps = [jnp.dot(p, p, preferred_element_type=F32).astype(BF16) for p in ps]
    for step in range(5):
        for h in heads:
            if step < 4:
                tp = jnp.dot(jnp.concatenate([ts[h].astype(BF16), ps[h]], axis=0), ps[h], preferred_element_type=F32)
                ts[h] = ts[h] + tp[:n]
                ps[h] = tp[n:].astype(BF16)
            else:
                ts[h] = ts[h] + jnp.dot(ts[h].astype(BF16), ps[h], preferred_element_type=F32)
    for h in heads:
        uw = jnp.dot(ts[h].astype(BF16), rhs[h], preferred_element_type=F32)
        u_ref[0, h] = uw[:, :DV_A].astype(u_ref.dtype)
        w_ref[0, h] = uw[:, DV_A:].astype(w_ref.dtype)


def _gdn_prep(qkv, g_t, beta_t, bsz, s):
    n, ph = GDN_GROUP, GDN_PH
    tok = lambda width, dt: jax.ShapeDtypeStruct((bsz, HA, s, width), dt)
    tspec = lambda width: pl.BlockSpec((1, ph, n, width), lambda b, h, i: (b, h, i, 0))
    qspec = lambda sec: pl.BlockSpec((1, 1, ph, n, DK_A), lambda b, h, i: (sec, b, h, i, 0))
    rspec = pl.BlockSpec((1, ph, 1, n), lambda b, h, i: (b, h, 0, i))
    return pl.pallas_call(
        _gdn_prep_kernel,
        out_shape=(tok(DV_A, BF16), tok(DK_A, BF16), tok(DK_A, BF16), tok(DK_A, BF16), tok(CHUNK, BF16),
                   jax.ShapeDtypeStruct((bsz, HA, s // n, n // CHUNK, LANES), F32)),
        grid=(bsz, HA // ph, s // n),
        in_specs=[qspec(0), qspec(1), qspec(2), rspec, rspec],
        out_specs=(tspec(DV_A), tspec(DK_A), tspec(DK_A), tspec(DK_A), tspec(CHUNK),
                   pl.BlockSpec((1, ph, 1, n // CHUNK, LANES), lambda b, h, i: (b, h, i, 0, 0))),
        compiler_params=_cparams(("parallel", "parallel", "parallel")),
        name="gdn_chunk_prep",
    )(qkv, qkv, qkv, g_t, beta_t)


def _gdn_scan_kernel(u_ref, w_ref, qg_ref, kd_ref, a_ref, egl_ref, z_ref, ng_ref, o_ref, state_ref, *, nchunk):
    @pl.when(pl.program_id(1) == 0)
    def _():
        state_ref[...] = jnp.zeros_like(state_ref)

    tn = (((0,), (0,)), ((), ()))
    heads = range(HA)

    def body(c, carry):
        rows = pl.ds(pl.multiple_of(c * CHUNK, CHUNK), CHUNK)
        st = [state_ref[h] for h in heads]
        sb = [x.astype(BF16) for x in st]
        vb = [(u_ref[0, h, rows, :].astype(F32)
               - jnp.dot(w_ref[0, h, rows, :], sb[h], preferred_element_type=F32)).astype(BF16) for h in heads]
        o = [jnp.dot(qg_ref[0, h, rows, :], sb[h], preferred_element_type=F32)
             + jnp.dot(a_ref[0, h, rows, :], vb[h], preferred_element_type=F32) for h in heads]
        for h in heads:
            state_ref[h] = (st[h] * egl_ref[0, h, pl.ds(c, 1), :]
                            + lax.dot_general(kd_ref[0, h, rows, :], vb[h], tn, preferred_element_type=F32))
        for h in heads:
            cols = slice(h * DV_A, (h + 1) * DV_A)
            z = z_ref[0, rows, cols].astype(F32)
            on = o[h] * lax.rsqrt(jnp.mean(o[h] * o[h], axis=-1, keepdims=True) + EPS) * ng_ref[...]
            o_ref[0, rows, cols] = (on * (z * jax.nn.sigmoid(z))).astype(o_ref.dtype)
        return carry

    lax.fori_loop(0, nchunk, body, 0)


def _gdn_scan(u, w, qg, kd, a, egl, main, norm_gdn, bsz, s):
    ts = GDN_TS
    z_blk0 = 3 * HA * DK_A // (HA * DV_A)
    hspec = lambda width: pl.BlockSpec((1, HA, ts, width), lambda b, i: (b, 0, i, 0))
    return pl.pallas_call(
        functools.partial(_gdn_scan_kernel, nchunk=ts // CHUNK),
        out_shape=jax.ShapeDtypeStruct((bsz, s, HA * DV_A), BF16),
        grid=(bsz, s // ts),
        in_specs=[hspec(DV_A), hspec(DK_A), hspec(DK_A), hspec(DK_A), hspec(CHUNK),
                  pl.BlockSpec((1, HA, ts // CHUNK, LANES), lambda b, i: (b, 0, i, 0)),
                  pl.BlockSpec((1, ts, HA * DV_A), lambda b, i: (b, i, z_blk0)),
                  pl.BlockSpec((1, DV_A), lambda b, i: (0, 0))],
        out_specs=pl.BlockSpec((1, ts, HA * DV_A), lambda b, i: (b, i, 0)),
        scratch_shapes=[pltpu.VMEM((HA, DK_A, DV_A), F32)],
        compiler_params=_cparams(("parallel", "arbitrary")),
        name="gdn_delta_scan",
    )(u, w, qg, kd, a, egl, main, norm_gdn.reshape(1, DV_A))


def _gated_deltanet(main, a_in, b_in, conv_w, a_log, dt_bias, norm_gdn, bsz, s):
    qkv = _gdn_conv(main, conv_w, bsz, s)
    g = -jnp.exp(a_log.astype(F32)) * jax.nn.softplus(a_in.astype(F32) + dt_bias.astype(F32))
    beta = jax.nn.sigmoid(b_in.astype(F32))
    g_t = g.transpose(0, 2, 1).reshape(bsz, HA, 1, s)
    beta_t = beta.transpose(0, 2, 1).reshape(bsz, HA, 1, s)
    u, w, qg, kd, a, egl = _gdn_prep(qkv, g_t, beta_t, bsz, s)
    egl = egl.reshape(bsz, HA, s // CHUNK, LANES)
    return _gdn_scan(u, w, qg, kd, a, egl, main, norm_gdn, bsz, s)


NSA_TS = 512
BIG = 1e30
DROPPED = -3e38


def _rope(x, cos, sin, lane):
    half = ROT_DIM // 2
    partner = jnp.where(lane < half, pltpu.roll(x, DH_B - half, axis=1), pltpu.roll(x, half, axis=1))
    return x * cos + partner * sin


def _nsa_rope_kernel(q_ref, kc_ref, vc_ref, ks_ref, kw_ref, cos_ref, sin_ref,
                     qo_ref, kco_ref, vco_ref, kso_ref, kwo_ref):
    cos, sin = cos_ref[0], sin_ref[0]
    lane = lax.broadcasted_iota(jnp.int32, cos.shape, 1)
    head = lambda ref, h: ref[0, :, h * DH_B:(h + 1) * DH_B].astype(F32)
    for h in range(HB):
        qo_ref[0, :, h * DH_B:(h + 1) * DH_B] = (_rope(head(q_ref, h), cos, sin, lane) * DH_B ** -0.5).astype(qo_ref.dtype)
    for g in range(G_KV):
        cols = slice(g * DH_B, (g + 1) * DH_B)
        kco_ref[0, g] = _rope(head(kc_ref, g), cos, sin, lane).astype(kco_ref.dtype)
        vco_ref[0, g] = vc_ref[0, :, cols]
        kso_ref[0, :, cols] = _rope(head(ks_ref, g), cos, sin, lane).astype(kso_ref.dtype)
        kwo_ref[0, :, cols] = _rope(head(kw_ref, g), cos, sin, lane).astype(kwo_ref.dtype)


def _nsa_rope(main, cos, sin, col, bsz, s):
    ts = NSA_TS
    kvw = G_KV * DH_B
    tok = lambda width, c0: pl.BlockSpec((1, ts, width), lambda b, i: (b, i, c0 // width))
    tab = pl.BlockSpec((1, ts, DH_B), lambda b, i: (b, i, 0))
    grp = pl.BlockSpec((1, G_KV, ts, DH_B), lambda b, i: (b, 0, i, 0))
    flat = lambda width: pl.BlockSpec((1, ts, width), lambda b, i: (b, i, 0))
    return pl.pallas_call(
        _nsa_rope_kernel,
        out_shape=(jax.ShapeDtypeStruct((bsz, s, HB * DH_B), BF16),
                   jax.ShapeDtypeStruct((bsz, G_KV, s, DH_B), BF16), jax.ShapeDtypeStruct((bsz, G_KV, s, DH_B), BF16),
                   jax.ShapeDtypeStruct((bsz, s, kvw), BF16), jax.ShapeDtypeStruct((bsz, s, kvw), BF16)),
        grid=(bsz, s // ts),
        in_specs=[tok(HB * DH_B, col["qb"]), tok(kvw, col["kc"]), tok(kvw, col["vc"]), tok(kvw, col["ks"]),
                  tok(kvw, col["kw"]), tab, tab],
        out_specs=(flat(HB * DH_B), grp, grp, flat(kvw), flat(kvw)),
        compiler_params=_cparams(("parallel", "parallel")),
        name="nsa_rotary",
    )(main, main, main, main, main, cos, sin)


def _gelu_tanh(x):
    return 0.5 * x * (1.0 + jnp.tanh(math.sqrt(2.0 / math.pi) * (x + 0.044715 * x * x * x)))


def _nsa_compress_kernel(k_ref, v_ref, pek_ref, pev_ref, w1k_ref, w1v_ref, w2k_ref, w2v_ref, ko_ref, vo_ref):
    nseg = k_ref.shape[2]
    for x_ref, pe_ref, w1_ref, w2_ref, o_ref in ((k_ref, pek_ref, w1k_ref, w2k_ref, ko_ref),
                                                 (v_ref, pev_ref, w1v_ref, w2v_ref, vo_ref)):
        for g in range(G_KV):
            x = x_ref[0, g].astype(F32)
            lo = jnp.dot((x + pe_ref[0:1]).astype(BF16), w1_ref[0], preferred_element_type=F32)
            hi = jnp.dot((x + pe_ref[1:2]).astype(BF16), w1_ref[1], preferred_element_type=F32)
            pre = lo + pltpu.roll(hi, nseg - 1, axis=0)
            o_ref[0, g] = jnp.dot(_gelu_tanh(pre).astype(BF16), w2_ref[...],
                                  preferred_element_type=F32).astype(o_ref.dtype)


def _nsa_compress(kc_t, vc_t, pe_ck, w1_ck, w2_ck, pe_cv, w1_cv, w2_cv, bsz, s):
    assert L_CMP == 2 * STRIDE_CMP
    nseg = s // STRIDE_CMP
    width = STRIDE_CMP * DH_B
    seg = lambda t: t.reshape(bsz, G_KV, nseg, width)
    pe2 = lambda pe: pe.reshape(2, width)
    w1h = lambda w: w.reshape(2, width, DH_B).astype(BF16)
    xspec = pl.BlockSpec((1, G_KV, nseg, width), lambda b: (b, 0, 0, 0))
    pspec = pl.BlockSpec((2, width), lambda b: (0, 0))
    w1spec = pl.BlockSpec((2, width, DH_B), lambda b: (0, 0, 0))
    w2spec = pl.BlockSpec((DH_B, DH_B), lambda b: (0, 0))
    ospec = pl.BlockSpec((1, G_KV, nseg, DH_B), lambda b: (b, 0, 0, 0))
    oshape = jax.ShapeDtypeStruct((bsz, G_KV, nseg, DH_B), BF16)
    return pl.pallas_call(
        _nsa_compress_kernel,
        out_shape=(oshape, oshape),
        grid=(bsz,),
        in_specs=[xspec, xspec, pspec, pspec, w1spec, w1spec, w2spec, w2spec],
        out_specs=(ospec, ospec),
        compiler_params=_cparams(("parallel",)),
        name="nsa_compress",
    )(seg(kc_t), seg(vc_t), pe2(pe_ck), pe2(pe_cv), w1h(w1_ck), w1h(w1_cv), w2_ck.astype(BF16), w2_cv.astype(BF16))


def _nsa_cmp_kernel(q_ref, kc_ref, vc_ref, ov_ref, o_ref, mb_ref, *, tq, n_slc):
    i = pl.program_id(2)
    rows = R_GRP * tq
    ncmp = kc_ref.shape[2]
    qa = _stack_heads(q_ref[0], None)
    s = lax.dot_general(qa, kc_ref[0, 0], (((1,), (1,)), ((), ())), preferred_element_type=F32)
    t_row = (lax.broadcasted_iota(jnp.int32, (rows, ncmp), 0) & (tq - 1)) + i * tq
    c_end = lax.broadcasted_iota(jnp.int32, (rows, ncmp), 1) * STRIDE_CMP + (L_CMP - 1)
    valid = c_end <= t_row
    sm = jnp.where(valid, s, NEG)
    p = jnp.where(valid, jnp.exp(sm - jnp.max(sm, axis=1, keepdims=True)), 0.0)
    l = jnp.sum(p, axis=1, keepdims=True)
    p = p * (1.0 / jnp.where(l > 0.0, l, 1.0))
    _unstack_heads(o_ref, jnp.dot(p.astype(BF16), vc_ref[0, 0], preferred_element_type=F32), tq)
    psum = p[0:tq]
    for r in range(1, R_GRP):
        psum = psum + p[r * tq:(r + 1) * tq]
    imp = _dot_hilo(psum, ov_ref[...])
    nb = -(-n_slc // 8) * 8
    v = imp.T[:nb]
    blk = lax.broadcasted_iota(jnp.int32, (nb, tq), 0)
    cur = (lax.broadcasted_iota(jnp.int32, (nb, tq), 1) + i * tq) // L_SLC
    forced = (blk == 0) | (blk == cur)
    v = jnp.where(forced, BIG, jnp.where(blk <= cur, v, -BIG))
    sel = jnp.zeros((nb, tq), F32)
    for _ in range(T_SEL):
        m = jnp.max(v, axis=0, keepdims=True)
        first = jnp.min(jnp.where(v == m, blk, LANES), axis=0, keepdims=True)
        hit = blk == first
        sel = jnp.where(hit, 1.0, sel)
        v = jnp.where(hit, DROPPED, v)
    bias = (jnp.where(blk <= cur, sel, 0.0) - 1.0) * BIG
    bias = jnp.concatenate([bias, jnp.zeros((LANES - nb, tq), F32)], axis=0)
    mb_ref[0, 0] = bias.T.astype(mb_ref.dtype)


def _nsa_cmp_select(q, k_cmp, v_cmp, bsz, s):
    tq = ATT_TQ
    nseg = s // STRIDE_CMP
    n_slc = s // L_SLC
    assert nseg <= LANES or nseg % LANES == 0
    c_start = np.arange(nseg) * STRIDE_CMP
    j_start = np.arange(n_slc) * L_SLC
    overlap = ((c_start[:, None] < j_start[None, :] + L_SLC) & (c_start[:, None] + L_CMP > j_start[None, :]))
    overlap = jnp.asarray(np.pad(overlap.astype(np.float32), ((0, 0), (0, LANES - n_slc))), BF16)
    qspec = pl.BlockSpec((1, tq, R_GRP * DH_B), lambda b, g, i: (b, i, g))
    cspec = pl.BlockSpec((1, 1, nseg, DH_B), lambda b, g, i: (b, g, 0, 0))
    return pl.pallas_call(
        functools.partial(_nsa_cmp_kernel, tq=tq, n_slc=n_slc),
        out_shape=(jax.ShapeDtypeStruct((bsz, s, HB * DH_B), BF16), jax.ShapeDtypeStruct((bsz, G_KV, s, LANES), BF16)),
        grid=(bsz, G_KV, s // tq),
        in_specs=[qspec, cspec, cspec, pl.BlockSpec((nseg, LANES), lambda b, g, i: (0, 0))],
        out_specs=(qspec, pl.BlockSpec((1, 1, tq, LANES), lambda b, g, i: (b, g, i, 0))),
        compiler_params=_cparams(("parallel", "parallel", "parallel")),
        name="nsa_compressed_select",
    )(q, k_cmp, v_cmp, overlap)


def _stack_heads(q, extra):
    parts = []
    for r in range(R_GRP):
        qr = q[:, r * DH_B:(r + 1) * DH_B]
        parts.append(qr if extra is None else jnp.concatenate([qr, extra], axis=1))
    return jnp.concatenate(parts, axis=0)


def _unstack_heads(o_ref, o, tq):
    for r in range(R_GRP):
        o_ref[0, :, r * DH_B:(r + 1) * DH_B] = o[r * tq:(r + 1) * tq].astype(o_ref.dtype)


def _heads_t(q, extra_t):
    parts = []
    for r in range(R_GRP):
        qt = q[:, r * DH_B:(r + 1) * DH_B].astype(F32).T.astype(BF16)
        parts.append(qt if extra_t is None else jnp.concatenate([qt, extra_t], axis=0))
    return jnp.concatenate(parts, axis=1)


def _softmax_step_t(ss, vs, carry):
    m, l, acc = carry
    m_new = functools.reduce(jnp.maximum, [jnp.max(s, axis=0, keepdims=True) for s in ss], m)
    alpha = jnp.exp(m - m_new)
    ps = [jnp.exp(s - m_new) for s in ss]
    l = alpha * l + functools.reduce(lambda a, b: a + b, [jnp.sum(p, axis=0, keepdims=True) for p in ps])
    tn = (((0,), (0,)), ((), ()))
    pv = [lax.dot_general(v, p.astype(BF16), tn, preferred_element_type=F32) for v, p in zip(vs, ps)]
    return m_new, l, functools.reduce(lambda a, b: a + b, pv, alpha * acc)


def _softmax_init_t(rows):
    return (jnp.full((1, rows), -jnp.inf, F32), jnp.zeros((1, rows), F32), jnp.zeros((DH_B, rows), F32))


def _selected_branch(i, q, mb, ks_ref, oh_ref, vs_ref, tq):
    rows = R_GRP * tq
    qat = _heads_t(q, mb.astype(F32).T.astype(BF16))

    def scores(j):
        keys = pl.ds(pl.multiple_of(j * tq, tq), tq)
        k = jnp.concatenate([ks_ref[0, keys, :], oh_ref[keys, :]], axis=1)
        return jnp.dot(k, qat, preferred_element_type=F32)

    def values(j):
        return vs_ref[0, pl.ds(pl.multiple_of(j * tq, tq), tq), :]

    def pair(jj, c):
        j = 2 * jj
        return _softmax_step_t([scores(j), scores(j + 1)], [values(j), values(j + 1)], c)

    carry = lax.fori_loop(0, i // 2, pair, _softmax_init_t(rows))
    k_loc = lax.broadcasted_iota(jnp.int32, (tq, tq), 0)
    t_loc = lax.broadcasted_iota(jnp.int32, (tq, tq), 1)
    causal = jnp.concatenate([jnp.where(k_loc <= t_loc, 0.0, NEG)] * R_GRP, axis=1)
    diag = lambda c: _softmax_step_t([scores(i) + causal], [values(i)], c)
    both = lambda c: _softmax_step_t([scores(i) + causal, scores(i - 1)], [values(i), values(i - 1)], c)
    m, l, acc = lax.cond((i % 2) == 1, both, diag, carry)
    return acc / l


def _window_branch(i, q, kw_refs, vw_refs, tq):
    rows = R_GRP * tq
    qt = _heads_t(q, None)
    k_loc = lax.broadcasted_iota(jnp.int32, (tq, tq), 0)
    t_loc = lax.broadcasted_iota(jnp.int32, (tq, tq), 1)
    bound = jnp.minimum(t_loc[0:1, :] + (i * tq + 1), WINDOW)
    ss = []
    for n in range(3):
        dist = t_loc + (2 - n) * tq - k_loc
        bias = jnp.where(dist.astype(jnp.uint32) < bound.astype(jnp.uint32), 0.0, NEG)
        ss.append(jnp.dot(kw_refs[n][0], qt, preferred_element_type=F32) + jnp.concatenate([bias] * R_GRP, axis=1))
    m, l, acc = _softmax_step_t(ss, [r[0] for r in vw_refs], _softmax_init_t(rows))
    return acc / l


def _nsa_local_kernel(q_ref, mb_ref, ks_ref, oh_ref, vs_ref, kw0, kw1, kw2, vw0, vw1, vw2, oc_ref, gate_ref, o_ref,
                      *, tq, gate_lane0):
    g, i = pl.program_id(1), pl.program_id(2)
    q = q_ref[0]
    o_slc = _selected_branch(i, q, mb_ref[0, 0], ks_ref, oh_ref, vs_ref, tq)
    o_win = _window_branch(i, q, (kw0, kw1, kw2), (vw0, vw1, vw2), tq)
    gates = jax.nn.sigmoid(gate_ref[0])
    lane = lax.broadcasted_iota(jnp.int32, gates.shape, 1)
    pick = lambda idx: jnp.sum(jnp.where(lane == idx, gates, 0.0), axis=1, keepdims=True)
    for r in range(R_GRP):
        base = gate_lane0 + (g * R_GRP + r) * 3
        rows = slice(r * tq, (r + 1) * tq)
        cols = slice(r * DH_B, (r + 1) * DH_B)
        o = (pick(base) * oc_ref[0, :, cols].astype(F32) + pick(base + 1) * o_slc[:, rows].T
             + pick(base + 2) * o_win[:, rows].T)
        o_ref[0, :, cols] = o.astype(o_ref.dtype)


def _nsa_local(q, maskbias, ks, main, kw, o_cmp, small, col, gate_lane0, bsz, s):
    tq = ATT_TQ
    assert 2 * tq >= WINDOW
    onehot = jnp.asarray(np.arange(s)[:, None] // L_SLC == np.arange(LANES)[None, :], BF16)
    qspec = pl.BlockSpec((1, tq, R_GRP * DH_B), lambda b, g, i: (b, i, g))
    seq = lambda c0: pl.BlockSpec((1, s, DH_B), lambda b, g, i: (b, 0, c0 // DH_B + g))
    back = lambda c0, n: pl.BlockSpec((1, tq, DH_B), lambda b, g, i: (b, jnp.maximum(i - n, 0), c0 // DH_B + g))
    return pl.pallas_call(
        functools.partial(_nsa_local_kernel, tq=tq, gate_lane0=gate_lane0),
        out_shape=jax.ShapeDtypeStruct((bsz, s, HB * DH_B), BF16),
        grid=(bsz, G_KV, s // tq),
        in_specs=[qspec,
                  pl.BlockSpec((1, 1, tq, LANES), lambda b, g, i: (b, g, i, 0)),
                  seq(0), pl.BlockSpec((s, LANES), lambda b, g, i: (0, 0)), seq(col["vs"]),
                  back(0, 2), back(0, 1), back(0, 0),
                  back(col["vw"], 2), back(col["vw"], 1), back(col["vw"], 0),
                  qspec, pl.BlockSpec((1, tq, LANES), lambda b, g, i: (b, i, 0))],
        out_specs=qspec,
        compiler_params=_cparams(("parallel", "parallel", "arbitrary")),
        name="nsa_selected_window_combine",
    )(q, maskbias, ks, onehot, main, kw, kw, kw, main, main, main, o_cmp, small)


def _moe_sub_blocks(nv_ref, out_ref, compute):
    nsub = (nv_ref[pl.program_id(1)] + MOE_SUB - 1) // MOE_SUB
    rows_of = lambda sb: pl.ds(pl.multiple_of(sb * MOE_SUB, MOE_SUB), MOE_SUB)

    def pair(p, carry):
        rows = [rows_of(2 * p), rows_of(2 * p + 1)]
        for r, val in zip(rows, compute(rows)):
            out_ref[r, :] = val
        return carry

    def dead(sb, carry):
        out_ref[rows_of(sb), :] = jnp.zeros((MOE_SUB, out_ref.shape[1]), out_ref.dtype)
        return carry

    lax.fori_loop(0, nsub // 2, pair, 0)

    @pl.when(nsub % 2 == 1)
    def _():
        out_ref[rows_of(nsub - 1), :] = compute([rows_of(nsub - 1)])[0]

    lax.fori_loop(nsub, MOE_TM // MOE_SUB, dead, 0)


def _moe_new_expert(be_ref):
    i = pl.program_id(1)
    return (i == 0) | (be_ref[i] != be_ref[jnp.maximum(i - 1, 0)])


def _moe_up_kernel(be_ref, nv_ref, st_ref, x_ref, wg_ref, wu_ref, bg_ref, bu_ref, h_ref, wgb_ref, wub_ref):
    @pl.when(_moe_new_expert(be_ref))
    def _():
        wgb_ref[...] = wg_ref[0].astype(BF16)
        wub_ref[...] = wu_ref[0].astype(BF16)

    def compute(rows):
        xs = [jnp.concatenate(_unpack_bf16_pairs(x_ref[r, :]), axis=1).astype(BF16) for r in rows]
        gates = [jnp.dot(x, wgb_ref[...], preferred_element_type=F32) + bg_ref[0] for x in xs]
        ups = [jnp.dot(x, wub_ref[...], preferred_element_type=F32) + bu_ref[0] for x in xs]
        outs = []
        for gate, up in zip(gates, ups):
            gate = jnp.minimum(gate, SWIGLU_LIMIT)
            up = jnp.clip(up, -SWIGLU_LIMIT, SWIGLU_LIMIT)
            outs.append(((up + 1.0) * gate * jax.nn.sigmoid(SWIGLU_ALPHA * gate)).astype(h_ref.dtype))
        return outs

    _moe_sub_blocks(nv_ref, h_ref, compute)


def _moe_down_kernel(be_ref, nv_ref, st_ref, h_ref, wd_ref, bd_ref, y_ref, wdb_ref):
    @pl.when(_moe_new_expert(be_ref))
    def _():
        wdb_ref[...] = wd_ref[0].astype(BF16)

    half = wdb_ref.shape[1] // 2

    def compute(rows):
        ys = [jnp.dot(h_ref[r, :], wdb_ref[...], preferred_element_type=F32) + bd_ref[0] for r in rows]
        return [_pack_bf16_pairs(y[:, :half], y[:, half:]) for y in ys]

    _moe_sub_blocks(nv_ref, y_ref, compute)


def _moe_experts(rows, block_e, n_valid, src_tile, w_gate, b_gate, w_up, b_up, w_down, b_down):
    n_rows, d = rows.shape[0], 2 * rows.shape[1]
    n_blocks = n_rows // MOE_TM
    tn = MOE_TN
    tile = lambda width: pl.BlockSpec((MOE_TM, width), lambda n, i, be, nv, st: (st[i], 0))
    wcol = lambda k: pl.BlockSpec((1, k, tn), lambda n, i, be, nv, st: (be[i], 0, n))
    bcol = pl.BlockSpec((1, 1, tn), lambda n, i, be, nv, st: (be[i], 0, n))
    params = _cparams(("arbitrary", "arbitrary"))
    h = pl.pallas_call(
        _moe_up_kernel,
        out_shape=jax.ShapeDtypeStruct((n_rows, D_FF), BF16),
        grid_spec=pltpu.PrefetchScalarGridSpec(
            num_scalar_prefetch=3, grid=(D_FF // tn, n_blocks),
            in_specs=[tile(d // 2), wcol(d), wcol(d), bcol, bcol],
            out_specs=pl.BlockSpec((MOE_TM, tn), lambda n, i, be, nv, st: (i, n)),
            scratch_shapes=[pltpu.VMEM((d, tn), BF16), pltpu.VMEM((d, tn), BF16)]),
        compiler_params=params, name="moe_expert_up",
    )(block_e, n_valid, src_tile, rows, w_gate, w_up, b_gate.reshape(N_EXP, 1, D_FF), b_up.reshape(N_EXP, 1, D_FF))
    return pl.pallas_call(
        _moe_down_kernel,
        out_shape=jax.ShapeDtypeStruct((n_rows, d // 2), jnp.uint32),
        grid_spec=pltpu.PrefetchScalarGridSpec(
            num_scalar_prefetch=3, grid=(d // tn, n_blocks),
            in_specs=[tile(D_FF), wcol(D_FF), bcol],
            out_specs=pl.BlockSpec((MOE_TM, tn // 2), lambda n, i, be, nv, st: (i, n)),
            scratch_shapes=[pltpu.VMEM((D_FF, tn), BF16)]),
        compiler_params=params, name="moe_expert_down",
    )(block_e, n_valid, src_tile, h, w_down, b_down.reshape(N_EXP, 1, d))


ROUTE_TT = 512
MOVE_TT = 256
HALF = D_MODEL // 2


def _pack_bf16_pairs(lo, hi):
    as_bits = lambda v: pltpu.bitcast(v.astype(BF16).astype(F32), jnp.uint32)
    return (as_bits(lo) >> 16) | (as_bits(hi) & jnp.uint32(0xFFFF0000))


def _unpack_bf16_pairs(w):
    return pltpu.bitcast(w << 16, F32), pltpu.bitcast(w & jnp.uint32(0xFFFF0000), F32)


def _route_kernel(h_ref, wr_ref, br_ref, e_ref, w_ref, p_ref, cnt_ref, run_ref, *, tt):
    @pl.when(pl.program_id(0) == 0)
    def _():
        run_ref[...] = jnp.zeros_like(run_ref)

    h = h_ref[...]
    logits = (jnp.dot(h, wr_ref[0], preferred_element_type=F32) + jnp.dot(h, wr_ref[1], preferred_element_type=F32)
              + br_ref[...])
    lane = lax.broadcasted_iota(jnp.int32, (tt, LANES), 1)
    v = jnp.where(lane < N_EXP, logits, -BIG)
    tops, hits, firsts = [], [], []
    for _ in range(TOP_K):
        m = jnp.max(v, axis=1, keepdims=True)
        first = jnp.min(jnp.where(v == m, lane, LANES), axis=1, keepdims=True)
        hit = lane == first
        v = jnp.where(hit, DROPPED, v)
        tops.append(m), hits.append(hit), firsts.append(first)
    ex = [jnp.exp(m - tops[0]) for m in tops]
    inv = 1.0 / functools.reduce(lambda a, b: a + b, ex)
    onehot = functools.reduce(lambda a, b: a + b, [jnp.where(hh, 1.0, 0.0) for hh in hits]).astype(BF16)
    ri = lax.broadcasted_iota(jnp.int32, (tt, tt), 0)
    ci = lax.broadcasted_iota(jnp.int32, (tt, tt), 1)
    before = jnp.where(ci < ri, 1.0, 0.0).astype(BF16)
    rank = jnp.dot(before, onehot, preferred_element_type=F32) + run_ref[0:1]
    run_ref[...] = run_ref[...] + jnp.dot(jnp.ones((8, tt), BF16), onehot, preferred_element_type=F32)
    cnt_ref[...] = run_ref[...].astype(jnp.int32)
    place = lambda cols, zero: functools.reduce(
        lambda acc, kc: jnp.where(lane == kc[0], kc[1], acc), list(enumerate(cols)), zero)
    e_ref[...] = place(firsts, jnp.zeros((tt, LANES), jnp.int32))
    w_ref[...] = place([e * inv for e in ex], jnp.zeros((tt, LANES), F32))
    pos = [jnp.sum(jnp.where(hh, rank, 0.0), axis=1, keepdims=True).astype(jnp.int32) for hh in hits]
    p_ref[...] = place(pos, jnp.zeros((tt, LANES), jnp.int32))


def _route(h, w_router, b_router):
    n, d = h.shape
    tt = ROUTE_TT
    wr = jnp.pad(w_router, ((0, 0), (0, LANES - N_EXP)))
    hi = wr.astype(BF16)
    wr2 = jnp.stack([hi, (wr - hi.astype(F32)).astype(BF16)])
    br = jnp.pad(b_router, (0, LANES - N_EXP)).reshape(1, LANES)
    tok = lambda dt: jax.ShapeDtypeStruct((n, LANES), dt)
    tspec = pl.BlockSpec((tt, LANES), lambda i: (i, 0))
    return pl.pallas_call(
        functools.partial(_route_kernel, tt=tt),
        out_shape=(tok(jnp.int32), tok(F32), tok(jnp.int32), jax.ShapeDtypeStruct((8, LANES), jnp.int32)),
        grid=(n // tt,),
        in_specs=[pl.BlockSpec((tt, d), lambda i: (i, 0)), pl.BlockSpec((2, d, LANES), lambda i: (0, 0, 0)),
                  pl.BlockSpec((1, LANES), lambda i: (0, 0))],
        out_specs=(tspec, tspec, tspec, pl.BlockSpec((8, LANES), lambda i: (0, 0))),
        scratch_shapes=[pltpu.VMEM((8, LANES), F32)],
        compiler_params=_cparams(("arbitrary",)),
        name="moe_route",
    )(h, wr2, br)


def _dispatch_kernel(dest_ref, src_ref, init_ref, rows_ref, sem, *, tt):
    del init_ref
    t0 = pl.program_id(0) * tt

    def copy(j, k):
        return pltpu.make_async_copy(src_ref.at[pl.ds(j, 1)],
                                     rows_ref.at[pl.ds(dest_ref[(t0 + j) * TOP_K + k], 1)], sem)

    def start(j, carry):
        for k in range(TOP_K):
            copy(j, k).start(priority=k % 2)
        return carry

    def wait(j, carry):
        for k in range(TOP_K):
            copy(j, k).wait()
        return carry

    lax.fori_loop(0, tt, start, 0)
    lax.fori_loop(0, tt, wait, 0)


def _dispatch(dest, hpk, n_rows):
    n, width = hpk.shape
    tt = MOVE_TT
    return pl.pallas_call(
        functools.partial(_dispatch_kernel, tt=tt),
        out_shape=jax.ShapeDtypeStruct((n_rows, width), jnp.uint32),
        grid_spec=pltpu.PrefetchScalarGridSpec(
            num_scalar_prefetch=1, grid=(n // tt,),
            in_specs=[pl.BlockSpec((tt, width), lambda i, dest: (i, 0)), pl.BlockSpec(memory_space=pl.ANY)],
            out_specs=pl.BlockSpec(memory_space=pl.ANY),
            scratch_shapes=[pltpu.SemaphoreType.DMA]),
        input_output_aliases={2: 0},
        compiler_params=_cparams(("arbitrary",)),
        name="moe_dispatch",
    )(dest, hpk, jnp.zeros((n_rows, width), jnp.uint32))


def _combine_kernel(dest_ref, y_ref, w_ref, x1_ref, g_ref, o_ref, buf_ref, sem, *, tt, norm):
    step = pl.program_id(0)
    slot = step % 2

    def copy(tile, sl, j, k):
        return pltpu.make_async_copy(y_ref.at[pl.ds(dest_ref[(tile * tt + j) * TOP_K + k], 1)],
                                     buf_ref.at[sl, k, pl.ds(j, 1)], sem.at[sl])

    def gather(tile, sl):
        def start(j, carry):
            for k in range(TOP_K):
                copy(tile, sl, j, k).start(priority=k % 2)
            return carry
        lax.fori_loop(0, tt, start, 0)

    @pl.when(step == 0)
    def _():
        gather(step, slot)

    @pl.when(step + 1 < pl.num_programs(0))
    def _():
        gather(step + 1, 1 - slot)

    def wait(j, carry):
        for k in range(TOP_K):
            copy(step, slot, j, k).wait()
        return carry

    lax.fori_loop(0, tt, wait, 0)
    wts = w_ref[...]
    lane = lax.broadcasted_iota(jnp.int32, wts.shape, 1)
    x = x1_ref[...]
    nq = MOE_TN // 2
    for k in range(TOP_K):
        wk = jnp.sum(jnp.where(lane == k, wts, 0.0), axis=1, keepdims=True)
        parts = []
        for n in range(D_MODEL // MOE_TN):
            lo, hi = _unpack_bf16_pairs(buf_ref[slot, k, :, n * nq:(n + 1) * nq])
            parts += [lo, hi]
        x = x + wk * jnp.concatenate(parts, axis=1)
    if norm:
        x = x * lax.rsqrt(jnp.mean(x * x, axis=-1, keepdims=True) + EPS) * g_ref[...]
    o_ref[...] = x.astype(o_ref.dtype)


def _combine(dest, ypk, wts, x1, gain):
    n, d = x1.shape
    tt = MOVE_TT
    norm = gain is not None
    gain = gain if norm else jnp.ones((d,), F32)
    return pl.pallas_call(
        functools.partial(_combine_kernel, tt=tt, norm=norm),
        out_shape=jax.ShapeDtypeStruct((n, d), F32),
        grid_spec=pltpu.PrefetchScalarGridSpec(
            num_scalar_prefetch=1, grid=(n // tt,),
            in_specs=[pl.BlockSpec(memory_space=pl.ANY),
                      pl.BlockSpec((tt, LANES), lambda i, dest: (i, 0)),
                      pl.BlockSpec((tt, d), lambda i, dest: (i, 0)),
                      pl.BlockSpec((1, d), lambda i, dest: (0, 0))],
            out_specs=pl.BlockSpec((tt, d), lambda i, dest: (i, 0)),
            scratch_shapes=[pltpu.VMEM((2, TOP_K, tt, d // 2), jnp.uint32), pltpu.SemaphoreType.DMA((2,))]),
        compiler_params=_cparams(("arbitrary",)),
        name="moe_combine_norm",
    )(dest, ypk, wts, x1, gain.reshape(1, d))


def _native_sparse_attention(main, small, positions, pe_ck, w1_ck, w2_ck, pe_cv, w1_cv, w2_cv, col, gate_lane0, bsz, s):
    half = ROT_DIM // 2
    inv_freq = ROPE_THETA ** (-jnp.arange(half, dtype=F32) * 2.0 / ROT_DIM)
    ang = positions.astype(F32)[..., None] * inv_freq
    cos, sin = jnp.cos(ang), jnp.sin(ang)
    rest = (bsz, s, DH_B - ROT_DIM)
    cos_t = jnp.concatenate([cos, cos, jnp.ones(rest, F32)], axis=-1)
    sin_t = jnp.concatenate([-sin, sin, jnp.zeros(rest, F32)], axis=-1)
    q, kc_t, vc_t, ks, kw = _nsa_rope(main, cos_t, sin_t, col, bsz, s)
    k_cmp, v_cmp = _nsa_compress(kc_t, vc_t, pe_ck, w1_ck, w2_ck, pe_cv, w1_cv, w2_cv, bsz, s)
    o_cmp, maskbias = _nsa_cmp_select(q, k_cmp, v_cmp, bsz, s)
    return _nsa_local(q, maskbias, ks, main, kw, o_cmp, small, col, gate_lane0, bsz, s)


def _moe_ffn(hf, hpk, x1, final_gain, w_router, b_router, w_gate, b_gate, w_up, b_up, w_down, b_down):
    n_tok, d = x1.shape
    eidx, wts, pos, cnt = _route(hf, w_router, b_router)
    counts = cnt[0, :N_EXP]
    padded = ((counts + MOE_TM - 1) // MOE_TM) * MOE_TM
    pad_end = jnp.cumsum(padded)
    pad_start = pad_end - padded
    part = counts % MOE_TM
    gap = jnp.where(part > 0, MOE_TM - part, 0)
    pick = lambda table: jnp.sum(jnp.where(eidx[:, :TOP_K, None] == jnp.arange(N_EXP), table, 0), axis=-1)
    rank = pos[:, :TOP_K]
    dest = pick(pad_start) + rank + jnp.where(rank >= pick(part), pick(gap), 0)
    dest = dest.reshape(-1).astype(jnp.int32)
    n_blocks = (n_tok * TOP_K + N_EXP * (MOE_TM - 1) + MOE_TM - 1) // MOE_TM
    rows = _dispatch(dest, hpk, n_blocks * MOE_TM)
    tile0 = jnp.arange(n_blocks, dtype=jnp.int32) * MOE_TM
    block_e = jnp.minimum(jnp.searchsorted(pad_end, tile0, side='right'), N_EXP - 1).astype(jnp.int32)
    first = (tile0 == pad_start[block_e]) & (part[block_e] > 0)
    n_valid = jnp.where(tile0 < pad_end[-1], jnp.where(first, part[block_e], MOE_TM), 0).astype(jnp.int32)
    tile_id = jnp.arange(n_blocks, dtype=jnp.int32)
    src_tile = jnp.where(n_valid > 0, tile_id, jnp.maximum(pad_end[-1] // MOE_TM - 1, 0)).astype(jnp.int32)
    ypk = _moe_experts(rows, block_e, n_valid, src_tile, w_gate, b_gate, w_up, b_up, w_down, b_down)
    return _combine(dest, ypk, wts, x1, final_gain)


def _layer(x, positions, norm_mix, w_in, conv_w, a_log, dt_bias, norm_gdn, pe_ck, w1_ck, w2_ck,
           pe_cv, w1_cv, w2_cv, w_proj_a, w_proj_b, w_out, norm_ffn, w_router, b_router,
           w_gate, b_gate, w_up, b_up, w_down, b_down, final_gain):
    bsz, s, d = x.shape
    n_tok = bsz * s
    x2 = x.reshape(n_tok, d)
    h = _rmsnorm(x2, norm_mix, BF16)
    n_small = 2 * HA + 3 * HB
    sp = SPLIT_POINTS
    w_main = jnp.concatenate([w_in[:, :sp[3]], w_in[:, sp[5]:sp[12]], w_in[:, sp[13]:]], axis=1).astype(BF16)
    w_small = jnp.concatenate([w_in[:, sp[3]:sp[5]], w_in[:, sp[12]:sp[13]]], axis=1)
    w_small = jnp.pad(w_small, ((0, 0), (0, LANES - n_small))).astype(BF16)
    main = _matmul(h, w_main, BF16, tm=2048).reshape(bsz, s, -1)
    small = _matmul(h, w_small, F32).reshape(bsz, s, -1)
    names = ("qa", "ka", "va", "za", "qb", "kc", "vc", "ks", "vs", "kw", "vw", "gm")
    sizes = (HA * DK_A, HA * DK_A, HA * DV_A, HA * DV_A,
             HB * DH_B, G_KV * DH_B, G_KV * DH_B, G_KV * DH_B, G_KV * DH_B, G_KV * DH_B, G_KV * DH_B, 2 * D_MODEL)
    col = {nm: sum(sizes[:i]) for i, nm in enumerate(names)}
    aa, ba = small[..., :HA], small[..., HA:2 * HA]

    o_a = _gated_deltanet(main, aa, ba, conv_w, a_log, dt_bias, norm_gdn, bsz, s)
    o_b = _native_sparse_attention(main, small, positions, pe_ck, w1_ck, w2_ck, pe_cv, w1_cv, w2_cv,
                                   col, 2 * HA, bsz, s)
    merged = _merge(o_a.reshape(n_tok, d), o_b.reshape(n_tok, d), w_proj_a.astype(BF16),
                    w_proj_b.astype(BF16), main.reshape(n_tok, -1), col["gm"])
    x1, hf, hpk = _outproj(merged, w_out.astype(BF16), x2, norm_ffn)
    out = _moe_ffn(hf, hpk, x1, final_gain, w_router, b_router, w_gate, b_gate, w_up, b_up, w_down, b_down)
    return out.reshape(bsz, s, d)


def kernel(x, positions, norm_mix, w_in, conv_w, a_log, dt_bias, norm_gdn, pe_ck, w1_ck, w2_ck, pe_cv, w1_cv, w2_cv, w_proj_a, w_proj_b, w_out, norm_ffn, w_router, b_router, w_gate, b_gate, w_up, b_up, w_down, b_down, norm_final):
    depth = norm_mix.shape[0]
    for l in range(depth):
        x = _layer(x, positions, norm_mix[l], w_in[l], conv_w[l], a_log[l], dt_bias[l], norm_gdn[l],
                   pe_ck[l], w1_ck[l], w2_ck[l], pe_cv[l], w1_cv[l], w2_cv[l],
                   w_proj_a[l], w_proj_b[l], w_out[l], norm_ffn[l], w_router[l], b_router[l],
                   w_gate[l], b_gate[l], w_up[l], b_up[l], w_down[l], b_down[l],
                   norm_final if l + 1 == depth else None)
    return x
```

```python
import functools
import math

import jax
import jax.numpy as jnp
import numpy as np
from jax import lax
from jax.experimental import pallas as pl
from jax.experimental.pallas import tpu as pltpu

F32 = jnp.float32
BF16 = jnp.bfloat16

D_MODEL = 2048
EPS = 1e-6
NEG = -1e30
HA = D_MODEL // 128
DK_A = 128
DV_A = 128
CONV_W = 4
CHUNK = 64
HB = D_MODEL // 128
G_KV = 4
R_GRP = HB // G_KV
DH_B = 128
ROT_DIM = DH_B // 4
ROPE_THETA = 500000.0
L_CMP = 32
STRIDE_CMP = 16
L_SLC = 64
T_SEL = 8
WINDOW = 512
N_EXP = 32
TOP_K = 4
D_FF = D_MODEL
SWIGLU_LIMIT = 7.0
SWIGLU_ALPHA = 1.702
SPLIT_SIZES = (HA * DK_A, HA * DK_A, HA * DV_A, HA * DV_A, HA, HA,
               HB * DH_B, G_KV * DH_B, G_KV * DH_B, G_KV * DH_B, G_KV * DH_B, G_KV * DH_B, G_KV * DH_B,
               3 * HB, 2 * D_MODEL)
SPLIT_POINTS = tuple(sum(SPLIT_SIZES[:i + 1]) for i in range(len(SPLIT_SIZES) - 1))

V7X_VMEM_LIMIT_BYTES = 56 * 1024 * 1024
LANES = 128
MOE_TM = 1024
MOE_SUB = 256
MOE_TN = 512
MOE_TN_DOWN = 1024
ATT_TQ = 256


def _cparams(sem):
    return pltpu.CompilerParams(dimension_semantics=sem, vmem_limit_bytes=V7X_VMEM_LIMIT_BYTES)


def _mm_kernel(x_ref, w_ref, o_ref):
    o_ref[...] = jnp.dot(x_ref[...], w_ref[...], preferred_element_type=F32).astype(o_ref.dtype)


def _matmul(x, w, out_dtype, tm=1024, tn=1024):
    m, k = x.shape
    n = w.shape[1]
    tm, tn = min(tm, m), min(tn, n)
    assert m % tm == 0 and n % tn == 0
    return pl.pallas_call(
        _mm_kernel,
        out_shape=jax.ShapeDtypeStruct((m, n), out_dtype),
        grid=(n // tn, m // tm),
        in_specs=[pl.BlockSpec((tm, k), lambda j, i: (i, 0)),
                  pl.BlockSpec((k, tn), lambda j, i: (0, j))],
        out_specs=pl.BlockSpec((tm, tn), lambda j, i: (i, j)),
        compiler_params=_cparams(("parallel", "parallel")),
        name="dense_matmul",
    )(x, w)


def _rmsnorm_kernel(x_ref, g_ref, o_ref):
    x = x_ref[...]
    y = x * lax.rsqrt(jnp.mean(x * x, axis=-1, keepdims=True) + EPS)
    o_ref[...] = (y * g_ref[...]).astype(o_ref.dtype)


def _rmsnorm(x, gain, out_dtype, tm=512):
    m, d = x.shape
    return pl.pallas_call(
        _rmsnorm_kernel,
        out_shape=jax.ShapeDtypeStruct((m, d), out_dtype),
        grid=(m // tm,),
        in_specs=[pl.BlockSpec((tm, d), lambda i: (i, 0)),
                  pl.BlockSpec((1, d), lambda i: (0, 0))],
        out_specs=pl.BlockSpec((tm, d), lambda i: (i, 0)),
        compiler_params=_cparams(("parallel",)),
        name="rmsnorm",
    )(x, gain.reshape(1, d))


def _add_rmsnorm_kernel(x_ref, y_ref, g_ref, o_ref):
    x = x_ref[...] + y_ref[...]
    y = x * lax.rsqrt(jnp.mean(x * x, axis=-1, keepdims=True) + EPS)
    o_ref[...] = (y * g_ref[...]).astype(o_ref.dtype)


def _add_rmsnorm(x, y, gain, out_dtype, tm=512):
    m, d = x.shape
    row = pl.BlockSpec((tm, d), lambda i: (i, 0))
    return pl.pallas_call(
        _add_rmsnorm_kernel,
        out_shape=jax.ShapeDtypeStruct((m, d), out_dtype),
        grid=(m // tm,),
        in_specs=[row, row, pl.BlockSpec((1, d), lambda i: (0, 0))],
        out_specs=row,
        compiler_params=_cparams(("parallel",)),
        name="add_rmsnorm",
    )(x, y, gain.reshape(1, d))


def _merge_kernel(oa_ref, ob_ref, wa_ref, wb_ref, ga_ref, gb_ref, o_ref):
    ya = jnp.dot(oa_ref[...], wa_ref[...], preferred_element_type=F32)
    yb = jnp.dot(ob_ref[...], wb_ref[...], preferred_element_type=F32)
    o_ref[...] = (jax.nn.sigmoid(ga_ref[...].astype(F32)) * ya
                  + jax.nn.sigmoid(gb_ref[...].astype(F32)) * yb).astype(o_ref.dtype)


def _merge(o_a, o_b, w_a, w_b, main2d, gm_col0, tm=512, tn=1024):
    m, d = o_a.shape
    assert gm_col0 % tn == 0 and d % tn == 0
    g0 = gm_col0 // tn
    lhs = pl.BlockSpec((tm, d), lambda j, i: (i, 0))
    rhs = pl.BlockSpec((d, tn), lambda j, i: (0, j))
    return pl.pallas_call(
        _merge_kernel,
        out_shape=jax.ShapeDtypeStruct((m, d), BF16),
        grid=(d // tn, m // tm),
        in_specs=[lhs, lhs, rhs, rhs,
                  pl.BlockSpec((tm, tn), lambda j, i: (i, g0 + j)),
                  pl.BlockSpec((tm, tn), lambda j, i: (i, g0 + d // tn + j))],
        out_specs=pl.BlockSpec((tm, tn), lambda j, i: (i, j)),
        compiler_params=_cparams(("parallel", "parallel")),
        name="mixer_merge",
    )(o_a, o_b, w_a, w_b, main2d, main2d)


def _outproj_kernel(m_ref, w_ref, x_ref, g_ref, x1_ref, h_ref, hpk_ref):
    x1 = x_ref[...] + jnp.dot(m_ref[...], w_ref[...], preferred_element_type=F32)
    x1_ref[...] = x1
    y = x1 * lax.rsqrt(jnp.mean(x1 * x1, axis=-1, keepdims=True) + EPS) * g_ref[...]
    h_ref[...] = y.astype(h_ref.dtype)
    half = y.shape[1] // 2
    hpk_ref[...] = _pack_bf16_pairs(y[:, :half], y[:, half:])


def _outproj(merged, w_out, x, gain, tm=512):
    m, d = x.shape
    row = lambda width=d: pl.BlockSpec((tm, width), lambda i: (i, 0))
    return pl.pallas_call(
        _outproj_kernel,
        out_shape=(jax.ShapeDtypeStruct((m, d), F32), jax.ShapeDtypeStruct((m, d), BF16),
                   jax.ShapeDtypeStruct((m, d // 2), jnp.uint32)),
        grid=(m // tm,),
        in_specs=[row(), pl.BlockSpec((d, d), lambda i: (0, 0)), row(), pl.BlockSpec((1, d), lambda i: (0, 0))],
        out_specs=(row(), row(), row(d // 2)),
        compiler_params=_cparams(("parallel",)),
        name="out_proj_residual_norm",
    )(merged, w_out, x, gain.reshape(1, d))


GDN_COLS = 512
GDN_GROUP = 256
GDN_PH = 4
GDN_TS = 512


def _gdn_conv_kernel(x_ref, w_ref, o_ref):
    sec = pl.program_id(1) // (HA * DK_A // GDN_COLS)
    x = x_ref[0].astype(F32)
    w = w_ref[...]
    row = lax.broadcasted_iota(jnp.int32, x.shape, 0)
    y = x * w[CONV_W - 1:CONV_W]
    for i in range(CONV_W - 1):
        sh = CONV_W - 1 - i
        y = y + jnp.where(row >= sh, pltpu.roll(x, sh, axis=0), 0.0) * w[i:i + 1]
    y = y * jax.nn.sigmoid(y)
    qscale = jnp.where(sec == 0, DK_A ** -0.5, 1.0)
    for h in range(GDN_COLS // DK_A):
        yh = y[:, h * DK_A:(h + 1) * DK_A]
        inv = lax.rsqrt(jnp.sum(yh * yh, axis=-1, keepdims=True) + EPS) * qscale
        o_ref[0, 0, h] = (yh * jnp.where(sec < 2, inv, 1.0)).astype(o_ref.dtype)


def _gdn_conv(main, conv_w, bsz, s):
    ncol = 3 * HA * DK_A // GDN_COLS
    hpc = GDN_COLS // DK_A
    return pl.pallas_call(
        _gdn_conv_kernel,
        out_shape=jax.ShapeDtypeStruct((3, bsz, HA, s, DK_A), BF16),
        grid=(bsz, ncol),
        in_specs=[pl.BlockSpec((1, s, GDN_COLS), lambda b, c: (b, 0, c)),
                  pl.BlockSpec((CONV_W, GDN_COLS), lambda b, c: (0, c))],
        out_specs=pl.BlockSpec((1, 1, hpc, s, DK_A), lambda b, c: (c // (HA // hpc), b, c % (HA // hpc), 0, 0)),
        compiler_params=_cparams(("parallel", "parallel")),
        name="gdn_conv_silu_l2norm",
    )(main, conv_w)


def _col_rep(row, n):
    return jnp.broadcast_to(row, (LANES, n)).T


def _dot_hilo(x, m):
    hi = x.astype(BF16)
    lo = (x - hi.astype(F32)).astype(BF16)
    return jnp.dot(hi, m, preferred_element_type=F32) + jnp.dot(lo, m, preferred_element_type=F32)


def _gdn_prep_kernel(q_ref, k_ref, v_ref, g_ref, b_ref, u_ref, w_ref, qg_ref, kd_ref, a_ref, egl_ref):
    n = GDN_GROUP
    ri = lax.broadcasted_iota(jnp.int32, (n, n), 0)
    ci = lax.broadcasted_iota(jnp.int32, (n, n), 1)
    same = (ri // CHUNK) == (ci // CHUNK)
    incl = same & (ri >= ci)
    strict = same & (ri > ci)
    one_if = lambda m: jnp.where(m, 1.0, 0.0).astype(BF16)
    cum_m, tot_m = one_if(same & (ri <= ci)), one_if(same)
    eye = jnp.where(ri == ci, 1.0, 0.0)
    wide = lambda c: jnp.concatenate([c] * (n // LANES), axis=1)
    nt = (((1,), (1,)), ((), ()))
    heads = range(GDN_PH)
    ts, ps, rhs = [], [], []
    for h in heads:
        q, k, v = q_ref[0, 0, h], k_ref[0, 0, h], v_ref[0, 0, h]
        g8 = jnp.broadcast_to(g_ref[0, h], (8, n))
        gc_row = _dot_hilo(g8, cum_m)[0:1]
        gl_row = _dot_hilo(g8, tot_m)[0:1]
        gc_c, gl_c, b_c = _col_rep(gc_row, n), _col_rep(gl_row, n), _col_rep(b_ref[0, h], n)
        decay = jnp.exp(jnp.where(incl, wide(gc_c) - gc_row, NEG))
        kk = lax.dot_general(k, k, nt, preferred_element_type=F32)
        qk = lax.dot_general(q, k, nt, preferred_element_type=F32)
        xb = jnp.where(strict, -(kk * wide(b_c) * decay), 0.0).astype(BF16)
        a = qk * decay
        kf = k.astype(F32)
        egc = jnp.exp(gc_c)
        qg_ref[0, h] = (q.astype(F32) * egc).astype(qg_ref.dtype)
        kd_ref[0, h] = (kf * jnp.exp(gl_c - gc_c)).astype(kd_ref.dtype)
        for c in range(n // CHUNK):
            blk = slice(c * CHUNK, (c + 1) * CHUNK)
            a_ref[0, h, blk, :] = a[blk, blk].astype(a_ref.dtype)
        egl = jnp.exp(gl_c)
        egl_ref[0, h, 0] = jnp.concatenate([egl[c * CHUNK:c * CHUNK + 1] for c in range(n // CHUNK)], axis=0)
        rhs.append(jnp.concatenate([(v.astype(F32) * b_c).astype(BF16), (kf * b_c * egc).astype(BF16)], axis=1))
        ts.append(eye + xb.astype(F32))
        ps.append(xb)
    ps = [jnp.dot(p, p, preferred_element_type=F32).astype(BF16) for p in ps]
    for step in range(5):
        for h in heads:
            if step < 4:
                tp = jnp.dot(jnp.concatenate([ts[h].astype(BF16), ps[h]], axis=0), ps[h], preferred_element_type=F32)
                ts[h] = ts[h] + tp[:n]
                ps[h] = tp[n:].astype(BF16)
            else:
                ts[h] = ts[h] + jnp.dot(ts[h].astype(BF16), ps[h], preferred_element_type=F32)
    for h in heads:
        uw = jnp.dot(ts[h].astype(BF16), rhs[h], preferred_element_type=F32)
        u_ref[0, h] = uw[:, :DV_A].astype(u_ref.dtype)
        w_ref[0, h] = uw[:, DV_A:].astype(w_ref.dtype)


def _gdn_prep(qkv, g_t, beta_t, bsz, s):
    n, ph = GDN_GROUP, GDN_PH
    tok = lambda width, dt: jax.ShapeDtypeStruct((bsz, HA, s, width), dt)
    tspec = lambda width: pl.BlockSpec((1, ph, n, width), lambda b, h, i: (b, h, i, 0))
    qspec = lambda sec: pl.BlockSpec((1, 1, ph, n, DK_A), lambda b, h, i: (sec, b, h, i, 0))
    rspec = pl.BlockSpec((1, ph, 1, n), lambda b, h, i: (b, h, 0, i))
    return pl.pallas_call(
        _gdn_prep_kernel,
        out_shape=(tok(DV_A, BF16), tok(DK_A, BF16), tok(DK_A, BF16), tok(DK_A, BF16), tok(CHUNK, BF16),
                   jax.ShapeDtypeStruct((bsz, HA, s // n, n // CHUNK, LANES), F32)),
        grid=(bsz, HA // ph, s // n),
        in_specs=[qspec(0), qspec(1), qspec(2), rspec, rspec],
        out_specs=(tspec(DV_A), tspec(DK_A), tspec(DK_A), tspec(DK_A), tspec(CHUNK),
                   pl.BlockSpec((1, ph, 1, n // CHUNK, LANES), lambda b, h, i: (b, h, i, 0, 0))),
        compiler_params=_cparams(("parallel", "parallel", "parallel")),
        name="gdn_chunk_prep",
    )(qkv, qkv, qkv, g_t, beta_t)


def _gdn_scan_kernel(u_ref, w_ref, qg_ref, kd_ref, a_ref, egl_ref, z_ref, ng_ref, o_ref, state_ref, *, nchunk):
    @pl.when(pl.program_id(1) == 0)
    def _():
        state_ref[...] = jnp.zeros_like(state_ref)

    tn = (((0,), (0,)), ((), ()))
    heads = range(HA)

    def body(c, carry):
        rows = pl.ds(pl.multiple_of(c * CHUNK, CHUNK), CHUNK)
        st = [state_ref[h] for h in heads]
        sb = [x.astype(BF16) for x in st]
        vb = [(u_ref[0, h, rows, :].astype(F32)
               - jnp.dot(w_ref[0, h, rows, :], sb[h], preferred_element_type=F32)).astype(BF16) for h in heads]
        o = [jnp.dot(qg_ref[0, h, rows, :], sb[h], preferred_element_type=F32)
             + jnp.dot(a_ref[0, h, rows, :], vb[h], preferred_element_type=F32) for h in heads]
        for h in heads:
            state_ref[h] = (st[h] * egl_ref[0, h, pl.ds(c, 1), :]
                            + lax.dot_general(kd_ref[0, h, rows, :], vb[h], tn, preferred_element_type=F32))
        for h in heads:
            cols = slice(h * DV_A, (h + 1) * DV_A)
            z = z_ref[0, rows, cols].astype(F32)
            on = o[h] * lax.rsqrt(jnp.mean(o[h] * o[h], axis=-1, keepdims=True) + EPS) * ng_ref[...]
            o_ref[0, rows, cols] = (on * (z * jax.nn.sigmoid(z))).astype(o_ref.dtype)
        return carry

    lax.fori_loop(0, nchunk, body, 0)


def _gdn_scan(u, w, qg, kd, a, egl, main, norm_gdn, bsz, s):
    ts = GDN_TS
    z_blk0 = 3 * HA * DK_A // (HA * DV_A)
    hspec = lambda width: pl.BlockSpec((1, HA, ts, width), lambda b, i: (b, 0, i, 0))
    return pl.pallas_call(
        functools.partial(_gdn_scan_kernel, nchunk=ts // CHUNK),
        out_shape=jax.ShapeDtypeStruct((bsz, s, HA * DV_A), BF16),
        grid=(bsz, s // ts),
        in_specs=[hspec(DV_A), hspec(DK_A), hspec(DK_A), hspec(DK_A), hspec(CHUNK),
                  pl.BlockSpec((1, HA, ts // CHUNK, LANES), lambda b, i: (b, 0, i, 0)),
                  pl.BlockSpec((1, ts, HA * DV_A), lambda b, i: (b, i, z_blk0)),
                  pl.BlockSpec((1, DV_A), lambda b, i: (0, 0))],
        out_specs=pl.BlockSpec((1, ts, HA * DV_A), lambda b, i: (b, i, 0)),
        scratch_shapes=[pltpu.VMEM((HA, DK_A, DV_A), F32)],
        compiler_params=_cparams(("parallel", "arbitrary")),
        name="gdn_delta_scan",
    )(u, w, qg, kd, a, egl, main, norm_gdn.reshape(1, DV_A))


def _gated_deltanet(main, a_in, b_in, conv_w, a_log, dt_bias, norm_gdn, bsz, s):
    qkv = _gdn_conv(main, conv_w, bsz, s)
    g = -jnp.exp(a_log.astype(F32)) * jax.nn.softplus(a_in.astype(F32) + dt_bias.astype(F32))
    beta = jax.nn.sigmoid(b_in.astype(F32))
    g_t = g.transpose(0, 2, 1).reshape(bsz, HA, 1, s)
    beta_t = beta.transpose(0, 2, 1).reshape(bsz, HA, 1, s)
    u, w, qg, kd, a, egl = _gdn_prep(qkv, g_t, beta_t, bsz, s)
    egl = egl.reshape(bsz, HA, s // CHUNK, LANES)
    return _gdn_scan(u, w, qg, kd, a, egl, main, norm_gdn, bsz, s)


NSA_TS = 512
BIG = 1e30
DROPPED = -3e38


def _rope(x, cos, sin, lane):
    half = ROT_DIM // 2
    partner = jnp.where(lane < half, pltpu.roll(x, DH_B - half, axis=1), pltpu.roll(x, half, axis=1))
    return x * cos + partner * sin


def _nsa_rope_kernel(q_ref, kc_ref, vc_ref, ks_ref, kw_ref, cos_ref, sin_ref,
                     qo_ref, kco_ref, vco_ref, kso_ref, kwo_ref):
    cos, sin = cos_ref[0], sin_ref[0]
    lane = lax.broadcasted_iota(jnp.int32, cos.shape, 1)
    head = lambda ref, h: ref[0, :, h * DH_B:(h + 1) * DH_B].astype(F32)
    for h in range(HB):
        qo_ref[0, :, h * DH_B:(h + 1) * DH_B] = (_rope(head(q_ref, h), cos, sin, lane) * DH_B ** -0.5).astype(qo_ref.dtype)
    for g in range(G_KV):
        cols = slice(g * DH_B, (g + 1) * DH_B)
        kco_ref[0, g] = _rope(head(kc_ref, g), cos, sin, lane).astype(kco_ref.dtype)
        vco_ref[0, g] = vc_ref[0, :, cols]
        kso_ref[0, :, cols] = _rope(head(ks_ref, g), cos, sin, lane).astype(kso_ref.dtype)
        kwo_ref[0, :, cols] = _rope(head(kw_ref, g), cos, sin, lane).astype(kwo_ref.dtype)


def _nsa_rope(main, cos, sin, col, bsz, s):
    ts = NSA_TS
    kvw = G_KV * DH_B
    tok = lambda width, c0: pl.BlockSpec((1, ts, width), lambda b, i: (b, i, c0 // width))
    tab = pl.BlockSpec((1, ts, DH_B), lambda b, i: (b, i, 0))
    grp = pl.BlockSpec((1, G_KV, ts, DH_B), lambda b, i: (b, 0, i, 0))
    flat = lambda width: pl.BlockSpec((1, ts, width), lambda b, i: (b, i, 0))
    return pl.pallas_call(
        _nsa_rope_kernel,
        out_shape=(jax.ShapeDtypeStruct((bsz, s, HB * DH_B), BF16),
                   jax.ShapeDtypeStruct((bsz, G_KV, s, DH_B), BF16), jax.ShapeDtypeStruct((bsz, G_KV, s, DH_B), BF16),
                   jax.ShapeDtypeStruct((bsz, s, kvw), BF16), jax.ShapeDtypeStruct((bsz, s, kvw), BF16)),
        grid=(bsz, s // ts),
        in_specs=[tok(HB * DH_B, col["qb"]), tok(kvw, col["kc"]), tok(kvw, col["vc"]), tok(kvw, col["ks"]),
                  tok(kvw, col["kw"]), tab, tab],
        out_specs=(flat(HB * DH_B), grp, grp, flat(kvw), flat(kvw)),
        compiler_params=_cparams(("parallel", "parallel")),
        name="nsa_rotary",
    )(main, main, main, main, main, cos, sin)


def _gelu_tanh(x):
    return 0.5 * x * (1.0 + jnp.tanh(math.sqrt(2.0 / math.pi) * (x + 0.044715 * x * x * x)))


def _nsa_compress_kernel(k_ref, v_ref, pek_ref, pev_ref, w1k_ref, w1v_ref, w2k_ref, w2v_ref, ko_ref, vo_ref):
    nseg = k_ref.shape[2]
    for x_ref, pe_ref, w1_ref, w2_ref, o_ref in ((k_ref, pek_ref, w1k_ref, w2k_ref, ko_ref),
                                                 (v_ref, pev_ref, w1v_ref, w2v_ref, vo_ref)):
        for g in range(G_KV):
            x = x_ref[0, g].astype(F32)
            lo = jnp.dot((x + pe_ref[0:1]).astype(BF16), w1_ref[0], preferred_element_type=F32)
            hi = jnp.dot((x + pe_ref[1:2]).astype(BF16), w1_ref[1], preferred_element_type=F32)
            pre = lo + pltpu.roll(hi, nseg - 1, axis=0)
            o_ref[0, g] = jnp.dot(_gelu_tanh(pre).astype(BF16), w2_ref[...],
                                  preferred_element_type=F32).astype(o_ref.dtype)


def _nsa_compress(kc_t, vc_t, pe_ck, w1_ck, w2_ck, pe_cv, w1_cv, w2_cv, bsz, s):
    assert L_CMP == 2 * STRIDE_CMP
    nseg = s // STRIDE_CMP
    width = STRIDE_CMP * DH_B
    seg = lambda t: t.reshape(bsz, G_KV, nseg, width)
    pe2 = lambda pe: pe.reshape(2, width)
    w1h = lambda w: w.reshape(2, width, DH_B).astype(BF16)
    xspec = pl.BlockSpec((1, G_KV, nseg, width), lambda b: (b, 0, 0, 0))
    pspec = pl.BlockSpec((2, width), lambda b: (0, 0))
    w1spec = pl.BlockSpec((2, width, DH_B), lambda b: (0, 0, 0))
    w2spec = pl.BlockSpec((DH_B, DH_B), lambda b: (0, 0))
    ospec = pl.BlockSpec((1, G_KV, nseg, DH_B), lambda b: (b, 0, 0, 0))
    oshape = jax.ShapeDtypeStruct((bsz, G_KV, nseg, DH_B), BF16)
    return pl.pallas_call(
        _nsa_compress_kernel,
        out_shape=(oshape, oshape),
        grid=(bsz,),
        in_specs=[xspec, xspec, pspec, pspec, w1spec, w1spec, w2spec, w2spec],
        out_specs=(ospec, ospec),
        compiler_params=_cparams(("parallel",)),
        name="nsa_compress",
    )(seg(kc_t), seg(vc_t), pe2(pe_ck), pe2(pe_cv), w1h(w1_ck), w1h(w1_cv), w2_ck.astype(BF16), w2_cv.astype(BF16))


def _nsa_cmp_kernel(q_ref, kc_ref, vc_ref, ov_ref, o_ref, mb_ref, *, tq, n_slc):
    i = pl.program_id(2)
    rows = R_GRP * tq
    ncmp = kc_ref.shape[2]
    qa = _stack_heads(q_ref[0], None)
    s = lax.dot_general(qa, kc_ref[0, 0], (((1,), (1,)), ((), ())), preferred_element_type=F32)
    t_row = (lax.broadcasted_iota(jnp.int32, (rows, ncmp), 0) & (tq - 1)) + i * tq
    c_end = lax.broadcasted_iota(jnp.int32, (rows, ncmp), 1) * STRIDE_CMP + (L_CMP - 1)
    valid = c_end <= t_row
    sm = jnp.where(valid, s, NEG)
    p = jnp.where(valid, jnp.exp(sm - jnp.max(sm, axis=1, keepdims=True)), 0.0)
    l = jnp.sum(p, axis=1, keepdims=True)
    p = p * (1.0 / jnp.where(l > 0.0, l, 1.0))
    _unstack_heads(o_ref, jnp.dot(p.astype(BF16), vc_ref[0, 0], preferred_element_type=F32), tq)
    psum = p[0:tq]
    for r in range(1, R_GRP):
        psum = psum + p[r * tq:(r + 1) * tq]
    imp = _dot_hilo(psum, ov_ref[...])
    nb = -(-n_slc // 8) * 8
    v = imp.T[:nb]
    blk = lax.broadcasted_iota(jnp.int32, (nb, tq), 0)
    cur = (lax.broadcasted_iota(jnp.int32, (nb, tq), 1) + i * tq) // L_SLC
    forced = (blk == 0) | (blk == cur)
    v = jnp.where(forced, BIG, jnp.where(blk <= cur, v, -BIG))
    sel = jnp.zeros((nb, tq), F32)
    for _ in range(T_SEL):
        m = jnp.max(v, axis=0, keepdims=True)
        first = jnp.min(jnp.where(v == m, blk, LANES), axis=0, keepdims=True)
        hit = blk == first
        sel = jnp.where(hit, 1.0, sel)
        v = jnp.where(hit, DROPPED, v)
    bias = (jnp.where(blk <= cur, sel, 0.0) - 1.0) * BIG
    bias = jnp.concatenate([bias, jnp.zeros((LANES - nb, tq), F32)], axis=0)
    mb_ref[0, 0] = bias.T.astype(mb_ref.dtype)


def _nsa_cmp_select(q, k_cmp, v_cmp, bsz, s):
    tq = ATT_TQ
    nseg = s // STRIDE_CMP
    n_slc = s // L_SLC
    assert nseg <= LANES or nseg % LANES == 0
    c_start = np.arange(nseg) * STRIDE_CMP
    j_start = np.arange(n_slc) * L_SLC
    overlap = ((c_start[:, None] < j_start[None, :] + L_SLC) & (c_start[:, None] + L_CMP > j_start[None, :]))
    overlap = jnp.asarray(np.pad(overlap.astype(np.float32), ((0, 0), (0, LANES - n_slc))), BF16)
    qspec = pl.BlockSpec((1, tq, R_GRP * DH_B), lambda b, g, i: (b, i, g))
    cspec = pl.BlockSpec((1, 1, nseg, DH_B), lambda b, g, i: (b, g, 0, 0))
    return pl.pallas_call(
        functools.partial(_nsa_cmp_kernel, tq=tq, n_slc=n_slc),
        out_shape=(jax.ShapeDtypeStruct((bsz, s, HB * DH_B), BF16), jax.ShapeDtypeStruct((bsz, G_KV, s, LANES), BF16)),
        grid=(bsz, G_KV, s // tq),
        in_specs=[qspec, cspec, cspec, pl.BlockSpec((nseg, LANES), lambda b, g, i: (0, 0))],
        out_specs=(qspec, pl.BlockSpec((1, 1, tq, LANES), lambda b, g, i: (b, g, i, 0))),
        compiler_params=_cparams(("parallel", "parallel", "parallel")),
        name="nsa_compressed_select",
    )(q, k_cmp, v_cmp, overlap)


def _stack_heads(q, extra):
    parts = []
    for r in range(R_GRP):
        qr = q[:, r * DH_B:(r + 1) * DH_B]
        parts.append(qr if extra is None else jnp.concatenate([qr, extra], axis=1))
    return jnp.concatenate(parts, axis=0)


def _unstack_heads(o_ref, o, tq):
    for r in range(R_GRP):
        o_ref[0, :, r * DH_B:(r + 1) * DH_B] = o[r * tq:(r + 1) * tq].astype(o_ref.dtype)


def _heads_t(q, extra_t):
    parts = []
    for r in range(R_GRP):
        qt = q[:, r * DH_B:(r + 1) * DH_B].astype(F32).T.astype(BF16)
        parts.append(qt if extra_t is None else jnp.concatenate([qt, extra_t], axis=0))
    return jnp.concatenate(parts, axis=1)


def _softmax_step_t(ss, vs, carry):
    m, l, acc = carry
    m_new = functools.reduce(jnp.maximum, [jnp.max(s, axis=0, keepdims=True) for s in ss], m)
    alpha = jnp.exp(m - m_new)
    ps = [jnp.exp(s - m_new) for s in ss]
    l = alpha * l + functools.reduce(lambda a, b: a + b, [jnp.sum(p, axis=0, keepdims=True) for p in ps])
    tn = (((0,), (0,)), ((), ()))
    pv = [lax.dot_general(v, p.astype(BF16), tn, preferred_element_type=F32) for v, p in zip(vs, ps)]
    return m_new, l, functools.reduce(lambda a, b: a + b, pv, alpha * acc)


def _softmax_init_t(rows):
    return (jnp.full((1, rows), -jnp.inf, F32), jnp.zeros((1, rows), F32), jnp.zeros((DH_B, rows), F32))


def _selected_branch(i, q, mb, ks_ref, oh_ref, vs_ref, tq):
    rows = R_GRP * tq
    qat = _heads_t(q, mb.astype(F32).T.astype(BF16))

    def scores(j):
        keys = pl.ds(pl.multiple_of(j * tq, tq), tq)
        k = jnp.concatenate([ks_ref[0, keys, :], oh_ref[keys, :]], axis=1)
        return jnp.dot(k, qat, preferred_element_type=F32)

    def values(j):
        return vs_ref[0, pl.ds(pl.multiple_of(j * tq, tq), tq), :]

    def pair(jj, c):
        j = 2 * jj
        return _softmax_step_t([scores(j), scores(j + 1)], [values(j), values(j + 1)], c)

    carry = lax.fori_loop(0, i // 2, pair, _softmax_init_t(rows))
    k_loc = lax.broadcasted_iota(jnp.int32, (tq, tq), 0)
    t_loc = lax.broadcasted_iota(jnp.int32, (tq, tq), 1)
    causal = jnp.concatenate([jnp.where(k_loc <= t_loc, 0.0, NEG)] * R_GRP, axis=1)
    diag = lambda c: _softmax_step_t([scores(i) + causal], [values(i)], c)
    both = lambda c: _softmax_step_t([scores(i) + causal, scores(i - 1)], [values(i), values(i - 1)], c)
    m, l, acc = lax.cond((i % 2) == 1, both, diag, carry)
    return acc / l


def _window_branch(i, q, kw_refs, vw_refs, tq):
    rows = R_GRP * tq
    qt = _heads_t(q, None)
    k_loc = lax.broadcasted_iota(jnp.int32, (tq, tq), 0)
    t_loc = lax.broadcasted_iota(jnp.int32, (tq, tq), 1)
    bound = jnp.minimum(t_loc[0:1, :] + (i * tq + 1), WINDOW)
    ss = []
    for n in range(3):
        dist = t_loc + (2 - n) * tq - k_loc
        bias = jnp.where(dist.astype(jnp.uint32) < bound.astype(jnp.uint32), 0.0, NEG)
        ss.append(jnp.dot(kw_refs[n][0], qt, preferred_element_type=F32) + jnp.concatenate([bias] * R_GRP, axis=1))
    m, l, acc = _softmax_step_t(ss, [r[0] for r in vw_refs], _softmax_init_t(rows))
    return acc / l


def _nsa_local_kernel(q_ref, mb_ref, ks_ref, oh_ref, vs_ref, kw0, kw1, kw2, vw0, vw1, vw2, oc_ref, gate_ref, o_ref,
                      *, tq, gate_lane0):
    g, i = pl.program_id(1), pl.program_id(2)
    q = q_ref[0]
    o_slc = _selected_branch(i, q, mb_ref[0, 0], ks_ref, oh_ref, vs_ref, tq)
    o_win = _window_branch(i, q, (kw0, kw1, kw2), (vw0, vw1, vw2), tq)
    gates = jax.nn.sigmoid(gate_ref[0])
    lane = lax.broadcasted_iota(jnp.int32, gates.shape, 1)
    pick = lambda idx: jnp.sum(jnp.where(lane == idx, gates, 0.0), axis=1, keepdims=True)
    for r in range(R_GRP):
        base = gate_lane0 + (g * R_GRP + r) * 3
        rows = slice(r * tq, (r + 1) * tq)
        cols = slice(r * DH_B, (r + 1) * DH_B)
        o = (pick(base) * oc_ref[0, :, cols].astype(F32) + pick(base + 1) * o_slc[:, rows].T
             + pick(base + 2) * o_win[:, rows].T)
        o_ref[0, :, cols] = o.astype(o_ref.dtype)


def _nsa_local(q, maskbias, ks, main, kw, o_cmp, small, col, gate_lane0, bsz, s):
    tq = ATT_TQ
    assert 2 * tq >= WINDOW
    onehot = jnp.asarray(np.arange(s)[:, None] // L_SLC == np.arange(LANES)[None, :], BF16)
    qspec = pl.BlockSpec((1, tq, R_GRP * DH_B), lambda b, g, i: (b, i, g))
    seq = lambda c0: pl.BlockSpec((1, s, DH_B), lambda b, g, i: (b, 0, c0 // DH_B + g))
    back = lambda c0, n: pl.BlockSpec((1, tq, DH_B), lambda b, g, i: (b, jnp.maximum(i - n, 0), c0 // DH_B + g))
    return pl.pallas_call(
        functools.partial(_nsa_local_kernel, tq=tq, gate_lane0=gate_lane0),
        out_shape=jax.ShapeDtypeStruct((bsz, s, HB * DH_B), BF16),
        grid=(bsz, G_KV, s // tq),
        in_specs=[qspec,
                  pl.BlockSpec((1, 1, tq, LANES), lambda b, g, i: (b, g, i, 0)),
                  seq(0), pl.BlockSpec((s, LANES), lambda b, g, i: (0, 0)), seq(col["vs"]),
                  back(0, 2), back(0, 1), back(0, 0),
                  back(col["vw"], 2), back(col["vw"], 1), back(col["vw"], 0),
                  qspec, pl.BlockSpec((1, tq, LANES), lambda b, g, i: (b, i, 0))],
        out_specs=qspec,
        compiler_params=_cparams(("parallel", "parallel", "arbitrary")),
        name="nsa_selected_window_combine",
    )(q, maskbias, ks, onehot, main, kw, kw, kw, main, main, main, o_cmp, small)


def _moe_sub_blocks(nv_ref, out_ref, compute):
    nsub = (nv_ref[pl.program_id(1)] + MOE_SUB - 1) // MOE_SUB
    rows_of = lambda sb: pl.ds(pl.multiple_of(sb * MOE_SUB, MOE_SUB), MOE_SUB)

    def pair(p, carry):
        rows = [rows_of(2 * p), rows_of(2 * p + 1)]
        for r, val in zip(rows, compute(rows)):
            out_ref[r, :] = val
        return carry

    def dead(sb, carry):
        out_ref[rows_of(sb), :] = jnp.zeros((MOE_SUB, out_ref.shape[1]), out_ref.dtype)
        return carry

    lax.fori_loop(0, nsub // 2, pair, 0)

    @pl.when(nsub % 2 == 1)
    def _():
        out_ref[rows_of(nsub - 1), :] = compute([rows_of(nsub - 1)])[0]

    lax.fori_loop(nsub, MOE_TM // MOE_SUB, dead, 0)


def _moe_new_expert(be_ref):
    i = pl.program_id(1)
    return (i == 0) | (be_ref[i] != be_ref[jnp.maximum(i - 1, 0)])


def _moe_up_kernel(be_ref, nv_ref, st_ref, x_ref, wg_ref, wu_ref, bg_ref, bu_ref, h_ref, wgb_ref, wub_ref):
    @pl.when(_moe_new_expert(be_ref))
    def _():
        wgb_ref[...] = wg_ref[0].astype(BF16)
        wub_ref[...] = wu_ref[0].astype(BF16)

    def compute(rows):
        xs = [jnp.concatenate(_unpack_bf16_pairs(x_ref[r, :]), axis=1).astype(BF16) for r in rows]
        gates = [jnp.dot(x, wgb_ref[...], preferred_element_type=F32) + bg_ref[0] for x in xs]
        ups = [jnp.dot(x, wub_ref[...], preferred_element_type=F32) + bu_ref[0] for x in xs]
        outs = []
        for gate, up in zip(gates, ups):
            gate = jnp.minimum(gate, SWIGLU_LIMIT)
            up = jnp.clip(up, -SWIGLU_LIMIT, SWIGLU_LIMIT)
            outs.append(((up + 1.0) * gate * jax.nn.sigmoid(SWIGLU_ALPHA * gate)).astype(h_ref.dtype))
        return outs

    _moe_sub_blocks(nv_ref, h_ref, compute)


def _moe_down_kernel(be_ref, nv_ref, st_ref, h_ref, wd_ref, bd_ref, y_ref, wdb_ref):
    @pl.when(_moe_new_expert(be_ref))
    def _():
        wdb_ref[...] = wd_ref[0].astype(BF16)

    half = wdb_ref.shape[1] // 2

    def compute(rows):
        ys = [jnp.dot(h_ref[r, :], wdb_ref[...], preferred_element_type=F32) + bd_ref[0] for r in rows]
        return [_pack_bf16_pairs(y[:, :half], y[:, half:]) for y in ys]

    _moe_sub_blocks(nv_ref, y_ref, compute)


def _moe_experts(rows, block_e, n_valid, src_tile, w_gate, b_gate, w_up, b_up, w_down, b_down):
    n_rows, d = rows.shape[0], 2 * rows.shape[1]
    n_blocks = n_rows // MOE_TM
    tile = lambda width: pl.BlockSpec((MOE_TM, width), lambda n, i, be, nv, st: (st[i], 0))
    wcol = lambda k: pl.BlockSpec((1, k, tn), lambda n, i, be, nv, st: (be[i], 0, n))
    bcol = lambda: pl.BlockSpec((1, 1, tn), lambda n, i, be, nv, st: (be[i], 0, n))
    params = _cparams(("arbitrary", "arbitrary"))
    tn = MOE_TN
    h = pl.pallas_call(
        _moe_up_kernel,
        out_shape=jax.ShapeDtypeStruct((n_rows, D_FF), BF16),
        grid_spec=pltpu.PrefetchScalarGridSpec(
            num_scalar_prefetch=3, grid=(D_FF // tn, n_blocks),
            in_specs=[tile(d // 2), wcol(d), wcol(d), bcol(), bcol()],
            out_specs=pl.BlockSpec((MOE_TM, tn), lambda n, i, be, nv, st: (i, n)),
            scratch_shapes=[pltpu.VMEM((d, tn), BF16), pltpu.VMEM((d, tn), BF16)]),
        compiler_params=params, name="moe_expert_up",
    )(block_e, n_valid, src_tile, rows, w_gate, w_up, b_gate.reshape(N_EXP, 1, D_FF), b_up.reshape(N_EXP, 1, D_FF))
    tn = MOE_TN_DOWN
    return pl.pallas_call(
        _moe_down_kernel,
        out_shape=jax.ShapeDtypeStruct((n_rows, d // 2), jnp.uint32),
        grid_spec=pltpu.PrefetchScalarGridSpec(
            num_scalar_prefetch=3, grid=(d // tn, n_blocks),
            in_specs=[tile(D_FF), wcol(D_FF), bcol()],
            out_specs=pl.BlockSpec((MOE_TM, tn // 2), lambda n, i, be, nv, st: (i, n)),
            scratch_shapes=[pltpu.VMEM((D_FF, tn), BF16)]),
        compiler_params=params, name="moe_expert_down",
    )(block_e, n_valid, src_tile, h, w_down, b_down.reshape(N_EXP, 1, d))


ROUTE_TT = 512
MOVE_TT = 256
HALF = D_MODEL // 2


def _pack_bf16_pairs(lo, hi):
    as_bits = lambda v: pltpu.bitcast(v.astype(BF16).astype(F32), jnp.uint32)
    return (as_bits(lo) >> 16) | (as_bits(hi) & jnp.uint32(0xFFFF0000))


def _unpack_bf16_pairs(w):
    return pltpu.bitcast(w << 16, F32), pltpu.bitcast(w & jnp.uint32(0xFFFF0000), F32)


def _route_kernel(h_ref, wr_ref, br_ref, e_ref, w_ref, p_ref, cnt_ref, run_ref, *, tt):
    @pl.when(pl.program_id(0) == 0)
    def _():
        run_ref[...] = jnp.zeros_like(run_ref)

    h = h_ref[...]
    logits = (jnp.dot(h, wr_ref[0], preferred_element_type=F32) + jnp.dot(h, wr_ref[1], preferred_element_type=F32)
              + br_ref[...])
    lane = lax.broadcasted_iota(jnp.int32, (tt, LANES), 1)
    v = jnp.where(lane < N_EXP, logits, -BIG)
    tops, hits, firsts = [], [], []
    for _ in range(TOP_K):
        m = jnp.max(v, axis=1, keepdims=True)
        first = jnp.min(jnp.where(v == m, lane, LANES), axis=1, keepdims=True)
        hit = lane == first
        v = jnp.where(hit, DROPPED, v)
        tops.append(m), hits.append(hit), firsts.append(first)
    ex = [jnp.exp(m - tops[0]) for m in tops]
    inv = 1.0 / functools.reduce(lambda a, b: a + b, ex)
    onehot = functools.reduce(lambda a, b: a + b, [jnp.where(hh, 1.0, 0.0) for hh in hits]).astype(BF16)
    ri = lax.broadcasted_iota(jnp.int32, (tt, tt), 0)
    ci = lax.broadcasted_iota(jnp.int32, (tt, tt), 1)
    before = jnp.where(ci < ri, 1.0, 0.0).astype(BF16)
    rank = jnp.dot(before, onehot, preferred_element_type=F32) + run_ref[0:1]
    run_ref[...] = run_ref[...] + jnp.dot(jnp.ones((8, tt), BF16), onehot, preferred_element_type=F32)
    cnt_ref[...] = run_ref[...].astype(jnp.int32)
    place = lambda cols, zero: functools.reduce(
        lambda acc, kc: jnp.where(lane == kc[0], kc[1], acc), list(enumerate(cols)), zero)
    e_ref[...] = place(firsts, jnp.zeros((tt, LANES), jnp.int32))
    w_ref[...] = place([e * inv for e in ex], jnp.zeros((tt, LANES), F32))
    pos = [jnp.sum(jnp.where(hh, rank, 0.0), axis=1, keepdims=True).astype(jnp.int32) for hh in hits]
    p_ref[...] = place(pos, jnp.zeros((tt, LANES), jnp.int32))


def _route(h, w_router, b_router):
    n, d = h.shape
    tt = ROUTE_TT
    wr = jnp.pad(w_router, ((0, 0), (0, LANES - N_EXP)))
    hi = wr.astype(BF16)
    wr2 = jnp.stack([hi, (wr - hi.astype(F32)).astype(BF16)])
    br = jnp.pad(b_router, (0, LANES - N_EXP)).reshape(1, LANES)
    tok = lambda dt: jax.ShapeDtypeStruct((n, LANES), dt)
    tspec = pl.BlockSpec((tt, LANES), lambda i: (i, 0))
    return pl.pallas_call(
        functools.partial(_route_kernel, tt=tt),
        out_shape=(tok(jnp.int32), tok(F32), tok(jnp.int32), jax.ShapeDtypeStruct((8, LANES), jnp.int32)),
        grid=(n // tt,),
        in_specs=[pl.BlockSpec((tt, d), lambda i: (i, 0)), pl.BlockSpec((2, d, LANES), lambda i: (0, 0, 0)),
                  pl.BlockSpec((1, LANES), lambda i: (0, 0))],
        out_specs=(tspec, tspec, tspec, pl.BlockSpec((8, LANES), lambda i: (0, 0))),
        scratch_shapes=[pltpu.VMEM((8, LANES), F32)],
        compiler_params=_cparams(("arbitrary",)),
        name="moe_route",
    )(h, wr2, br)


def _dispatch_kernel(dest_ref, src_ref, init_ref, rows_ref, sem, *, tt):
    del init_ref
    t0 = pl.program_id(0) * tt

    def copy(j, k):
        return pltpu.make_async_copy(src_ref.at[pl.ds(j, 1)],
                                     rows_ref.at[pl.ds(dest_ref[(t0 + j) * TOP_K + k], 1)], sem)

    def start(j, carry):
        for k in range(TOP_K):
            copy(j, k).start(priority=k % 2)
        return carry

    def wait(j, carry):
        for k in range(TOP_K):
            copy(j, k).wait()
        return carry

    lax.fori_loop(0, tt, start, 0)
    lax.fori_loop(0, tt, wait, 0)


def _dispatch(dest, hpk, n_rows):
    n, width = hpk.shape
    tt = MOVE_TT
    return pl.pallas_call(
        functools.partial(_dispatch_kernel, tt=tt),
        out_shape=jax.ShapeDtypeStruct((n_rows, width), jnp.uint32),
        grid_spec=pltpu.PrefetchScalarGridSpec(
            num_scalar_prefetch=1, grid=(n // tt,),
            in_specs=[pl.BlockSpec((tt, width), lambda i, dest: (i, 0)), pl.BlockSpec(memory_space=pl.ANY)],
            out_specs=pl.BlockSpec(memory_space=pl.ANY),
            scratch_shapes=[pltpu.SemaphoreType.DMA]),
        input_output_aliases={2: 0},
        compiler_params=_cparams(("arbitrary",)),
        name="moe_dispatch",
    )(dest, hpk, jnp.zeros((n_rows, width), jnp.uint32))


def _combine_kernel(dest_ref, y_ref, w_ref, x1_ref, g_ref, o_ref, buf_ref, sem, *, tt, norm):
    step = pl.program_id(0)
    slot = step % 2

    def copy(tile, sl, j, k):
        return pltpu.make_async_copy(y_ref.at[pl.ds(dest_ref[(tile * tt + j) * TOP_K + k], 1)],
                                     buf_ref.at[sl, k, pl.ds(j, 1)], sem.at[sl])

    def gather(tile, sl):
        def start(j, carry):
            for k in range(TOP_K):
                copy(tile, sl, j, k).start(priority=k % 2)
            return carry
        lax.fori_loop(0, tt, start, 0)

    @pl.when(step == 0)
    def _():
        gather(step, slot)

    @pl.when(step + 1 < pl.num_programs(0))
    def _():
        gather(step + 1, 1 - slot)

    def wait(j, carry):
        for k in range(TOP_K):
            copy(step, slot, j, k).wait()
        return carry

    lax.fori_loop(0, tt, wait, 0)
    wts = w_ref[...]
    lane = lax.broadcasted_iota(jnp.int32, wts.shape, 1)
    x = x1_ref[...]
    nq = MOE_TN_DOWN // 2
    for k in range(TOP_K):
        wk = jnp.sum(jnp.where(lane == k, wts, 0.0), axis=1, keepdims=True)
        parts = []
        for n in range(D_MODEL // MOE_TN_DOWN):
            lo, hi = _unpack_bf16_pairs(buf_ref[slot, k, :, n * nq:(n + 1) * nq])
            parts += [lo, hi]
        x = x + wk * jnp.concatenate(parts, axis=1)
    if norm:
        x = x * lax.rsqrt(jnp.mean(x * x, axis=-1, keepdims=True) + EPS) * g_ref[...]
    o_ref[...] = x.astype(o_ref.dtype)


def _combine(dest, ypk, wts, x1, gain):
    n, d = x1.shape
    tt = MOVE_TT
    norm = gain is not None
    gain = gain if norm else jnp.ones((d,), F32)
    return pl.pallas_call(
        functools.partial(_combine_kernel, tt=tt, norm=norm),
        out_shape=jax.ShapeDtypeStruct((n, d), F32),
        grid_spec=pltpu.PrefetchScalarGridSpec(
            num_scalar_prefetch=1, grid=(n // tt,),
            in_specs=[pl.BlockSpec(memory_space=pl.ANY),
                      pl.BlockSpec((tt, LANES), lambda i, dest: (i, 0)),
                      pl.BlockSpec((tt, d), lambda i, dest: (i, 0)),
                      pl.BlockSpec((1, d), lambda i, dest: (0, 0))],
            out_specs=pl.BlockSpec((tt, d), lambda i, dest: (i, 0)),
            scratch_shapes=[pltpu.VMEM((2, TOP_K, tt, d // 2), jnp.uint32), pltpu.SemaphoreType.DMA((2,))]),
        compiler_params=_cparams(("arbitrary",)),
        name="moe_combine_norm",
    )(dest, ypk, wts, x1, gain.reshape(1, d))


def _native_sparse_attention(main, small, positions, pe_ck, w1_ck, w2_ck, pe_cv, w1_cv, w2_cv, col, gate_lane0, bsz, s):
    half = ROT_DIM // 2
    inv_freq = ROPE_THETA ** (-jnp.arange(half, dtype=F32) * 2.0 / ROT_DIM)
    ang = positions.astype(F32)[..., None] * inv_freq
    cos, sin = jnp.cos(ang), jnp.sin(ang)
    rest = (bsz, s, DH_B - ROT_DIM)
    cos_t = jnp.concatenate([cos, cos, jnp.ones(rest, F32)], axis=-1)
    sin_t = jnp.concatenate([-sin, sin, jnp.zeros(rest, F32)], axis=-1)
    q, kc_t, vc_t, ks, kw = _nsa_rope(main, cos_t, sin_t, col, bsz, s)
    k_cmp, v_cmp = _nsa_compress(kc_t, vc_t, pe_ck, w1_ck, w2_ck, pe_cv, w1_cv, w2_cv, bsz, s)
    o_cmp, maskbias = _nsa_cmp_select(q, k_cmp, v_cmp, bsz, s)
    return _nsa_local(q, maskbias, ks, main, kw, o_cmp, small, col, gate_lane0, bsz, s)


def _moe_ffn(hf, hpk, x1, final_gain, w_router, b_router, w_gate, b_gate, w_up, b_up, w_down, b_down):
    n_tok, d = x1.shape
    eidx, wts, pos, cnt = _route(hf, w_router, b_router)
    counts = cnt[0, :N_EXP]
    padded = ((counts + MOE_TM - 1) // MOE_TM) * MOE_TM
    pad_end = jnp.cumsum(padded)
    pad_start = pad_end - padded
    part = counts % MOE_TM
    gap = jnp.where(part > 0, MOE_TM - part, 0)
    pick = lambda table: jnp.sum(jnp.where(eidx[:, :TOP_K, None] == jnp.arange(N_EXP), table, 0), axis=-1)
    rank = pos[:, :TOP_K]
    dest = pick(pad_start) + rank + jnp.where(rank >= pick(part), pick(gap), 0)
    dest = dest.reshape(-1).astype(jnp.int32)
    n_blocks = (n_tok * TOP_K + N_EXP * (MOE_TM - 1) + MOE_TM - 1) // MOE_TM
    rows = _dispatch(dest, hpk, n_blocks * MOE_TM)
    tile0 = jnp.arange(n_blocks, dtype=jnp.int32) * MOE_TM
    block_e = jnp.minimum(jnp.searchsorted(pad_end, tile0, side='right'), N_EXP - 1).astype(jnp.int32)
    first = (tile0 == pad_start[block_e]) & (part[block_e] > 0)
    n_valid = jnp.where(tile0 < pad_end[-1], jnp.where(first, part[block_e], MOE_TM), 0).astype(jnp.int32)
    tile_id = jnp.arange(n_blocks, dtype=jnp.int32)
    src_tile = jnp.where(n_valid > 0, tile_id, jnp.maximum(pad_end[-1] // MOE_TM - 1, 0)).astype(jnp.int32)
    ypk = _moe_experts(rows, block_e, n_valid, src_tile, w_gate, b_gate, w_up, b_up, w_down, b_down)
    return _combine(dest, ypk, wts, x1, final_gain)


def _layer(x, positions, norm_mix, w_in, conv_w, a_log, dt_bias, norm_gdn, pe_ck, w1_ck, w2_ck,
           pe_cv, w1_cv, w2_cv, w_proj_a, w_proj_b, w_out, norm_ffn, w_router, b_router,
           w_gate, b_gate, w_up, b_up, w_down, b_down, final_gain):
    bsz, s, d = x.shape
    n_tok = bsz * s
    x2 = x.reshape(n_tok, d)
    h = _rmsnorm(x2, norm_mix, BF16)
    n_small = 2 * HA + 3 * HB
    sp = SPLIT_POINTS
    w_main = jnp.concatenate([w_in[:, :sp[3]], w_in[:, sp[5]:sp[12]], w_in[:, sp[13]:]], axis=1).astype(BF16)
    w_small = jnp.concatenate([w_in[:, sp[3]:sp[5]], w_in[:, sp[12]:sp[13]]], axis=1)
    w_small = jnp.pad(w_small, ((0, 0), (0, LANES - n_small))).astype(BF16)
    main = _matmul(h, w_main, BF16, tm=2048).reshape(bsz, s, -1)
    small = _matmul(h, w_small, F32).reshape(bsz, s, -1)
    names = ("qa", "ka", "va", "za", "qb", "kc", "vc", "ks", "vs", "kw", "vw", "gm")
    sizes = (HA * DK_A, HA * DK_A, HA * DV_A, HA * DV_A,
             HB * DH_B, G_KV * DH_B, G_KV * DH_B, G_KV * DH_B, G_KV * DH_B, G_KV * DH_B, G_KV * DH_B, 2 * D_MODEL)
    col = {nm: sum(sizes[:i]) for i, nm in enumerate(names)}
    aa, ba = small[..., :HA], small[..., HA:2 * HA]

    o_a = _gated_deltanet(main, aa, ba, conv_w, a_log, dt_bias, norm_gdn, bsz, s)
    o_b = _native_sparse_attention(main, small, positions, pe_ck, w1_ck, w2_ck, pe_cv, w1_cv, w2_cv,
                                   col, 2 * HA, bsz, s)
    merged = _merge(o_a.reshape(n_tok, d), o_b.reshape(n_tok, d), w_proj_a.astype(BF16),
                    w_proj_b.astype(BF16), main.reshape(n_tok, -1), col["gm"])
    x1, hf, hpk = _outproj(merged, w_out.astype(BF16), x2, norm_ffn)
    out = _moe_ffn(hf, hpk, x1, final_gain, w_router, b_router, w_gate, b_gate, w_up, b_up, w_down, b_down)
    return out.reshape(bsz, s, d)


def kernel(x, positions, norm_mix, w_in, conv_w, a_log, dt_bias, norm_gdn, pe_ck, w1_ck, w2_ck, pe_cv, w1_cv, w2_cv, w_proj_a, w_proj_b, w_out, norm_ffn, w_router, b_router, w_gate, b_gate, w_up, b_up, w_down, b_down, norm_final):
    depth = norm_mix.shape[0]
    for l in range(depth):
        x = _layer(x, positions, norm_mix[l], w_in[l], conv_w[l], a_log[l], dt_bias[l], norm_gdn[l],
                   pe_ck[l], w1_ck[l], w2_ck[l], pe_cv[l], w1_cv[l], w2_cv[l],
                   w_proj_a[l], w_proj_b[l], w_out[l], norm_ffn[l], w_router[l], b_router[l],
                   w_gate[l], b_gate[l], w_up[l], b_up[l], w_down[l], b_down[l],
                   norm_final if l + 1 == depth else None)
    return x
```

```python
import functools
import math

import jax
import jax.numpy as jnp
import numpy as np
from jax import lax
from jax.experimental import pallas as pl
from jax.experimental.pallas import tpu as pltpu

F32 = jnp.float32
BF16 = jnp.bfloat16

D_MODEL = 2048
EPS = 1e-6
NEG = -1e30
HA = D_MODEL // 128
DK_A = 128
DV_A = 128
CONV_W = 4
CHUNK = 64
HB = D_MODEL // 128
G_KV = 4
R_GRP = HB // G_KV
DH_B = 128
ROT_DIM = DH_B // 4
ROPE_THETA = 500000.0
L_CMP = 32
STRIDE_CMP = 16
L_SLC = 64
T_SEL = 8
WINDOW = 512
N_EXP = 32
TOP_K = 4
D_FF = D_MODEL
SWIGLU_LIMIT = 7.0
SWIGLU_ALPHA = 1.702
SPLIT_SIZES = (HA * DK_A, HA * DK_A, HA * DV_A, HA * DV_A, HA, HA,
               HB * DH_B, G_KV * DH_B, G_KV * DH_B, G_KV * DH_B, G_KV * DH_B, G_KV * DH_B, G_KV * DH_B,
               3 * HB, 2 * D_MODEL)
SPLIT_POINTS = tuple(sum(SPLIT_SIZES[:i + 1]) for i in range(len(SPLIT_SIZES) - 1))

V7X_VMEM_LIMIT_BYTES = 56 * 1024 * 1024
LANES = 128
SUBLANES = 8
MOE_TM = 1024
MOE_SUB = 256
MOE_TN = 512
MOE_TN_DOWN = 1024
INPROJ_TM = 1024
ATT_TQ = 256
CMP_TQ = 512


def _cparams(sem):
    return pltpu.CompilerParams(dimension_semantics=sem, vmem_limit_bytes=V7X_VMEM_LIMIT_BYTES)


def _mm_kernel(x_ref, w_ref, o_ref):
    o_ref[...] = jnp.dot(x_ref[...], w_ref[...], preferred_element_type=F32).astype(o_ref.dtype)


def _matmul(x, w, out_dtype, tm=1024, tn=1024):
    m, k = x.shape
    n = w.shape[1]
    tm, tn = min(tm, m), min(tn, n)
    assert m % tm == 0 and n % tn == 0
    return pl.pallas_call(
        _mm_kernel,
        out_shape=jax.ShapeDtypeStruct((m, n), out_dtype),
        grid=(n // tn, m // tm),
        in_specs=[pl.BlockSpec((tm, k), lambda j, i: (i, 0)),
                  pl.BlockSpec((k, tn), lambda j, i: (0, j))],
        out_specs=pl.BlockSpec((tm, tn), lambda j, i: (i, j)),
        compiler_params=_cparams(("parallel", "parallel")),
        name="dense_matmul",
    )(x, w)


def _inproj_kernel(x_ref, wa_ref, wr_ref, o_ref, wb_ref, *, n_head_blocks):
    j = pl.program_id(0)

    @pl.when(pl.program_id(1) == 0)
    def _():
        @pl.when(j < n_head_blocks)
        def _():
            wb_ref[...] = wa_ref[...].astype(BF16)

        @pl.when(j >= n_head_blocks)
        def _():
            wb_ref[...] = wr_ref[...].astype(BF16)

    o_ref[...] = jnp.dot(x_ref[...], wb_ref[...], preferred_element_type=F32).astype(o_ref.dtype)


def _inproj(x, w_full, n_head, w_rest, tm=INPROJ_TM, tn=1024):
    m, k = x.shape
    n = n_head + w_rest.shape[1]
    assert n_head % tn == 0 and w_rest.shape[1] % tn == 0 and m % tm == 0
    nh = n_head // tn
    return pl.pallas_call(
        functools.partial(_inproj_kernel, n_head_blocks=nh),
        out_shape=jax.ShapeDtypeStruct((m, n), BF16),
        grid=(n // tn, m // tm),
        in_specs=[pl.BlockSpec((tm, k), lambda j, i: (i, 0)),
                  pl.BlockSpec((k, tn), lambda j, i: (0, jnp.minimum(j, nh - 1))),
                  pl.BlockSpec((k, tn), lambda j, i: (0, jnp.maximum(j - nh, 0)))],
        out_specs=pl.BlockSpec((tm, tn), lambda j, i: (i, j)),
        scratch_shapes=[pltpu.VMEM((k, tn), BF16)],
        compiler_params=_cparams(("arbitrary", "arbitrary")),
        name="input_projection",
    )(x, w_full, w_rest)


def _rmsnorm_kernel(x_ref, g_ref, o_ref):
    x = x_ref[...]
    y = x * lax.rsqrt(jnp.mean(x * x, axis=-1, keepdims=True) + EPS)
    o_ref[...] = (y * g_ref[...]).astype(o_ref.dtype)


def _rmsnorm(x, gain, out_dtype, tm=512):
    m, d = x.shape
    return pl.pallas_call(
        _rmsnorm_kernel,
        out_shape=jax.ShapeDtypeStruct((m, d), out_dtype),
        grid=(m // tm,),
        in_specs=[pl.BlockSpec((tm, d), lambda i: (i, 0)),
                  pl.BlockSpec((1, d), lambda i: (0, 0))],
        out_specs=pl.BlockSpec((tm, d), lambda i: (i, 0)),
        compiler_params=_cparams(("parallel",)),
        name="rmsnorm",
    )(x, gain.reshape(1, d))


def _merge_kernel(oa_ref, ob_ref, wa_ref, wb_ref, ga_ref, gb_ref, o_ref):
    ya = jnp.dot(oa_ref[...], wa_ref[...], preferred_element_type=F32)
    yb = jnp.dot(ob_ref[...], wb_ref[...], preferred_element_type=F32)
    o_ref[...] = (jax.nn.sigmoid(ga_ref[...].astype(F32)) * ya
                  + jax.nn.sigmoid(gb_ref[...].astype(F32)) * yb).astype(o_ref.dtype)


def _merge(o_a, o_b, w_a, w_b, main2d, gm_col0, tm=512, tn=1024):
    m, d = o_a.shape
    assert gm_col0 % tn == 0 and d % tn == 0
    g0 = gm_col0 // tn
    lhs = pl.BlockSpec((tm, d), lambda j, i: (i, 0))
    rhs = pl.BlockSpec((d, tn), lambda j, i: (0, j))
    return pl.pallas_call(
        _merge_kernel,
        out_shape=jax.ShapeDtypeStruct((m, d), BF16),
        grid=(d // tn, m // tm),
        in_specs=[lhs, lhs, rhs, rhs,
                  pl.BlockSpec((tm, tn), lambda j, i: (i, g0 + j)),
                  pl.BlockSpec((tm, tn), lambda j, i: (i, g0 + d // tn + j))],
        out_specs=pl.BlockSpec((tm, tn), lambda j, i: (i, j)),
        compiler_params=_cparams(("parallel", "parallel")),
        name="mixer_merge",
    )(o_a, o_b, w_a, w_b, main2d, main2d)


def _outproj_kernel(m_ref, w_ref, x_ref, g_ref, x1_ref, h_ref, hpk_ref):
    x1 = x_ref[...] + jnp.dot(m_ref[...], w_ref[...], preferred_element_type=F32)
    x1_ref[...] = x1
    y = x1 * lax.rsqrt(jnp.mean(x1 * x1, axis=-1, keepdims=True) + EPS) * g_ref[...]
    h_ref[...] = y.astype(h_ref.dtype)
    half = y.shape[1] // 2
    hpk_ref[...] = _pack_bf16_pairs(y[:, :half], y[:, half:])


def _outproj(merged, w_out, x, gain, tm=512):
    m, d = x.shape
    row = lambda width=d: pl.BlockSpec((tm, width), lambda i: (i, 0))
    return pl.pallas_call(
        _outproj_kernel,
        out_shape=(jax.ShapeDtypeStruct((m, d), F32), jax.ShapeDtypeStruct((m, d), BF16),
                   jax.ShapeDtypeStruct((m, d // 2), jnp.uint32)),
        grid=(m // tm,),
        in_specs=[row(), pl.BlockSpec((d, d), lambda i: (0, 0)), row(), pl.BlockSpec((1, d), lambda i: (0, 0))],
        out_specs=(row(), row(), row(d // 2)),
        compiler_params=_cparams(("parallel",)),
        name="out_proj_residual_norm",
    )(merged, w_out, x, gain.reshape(1, d))


GDN_COLS = 512
GDN_GROUP = 256
GDN_PH = 8
GDN_TS = 512


def _gdn_conv_kernel(x_ref, w_ref, o_ref):
    sec = pl.program_id(1) // (HA * DK_A // GDN_COLS)
    x = x_ref[0].astype(F32)
    w = w_ref[...]
    row = lax.broadcasted_iota(jnp.int32, x.shape, 0)
    y = x * w[CONV_W - 1:CONV_W]
    for i in range(CONV_W - 1):
        sh = CONV_W - 1 - i
        y = y + jnp.where(row >= sh, pltpu.roll(x, sh, axis=0), 0.0) * w[i:i + 1]
    y = y * jax.nn.sigmoid(y)
    qscale = jnp.where(sec == 0, DK_A ** -0.5, 1.0)
    for h in range(GDN_COLS // DK_A):
        yh = y[:, h * DK_A:(h + 1) * DK_A]
        inv = lax.rsqrt(jnp.sum(yh * yh, axis=-1, keepdims=True) + EPS) * qscale
        o_ref[0, 0, h] = (yh * jnp.where(sec < 2, inv, 1.0)).astype(o_ref.dtype)


def _gdn_conv(main, conv_w, bsz, s):
    ncol = 3 * HA * DK_A // GDN_COLS
    hpc = GDN_COLS // DK_A
    return pl.pallas_call(
        _gdn_conv_kernel,
        out_shape=jax.ShapeDtypeStruct((3, bsz, HA, s, DK_A), BF16),
        grid=(bsz, ncol),
        in_specs=[pl.BlockSpec((1, s, GDN_COLS), lambda b, c: (b, 0, c)),
                  pl.BlockSpec((CONV_W, GDN_COLS), lambda b, c: (0, c))],
        out_specs=pl.BlockSpec((1, 1, hpc, s, DK_A), lambda b, c: (c // (HA // hpc), b, c % (HA // hpc), 0, 0)),
        compiler_params=_cparams(("parallel", "parallel")),
        name="gdn_conv_silu_l2norm",
    )(main, conv_w)


def _col_rep(row, n):
    return jnp.broadcast_to(row, (LANES, n)).T


def _dot_hilo(x, m):
    hi = x.astype(BF16)
    lo = (x - hi.astype(F32)).astype(BF16)
    return jnp.dot(hi, m, preferred_element_type=F32) + jnp.dot(lo, m, preferred_element_type=F32)


def _gdn_prep_kernel(q_ref, k_ref, v_ref, g_ref, b_ref, u_ref, w_ref, qg_ref, kd_ref, a_ref, egl_ref):
    n = GDN_GROUP
    ri = lax.broadcasted_iota(jnp.int32, (n, n), 0)
    ci = lax.broadcasted_iota(jnp.int32, (n, n), 1)
    same = (ri // CHUNK) == (ci // CHUNK)
    incl = same & (ri >= ci)
    strict = same & (ri > ci)
    one_if = lambda m: jnp.where(m, 1.0, 0.0).astype(BF16)
    cum_m, tot_m = one_if(same & (ri <= ci)), one_if(same)
    eye = jnp.where(ri == ci, 1.0, 0.0)
    wide = lambda c: jnp.concatenate([c] * (n // LANES), axis=1)
    nt = (((1,), (1,)), ((), ()))
    heads = range(GDN_PH)
    ts, ps, rhs = [], [], []
    for h in heads:
        q, k, v = q_ref[0, 0, h], k_ref[0, 0, h], v_ref[0, 0, h]
        g8 = jnp.broadcast_to(g_ref[0, h], (SUBLANES, n))
        gc_row = _dot_hilo(g8, cum_m)[0:1]
        gl_row = _dot_hilo(g8, tot_m)[0:1]
        gc_c, gl_c, b_c = _col_rep(gc_row, n), _col_rep(gl_row, n), _col_rep(b_ref[0, h], n)
        decay = jnp.exp(jnp.where(incl, wide(gc_c) - gc_row, NEG))
        kk = lax.dot_general(k, k, nt, preferred_element_type=F32)
        qk = lax.dot_general(q, k, nt, preferred_element_type=F32)
        xb = jnp.where(strict, -(kk * wide(b_c) * decay), 0.0).astype(BF16)
        a = qk * decay
        kf = k.astype(F32)
        egc = jnp.exp(gc_c)
        qg_ref[0, h] = (q.astype(F32) * egc).astype(qg_ref.dtype)
        kd_ref[0, h] = (kf * jnp.exp(gl_c - gc_c)).astype(kd_ref.dtype)
        for c in range(n // CHUNK):
            blk = slice(c * CHUNK, (c + 1) * CHUNK)
            a_ref[0, h, blk, :] = a[blk, blk].astype(a_ref.dtype)
        egl = jnp.exp(gl_c)
        egl_ref[0, h, 0] = jnp.concatenate([egl[c * CHUNK:c * CHUNK + 1] for c in range(n // CHUNK)], axis=0)
        rhs.append(jnp.concatenate([(v.astype(F32) * b_c).astype(BF16), (kf * b_c * egc).astype(BF16)], axis=1))
        ts.append(eye + xb.astype(F32))
        ps.append(xb)
    ps = [jnp.dot(p, p, preferred_element_type=F32).astype(BF16) for p in ps]
    for step in range(5):
        for h in heads:
            if step < 4:
                tp = jnp.dot(jnp.concatenate([ts[h].astype(BF16), ps[h]], axis=0), ps[h], preferred_element_type=F32)
                ts[h] = ts[h] + tp[:n]
                ps[h] = tp[n:].astype(BF16)
            else:
                ts[h] = ts[h] + jnp.dot(ts[h].astype(BF16), ps[h], preferred_element_type=F32)
    for h in heads:
        uw = jnp.dot(ts[h].astype(BF16), rhs[h], preferred_element_type=F32)
        u_ref[0, h] = uw[:, :DV_A].astype(u_ref.dtype)
        w_ref[0, h] = uw[:, DV_A:].astype(w_ref.dtype)


def _gdn_prep(qkv, g_t, beta_t, bsz, s):
    n, ph = GDN_GROUP, GDN_PH
    tok = lambda width, dt: jax.ShapeDtypeStruct((bsz, HA, s, width), dt)
    tspec = lambda width: pl.BlockSpec((1, ph, n, width), lambda b, h, i: (b, h, i, 0))
    qspec = lambda sec: pl.BlockSpec((1, 1, ph, n, DK_A), lambda b, h, i: (sec, b, h, i, 0))
    rspec = pl.BlockSpec((1, ph, 1, n), lambda b, h, i: (b, h, 0, i))
    return pl.pallas_call(
        _gdn_prep_kernel,
        out_shape=(tok(DV_A, BF16), tok(DK_A, BF16), tok(DK_A, BF16), tok(DK_A, BF16), tok(CHUNK, BF16),
                   jax.ShapeDtypeStruct((bsz, HA, s // n, n // CHUNK, LANES), F32)),
        grid=(bsz, HA // ph, s // n),
        in_specs=[qspec(0), qspec(1), qspec(2), rspec, rspec],
        out_specs=(tspec(DV_A), tspec(DK_A), tspec(DK_A), tspec(DK_A), tspec(CHUNK),
                   pl.BlockSpec((1, ph, 1, n // CHUNK, LANES), lambda b, h, i: (b, h, i, 0, 0))),
        compiler_params=_cparams(("parallel", "parallel", "parallel")),
        name="gdn_chunk_prep",
    )(qkv, qkv, qkv, g_t, beta_t)


def _gdn_scan_kernel(u_ref, w_ref, qg_ref, kd_ref, a_ref, egl_ref, z_ref, ng_ref, o_ref, state_ref, *, nchunk):
    @pl.when(pl.program_id(1) == 0)
    def _():
        state_ref[...] = jnp.zeros_like(state_ref)

    tn = (((0,), (0,)), ((), ()))
    heads = range(HA)

    def body(c, carry):
        rows = pl.ds(pl.multiple_of(c * CHUNK, CHUNK), CHUNK)
        st = [state_ref[h] for h in heads]
        sb = [x.astype(BF16) for x in st]
        vb = [(u_ref[0, h, rows, :].astype(F32)
               - jnp.dot(w_ref[0, h, rows, :], sb[h], preferred_element_type=F32)).astype(BF16) for h in heads]
        o = [jnp.dot(qg_ref[0, h, rows, :], sb[h], preferred_element_type=F32)
             + jnp.dot(a_ref[0, h, rows, :], vb[h], preferred_element_type=F32) for h in heads]
        for h in heads:
            state_ref[h] = (st[h] * egl_ref[0, h, pl.ds(c, 1), :]
                            + lax.dot_general(kd_ref[0, h, rows, :], vb[h], tn, preferred_element_type=F32))
        for h in heads:
            cols = slice(h * DV_A, (h + 1) * DV_A)
            z = z_ref[0, rows, cols].astype(F32)
            on = o[h] * lax.rsqrt(jnp.mean(o[h] * o[h], axis=-1, keepdims=True) + EPS) * ng_ref[...]
            o_ref[0, rows, cols] = (on * (z * jax.nn.sigmoid(z))).astype(o_ref.dtype)
        return carry

    lax.fori_loop(0, nchunk, body, 0)


def _gdn_scan(u, w, qg, kd, a, egl, main, norm_gdn, bsz, s):
    ts = GDN_TS
    z_blk0 = 3 * HA * DK_A // (HA * DV_A)
    hspec = lambda width: pl.BlockSpec((1, HA, ts, width), lambda b, i: (b, 0, i, 0))
    return pl.pallas_call(
        functools.partial(_gdn_scan_kernel, nchunk=ts // CHUNK),
        out_shape=jax.ShapeDtypeStruct((bsz, s, HA * DV_A), BF16),
        grid=(bsz, s // ts),
        in_specs=[hspec(DV_A), hspec(DK_A), hspec(DK_A), hspec(DK_A), hspec(CHUNK),
                  pl.BlockSpec((1, HA, ts // CHUNK, LANES), lambda b, i: (b, 0, i, 0)),
                  pl.BlockSpec((1, ts, HA * DV_A), lambda b, i: (b, i, z_blk0)),
                  pl.BlockSpec((1, DV_A), lambda b, i: (0, 0))],
        out_specs=pl.BlockSpec((1, ts, HA * DV_A), lambda b, i: (b, i, 0)),
        scratch_shapes=[pltpu.VMEM((HA, DK_A, DV_A), F32)],
        compiler_params=_cparams(("parallel", "arbitrary")),
        name="gdn_delta_scan",
    )(u, w, qg, kd, a, egl, main, norm_gdn.reshape(1, DV_A))


def _gated_deltanet(main, a_in, b_in, conv_w, a_log, dt_bias, norm_gdn, bsz, s):
    qkv = _gdn_conv(main, conv_w, bsz, s)
    g = -jnp.exp(a_log.astype(F32)) * jax.nn.softplus(a_in.astype(F32) + dt_bias.astype(F32))
    beta = jax.nn.sigmoid(b_in.astype(F32))
    g_t = g.transpose(0, 2, 1).reshape(bsz, HA, 1, s)
    beta_t = beta.transpose(0, 2, 1).reshape(bsz, HA, 1, s)
    u, w, qg, kd, a, egl = _gdn_prep(qkv, g_t, beta_t, bsz, s)
    egl = egl.reshape(bsz, HA, s // CHUNK, LANES)
    return _gdn_scan(u, w, qg, kd, a, egl, main, norm_gdn, bsz, s)


NSA_TS = 512
BIG = 1e30
DROPPED = -3e38


def _rope(x, cos, sin, lane):
    half = ROT_DIM // 2
    partner = jnp.where(lane < half, pltpu.roll(x, DH_B - half, axis=1), pltpu.roll(x, half, axis=1))
    return x * cos + partner * sin


def _nsa_rope_kernel(q_ref, kc_ref, vc_ref, ks_ref, kw_ref, cos_ref, sin_ref,
                     qo_ref, kco_ref, vco_ref, kso_ref, kwo_ref):
    cos, sin = cos_ref[0], sin_ref[0]
    lane = lax.broadcasted_iota(jnp.int32, cos.shape, 1)
    head = lambda ref, h: ref[0, :, h * DH_B:(h + 1) * DH_B].astype(F32)
    for h in range(HB):
        qo_ref[0, :, h * DH_B:(h + 1) * DH_B] = (_rope(head(q_ref, h), cos, sin, lane) * DH_B ** -0.5).astype(qo_ref.dtype)
    for g in range(G_KV):
        cols = slice(g * DH_B, (g + 1) * DH_B)
        kco_ref[0, g] = _rope(head(kc_ref, g), cos, sin, lane).astype(kco_ref.dtype)
        vco_ref[0, g] = vc_ref[0, :, cols]
        kso_ref[0, :, cols] = _rope(head(ks_ref, g), cos, sin, lane).astype(kso_ref.dtype)
        kwo_ref[0, :, cols] = _rope(head(kw_ref, g), cos, sin, lane).astype(kwo_ref.dtype)


def _nsa_rope(main, cos, sin, col, bsz, s):
    ts = NSA_TS
    kvw = G_KV * DH_B
    tok = lambda width, c0: pl.BlockSpec((1, ts, width), lambda b, i: (b, i, c0 // width))
    tab = pl.BlockSpec((1, ts, DH_B), lambda b, i: (b, i, 0))
    grp = pl.BlockSpec((1, G_KV, ts, DH_B), lambda b, i: (b, 0, i, 0))
    flat = lambda width: pl.BlockSpec((1, ts, width), lambda b, i: (b, i, 0))
    return pl.pallas_call(
        _nsa_rope_kernel,
        out_shape=(jax.ShapeDtypeStruct((bsz, s, HB * DH_B), BF16),
                   jax.ShapeDtypeStruct((bsz, G_KV, s, DH_B), BF16), jax.ShapeDtypeStruct((bsz, G_KV, s, DH_B), BF16),
                   jax.ShapeDtypeStruct((bsz, s, kvw), BF16), jax.ShapeDtypeStruct((bsz, s, kvw), BF16)),
        grid=(bsz, s // ts),
        in_specs=[tok(HB * DH_B, col["qb"]), tok(kvw, col["kc"]), tok(kvw, col["vc"]), tok(kvw, col["ks"]),
                  tok(kvw, col["kw"]), tab, tab],
        out_specs=(flat(HB * DH_B), grp, grp, flat(kvw), flat(kvw)),
        compiler_params=_cparams(("parallel", "parallel")),
        name="nsa_rotary",
    )(main, main, main, main, main, cos, sin)


def _gelu_tanh(x):
    return 0.5 * x * (1.0 + jnp.tanh(math.sqrt(2.0 / math.pi) * (x + 0.044715 * x * x * x)))


def _nsa_compress_kernel(k_ref, v_ref, pek_ref, pev_ref, w1k_ref, w1v_ref, w2k_ref, w2v_ref, ko_ref, vo_ref):
    nseg = k_ref.shape[2]
    for x_ref, pe_ref, w1_ref, w2_ref, o_ref in ((k_ref, pek_ref, w1k_ref, w2k_ref, ko_ref),
                                                 (v_ref, pev_ref, w1v_ref, w2v_ref, vo_ref)):
        for g in range(G_KV):
            x = x_ref[0, g].astype(F32)
            lo = jnp.dot((x + pe_ref[0:1]).astype(BF16), w1_ref[0], preferred_element_type=F32)
            hi = jnp.dot((x + pe_ref[1:2]).astype(BF16), w1_ref[1], preferred_element_type=F32)
            pre = lo + pltpu.roll(hi, nseg - 1, axis=0)
            o_ref[0, g] = jnp.dot(_gelu_tanh(pre).astype(BF16), w2_ref[...],
                                  preferred_element_type=F32).astype(o_ref.dtype)


def _nsa_compress(kc_t, vc_t, pe_ck, w1_ck, w2_ck, pe_cv, w1_cv, w2_cv, bsz, s):
    assert L_CMP == 2 * STRIDE_CMP
    nseg = s // STRIDE_CMP
    width = STRIDE_CMP * DH_B
    seg = lambda t: t.reshape(bsz, G_KV, nseg, width)
    pe2 = lambda pe: pe.reshape(2, width)
    w1h = lambda w: w.reshape(2, width, DH_B).astype(BF16)
    xspec = pl.BlockSpec((1, G_KV, nseg, width), lambda b: (b, 0, 0, 0))
    pspec = pl.BlockSpec((2, width), lambda b: (0, 0))
    w1spec = pl.BlockSpec((2, width, DH_B), lambda b: (0, 0, 0))
    w2spec = pl.BlockSpec((DH_B, DH_B), lambda b: (0, 0))
    ospec = pl.BlockSpec((1, G_KV, nseg, DH_B), lambda b: (b, 0, 0, 0))
    oshape = jax.ShapeDtypeStruct((bsz, G_KV, nseg, DH_B), BF16)
    return pl.pallas_call(
        _nsa_compress_kernel,
        out_shape=(oshape, oshape),
        grid=(bsz,),
        in_specs=[xspec, xspec, pspec, pspec, w1spec, w1spec, w2spec, w2spec],
        out_specs=(ospec, ospec),
        compiler_params=_cparams(("parallel",)),
        name="nsa_compress",
    )(seg(kc_t), seg(vc_t), pe2(pe_ck), pe2(pe_cv), w1h(w1_ck), w1h(w1_cv), w2_ck.astype(BF16), w2_cv.astype(BF16))


def _nsa_cmp_kernel(q_ref, kc_ref, vc_ref, ov_ref, o_ref, mb_ref, *, tq, n_slc):
    i = pl.program_id(2)
    rows = R_GRP * tq
    ncmp = kc_ref.shape[2]
    qa = _stack_heads(q_ref[0], None)
    s = lax.dot_general(qa, kc_ref[0, 0], (((1,), (1,)), ((), ())), preferred_element_type=F32)
    t_row = (lax.broadcasted_iota(jnp.int32, (rows, ncmp), 0) & (tq - 1)) + i * tq
    c_end = lax.broadcasted_iota(jnp.int32, (rows, ncmp), 1) * STRIDE_CMP + (L_CMP - 1)
    valid = c_end <= t_row
    sm = jnp.where(valid, s, NEG)
    p = jnp.where(valid, jnp.exp(sm - jnp.max(sm, axis=1, keepdims=True)), 0.0)
    l = jnp.sum(p, axis=1, keepdims=True)
    p = p * (1.0 / jnp.where(l > 0.0, l, 1.0))
    _unstack_heads(o_ref, jnp.dot(p.astype(BF16), vc_ref[0, 0], preferred_element_type=F32), tq)
    psum = p[0:tq]
    for r in range(1, R_GRP):
        psum = psum + p[r * tq:(r + 1) * tq]
    imp = _dot_hilo(psum, ov_ref[...])
    nb = -(-n_slc // SUBLANES) * SUBLANES
    v = imp.T[:nb]
    blk = lax.broadcasted_iota(jnp.int32, (nb, tq), 0)
    cur = (lax.broadcasted_iota(jnp.int32, (nb, tq), 1) + i * tq) // L_SLC
    forced = (blk == 0) | (blk == cur)
    v = jnp.where(forced, BIG, jnp.where(blk <= cur, v, -BIG))
    sel = jnp.zeros((nb, tq), F32)
    for _ in range(T_SEL):
        m = jnp.max(v, axis=0, keepdims=True)
        first = jnp.min(jnp.where(v == m, blk, LANES), axis=0, keepdims=True)
        hit = blk == first
        sel = jnp.where(hit, 1.0, sel)
        v = jnp.where(hit, DROPPED, v)
    bias = (jnp.where(blk <= cur, sel, 0.0) - 1.0) * BIG
    bias = jnp.concatenate([bias, jnp.zeros((LANES - nb, tq), F32)], axis=0)
    mb_ref[0, 0] = bias.T.astype(mb_ref.dtype)


def _nsa_cmp_select(q, k_cmp, v_cmp, bsz, s):
    tq = CMP_TQ
    nseg = s // STRIDE_CMP
    n_slc = s // L_SLC
    assert nseg <= LANES or nseg % LANES == 0
    c_start = np.arange(nseg) * STRIDE_CMP
    j_start = np.arange(n_slc) * L_SLC
    overlap = ((c_start[:, None] < j_start[None, :] + L_SLC) & (c_start[:, None] + L_CMP > j_start[None, :]))
    overlap = jnp.asarray(np.pad(overlap.astype(np.float32), ((0, 0), (0, LANES - n_slc))), BF16)
    qspec = pl.BlockSpec((1, tq, R_GRP * DH_B), lambda b, g, i: (b, i, g))
    cspec = pl.BlockSpec((1, 1, nseg, DH_B), lambda b, g, i: (b, g, 0, 0))
    return pl.pallas_call(
        functools.partial(_nsa_cmp_kernel, tq=tq, n_slc=n_slc),
        out_shape=(jax.ShapeDtypeStruct((bsz, s, HB * DH_B), BF16), jax.ShapeDtypeStruct((bsz, G_KV, s, LANES), BF16)),
        grid=(bsz, G_KV, s // tq),
        in_specs=[qspec, cspec, cspec, pl.BlockSpec((nseg, LANES), lambda b, g, i: (0, 0))],
        out_specs=(qspec, pl.BlockSpec((1, 1, tq, LANES), lambda b, g, i: (b, g, i, 0))),
        compiler_params=_cparams(("parallel", "parallel", "parallel")),
        name="nsa_compressed_select",
    )(q, k_cmp, v_cmp, overlap)


def _stack_heads(q, extra):
    parts = []
    for r in range(R_GRP):
        qr = q[:, r * DH_B:(r + 1) * DH_B]
        parts.append(qr if extra is None else jnp.concatenate([qr, extra], axis=1))
    return jnp.concatenate(parts, axis=0)


def _unstack_heads(o_ref, o, tq):
    for r in range(R_GRP):
        o_ref[0, :, r * DH_B:(r + 1) * DH_B] = o[r * tq:(r + 1) * tq].astype(o_ref.dtype)


def _heads_t(q):
    return [q[:, r * DH_B:(r + 1) * DH_B].astype(F32).T.astype(BF16) for r in range(R_GRP)]


def _softmax_step_t(ss, vs, carry):
    m, l, acc = carry
    m_new = functools.reduce(jnp.maximum, [jnp.max(s, axis=0, keepdims=True) for s in ss], m)
    alpha = jnp.exp(m - m_new)
    ps = [jnp.exp(s - m_new) for s in ss]
    l = alpha * l + functools.reduce(lambda a, b: a + b, [jnp.sum(p, axis=0, keepdims=True) for p in ps])
    tn = (((0,), (0,)), ((), ()))
    pv = [lax.dot_general(v, p.astype(BF16), tn, preferred_element_type=F32) for v, p in zip(vs, ps)]
    return m_new, l, functools.reduce(lambda a, b: a + b, pv, alpha * acc)


def _softmax_init_t(rows):
    return (jnp.full((1, rows), -jnp.inf, F32), jnp.zeros((1, rows), F32), jnp.zeros((DH_B, rows), F32))


def _selected_branch(i, qts, mb, ks_ref, oh_ref, vs_ref, tq):
    rows = R_GRP * tq
    mbt = mb.astype(F32).T.astype(BF16)
    qat = jnp.concatenate([jnp.concatenate([qt, mbt], axis=0) for qt in qts], axis=1)

    def scores(j):
        keys = pl.ds(pl.multiple_of(j * tq, tq), tq)
        k = jnp.concatenate([ks_ref[0, keys, :], oh_ref[keys, :]], axis=1)
        return jnp.dot(k, qat, preferred_element_type=F32)

    def values(j):
        return vs_ref[0, pl.ds(pl.multiple_of(j * tq, tq), tq), :]

    def pair(jj, c):
        j = 2 * jj
        return _softmax_step_t([scores(j), scores(j + 1)], [values(j), values(j + 1)], c)

    carry = lax.fori_loop(0, i // 2, pair, _softmax_init_t(rows))
    k_loc = lax.broadcasted_iota(jnp.int32, (tq, tq), 0)
    t_loc = lax.broadcasted_iota(jnp.int32, (tq, tq), 1)
    causal = jnp.concatenate([jnp.where(k_loc <= t_loc, 0.0, NEG)] * R_GRP, axis=1)
    diag = lambda c: _softmax_step_t([scores(i) + causal], [values(i)], c)
    both = lambda c: _softmax_step_t([scores(i) + causal, scores(i - 1)], [values(i), values(i - 1)], c)
    m, l, acc = lax.cond((i % 2) == 1, both, diag, carry)
    return acc / l


def _window_branch(i, qts, kw_refs, vw_refs, tq):
    rows = R_GRP * tq
    qt = jnp.concatenate(qts, axis=1)
    k_loc = lax.broadcasted_iota(jnp.int32, (tq, tq), 0)
    t_loc = lax.broadcasted_iota(jnp.int32, (tq, tq), 1)
    bound = jnp.minimum(t_loc[0:1, :] + (i * tq + 1), WINDOW)
    ss = []
    for n in range(3):
        dist = t_loc + (2 - n) * tq - k_loc
        bias = jnp.where(dist.astype(jnp.uint32) < bound.astype(jnp.uint32), 0.0, NEG)
        ss.append(jnp.dot(kw_refs[n][0], qt, preferred_element_type=F32) + jnp.concatenate([bias] * R_GRP, axis=1))
    m, l, acc = _softmax_step_t(ss, [r[0] for r in vw_refs], _softmax_init_t(rows))
    return acc / l


def _nsa_local_kernel(q_ref, mb_ref, ks_ref, oh_ref, vs_ref, kw0, kw1, kw2, vw0, vw1, vw2, oc_ref, gate_ref, o_ref,
                      *, tq, gate_lane0):
    g, i = pl.program_id(1), pl.program_id(2)
    q = q_ref[0]
    qts = _heads_t(q)
    o_slc = _selected_branch(i, qts, mb_ref[0, 0], ks_ref, oh_ref, vs_ref, tq)
    o_win = _window_branch(i, qts, (kw0, kw1, kw2), (vw0, vw1, vw2), tq)
    gates = jax.nn.sigmoid(gate_ref[0])
    lane = lax.broadcasted_iota(jnp.int32, gates.shape, 1)
    pick = lambda idx: jnp.sum(jnp.where(lane == idx, gates, 0.0), axis=1, keepdims=True)
    for r in range(R_GRP):
        base = gate_lane0 + (g * R_GRP + r) * 3
        rows = slice(r * tq, (r + 1) * tq)
        cols = slice(r * DH_B, (r + 1) * DH_B)
        o = (pick(base) * oc_ref[0, :, cols].astype(F32) + pick(base + 1) * o_slc[:, rows].T
             + pick(base + 2) * o_win[:, rows].T)
        o_ref[0, :, cols] = o.astype(o_ref.dtype)


def _nsa_local(q, maskbias, ks, main, kw, o_cmp, small, col, gate_lane0, bsz, s):
    tq = ATT_TQ
    assert 2 * tq >= WINDOW
    onehot = jnp.asarray(np.arange(s)[:, None] // L_SLC == np.arange(LANES)[None, :], BF16)
    qspec = pl.BlockSpec((1, tq, R_GRP * DH_B), lambda b, g, i: (b, i, g))
    seq = lambda c0: pl.BlockSpec((1, s, DH_B), lambda b, g, i: (b, 0, c0 // DH_B + g))
    back = lambda c0, n: pl.BlockSpec((1, tq, DH_B), lambda b, g, i: (b, jnp.maximum(i - n, 0), c0 // DH_B + g))
    return pl.pallas_call(
        functools.partial(_nsa_local_kernel, tq=tq, gate_lane0=gate_lane0),
        out_shape=jax.ShapeDtypeStruct((bsz, s, HB * DH_B), BF16),
        grid=(bsz, G_KV, s // tq),
        in_specs=[qspec,
                  pl.BlockSpec((1, 1, tq, LANES), lambda b, g, i: (b, g, i, 0)),
                  seq(0), pl.BlockSpec((s, LANES), lambda b, g, i: (0, 0)), seq(col["vs"]),
                  back(0, 2), back(0, 1), back(0, 0),
                  back(col["vw"], 2), back(col["vw"], 1), back(col["vw"], 0),
                  qspec, pl.BlockSpec((1, tq, LANES), lambda b, g, i: (b, i, 0))],
        out_specs=qspec,
        compiler_params=_cparams(("parallel", "parallel", "arbitrary")),
        name="nsa_selected_window_combine",
    )(q, maskbias, ks, onehot, main, kw, kw, kw, main, main, main, o_cmp, small)


def _moe_sub_blocks(nv_ref, out_ref, compute):
    nsub = (nv_ref[pl.program_id(1)] + MOE_SUB - 1) // MOE_SUB
    rows_of = lambda sb: pl.ds(pl.multiple_of(sb * MOE_SUB, MOE_SUB), MOE_SUB)

    def run(first, count):
        rows = [rows_of(first + u) for u in range(count)]
        for r, val in zip(rows, compute(rows)):
            out_ref[r, :] = val

    def quad(p, carry):
        run(4 * p, 4)
        return carry

    def dead(sb, carry):
        out_ref[rows_of(sb), :] = jnp.zeros((MOE_SUB, out_ref.shape[1]), out_ref.dtype)
        return carry

    lax.fori_loop(0, nsub // 4, quad, 0)
    done = (nsub // 4) * 4

    @pl.when((nsub & 2) != 0)
    def _():
        run(done, 2)

    @pl.when((nsub & 1) != 0)
    def _():
        run(done + (nsub & 2), 1)

    lax.fori_loop(nsub, MOE_TM // MOE_SUB, dead, 0)


def _moe_new_expert(be_ref):
    i = pl.program_id(1)
    return (i == 0) | (be_ref[i] != be_ref[jnp.maximum(i - 1, 0)])


def _moe_up_kernel(be_ref, nv_ref, st_ref, x_ref, wg_ref, wu_ref, bg_ref, bu_ref, h_ref, wgb_ref, wub_ref):
    @pl.when(_moe_new_expert(be_ref))
    def _():
        wgb_ref[...] = wg_ref[0].astype(BF16)
        wub_ref[...] = wu_ref[0].astype(BF16)

    def compute(rows):
        xs = [jnp.concatenate(_unpack_bf16_pairs(x_ref[r, :]), axis=1).astype(BF16) for r in rows]
        gates = [jnp.dot(x, wgb_ref[...], preferred_element_type=F32) + bg_ref[0] for x in xs]
        ups = [jnp.dot(x, wub_ref[...], preferred_element_type=F32) + bu_ref[0] for x in xs]
        outs = []
        for gate, up in zip(gates, ups):
            gate = jnp.minimum(gate, SWIGLU_LIMIT)
            up = jnp.clip(up, -SWIGLU_LIMIT, SWIGLU_LIMIT)
            outs.append(((up + 1.0) * gate * jax.nn.sigmoid(SWIGLU_ALPHA * gate)).astype(h_ref.dtype))
        return outs

    _moe_sub_blocks(nv_ref, h_ref, compute)


def _moe_down_kernel(be_ref, nv_ref, st_ref, h_ref, wd_ref, bd_ref, y_ref, wdb_ref):
    @pl.when(_moe_new_expert(be_ref))
    def _():
        wdb_ref[...] = wd_ref[0].astype(BF16)

    half = wdb_ref.shape[1] // 2

    def compute(rows):
        ys = [jnp.dot(h_ref[r, :], wdb_ref[...], preferred_element_type=F32) + bd_ref[0] for r in rows]
        return [_pack_bf16_pairs(y[:, :half], y[:, half:]) for y in ys]

    _moe_sub_blocks(nv_ref, y_ref, compute)


def _moe_experts(rows, block_e, n_valid, src_tile, w_gate, b_gate, w_up, b_up, w_down, b_down):
    n_rows, d = rows.shape[0], 2 * rows.shape[1]
    n_blocks = n_rows // MOE_TM
    tile = lambda width: pl.BlockSpec((MOE_TM, width), lambda n, i, be, nv, st: (st[i], 0))
    wcol = lambda k: pl.BlockSpec((1, k, tn), lambda n, i, be, nv, st: (be[i], 0, n))
    bcol = lambda: pl.BlockSpec((1, 1, tn), lambda n, i, be, nv, st: (be[i], 0, n))
    params = _cparams(("arbitrary", "arbitrary"))
    tn = MOE_TN
    h = pl.pallas_call(
        _moe_up_kernel,
        out_shape=jax.ShapeDtypeStruct((n_rows, D_FF), BF16),
        grid_spec=pltpu.PrefetchScalarGridSpec(
            num_scalar_prefetch=3, grid=(D_FF // tn, n_blocks),
            in_specs=[tile(d // 2), wcol(d), wcol(d), bcol(), bcol()],
            out_specs=pl.BlockSpec((MOE_TM, tn), lambda n, i, be, nv, st: (i, n)),
            scratch_shapes=[pltpu.VMEM((d, tn), BF16), pltpu.VMEM((d, tn), BF16)]),
        compiler_params=params, name="moe_expert_up",
    )(block_e, n_valid, src_tile, rows, w_gate, w_up, b_gate.reshape(N_EXP, 1, D_FF), b_up.reshape(N_EXP, 1, D_FF))
    tn = MOE_TN_DOWN
    return pl.pallas_call(
        _moe_down_kernel,
        out_shape=jax.ShapeDtypeStruct((n_rows, d // 2), jnp.uint32),
        grid_spec=pltpu.PrefetchScalarGridSpec(
            num_scalar_prefetch=3, grid=(d // tn, n_blocks),
            in_specs=[tile(D_FF), wcol(D_FF), bcol()],
            out_specs=pl.BlockSpec((MOE_TM, tn // 2), lambda n, i, be, nv, st: (i, n)),
            scratch_shapes=[pltpu.VMEM((D_FF, tn), BF16)]),
        compiler_params=params, name="moe_expert_down",
    )(block_e, n_valid, src_tile, h, w_down, b_down.reshape(N_EXP, 1, d))


ROUTE_TT = 512
MOVE_TT = 256


def _pack_bf16_pairs(lo, hi):
    as_bits = lambda v: pltpu.bitcast(v.astype(BF16).astype(F32), jnp.uint32)
    return (as_bits(lo) >> 16) | (as_bits(hi) & jnp.uint32(0xFFFF0000))


def _unpack_bf16_pairs(w):
    return pltpu.bitcast(w << 16, F32), pltpu.bitcast(w & jnp.uint32(0xFFFF0000), F32)


def _route_kernel(h_ref, wr_ref, br_ref, e_ref, w_ref, p_ref, cnt_ref, run_ref, *, tt):
    @pl.when(pl.program_id(0) == 0)
    def _():
        run_ref[...] = jnp.zeros_like(run_ref)

    h = h_ref[...]
    logits = (jnp.dot(h, wr_ref[0], preferred_element_type=F32) + jnp.dot(h, wr_ref[1], preferred_element_type=F32)
              + br_ref[...])
    lane = lax.broadcasted_iota(jnp.int32, (tt, LANES), 1)
    v = jnp.where(lane < N_EXP, logits, -BIG)
    tops, hits, firsts = [], [], []
    for _ in range(TOP_K):
        m = jnp.max(v, axis=1, keepdims=True)
        first = jnp.min(jnp.where(v == m, lane, LANES), axis=1, keepdims=True)
        hit = lane == first
        v = jnp.where(hit, DROPPED, v)
        tops.append(m), hits.append(hit), firsts.append(first)
    ex = [jnp.exp(m - tops[0]) for m in tops]
    inv = 1.0 / functools.reduce(lambda a, b: a + b, ex)
    onehot = functools.reduce(lambda a, b: a + b, [jnp.where(hh, 1.0, 0.0) for hh in hits]).astype(BF16)
    ri = lax.broadcasted_iota(jnp.int32, (tt, tt), 0)
    ci = lax.broadcasted_iota(jnp.int32, (tt, tt), 1)
    before = jnp.where(ci < ri, 1.0, 0.0).astype(BF16)
    rank = jnp.dot(before, onehot, preferred_element_type=F32) + run_ref[0:1]
    run_ref[...] = run_ref[...] + jnp.dot(jnp.ones((SUBLANES, tt), BF16), onehot, preferred_element_type=F32)
    cnt_ref[...] = run_ref[...].astype(jnp.int32)
    place = lambda cols, zero: functools.reduce(
        lambda acc, kc: jnp.where(lane == kc[0], kc[1], acc), list(enumerate(cols)), zero)
    e_ref[...] = place(firsts, jnp.zeros((tt, LANES), jnp.int32))
    w_ref[...] = place([e * inv for e in ex], jnp.zeros((tt, LANES), F32))
    pos = [jnp.sum(jnp.where(hh, rank, 0.0), axis=1, keepdims=True).astype(jnp.int32) for hh in hits]
    p_ref[...] = place(pos, jnp.zeros((tt, LANES), jnp.int32))


def _route(h, w_router, b_router):
    n, d = h.shape
    tt = ROUTE_TT
    wr = jnp.pad(w_router, ((0, 0), (0, LANES - N_EXP)))
    hi = wr.astype(BF16)
    wr2 = jnp.stack([hi, (wr - hi.astype(F32)).astype(BF16)])
    br = jnp.pad(b_router, (0, LANES - N_EXP)).reshape(1, LANES)
    tok = lambda dt: jax.ShapeDtypeStruct((n, LANES), dt)
    tspec = pl.BlockSpec((tt, LANES), lambda i: (i, 0))
    return pl.pallas_call(
        functools.partial(_route_kernel, tt=tt),
        out_shape=(tok(jnp.int32), tok(F32), tok(jnp.int32), jax.ShapeDtypeStruct((SUBLANES, LANES), jnp.int32)),
        grid=(n // tt,),
        in_specs=[pl.BlockSpec((tt, d), lambda i: (i, 0)), pl.BlockSpec((2, d, LANES), lambda i: (0, 0, 0)),
                  pl.BlockSpec((1, LANES), lambda i: (0, 0))],
        out_specs=(tspec, tspec, tspec, pl.BlockSpec((SUBLANES, LANES), lambda i: (0, 0))),
        scratch_shapes=[pltpu.VMEM((SUBLANES, LANES), F32)],
        compiler_params=_cparams(("arbitrary",)),
        name="moe_route",
    )(h, wr2, br)


def _dispatch_kernel(dest_ref, src_ref, init_ref, rows_ref, sem, *, tt):
    del init_ref
    t0 = pl.program_id(0) * tt

    def copy(j, k):
        return pltpu.make_async_copy(src_ref.at[pl.ds(j, 1)],
                                     rows_ref.at[pl.ds(dest_ref[(t0 + j) * TOP_K + k], 1)], sem)

    def start(j, carry):
        for k in range(TOP_K):
            copy(j, k).start(priority=k % 2)
        return carry

    def wait(j, carry):
        for k in range(TOP_K):
            copy(j, k).wait()
        return carry

    lax.fori_loop(0, tt, start, 0)
    lax.fori_loop(0, tt, wait, 0)


def _dispatch(dest, hpk, n_rows):
    n, width = hpk.shape
    tt = MOVE_TT
    return pl.pallas_call(
        functools.partial(_dispatch_kernel, tt=tt),
        out_shape=jax.ShapeDtypeStruct((n_rows, width), jnp.uint32),
        grid_spec=pltpu.PrefetchScalarGridSpec(
            num_scalar_prefetch=1, grid=(n // tt,),
            in_specs=[pl.BlockSpec((tt, width), lambda i, dest: (i, 0)), pl.BlockSpec(memory_space=pl.ANY)],
            out_specs=pl.BlockSpec(memory_space=pl.ANY),
            scratch_shapes=[pltpu.SemaphoreType.DMA]),
        input_output_aliases={2: 0},
        compiler_params=_cparams(("arbitrary",)),
        name="moe_dispatch",
    )(dest, hpk, jnp.zeros((n_rows, width), jnp.uint32))


def _combine_kernel(dest_ref, y_ref, w_ref, x1_ref, g_ref, o_ref, buf_ref, sem, *, tt, norm):
    step = pl.program_id(0)
    slot = step % 2

    def copy(tile, sl, j, k):
        return pltpu.make_async_copy(y_ref.at[pl.ds(dest_ref[(tile * tt + j) * TOP_K + k], 1)],
                                     buf_ref.at[sl, k, pl.ds(j, 1)], sem.at[sl])

    def gather(tile, sl):
        def start(j, carry):
            for k in range(TOP_K):
                copy(tile, sl, j, k).start(priority=k % 2)
            return carry
        lax.fori_loop(0, tt, start, 0)

    @pl.when(step == 0)
    def _():
        gather(step, slot)

    @pl.when(step + 1 < pl.num_programs(0))
    def _():
        gather(step + 1, 1 - slot)

    def wait(j, carry):
        for k in range(TOP_K):
            copy(step, slot, j, k).wait()
        return carry

    lax.fori_loop(0, tt, wait, 0)
    wts = w_ref[...]
    lane = lax.broadcasted_iota(jnp.int32, wts.shape, 1)
    x = x1_ref[...]
    nq = MOE_TN_DOWN // 2
    for k in range(TOP_K):
        wk = jnp.sum(jnp.where(lane == k, wts, 0.0), axis=1, keepdims=True)
        parts = []
        for n in range(D_MODEL // MOE_TN_DOWN):
            lo, hi = _unpack_bf16_pairs(buf_ref[slot, k, :, n * nq:(n + 1) * nq])
            parts += [lo, hi]
        x = x + wk * jnp.concatenate(parts, axis=1)
    if norm:
        x = x * lax.rsqrt(jnp.mean(x * x, axis=-1, keepdims=True) + EPS) * g_ref[...]
    o_ref[...] = x.astype(o_ref.dtype)


def _combine(dest, ypk, wts, x1, gain):
    n, d = x1.shape
    tt = MOVE_TT
    norm = gain is not None
    gain = gain if norm else jnp.ones((d,), F32)
    return pl.pallas_call(
        functools.partial(_combine_kernel, tt=tt, norm=norm),
        out_shape=jax.ShapeDtypeStruct((n, d), F32),
        grid_spec=pltpu.PrefetchScalarGridSpec(
            num_scalar_prefetch=1, grid=(n // tt,),
            in_specs=[pl.BlockSpec(memory_space=pl.ANY),
                      pl.BlockSpec((tt, LANES), lambda i, dest: (i, 0)),
                      pl.BlockSpec((tt, d), lambda i, dest: (i, 0)),
                      pl.BlockSpec((1, d), lambda i, dest: (0, 0))],
            out_specs=pl.BlockSpec((tt, d), lambda i, dest: (i, 0)),
            scratch_shapes=[pltpu.VMEM((2, TOP_K, tt, d // 2), jnp.uint32), pltpu.SemaphoreType.DMA((2,))]),
        compiler_params=_cparams(("arbitrary",)),
        name="moe_combine_norm",
    )(dest, ypk, wts, x1, gain.reshape(1, d))


def _native_sparse_attention(main, small, positions, pe_ck, w1_ck, w2_ck, pe_cv, w1_cv, w2_cv, col, gate_lane0, bsz, s):
    half = ROT_DIM // 2
    inv_freq = ROPE_THETA ** (-jnp.arange(half, dtype=F32) * 2.0 / ROT_DIM)
    ang = positions.astype(F32)[..., None] * inv_freq
    cos, sin = jnp.cos(ang), jnp.sin(ang)
    rest = (bsz, s, DH_B - ROT_DIM)
    cos_t = jnp.concatenate([cos, cos, jnp.ones(rest, F32)], axis=-1)
    sin_t = jnp.concatenate([-sin, sin, jnp.zeros(rest, F32)], axis=-1)
    q, kc_t, vc_t, ks, kw = _nsa_rope(main, cos_t, sin_t, col, bsz, s)
    k_cmp, v_cmp = _nsa_compress(kc_t, vc_t, pe_ck, w1_ck, w2_ck, pe_cv, w1_cv, w2_cv, bsz, s)
    o_cmp, maskbias = _nsa_cmp_select(q, k_cmp, v_cmp, bsz, s)
    return _nsa_local(q, maskbias, ks, main, kw, o_cmp, small, col, gate_lane0, bsz, s)


def _moe_ffn(hf, hpk, x1, final_gain, w_router, b_router, w_gate, b_gate, w_up, b_up, w_down, b_down):
    n_tok, d = x1.shape
    eidx, wts, pos, cnt = _route(hf, w_router, b_router)
    counts = cnt[0, :N_EXP]
    padded = ((counts + MOE_TM - 1) // MOE_TM) * MOE_TM
    pad_end = jnp.cumsum(padded)
    pad_start = pad_end - padded
    part = counts % MOE_TM
    gap = jnp.where(part > 0, MOE_TM - part, 0)
    pick = lambda table: jnp.sum(jnp.where(eidx[:, :TOP_K, None] == jnp.arange(N_EXP), table, 0), axis=-1)
    rank = pos[:, :TOP_K]
    dest = pick(pad_start) + rank + jnp.where(rank >= pick(part), pick(gap), 0)
    dest = dest.reshape(-1).astype(jnp.int32)
    n_blocks = (n_tok * TOP_K + N_EXP * (MOE_TM - 1) + MOE_TM - 1) // MOE_TM
    rows = _dispatch(dest, hpk, n_blocks * MOE_TM)
    tile0 = jnp.arange(n_blocks, dtype=jnp.int32) * MOE_TM
    block_e = jnp.minimum(jnp.searchsorted(pad_end, tile0, side='right'), N_EXP - 1).astype(jnp.int32)
    first = (tile0 == pad_start[block_e]) & (part[block_e] > 0)
    n_valid = jnp.where(tile0 < pad_end[-1], jnp.where(first, part[block_e], MOE_TM), 0).astype(jnp.int32)
    tile_id = jnp.arange(n_blocks, dtype=jnp.int32)
    src_tile = jnp.where(n_valid > 0, tile_id, jnp.maximum(pad_end[-1] // MOE_TM - 1, 0)).astype(jnp.int32)
    ypk = _moe_experts(rows, block_e, n_valid, src_tile, w_gate, b_gate, w_up, b_up, w_down, b_down)
    return _combine(dest, ypk, wts, x1, final_gain)


def _layer(x, positions, norm_mix, w_in, conv_w, a_log, dt_bias, norm_gdn, pe_ck, w1_ck, w2_ck,
           pe_cv, w1_cv, w2_cv, w_proj_a, w_proj_b, w_out, norm_ffn, w_router, b_router,
           w_gate, b_gate, w_up, b_up, w_down, b_down, final_gain):
    bsz, s, d = x.shape
    n_tok = bsz * s
    x2 = x.reshape(n_tok, d)
    h = _rmsnorm(x2, norm_mix, BF16)
    n_small = 2 * HA + 3 * HB
    sp = SPLIT_POINTS
    w_rest = jnp.concatenate([w_in[:, sp[5]:sp[12]], w_in[:, sp[13]:]], axis=1)
    w_small = jnp.concatenate([w_in[:, sp[3]:sp[5]], w_in[:, sp[12]:sp[13]]], axis=1)
    w_small = jnp.pad(w_small, ((0, 0), (0, LANES - n_small))).astype(BF16)
    main = _inproj(h, w_in, sp[3], w_rest).reshape(bsz, s, -1)
    small = _matmul(h, w_small, F32).reshape(bsz, s, -1)
    names = ("qa", "ka", "va", "za", "qb", "kc", "vc", "ks", "vs", "kw", "vw", "gm")
    sizes = (HA * DK_A, HA * DK_A, HA * DV_A, HA * DV_A,
             HB * DH_B, G_KV * DH_B, G_KV * DH_B, G_KV * DH_B, G_KV * DH_B, G_KV * DH_B, G_KV * DH_B, 2 * D_MODEL)
    col = {nm: sum(sizes[:i]) for i, nm in enumerate(names)}
    aa, ba = small[..., :HA], small[..., HA:2 * HA]

    o_a = _gated_deltanet(main, aa, ba, conv_w, a_log, dt_bias, norm_gdn, bsz, s)
    o_b = _native_sparse_attention(main, small, positions, pe_ck, w1_ck, w2_ck, pe_cv, w1_cv, w2_cv,
                                   col, 2 * HA, bsz, s)
    merged = _merge(o_a.reshape(n_tok, d), o_b.reshape(n_tok, d), w_proj_a.astype(BF16),
                    w_proj_b.astype(BF16), main.reshape(n_tok, -1), col["gm"])
    x1, hf, hpk = _outproj(merged, w_out.astype(BF16), x2, norm_ffn)
    out = _moe_ffn(hf, hpk, x1, final_gain, w_router, b_router, w_gate, b_gate, w_up, b_up, w_down, b_down)
    return out.reshape(bsz, s, d)


def kernel(x, positions, norm_mix, w_in, conv_w, a_log, dt_bias, norm_gdn, pe_ck, w1_ck, w2_ck, pe_cv, w1_cv, w2_cv, w_proj_a, w_proj_b, w_out, norm_ffn, w_router, b_router, w_gate, b_gate, w_up, b_up, w_down, b_down, norm_final):
    depth = norm_mix.shape[0]
    for l in range(depth):
        x = _layer(x, positions, norm_mix[l], w_in[l], conv_w[l], a_log[l], dt_bias[l], norm_gdn[l],
                   pe_ck[l], w1_ck[l], w2_ck[l], pe_cv[l], w1_cv[l], w2_cv[l],
                   w_proj_a[l], w_proj_b[l], w_out[l], norm_ffn[l], w_router[l], b_router[l],
                   w_gate[l], b_gate[l], w_up[l], b_up[l], w_down[l], b_down[l],
                   norm_final if l + 1 == depth else None)
    return x
```

```python
import functools
import math

import jax
import jax.numpy as jnp
import numpy as np
from jax import lax
from jax.experimental import pallas as pl
from jax.experimental.pallas import tpu as pltpu

F32 = jnp.float32
BF16 = jnp.bfloat16

D_MODEL = 2048
EPS = 1e-6
NEG = -1e30
HA = D_MODEL // 128
DK_A = 128
DV_A = 128
CONV_W = 4
CHUNK = 64
HB = D_MODEL // 128
G_KV = 4
R_GRP = HB // G_KV
DH_B = 128
ROT_DIM = DH_B // 4
ROPE_THETA = 500000.0
L_CMP = 32
STRIDE_CMP = 16
L_SLC = 64
T_SEL = 8
WINDOW = 512
N_EXP = 32
TOP_K = 4
D_FF = D_MODEL
SWIGLU_LIMIT = 7.0
SWIGLU_ALPHA = 1.702
SPLIT_SIZES = (HA * DK_A, HA * DK_A, HA * DV_A, HA * DV_A, HA, HA,
               HB * DH_B, G_KV * DH_B, G_KV * DH_B, G_KV * DH_B, G_KV * DH_B, G_KV * DH_B, G_KV * DH_B,
               3 * HB, 2 * D_MODEL)
SPLIT_POINTS = tuple(sum(SPLIT_SIZES[:i + 1]) for i in range(len(SPLIT_SIZES) - 1))

V7X_VMEM_LIMIT_BYTES = 56 * 1024 * 1024
LANES = 128
SUBLANES = 8
MOE_TM = 1024
MOE_SUB = 256
MOE_TN = 512
MOE_TN_DOWN = 1024
INPROJ_TM = 2048
ATT_TQ = 256
CMP_TQ = 512


def _cparams(sem):
    return pltpu.CompilerParams(dimension_semantics=sem, vmem_limit_bytes=V7X_VMEM_LIMIT_BYTES)


def _mm_kernel(x_ref, w_ref, o_ref):
    o_ref[...] = jnp.dot(x_ref[...], w_ref[...], preferred_element_type=F32).astype(o_ref.dtype)


def _matmul(x, w, out_dtype, tm=1024, tn=1024):
    m, k = x.shape
    n = w.shape[1]
    tm, tn = min(tm, m), min(tn, n)
    assert m % tm == 0 and n % tn == 0
    return pl.pallas_call(
        _mm_kernel,
        out_shape=jax.ShapeDtypeStruct((m, n), out_dtype),
        grid=(n // tn, m // tm),
        in_specs=[pl.BlockSpec((tm, k), lambda j, i: (i, 0)),
                  pl.BlockSpec((k, tn), lambda j, i: (0, j))],
        out_specs=pl.BlockSpec((tm, tn), lambda j, i: (i, j)),
        compiler_params=_cparams(("parallel", "parallel")),
        name="dense_matmul",
    )(x, w)


def _rmsnorm_kernel(x_ref, g_ref, o_ref):
    x = x_ref[...]
    y = x * lax.rsqrt(jnp.mean(x * x, axis=-1, keepdims=True) + EPS)
    o_ref[...] = (y * g_ref[...]).astype(o_ref.dtype)


def _rmsnorm(x, gain, out_dtype, tm=512):
    m, d = x.shape
    return pl.pallas_call(
        _rmsnorm_kernel,
        out_shape=jax.ShapeDtypeStruct((m, d), out_dtype),
        grid=(m // tm,),
        in_specs=[pl.BlockSpec((tm, d), lambda i: (i, 0)),
                  pl.BlockSpec((1, d), lambda i: (0, 0))],
        out_specs=pl.BlockSpec((tm, d), lambda i: (i, 0)),
        compiler_params=_cparams(("parallel",)),
        name="rmsnorm",
    )(x, gain.reshape(1, d))


def _merge_kernel(oa_ref, ob_ref, wa_ref, wb_ref, ga_ref, gb_ref, o_ref):
    ya = jnp.dot(oa_ref[...], wa_ref[...], preferred_element_type=F32)
    yb = jnp.dot(ob_ref[...], wb_ref[...], preferred_element_type=F32)
    o_ref[...] = (jax.nn.sigmoid(ga_ref[...].astype(F32)) * ya
                  + jax.nn.sigmoid(gb_ref[...].astype(F32)) * yb).astype(o_ref.dtype)


def _merge(o_a, o_b, w_a, w_b, main2d, gm_col0, tm=512, tn=1024):
    m, d = o_a.shape
    assert gm_col0 % tn == 0 and d % tn == 0
    g0 = gm_col0 // tn
    lhs = pl.BlockSpec((tm, d), lambda j, i: (i, 0))
    rhs = pl.BlockSpec((d, tn), lambda j, i: (0, j))
    return pl.pallas_call(
        _merge_kernel,
        out_shape=jax.ShapeDtypeStruct((m, d), BF16),
        grid=(d // tn, m // tm),
        in_specs=[lhs, lhs, rhs, rhs,
                  pl.BlockSpec((tm, tn), lambda j, i: (i, g0 + j)),
                  pl.BlockSpec((tm, tn), lambda j, i: (i, g0 + d // tn + j))],
        out_specs=pl.BlockSpec((tm, tn), lambda j, i: (i, j)),
        compiler_params=_cparams(("parallel", "parallel")),
        name="mixer_merge",
    )(o_a, o_b, w_a, w_b, main2d, main2d)


def _outproj_kernel(m_ref, w_ref, x_ref, g_ref, x1_ref, h_ref, hpk_ref):
    x1 = x_ref[...] + jnp.dot(m_ref[...], w_ref[...], preferred_element_type=F32)
    x1_ref[...] = x1
    y = x1 * lax.rsqrt(jnp.mean(x1 * x1, axis=-1, keepdims=True) + EPS) * g_ref[...]
    h_ref[...] = y.astype(h_ref.dtype)
    half = y.shape[1] // 2
    hpk_ref[...] = _pack_bf16_pairs(y[:, :half], y[:, half:])


def _outproj(merged, w_out, x, gain, tm=512):
    m, d = x.shape
    row = lambda width=d: pl.BlockSpec((tm, width), lambda i: (i, 0))
    return pl.pallas_call(
        _outproj_kernel,
        out_shape=(jax.ShapeDtypeStruct((m, d), F32), jax.ShapeDtypeStruct((m, d), BF16),
                   jax.ShapeDtypeStruct((m, d // 2), jnp.uint32)),
        grid=(m // tm,),
        in_specs=[row(), pl.BlockSpec((d, d), lambda i: (0, 0)), row(), pl.BlockSpec((1, d), lambda i: (0, 0))],
        out_specs=(row(), row(), row(d // 2)),
        compiler_params=_cparams(("parallel",)),
        name="out_proj_residual_norm",
    )(merged, w_out, x, gain.reshape(1, d))


GDN_COLS = 512
GDN_GROUP = 256
GDN_PH = 8
GDN_TS = 512


def _gdn_conv_kernel(x_ref, w_ref, o_ref):
    sec = pl.program_id(1) // (HA * DK_A // GDN_COLS)
    x = x_ref[0].astype(F32)
    w = w_ref[...]
    row = lax.broadcasted_iota(jnp.int32, x.shape, 0)
    y = x * w[CONV_W - 1:CONV_W]
    for i in range(CONV_W - 1):
        sh = CONV_W - 1 - i
        y = y + jnp.where(row >= sh, pltpu.roll(x, sh, axis=0), 0.0) * w[i:i + 1]
    y = y * jax.nn.sigmoid(y)
    qscale = jnp.where(sec == 0, DK_A ** -0.5, 1.0)
    for h in range(GDN_COLS // DK_A):
        yh = y[:, h * DK_A:(h + 1) * DK_A]
        inv = lax.rsqrt(jnp.sum(yh * yh, axis=-1, keepdims=True) + EPS) * qscale
        o_ref[0, 0, h] = (yh * jnp.where(sec < 2, inv, 1.0)).astype(o_ref.dtype)


def _gdn_conv(main, conv_w, bsz, s):
    ncol = 3 * HA * DK_A // GDN_COLS
    hpc = GDN_COLS // DK_A
    return pl.pallas_call(
        _gdn_conv_kernel,
        out_shape=jax.ShapeDtypeStruct((3, bsz, HA, s, DK_A), BF16),
        grid=(bsz, ncol),
        in_specs=[pl.BlockSpec((1, s, GDN_COLS), lambda b, c: (b, 0, c)),
                  pl.BlockSpec((CONV_W, GDN_COLS), lambda b, c: (0, c))],
        out_specs=pl.BlockSpec((1, 1, hpc, s, DK_A), lambda b, c: (c // (HA // hpc), b, c % (HA // hpc), 0, 0)),
        compiler_params=_cparams(("parallel", "parallel")),
        name="gdn_conv_silu_l2norm",
    )(main, conv_w)


def _col_rep(row, n):
    return jnp.broadcast_to(row, (LANES, n)).T


def _dot_hilo(x, m):
    hi = x.astype(BF16)
    lo = (x - hi.astype(F32)).astype(BF16)
    return jnp.dot(hi, m, preferred_element_type=F32) + jnp.dot(lo, m, preferred_element_type=F32)


def _gdn_prep_kernel(q_ref, k_ref, v_ref, g_ref, b_ref, u_ref, w_ref, qg_ref, kd_ref, a_ref, egl_ref):
    n = GDN_GROUP
    ri = lax.broadcasted_iota(jnp.int32, (n, n), 0)
    ci = lax.broadcasted_iota(jnp.int32, (n, n), 1)
    same = (ri // CHUNK) == (ci // CHUNK)
    incl = same & (ri >= ci)
    strict = same & (ri > ci)
    one_if = lambda m: jnp.where(m, 1.0, 0.0).astype(BF16)
    cum_m, tot_m = one_if(same & (ri <= ci)), one_if(same)
    eye = jnp.where(ri == ci, 1.0, 0.0)
    wide = lambda c: jnp.concatenate([c] * (n // LANES), axis=1)
    nt = (((1,), (1,)), ((), ()))
    heads = range(GDN_PH)
    ts, ps, rhs = [], [], []
    for h in heads:
        q, k, v = q_ref[0, 0, h], k_ref[0, 0, h], v_ref[0, 0, h]
        g8 = jnp.broadcast_to(g_ref[0, h], (SUBLANES, n))
        gc_row = _dot_hilo(g8, cum_m)[0:1]
        gl_row = _dot_hilo(g8, tot_m)[0:1]
        gc_c, gl_c, b_c = _col_rep(gc_row, n), _col_rep(gl_row, n), _col_rep(b_ref[0, h], n)
        decay = jnp.exp(jnp.where(incl, wide(gc_c) - gc_row, NEG))
        kk = lax.dot_general(k, k, nt, preferred_element_type=F32)
        qk = lax.dot_general(q, k, nt, preferred_element_type=F32)
        xb = jnp.where(strict, -(kk * wide(b_c) * decay), 0.0).astype(BF16)
        a = qk * decay
        kf = k.astype(F32)
        egc = jnp.exp(gc_c)
        qg_ref[0, h] = (q.astype(F32) * egc).astype(qg_ref.dtype)
        kd_ref[0, h] = (kf * jnp.exp(gl_c - gc_c)).astype(kd_ref.dtype)
        for c in range(n // CHUNK):
            blk = slice(c * CHUNK, (c + 1) * CHUNK)
            a_ref[0, h, blk, :] = a[blk, blk].astype(a_ref.dtype)
        egl = jnp.exp(gl_c)
        egl_ref[0, h, 0] = jnp.concatenate([egl[c * CHUNK:c * CHUNK + 1] for c in range(n // CHUNK)], axis=0)
        rhs.append(jnp.concatenate([(v.astype(F32) * b_c).astype(BF16), (kf * b_c * egc).astype(BF16)], axis=1))
        ts.append(eye + xb.astype(F32))
        ps.append(xb)
    ps = [jnp.dot(p, p, preferred_element_type=F32).astype(BF16) for p in ps]
    for step in range(5):
        for h in heads:
            if step < 4:
                tp = jnp.dot(jnp.concatenate([ts[h].astype(BF16), ps[h]], axis=0), ps[h], preferred_element_type=F32)
                ts[h] = ts[h] + tp[:n]
                ps[h] = tp[n:].astype(BF16)
            else:
                ts[h] = ts[h] + jnp.dot(ts[h].astype(BF16), ps[h], preferred_element_type=F32)
    for h in heads:
        uw = jnp.dot(ts[h].astype(BF16), rhs[h], preferred_element_type=F32)
        u_ref[0, h] = uw[:, :DV_A].astype(u_ref.dtype)
        w_ref[0, h] = uw[:, DV_A:].astype(w_ref.dtype)


def _gdn_prep(qkv, g_t, beta_t, bsz, s):
    n, ph = GDN_GROUP, GDN_PH
    tok = lambda width, dt: jax.ShapeDtypeStruct((bsz, HA, s, width), dt)
    tspec = lambda width: pl.BlockSpec((1, ph, n, width), lambda b, h, i: (b, h, i, 0))
    qspec = lambda sec: pl.BlockSpec((1, 1, ph, n, DK_A), lambda b, h, i: (sec, b, h, i, 0))
    rspec = pl.BlockSpec((1, ph, 1, n), lambda b, h, i: (b, h, 0, i))
    return pl.pallas_call(
        _gdn_prep_kernel,
        out_shape=(tok(DV_A, BF16), tok(DK_A, BF16), tok(DK_A, BF16), tok(DK_A, BF16), tok(CHUNK, BF16),
                   jax.ShapeDtypeStruct((bsz, HA, s // n, n // CHUNK, LANES), F32)),
        grid=(bsz, HA // ph, s // n),
        in_specs=[qspec(0), qspec(1), qspec(2), rspec, rspec],
        out_specs=(tspec(DV_A), tspec(DK_A), tspec(DK_A), tspec(DK_A), tspec(CHUNK),
                   pl.BlockSpec((1, ph, 1, n // CHUNK, LANES), lambda b, h, i: (b, h, i, 0, 0))),
        compiler_params=_cparams(("parallel", "parallel", "parallel")),
        name="gdn_chunk_prep",
    )(qkv, qkv, qkv, g_t, beta_t)


def _gdn_scan_kernel(u_ref, w_ref, qg_ref, kd_ref, a_ref, egl_ref, z_ref, ng_ref, o_ref, state_ref, *, nchunk):
    @pl.when(pl.program_id(1) == 0)
    def _():
        state_ref[...] = jnp.zeros_like(state_ref)

    tn = (((0,), (0,)), ((), ()))
    heads = range(HA)

    def body(c, carry):
        rows = pl.ds(pl.multiple_of(c * CHUNK, CHUNK), CHUNK)
        st = [state_ref[h] for h in heads]
        sb = [x.astype(BF16) for x in st]
        vb = [(u_ref[0, h, rows, :].astype(F32)
               - jnp.dot(w_ref[0, h, rows, :], sb[h], preferred_element_type=F32)).astype(BF16) for h in heads]
        o = [jnp.dot(qg_ref[0, h, rows, :], sb[h], preferred_element_type=F32)
             + jnp.dot(a_ref[0, h, rows, :], vb[h], preferred_element_type=F32) for h in heads]
        for h in heads:
            state_ref[h] = (st[h] * egl_ref[0, h, pl.ds(c, 1), :]
                            + lax.dot_general(kd_ref[0, h, rows, :], vb[h], tn, preferred_element_type=F32))
        for h in heads:
            cols = slice(h * DV_A, (h + 1) * DV_A)
            z = z_ref[0, rows, cols].astype(F32)
            on = o[h] * lax.rsqrt(jnp.mean(o[h] * o[h], axis=-1, keepdims=True) + EPS) * ng_ref[...]
            o_ref[0, rows, cols] = (on * (z * jax.nn.sigmoid(z))).astype(o_ref.dtype)
        return carry

    lax.fori_loop(0, nchunk, body, 0)


def _gdn_scan(u, w, qg, kd, a, egl, main, norm_gdn, bsz, s):
    ts = GDN_TS
    z_blk0 = 3 * HA * DK_A // (HA * DV_A)
    hspec = lambda width: pl.BlockSpec((1, HA, ts, width), lambda b, i: (b, 0, i, 0))
    return pl.pallas_call(
        functools.partial(_gdn_scan_kernel, nchunk=ts // CHUNK),
        out_shape=jax.ShapeDtypeStruct((bsz, s, HA * DV_A), BF16),
        grid=(bsz, s // ts),
        in_specs=[hspec(DV_A), hspec(DK_A), hspec(DK_A), hspec(DK_A), hspec(CHUNK),
                  pl.BlockSpec((1, HA, ts // CHUNK, LANES), lambda b, i: (b, 0, i, 0)),
                  pl.BlockSpec((1, ts, HA * DV_A), lambda b, i: (b, i, z_blk0)),
                  pl.BlockSpec((1, DV_A), lambda b, i: (0, 0))],
        out_specs=pl.BlockSpec((1, ts, HA * DV_A), lambda b, i: (b, i, 0)),
        scratch_shapes=[pltpu.VMEM((HA, DK_A, DV_A), F32)],
        compiler_params=_cparams(("parallel", "arbitrary")),
        name="gdn_delta_scan",
    )(u, w, qg, kd, a, egl, main, norm_gdn.reshape(1, DV_A))


def _gated_deltanet(main, a_in, b_in, conv_w, a_log, dt_bias, norm_gdn, bsz, s):
    qkv = _gdn_conv(main, conv_w, bsz, s)
    g = -jnp.exp(a_log.astype(F32)) * jax.nn.softplus(a_in.astype(F32) + dt_bias.astype(F32))
    beta = jax.nn.sigmoid(b_in.astype(F32))
    g_t = g.transpose(0, 2, 1).reshape(bsz, HA, 1, s)
    beta_t = beta.transpose(0, 2, 1).reshape(bsz, HA, 1, s)
    u, w, qg, kd, a, egl = _gdn_prep(qkv, g_t, beta_t, bsz, s)
    egl = egl.reshape(bsz, HA, s // CHUNK, LANES)
    return _gdn_scan(u, w, qg, kd, a, egl, main, norm_gdn, bsz, s)


NSA_TS = 512
BIG = 1e30
DROPPED = -3e38


def _rope(x, cos, sin, lane):
    half = ROT_DIM // 2
    partner = jnp.where(lane < half, pltpu.roll(x, DH_B - half, axis=1), pltpu.roll(x, half, axis=1))
    return x * cos + partner * sin


def _nsa_rope_kernel(q_ref, kc_ref, vc_ref, ks_ref, kw_ref, cos_ref, sin_ref,
                     qo_ref, kco_ref, vco_ref, kso_ref, kwo_ref):
    cos, sin = cos_ref[0], sin_ref[0]
    lane = lax.broadcasted_iota(jnp.int32, cos.shape, 1)
    head = lambda ref, h: ref[0, :, h * DH_B:(h + 1) * DH_B].astype(F32)
    for h in range(HB):
        qo_ref[0, :, h * DH_B:(h + 1) * DH_B] = (_rope(head(q_ref, h), cos, sin, lane) * DH_B ** -0.5).astype(qo_ref.dtype)
    for g in range(G_KV):
        cols = slice(g * DH_B, (g + 1) * DH_B)
        kco_ref[0, g] = _rope(head(kc_ref, g), cos, sin, lane).astype(kco_ref.dtype)
        vco_ref[0, g] = vc_ref[0, :, cols]
        kso_ref[0, :, cols] = _rope(head(ks_ref, g), cos, sin, lane).astype(kso_ref.dtype)
        kwo_ref[0, :, cols] = _rope(head(kw_ref, g), cos, sin, lane).astype(kwo_ref.dtype)


def _nsa_rope(main, cos, sin, col, bsz, s):
    ts = NSA_TS
    kvw = G_KV * DH_B
    tok = lambda width, c0: pl.BlockSpec((1, ts, width), lambda b, i: (b, i, c0 // width))
    tab = pl.BlockSpec((1, ts, DH_B), lambda b, i: (b, i, 0))
    grp = pl.BlockSpec((1, G_KV, ts, DH_B), lambda b, i: (b, 0, i, 0))
    flat = lambda width: pl.BlockSpec((1, ts, width), lambda b, i: (b, i, 0))
    return pl.pallas_call(
        _nsa_rope_kernel,
        out_shape=(jax.ShapeDtypeStruct((bsz, s, HB * DH_B), BF16),
                   jax.ShapeDtypeStruct((bsz, G_KV, s, DH_B), BF16), jax.ShapeDtypeStruct((bsz, G_KV, s, DH_B), BF16),
                   jax.ShapeDtypeStruct((bsz, s, kvw), BF16), jax.ShapeDtypeStruct((bsz, s, kvw), BF16)),
        grid=(bsz, s // ts),
        in_specs=[tok(HB * DH_B, col["qb"]), tok(kvw, col["kc"]), tok(kvw, col["vc"]), tok(kvw, col["ks"]),
                  tok(kvw, col["kw"]), tab, tab],
        out_specs=(flat(HB * DH_B), grp, grp, flat(kvw), flat(kvw)),
        compiler_params=_cparams(("parallel", "parallel")),
        name="nsa_rotary",
    )(main, main, main, main, main, cos, sin)


def _gelu_tanh(x):
    return 0.5 * x * (1.0 + jnp.tanh(math.sqrt(2.0 / math.pi) * (x + 0.044715 * x * x * x)))


def _nsa_compress_kernel(k_ref, v_ref, pek_ref, pev_ref, w1k_ref, w1v_ref, w2k_ref, w2v_ref, ko_ref, vo_ref):
    nseg = k_ref.shape[2]
    for x_ref, pe_ref, w1_ref, w2_ref, o_ref in ((k_ref, pek_ref, w1k_ref, w2k_ref, ko_ref),
                                                 (v_ref, pev_ref, w1v_ref, w2v_ref, vo_ref)):
        for g in range(G_KV):
            x = x_ref[0, g].astype(F32)
            lo = jnp.dot((x + pe_ref[0:1]).astype(BF16), w1_ref[0], preferred_element_type=F32)
            hi = jnp.dot((x + pe_ref[1:2]).astype(BF16), w1_ref[1], preferred_element_type=F32)
            pre = lo + pltpu.roll(hi, nseg - 1, axis=0)
            o_ref[0, g] = jnp.dot(_gelu_tanh(pre).astype(BF16), w2_ref[...],
                                  preferred_element_type=F32).astype(o_ref.dtype)


def _nsa_compress(kc_t, vc_t, pe_ck, w1_ck, w2_ck, pe_cv, w1_cv, w2_cv, bsz, s):
    assert L_CMP == 2 * STRIDE_CMP
    nseg = s // STRIDE_CMP
    width = STRIDE_CMP * DH_B
    seg = lambda t: t.reshape(bsz, G_KV, nseg, width)
    pe2 = lambda pe: pe.reshape(2, width)
    w1h = lambda w: w.reshape(2, width, DH_B).astype(BF16)
    xspec = pl.BlockSpec((1, G_KV, nseg, width), lambda b: (b, 0, 0, 0))
    pspec = pl.BlockSpec((2, width), lambda b: (0, 0))
    w1spec = pl.BlockSpec((2, width, DH_B), lambda b: (0, 0, 0))
    w2spec = pl.BlockSpec((DH_B, DH_B), lambda b: (0, 0))
    ospec = pl.BlockSpec((1, G_KV, nseg, DH_B), lambda b: (b, 0, 0, 0))
    oshape = jax.ShapeDtypeStruct((bsz, G_KV, nseg, DH_B), BF16)
    return pl.pallas_call(
        _nsa_compress_kernel,
        out_shape=(oshape, oshape),
        grid=(bsz,),
        in_specs=[xspec, xspec, pspec, pspec, w1spec, w1spec, w2spec, w2spec],
        out_specs=(ospec, ospec),
        compiler_params=_cparams(("parallel",)),
        name="nsa_compress",
    )(seg(kc_t), seg(vc_t), pe2(pe_ck), pe2(pe_cv), w1h(w1_ck), w1h(w1_cv), w2_ck.astype(BF16), w2_cv.astype(BF16))


def _nsa_cmp_kernel(q_ref, kc_ref, vc_ref, ov_ref, o_ref, mb_ref, *, tq, n_slc):
    i = pl.program_id(2)
    rows = R_GRP * tq
    ncmp = kc_ref.shape[2]
    qa = _stack_heads(q_ref[0], None)
    s = lax.dot_general(qa, kc_ref[0, 0], (((1,), (1,)), ((), ())), preferred_element_type=F32)
    t_row = (lax.broadcasted_iota(jnp.int32, (rows, ncmp), 0) & (tq - 1)) + i * tq
    c_end = lax.broadcasted_iota(jnp.int32, (rows, ncmp), 1) * STRIDE_CMP + (L_CMP - 1)
    valid = c_end <= t_row
    sm = jnp.where(valid, s, NEG)
    p = jnp.where(valid, jnp.exp(sm - jnp.max(sm, axis=1, keepdims=True)), 0.0)
    l = jnp.sum(p, axis=1, keepdims=True)
    p = p * (1.0 / jnp.where(l > 0.0, l, 1.0))
    _unstack_heads(o_ref, jnp.dot(p.astype(BF16), vc_ref[0, 0], preferred_element_type=F32), tq)
    psum = p[0:tq]
    for r in range(1, R_GRP):
        psum = psum + p[r * tq:(r + 1) * tq]
    imp = _dot_hilo(psum, ov_ref[...])
    nb = -(-n_slc // SUBLANES) * SUBLANES
    v = imp.T[:nb]
    blk = lax.broadcasted_iota(jnp.int32, (nb, tq), 0)
    cur = (lax.broadcasted_iota(jnp.int32, (nb, tq), 1) + i * tq) // L_SLC
    forced = (blk == 0) | (blk == cur)
    v = jnp.where(forced, BIG, jnp.where(blk <= cur, v, -BIG))
    sel = jnp.zeros((nb, tq), F32)
    for _ in range(T_SEL):
        m = jnp.max(v, axis=0, keepdims=True)
        first = jnp.min(jnp.where(v == m, blk, LANES), axis=0, keepdims=True)
        hit = blk == first
        sel = jnp.where(hit, 1.0, sel)
        v = jnp.where(hit, DROPPED, v)
    bias = (jnp.where(blk <= cur, sel, 0.0) - 1.0) * BIG
    bias = jnp.concatenate([bias, jnp.zeros((LANES - nb, tq), F32)], axis=0)
    mb_ref[0, 0] = bias.T.astype(mb_ref.dtype)


def _nsa_cmp_select(q, k_cmp, v_cmp, bsz, s):
    tq = CMP_TQ
    nseg = s // STRIDE_CMP
    n_slc = s // L_SLC
    assert nseg <= LANES or nseg % LANES == 0
    c_start = np.arange(nseg) * STRIDE_CMP
    j_start = np.arange(n_slc) * L_SLC
    overlap = ((c_start[:, None] < j_start[None, :] + L_SLC) & (c_start[:, None] + L_CMP > j_start[None, :]))
    overlap = jnp.asarray(np.pad(overlap.astype(np.float32), ((0, 0), (0, LANES - n_slc))), BF16)
    qspec = pl.BlockSpec((1, tq, R_GRP * DH_B), lambda b, g, i: (b, i, g))
    cspec = pl.BlockSpec((1, 1, nseg, DH_B), lambda b, g, i: (b, g, 0, 0))
    return pl.pallas_call(
        functools.partial(_nsa_cmp_kernel, tq=tq, n_slc=n_slc),
        out_shape=(jax.ShapeDtypeStruct((bsz, s, HB * DH_B), BF16), jax.ShapeDtypeStruct((bsz, G_KV, s, LANES), BF16)),
        grid=(bsz, G_KV, s // tq),
        in_specs=[qspec, cspec, cspec, pl.BlockSpec((nseg, LANES), lambda b, g, i: (0, 0))],
        out_specs=(qspec, pl.BlockSpec((1, 1, tq, LANES), lambda b, g, i: (b, g, i, 0))),
        compiler_params=_cparams(("parallel", "parallel", "parallel")),
        name="nsa_compressed_select",
    )(q, k_cmp, v_cmp, overlap)


def _stack_heads(q, extra):
    parts = []
    for r in range(R_GRP):
        qr = q[:, r * DH_B:(r + 1) * DH_B]
        parts.append(qr if extra is None else jnp.concatenate([qr, extra], axis=1))
    return jnp.concatenate(parts, axis=0)


def _unstack_heads(o_ref, o, tq):
    for r in range(R_GRP):
        o_ref[0, :, r * DH_B:(r + 1) * DH_B] = o[r * tq:(r + 1) * tq].astype(o_ref.dtype)


def _heads_t(q):
    return [q[:, r * DH_B:(r + 1) * DH_B].astype(F32).T.astype(BF16) for r in range(R_GRP)]


def _softmax_step_t(ss, vs, carry):
    m, l, acc = carry
    m_new = functools.reduce(jnp.maximum, [jnp.max(s, axis=0, keepdims=True) for s in ss], m)
    alpha = jnp.exp(m - m_new)
    ps = [jnp.exp(s - m_new) for s in ss]
    l = alpha * l + functools.reduce(lambda a, b: a + b, [jnp.sum(p, axis=0, keepdims=True) for p in ps])
    tn = (((0,), (0,)), ((), ()))
    pv = [lax.dot_general(v, p.astype(BF16), tn, preferred_element_type=F32) for v, p in zip(vs, ps)]
    return m_new, l, functools.reduce(lambda a, b: a + b, pv, alpha * acc)


def _softmax_init_t(rows):
    return (jnp.full((1, rows), -jnp.inf, F32), jnp.zeros((1, rows), F32), jnp.zeros((DH_B, rows), F32))


def _selected_branch(i, qts, mb, ks_ref, oh_ref, vs_ref, tq):
    rows = R_GRP * tq
    mbt = mb.astype(F32).T.astype(BF16)
    qat = jnp.concatenate([jnp.concatenate([qt, mbt], axis=0) for qt in qts], axis=1)

    def scores(j):
        keys = pl.ds(pl.multiple_of(j * tq, tq), tq)
        k = jnp.concatenate([ks_ref[0, keys, :], oh_ref[keys, :]], axis=1)
        return jnp.dot(k, qat, preferred_element_type=F32)

    def values(j):
        return vs_ref[0, pl.ds(pl.multiple_of(j * tq, tq), tq), :]

    def pair(jj, c):
        j = 2 * jj
        return _softmax_step_t([scores(j), scores(j + 1)], [values(j), values(j + 1)], c)

    carry = lax.fori_loop(0, i // 2, pair, _softmax_init_t(rows))
    k_loc = lax.broadcasted_iota(jnp.int32, (tq, tq), 0)
    t_loc = lax.broadcasted_iota(jnp.int32, (tq, tq), 1)
    causal = jnp.concatenate([jnp.where(k_loc <= t_loc, 0.0, NEG)] * R_GRP, axis=1)
    diag = lambda c: _softmax_step_t([scores(i) + causal], [values(i)], c)
    both = lambda c: _softmax_step_t([scores(i) + causal, scores(i - 1)], [values(i), values(i - 1)], c)
    m, l, acc = lax.cond((i % 2) == 1, both, diag, carry)
    return acc / l


def _window_branch(i, qts, kw_refs, vw_refs, tq):
    rows = R_GRP * tq
    qt = jnp.concatenate(qts, axis=1)
    k_loc = lax.broadcasted_iota(jnp.int32, (tq, tq), 0)
    t_loc = lax.broadcasted_iota(jnp.int32, (tq, tq), 1)
    bound = jnp.minimum(t_loc[0:1, :] + (i * tq + 1), WINDOW)
    ss = []
    for n in range(3):
        dist = t_loc + (2 - n) * tq - k_loc
        bias = jnp.where(dist.astype(jnp.uint32) < bound.astype(jnp.uint32), 0.0, NEG)
        ss.append(jnp.dot(kw_refs[n][0], qt, preferred_element_type=F32) + jnp.concatenate([bias] * R_GRP, axis=1))
    m, l, acc = _softmax_step_t(ss, [r[0] for r in vw_refs], _softmax_init_t(rows))
    return acc / l


def _nsa_local_kernel(q_ref, mb_ref, ks_ref, oh_ref, vs_ref, kw0, kw1, kw2, vw0, vw1, vw2, oc_ref, gate_ref, o_ref,
                      *, tq, gate_lane0):
    g, i = pl.program_id(1), pl.program_id(2)
    q = q_ref[0]
    qts = _heads_t(q)
    o_slc = _selected_branch(i, qts, mb_ref[0, 0], ks_ref, oh_ref, vs_ref, tq)
    o_win = _window_branch(i, qts, (kw0, kw1, kw2), (vw0, vw1, vw2), tq)
    gates = jax.nn.sigmoid(gate_ref[0])
    lane = lax.broadcasted_iota(jnp.int32, gates.shape, 1)
    pick = lambda idx: jnp.sum(jnp.where(lane == idx, gates, 0.0), axis=1, keepdims=True)
    for r in range(R_GRP):
        base = gate_lane0 + (g * R_GRP + r) * 3
        rows = slice(r * tq, (r + 1) * tq)
        cols = slice(r * DH_B, (r + 1) * DH_B)
        o = (pick(base) * oc_ref[0, :, cols].astype(F32) + pick(base + 1) * o_slc[:, rows].T
             + pick(base + 2) * o_win[:, rows].T)
        o_ref[0, :, cols] = o.astype(o_ref.dtype)


def _nsa_local(q, maskbias, ks, main, kw, o_cmp, small, col, gate_lane0, bsz, s):
    tq = ATT_TQ
    assert 2 * tq >= WINDOW
    onehot = jnp.asarray(np.arange(s)[:, None] // L_SLC == np.arange(LANES)[None, :], BF16)
    qspec = pl.BlockSpec((1, tq, R_GRP * DH_B), lambda b, g, i: (b, i, g))
    seq = lambda c0: pl.BlockSpec((1, s, DH_B), lambda b, g, i: (b, 0, c0 // DH_B + g))
    back = lambda c0, n: pl.BlockSpec((1, tq, DH_B), lambda b, g, i: (b, jnp.maximum(i - n, 0), c0 // DH_B + g))
    return pl.pallas_call(
        functools.partial(_nsa_local_kernel, tq=tq, gate_lane0=gate_lane0),
        out_shape=jax.ShapeDtypeStruct((bsz, s, HB * DH_B), BF16),
        grid=(bsz, G_KV, s // tq),
        in_specs=[qspec,
                  pl.BlockSpec((1, 1, tq, LANES), lambda b, g, i: (b, g, i, 0)),
                  seq(0), pl.BlockSpec((s, LANES), lambda b, g, i: (0, 0)), seq(col["vs"]),
                  back(0, 2), back(0, 1), back(0, 0),
                  back(col["vw"], 2), back(col["vw"], 1), back(col["vw"], 0),
                  qspec, pl.BlockSpec((1, tq, LANES), lambda b, g, i: (b, i, 0))],
        out_specs=qspec,
        compiler_params=_cparams(("parallel", "parallel", "arbitrary")),
        name="nsa_selected_window_combine",
    )(q, maskbias, ks, onehot, main, kw, kw, kw, main, main, main, o_cmp, small)


def _moe_sub_blocks(nv_ref, out_ref, compute):
    nsub = (nv_ref[pl.program_id(1)] + MOE_SUB - 1) // MOE_SUB
    rows_of = lambda sb: pl.ds(pl.multiple_of(sb * MOE_SUB, MOE_SUB), MOE_SUB)

    def run(first, count):
        rows = [rows_of(first + u) for u in range(count)]
        for r, val in zip(rows, compute(rows)):
            out_ref[r, :] = val

    def quad(p, carry):
        run(4 * p, 4)
        return carry

    def dead(sb, carry):
        out_ref[rows_of(sb), :] = jnp.zeros((MOE_SUB, out_ref.shape[1]), out_ref.dtype)
        return carry

    lax.fori_loop(0, nsub // 4, quad, 0)
    done = (nsub // 4) * 4

    @pl.when((nsub & 2) != 0)
    def _():
        run(done, 2)

    @pl.when((nsub & 1) != 0)
    def _():
        run(done + (nsub & 2), 1)

    lax.fori_loop(nsub, MOE_TM // MOE_SUB, dead, 0)


def _moe_new_expert(be_ref):
    i = pl.program_id(1)
    return (i == 0) | (be_ref[i] != be_ref[jnp.maximum(i - 1, 0)])


def _moe_up_kernel(be_ref, nv_ref, st_ref, x_ref, wg_ref, wu_ref, bg_ref, bu_ref, h_ref, wgb_ref, wub_ref):
    @pl.when(_moe_new_expert(be_ref))
    def _():
        wgb_ref[...] = wg_ref[0].astype(BF16)
        wub_ref[...] = wu_ref[0].astype(BF16)

    def compute(rows):
        xs = [jnp.concatenate(_unpack_bf16_pairs(x_ref[r, :]), axis=1).astype(BF16) for r in rows]
        gates = [jnp.dot(x, wgb_ref[...], preferred_element_type=F32) + bg_ref[0] for x in xs]
        ups = [jnp.dot(x, wub_ref[...], preferred_element_type=F32) + bu_ref[0] for x in xs]
        outs = []
        for gate, up in zip(gates, ups):
            gate = jnp.minimum(gate, SWIGLU_LIMIT)
            up = jnp.clip(up, -SWIGLU_LIMIT, SWIGLU_LIMIT)
            outs.append(((up + 1.0) * gate * jax.nn.sigmoid(SWIGLU_ALPHA * gate)).astype(h_ref.dtype))
        return outs

    _moe_sub_blocks(nv_ref, h_ref, compute)


def _moe_down_kernel(be_ref, nv_ref, st_ref, h_ref, wd_ref, bd_ref, y_ref, wdb_ref):
    @pl.when(_moe_new_expert(be_ref))
    def _():
        wdb_ref[...] = wd_ref[0].astype(BF16)

    half = wdb_ref.shape[1] // 2

    def compute(rows):
        ys = [jnp.dot(h_ref[r, :], wdb_ref[...], preferred_element_type=F32) + bd_ref[0] for r in rows]
        return [_pack_bf16_pairs(y[:, :half], y[:, half:]) for y in ys]

    _moe_sub_blocks(nv_ref, y_ref, compute)


def _moe_experts(rows, block_e, n_valid, src_tile, w_gate, b_gate, w_up, b_up, w_down, b_down):
    n_rows, d = rows.shape[0], 2 * rows.shape[1]
    n_blocks = n_rows // MOE_TM
    tile = lambda width: pl.BlockSpec((MOE_TM, width), lambda n, i, be, nv, st: (st[i], 0))
    wcol = lambda k: pl.BlockSpec((1, k, tn), lambda n, i, be, nv, st: (be[i], 0, n))
    bcol = lambda: pl.BlockSpec((1, 1, tn), lambda n, i, be, nv, st: (be[i], 0, n))
    params = _cparams(("arbitrary", "arbitrary"))
    tn = MOE_TN
    h = pl.pallas_call(
        _moe_up_kernel,
        out_shape=jax.ShapeDtypeStruct((n_rows, D_FF), BF16),
        grid_spec=pltpu.PrefetchScalarGridSpec(
            num_scalar_prefetch=3, grid=(D_FF // tn, n_blocks),
            in_specs=[tile(d // 2), wcol(d), wcol(d), bcol(), bcol()],
            out_specs=pl.BlockSpec((MOE_TM, tn), lambda n, i, be, nv, st: (i, n)),
            scratch_shapes=[pltpu.VMEM((d, tn), BF16), pltpu.VMEM((d, tn), BF16)]),
        compiler_params=params, name="moe_expert_up",
    )(block_e, n_valid, src_tile, rows, w_gate, w_up, b_gate.reshape(N_EXP, 1, D_FF), b_up.reshape(N_EXP, 1, D_FF))
    tn = MOE_TN_DOWN
    return pl.pallas_call(
        _moe_down_kernel,
        out_shape=jax.ShapeDtypeStruct((n_rows, d // 2), jnp.uint32),
        grid_spec=pltpu.PrefetchScalarGridSpec(
            num_scalar_prefetch=3, grid=(d // tn, n_blocks),
            in_specs=[tile(D_FF), wcol(D_FF), bcol()],
            out_specs=pl.BlockSpec((MOE_TM, tn // 2), lambda n, i, be, nv, st: (i, n)),
            scratch_shapes=[pltpu.VMEM((D_FF, tn), BF16)]),
        compiler_params=params, name="moe_expert_down",
    )(block_e, n_valid, src_tile, h, w_down, b_down.reshape(N_EXP, 1, d))


ROUTE_TT = 512
MOVE_TT = 256


def _pack_bf16_pairs(lo, hi):
    as_bits = lambda v: pltpu.bitcast(v.astype(BF16).astype(F32), jnp.uint32)
    return (as_bits(lo) >> 16) | (as_bits(hi) & jnp.uint32(0xFFFF0000))


def _unpack_bf16_pairs(w):
    return pltpu.bitcast(w << 16, F32), pltpu.bitcast(w & jnp.uint32(0xFFFF0000), F32)


def _route_kernel(h_ref, wr_ref, br_ref, e_ref, w_ref, p_ref, cnt_ref, run_ref, *, tt):
    @pl.when(pl.program_id(0) == 0)
    def _():
        run_ref[...] = jnp.zeros_like(run_ref)

    h = h_ref[...]
    logits = (jnp.dot(h, wr_ref[0], preferred_element_type=F32) + jnp.dot(h, wr_ref[1], preferred_element_type=F32)
              + br_ref[...])
    lane = lax.broadcasted_iota(jnp.int32, (tt, LANES), 1)
    v = jnp.where(lane < N_EXP, logits, -BIG)
    tops, hits, firsts = [], [], []
    for _ in range(TOP_K):
        m = jnp.max(v, axis=1, keepdims=True)
        first = jnp.min(jnp.where(v == m, lane, LANES), axis=1, keepdims=True)
        hit = lane == first
        v = jnp.where(hit, DROPPED, v)
        tops.append(m), hits.append(hit), firsts.append(first)
    ex = [jnp.exp(m - tops[0]) for m in tops]
    inv = 1.0 / functools.reduce(lambda a, b: a + b, ex)
    onehot = functools.reduce(lambda a, b: a + b, [jnp.where(hh, 1.0, 0.0) for hh in hits]).astype(BF16)
    ri = lax.broadcasted_iota(jnp.int32, (tt, tt), 0)
    ci = lax.broadcasted_iota(jnp.int32, (tt, tt), 1)
    before = jnp.where(ci < ri, 1.0, 0.0).astype(BF16)
    rank = jnp.dot(before, onehot, preferred_element_type=F32) + run_ref[0:1]
    run_ref[...] = run_ref[...] + jnp.dot(jnp.ones((SUBLANES, tt), BF16), onehot, preferred_element_type=F32)
    cnt_ref[...] = run_ref[...].astype(jnp.int32)
    place = lambda cols, zero: functools.reduce(
        lambda acc, kc: jnp.where(lane == kc[0], kc[1], acc), list(enumerate(cols)), zero)
    e_ref[...] = place(firsts, jnp.zeros((tt, LANES), jnp.int32))
    w_ref[...] = place([e * inv for e in ex], jnp.zeros((tt, LANES), F32))
    pos = [jnp.sum(jnp.where(hh, rank, 0.0), axis=1, keepdims=True).astype(jnp.int32) for hh in hits]
    p_ref[...] = place(pos, jnp.zeros((tt, LANES), jnp.int32))


def _route(h, w_router, b_router):
    n, d = h.shape
    tt = ROUTE_TT
    wr = jnp.pad(w_router, ((0, 0), (0, LANES - N_EXP)))
    hi = wr.astype(BF16)
    wr2 = jnp.stack([hi, (wr - hi.astype(F32)).astype(BF16)])
    br = jnp.pad(b_router, (0, LANES - N_EXP)).reshape(1, LANES)
    tok = lambda dt: jax.ShapeDtypeStruct((n, LANES), dt)
    tspec = pl.BlockSpec((tt, LANES), lambda i: (i, 0))
    return pl.pallas_call(
        functools.partial(_route_kernel, tt=tt),
        out_shape=(tok(jnp.int32), tok(F32), tok(jnp.int32), jax.ShapeDtypeStruct((SUBLANES, LANES), jnp.int32)),
        grid=(n // tt,),
        in_specs=[pl.BlockSpec((tt, d), lambda i: (i, 0)), pl.BlockSpec((2, d, LANES), lambda i: (0, 0, 0)),
                  pl.BlockSpec((1, LANES), lambda i: (0, 0))],
        out_specs=(tspec, tspec, tspec, pl.BlockSpec((SUBLANES, LANES), lambda i: (0, 0))),
        scratch_shapes=[pltpu.VMEM((SUBLANES, LANES), F32)],
        compiler_params=_cparams(("arbitrary",)),
        name="moe_route",
    )(h, wr2, br)


def _dispatch_kernel(dest_ref, src_ref, init_ref, rows_ref, sem, *, tt):
    del init_ref
    t0 = pl.program_id(0) * tt

    def copy(j, k):
        return pltpu.make_async_copy(src_ref.at[pl.ds(j, 1)],
                                     rows_ref.at[pl.ds(dest_ref[(t0 + j) * TOP_K + k], 1)], sem)

    def start(j, carry):
        for k in range(TOP_K):
            copy(j, k).start(priority=k % 2)
        return carry

    def wait(j, carry):
        for k in range(TOP_K):
            copy(j, k).wait()
        return carry

    lax.fori_loop(0, tt, start, 0)
    lax.fori_loop(0, tt, wait, 0)


def _dispatch(dest, hpk, n_rows):
    n, width = hpk.shape
    tt = MOVE_TT
    return pl.pallas_call(
        functools.partial(_dispatch_kernel, tt=tt),
        out_shape=jax.ShapeDtypeStruct((n_rows, width), jnp.uint32),
        grid_spec=pltpu.PrefetchScalarGridSpec(
            num_scalar_prefetch=1, grid=(n // tt,),
            in_specs=[pl.BlockSpec((tt, width), lambda i, dest: (i, 0)), pl.BlockSpec(memory_space=pl.ANY)],
            out_specs=pl.BlockSpec(memory_space=pl.ANY),
            scratch_shapes=[pltpu.SemaphoreType.DMA]),
        input_output_aliases={2: 0},
        compiler_params=_cparams(("arbitrary",)),
        name="moe_dispatch",
    )(dest, hpk, jnp.zeros((n_rows, width), jnp.uint32))


def _combine_kernel(dest_ref, y_ref, w_ref, x1_ref, g_ref, o_ref, buf_ref, sem, *, tt, norm):
    step = pl.program_id(0)
    slot = step % 2

    def copy(tile, sl, j, k):
        return pltpu.make_async_copy(y_ref.at[pl.ds(dest_ref[(tile * tt + j) * TOP_K + k], 1)],
                                     buf_ref.at[sl, k, pl.ds(j, 1)], sem.at[sl])

    def gather(tile, sl):
        def start(j, carry):
            for k in range(TOP_K):
                copy(tile, sl, j, k).start(priority=k % 2)
            return carry
        lax.fori_loop(0, tt, start, 0)

    @pl.when(step == 0)
    def _():
        gather(step, slot)

    @pl.when(step + 1 < pl.num_programs(0))
    def _():
        gather(step + 1, 1 - slot)

    def wait(j, carry):
        for k in range(TOP_K):
            copy(step, slot, j, k).wait()
        return carry

    lax.fori_loop(0, tt, wait, 0)
    wts = w_ref[...]
    lane = lax.broadcasted_iota(jnp.int32, wts.shape, 1)
    x = x1_ref[...]
    nq = MOE_TN_DOWN // 2
    for k in range(TOP_K):
        wk = jnp.sum(jnp.where(lane == k, wts, 0.0), axis=1, keepdims=True)
        parts = []
        for n in range(D_MODEL // MOE_TN_DOWN):
            lo, hi = _unpack_bf16_pairs(buf_ref[slot, k, :, n * nq:(n + 1) * nq])
            parts += [lo, hi]
        x = x + wk * jnp.concatenate(parts, axis=1)
    if norm:
        x = x * lax.rsqrt(jnp.mean(x * x, axis=-1, keepdims=True) + EPS) * g_ref[...]
    o_ref[...] = x.astype(o_ref.dtype)


def _combine(dest, ypk, wts, x1, gain):
    n, d = x1.shape
    tt = MOVE_TT
    norm = gain is not None
    gain = gain if norm else jnp.ones((d,), F32)
    return pl.pallas_call(
        functools.partial(_combine_kernel, tt=tt, norm=norm),
        out_shape=jax.ShapeDtypeStruct((n, d), F32),
        grid_spec=pltpu.PrefetchScalarGridSpec(
            num_scalar_prefetch=1, grid=(n // tt,),
            in_specs=[pl.BlockSpec(memory_space=pl.ANY),
                      pl.BlockSpec((tt, LANES), lambda i, dest: (i, 0)),
                      pl.BlockSpec((tt, d), lambda i, dest: (i, 0)),
                      pl.BlockSpec((1, d), lambda i, dest: (0, 0))],
            out_specs=pl.BlockSpec((tt, d), lambda i, dest: (i, 0)),
            scratch_shapes=[pltpu.VMEM((2, TOP_K, tt, d // 2), jnp.uint32), pltpu.SemaphoreType.DMA((2,))]),
        compiler_params=_cparams(("arbitrary",)),
        name="moe_combine_norm",
    )(dest, ypk, wts, x1, gain.reshape(1, d))


def _native_sparse_attention(main, small, positions, pe_ck, w1_ck, w2_ck, pe_cv, w1_cv, w2_cv, col, gate_lane0, bsz, s):
    half = ROT_DIM // 2
    inv_freq = ROPE_THETA ** (-jnp.arange(half, dtype=F32) * 2.0 / ROT_DIM)
    ang = positions.astype(F32)[..., None] * inv_freq
    cos, sin = jnp.cos(ang), jnp.sin(ang)
    rest = (bsz, s, DH_B - ROT_DIM)
    cos_t = jnp.concatenate([cos, cos, jnp.ones(rest, F32)], axis=-1)
    sin_t = jnp.concatenate([-sin, sin, jnp.zeros(rest, F32)], axis=-1)
    q, kc_t, vc_t, ks, kw = _nsa_rope(main, cos_t, sin_t, col, bsz, s)
    k_cmp, v_cmp = _nsa_compress(kc_t, vc_t, pe_ck, w1_ck, w2_ck, pe_cv, w1_cv, w2_cv, bsz, s)
    o_cmp, maskbias = _nsa_cmp_select(q, k_cmp, v_cmp, bsz, s)
    return _nsa_local(q, maskbias, ks, main, kw, o_cmp, small, col, gate_lane0, bsz, s)


def _moe_ffn(hf, hpk, x1, final_gain, w_router, b_router, w_gate, b_gate, w_up, b_up, w_down, b_down):
    n_tok, d = x1.shape
    eidx, wts, pos, cnt = _route(hf, w_router, b_router)
    counts = cnt[0, :N_EXP]
    padded = ((counts + MOE_TM - 1) // MOE_TM) * MOE_TM
    pad_end = jnp.cumsum(padded)
    pad_start = pad_end - padded
    part = counts % MOE_TM
    gap = jnp.where(part > 0, MOE_TM - part, 0)
    pick = lambda table: jnp.sum(jnp.where(eidx[:, :TOP_K, None] == jnp.arange(N_EXP), table, 0), axis=-1)
    rank = pos[:, :TOP_K]
    dest = pick(pad_start) + rank + jnp.where(rank >= pick(part), pick(gap), 0)
    dest = dest.reshape(-1).astype(jnp.int32)
    n_blocks = (n_tok * TOP_K + N_EXP * (MOE_TM - 1) + MOE_TM - 1) // MOE_TM
    rows = _dispatch(dest, hpk, n_blocks * MOE_TM)
    tile0 = jnp.arange(n_blocks, dtype=jnp.int32) * MOE_TM
    block_e = jnp.minimum(jnp.searchsorted(pad_end, tile0, side='right'), N_EXP - 1).astype(jnp.int32)
    first = (tile0 == pad_start[block_e]) & (part[block_e] > 0)
    n_valid = jnp.where(tile0 < pad_end[-1], jnp.where(first, part[block_e], MOE_TM), 0).astype(jnp.int32)
    tile_id = jnp.arange(n_blocks, dtype=jnp.int32)
    src_tile = jnp.where(n_valid > 0, tile_id, jnp.maximum(pad_end[-1] // MOE_TM - 1, 0)).astype(jnp.int32)
    ypk = _moe_experts(rows, block_e, n_valid, src_tile, w_gate, b_gate, w_up, b_up, w_down, b_down)
    return _combine(dest, ypk, wts, x1, final_gain)


def _layer(x, positions, norm_mix, w_in, conv_w, a_log, dt_bias, norm_gdn, pe_ck, w1_ck, w2_ck,
           pe_cv, w1_cv, w2_cv, w_proj_a, w_proj_b, w_out, norm_ffn, w_router, b_router,
           w_gate, b_gate, w_up, b_up, w_down, b_down, final_gain):
    bsz, s, d = x.shape
    n_tok = bsz * s
    x2 = x.reshape(n_tok, d)
    h = _rmsnorm(x2, norm_mix, BF16)
    n_small = 2 * HA + 3 * HB
    sp = SPLIT_POINTS
    w_main = jnp.concatenate([w_in[:, :sp[3]], w_in[:, sp[5]:sp[12]], w_in[:, sp[13]:]], axis=1).astype(BF16)
    w_small = jnp.concatenate([w_in[:, sp[3]:sp[5]], w_in[:, sp[12]:sp[13]]], axis=1)
    w_small = jnp.pad(w_small, ((0, 0), (0, LANES - n_small))).astype(BF16)
    main = _matmul(h, w_main, BF16, tm=INPROJ_TM).reshape(bsz, s, -1)
    small = _matmul(h, w_small, F32).reshape(bsz, s, -1)
    names = ("qa", "ka", "va", "za", "qb", "kc", "vc", "ks", "vs", "kw", "vw", "gm")
    sizes = (HA * DK_A, HA * DK_A, HA * DV_A, HA * DV_A,
             HB * DH_B, G_KV * DH_B, G_KV * DH_B, G_KV * DH_B, G_KV * DH_B, G_KV * DH_B, G_KV * DH_B, 2 * D_MODEL)
    col = {nm: sum(sizes[:i]) for i, nm in enumerate(names)}
    aa, ba = small[..., :HA], small[..., HA:2 * HA]

    o_a = _gated_deltanet(main, aa, ba, conv_w, a_log, dt_bias, norm_gdn, bsz, s)
    o_b = _native_sparse_attention(main, small, positions, pe_ck, w1_ck, w2_ck, pe_cv, w1_cv, w2_cv,
                                   col, 2 * HA, bsz, s)
    merged = _merge(o_a.reshape(n_tok, d), o_b.reshape(n_tok, d), w_proj_a.astype(BF16),
                    w_proj_b.astype(BF16), main.reshape(n_tok, -1), col["gm"])
    x1, hf, hpk = _outproj(merged, w_out.astype(BF16), x2, norm_ffn)
    out = _moe_ffn(hf, hpk, x1, final_gain, w_router, b_router, w_gate, b_gate, w_up, b_up, w_down, b_down)
    return out.reshape(bsz, s, d)


def kernel(x, positions, norm_mix, w_in, conv_w, a_log, dt_bias, norm_gdn, pe_ck, w1_ck, w2_ck, pe_cv, w1_cv, w2_cv, w_proj_a, w_proj_b, w_out, norm_ffn, w_router, b_router, w_gate, b_gate, w_up, b_up, w_down, b_down, norm_final):
    depth = norm_mix.shape[0]
    for l in range(depth):
        x = _layer(x, positions, norm_mix[l], w_in[l], conv_w[l], a_log[l], dt_bias[l], norm_gdn[l],
                   pe_ck[l], w1_ck[l], w2_ck[l], pe_cv[l], w1_cv[l], w2_cv[l],
                   w_proj_a[l], w_proj_b[l], w_out[l], norm_ffn[l], w_router[l], b_router[l],
                   w_gate[l], b_gate[l], w_up[l], b_up[l], w_down[l], b_down[l],
                   norm_final if l + 1 == depth else None)
    return x
```

```python
import functools
import math

import jax
import jax.numpy as jnp
import numpy as np
from jax import lax
from jax.experimental import pallas as pl
from jax.experimental.pallas import tpu as pltpu

F32 = jnp.float32
BF16 = jnp.bfloat16

D_MODEL = 2048
EPS = 1e-6
NEG = -1e30
HA = D_MODEL // 128
DK_A = 128
DV_A = 128
CONV_W = 4
CHUNK = 64
HB = D_MODEL // 128
G_KV = 4
R_GRP = HB // G_KV
DH_B = 128
ROT_DIM = DH_B // 4
ROPE_THETA = 500000.0
L_CMP = 32
STRIDE_CMP = 16
L_SLC = 64
T_SEL = 8
WINDOW = 512
N_EXP = 32
TOP_K = 4
D_FF = D_MODEL
SWIGLU_LIMIT = 7.0
SWIGLU_ALPHA = 1.702
SPLIT_SIZES = (HA * DK_A, HA * DK_A, HA * DV_A, HA * DV_A, HA, HA,
               HB * DH_B, G_KV * DH_B, G_KV * DH_B, G_KV * DH_B, G_KV * DH_B, G_KV * DH_B, G_KV * DH_B,
               3 * HB, 2 * D_MODEL)
SPLIT_POINTS = tuple(sum(SPLIT_SIZES[:i + 1]) for i in range(len(SPLIT_SIZES) - 1))

V7X_VMEM_LIMIT_BYTES = 56 * 1024 * 1024
LANES = 128
SUBLANES = 8
MOE_TM = 1024
MOE_SUB = 256
MOE_TN = 512
MOE_TN_DOWN = 1024
INPROJ_TM = 2048
ATT_TQ = 256
CMP_TQ = 512


def _cparams(sem):
    return pltpu.CompilerParams(dimension_semantics=sem, vmem_limit_bytes=V7X_VMEM_LIMIT_BYTES)


def _mm_kernel(x_ref, w_ref, o_ref):
    o_ref[...] = jnp.dot(x_ref[...], w_ref[...], preferred_element_type=F32).astype(o_ref.dtype)


def _matmul(x, w, out_dtype, tm=1024, tn=1024):
    m, k = x.shape
    n = w.shape[1]
    tm, tn = min(tm, m), min(tn, n)
    assert m % tm == 0 and n % tn == 0
    return pl.pallas_call(
        _mm_kernel,
        out_shape=jax.ShapeDtypeStruct((m, n), out_dtype),
        grid=(n // tn, m // tm),
        in_specs=[pl.BlockSpec((tm, k), lambda j, i: (i, 0)),
                  pl.BlockSpec((k, tn), lambda j, i: (0, j))],
        out_specs=pl.BlockSpec((tm, tn), lambda j, i: (i, j)),
        compiler_params=_cparams(("parallel", "parallel")),
        name="dense_matmul",
    )(x, w)


def _rmsnorm_kernel(x_ref, g_ref, o_ref):
    x = x_ref[...]
    y = x * lax.rsqrt(jnp.mean(x * x, axis=-1, keepdims=True) + EPS)
    o_ref[...] = (y * g_ref[...]).astype(o_ref.dtype)


def _rmsnorm(x, gain, out_dtype, tm=512):
    m, d = x.shape
    return pl.pallas_call(
        _rmsnorm_kernel,
        out_shape=jax.ShapeDtypeStruct((m, d), out_dtype),
        grid=(m // tm,),
        in_specs=[pl.BlockSpec((tm, d), lambda i: (i, 0)),
                  pl.BlockSpec((1, d), lambda i: (0, 0))],
        out_specs=pl.BlockSpec((tm, d), lambda i: (i, 0)),
        compiler_params=_cparams(("parallel",)),
        name="rmsnorm",
    )(x, gain.reshape(1, d))


def _merge_kernel(oa_ref, ob_ref, wa_ref, wb_ref, ga_ref, gb_ref, o_ref):
    ya = jnp.dot(oa_ref[...], wa_ref[...], preferred_element_type=F32)
    yb = jnp.dot(ob_ref[...], wb_ref[...], preferred_element_type=F32)
    o_ref[...] = (jax.nn.sigmoid(ga_ref[...].astype(F32)) * ya
                  + jax.nn.sigmoid(gb_ref[...].astype(F32)) * yb).astype(o_ref.dtype)


def _merge(o_a, o_b, w_a, w_b, main2d, gm_col0, tm=1024, tn=1024):
    m, d = o_a.shape
    assert gm_col0 % tn == 0 and d % tn == 0
    g0 = gm_col0 // tn
    lhs = pl.BlockSpec((tm, d), lambda j, i: (i, 0))
    rhs = pl.BlockSpec((d, tn), lambda j, i: (0, j))
    return pl.pallas_call(
        _merge_kernel,
        out_shape=jax.ShapeDtypeStruct((m, d), BF16),
        grid=(d // tn, m // tm),
        in_specs=[lhs, lhs, rhs, rhs,
                  pl.BlockSpec((tm, tn), lambda j, i: (i, g0 + j)),
                  pl.BlockSpec((tm, tn), lambda j, i: (i, g0 + d // tn + j))],
        out_specs=pl.BlockSpec((tm, tn), lambda j, i: (i, j)),
        compiler_params=_cparams(("parallel", "parallel")),
        name="mixer_merge",
    )(o_a, o_b, w_a, w_b, main2d, main2d)


def _outproj_kernel(m_ref, w_ref, x_ref, g_ref, x1_ref, h_ref, hpk_ref):
    x1 = x_ref[...] + jnp.dot(m_ref[...], w_ref[...], preferred_element_type=F32)
    x1_ref[...] = x1
    y = x1 * lax.rsqrt(jnp.mean(x1 * x1, axis=-1, keepdims=True) + EPS) * g_ref[...]
    h_ref[...] = y.astype(h_ref.dtype)
    half = y.shape[1] // 2
    hpk_ref[...] = _pack_bf16_pairs(y[:, :half], y[:, half:])


def _outproj(merged, w_out, x, gain, tm=512):
    m, d = x.shape
    row = lambda width=d: pl.BlockSpec((tm, width), lambda i: (i, 0))
    return pl.pallas_call(
        _outproj_kernel,
        out_shape=(jax.ShapeDtypeStruct((m, d), F32), jax.ShapeDtypeStruct((m, d), BF16),
                   jax.ShapeDtypeStruct((m, d // 2), jnp.uint32)),
        grid=(m // tm,),
        in_specs=[row(), pl.BlockSpec((d, d), lambda i: (0, 0)), row(), pl.BlockSpec((1, d), lambda i: (0, 0))],
        out_specs=(row(), row(), row(d // 2)),
        compiler_params=_cparams(("parallel",)),
        name="out_proj_residual_norm",
    )(merged, w_out, x, gain.reshape(1, d))


GDN_COLS = 512
GDN_GROUP = 256
GDN_PH = 8
GDN_TS = 512


def _gdn_conv_kernel(x_ref, w_ref, o_ref):
    sec = pl.program_id(1) // (HA * DK_A // GDN_COLS)
    x = x_ref[0].astype(F32)
    w = w_ref[...]
    row = lax.broadcasted_iota(jnp.int32, x.shape, 0)
    y = x * w[CONV_W - 1:CONV_W]
    for i in range(CONV_W - 1):
        sh = CONV_W - 1 - i
        y = y + jnp.where(row >= sh, pltpu.roll(x, sh, axis=0), 0.0) * w[i:i + 1]
    y = y * jax.nn.sigmoid(y)
    qscale = jnp.where(sec == 0, DK_A ** -0.5, 1.0)
    for h in range(GDN_COLS // DK_A):
        yh = y[:, h * DK_A:(h + 1) * DK_A]
        inv = lax.rsqrt(jnp.sum(yh * yh, axis=-1, keepdims=True) + EPS) * qscale
        o_ref[0, 0, h] = (yh * jnp.where(sec < 2, inv, 1.0)).astype(o_ref.dtype)


def _gdn_conv(main, conv_w, bsz, s):
    ncol = 3 * HA * DK_A // GDN_COLS
    hpc = GDN_COLS // DK_A
    return pl.pallas_call(
        _gdn_conv_kernel,
        out_shape=jax.ShapeDtypeStruct((3, bsz, HA, s, DK_A), BF16),
        grid=(bsz, ncol),
        in_specs=[pl.BlockSpec((1, s, GDN_COLS), lambda b, c: (b, 0, c)),
                  pl.BlockSpec((CONV_W, GDN_COLS), lambda b, c: (0, c))],
        out_specs=pl.BlockSpec((1, 1, hpc, s, DK_A), lambda b, c: (c // (HA // hpc), b, c % (HA // hpc), 0, 0)),
        compiler_params=_cparams(("parallel", "parallel")),
        name="gdn_conv_silu_l2norm",
    )(main, conv_w)


def _col_rep(row, n):
    return jnp.broadcast_to(row, (LANES, n)).T


def _dot_hilo(x, m):
    hi = x.astype(BF16)
    lo = (x - hi.astype(F32)).astype(BF16)
    return jnp.dot(hi, m, preferred_element_type=F32) + jnp.dot(lo, m, preferred_element_type=F32)


def _gdn_prep_kernel(q_ref, k_ref, v_ref, g_ref, b_ref, u_ref, w_ref, qg_ref, kd_ref, a_ref, egl_ref):
    n = GDN_GROUP
    ri = lax.broadcasted_iota(jnp.int32, (n, n), 0)
    ci = lax.broadcasted_iota(jnp.int32, (n, n), 1)
    same = (ri // CHUNK) == (ci // CHUNK)
    incl = same & (ri >= ci)
    strict = same & (ri > ci)
    one_if = lambda m: jnp.where(m, 1.0, 0.0).astype(BF16)
    cum_m, tot_m = one_if(same & (ri <= ci)), one_if(same)
    eye = jnp.where(ri == ci, 1.0, 0.0)
    wide = lambda c: jnp.concatenate([c] * (n // LANES), axis=1)
    nt = (((1,), (1,)), ((), ()))
    heads = range(GDN_PH)
    ts, ps, rhs = [], [], []
    for h in heads:
        q, k, v = q_ref[0, 0, h], k_ref[0, 0, h], v_ref[0, 0, h]
        g8 = jnp.broadcast_to(g_ref[0, h], (SUBLANES, n))
        gc_row = _dot_hilo(g8, cum_m)[0:1]
        gl_row = _dot_hilo(g8, tot_m)[0:1]
        gc_c, gl_c, b_c = _col_rep(gc_row, n), _col_rep(gl_row, n), _col_rep(b_ref[0, h], n)
        decay = jnp.exp(jnp.where(incl, wide(gc_c) - gc_row, NEG))
        kk = lax.dot_general(k, k, nt, preferred_element_type=F32)
        qk = lax.dot_general(q, k, nt, preferred_element_type=F32)
        xb = jnp.where(strict, -(kk * wide(b_c) * decay), 0.0).astype(BF16)
        a = qk * decay
        kf = k.astype(F32)
        egc = jnp.exp(gc_c)
        qg_ref[0, h] = (q.astype(F32) * egc).astype(qg_ref.dtype)
        kd_ref[0, h] = (kf * jnp.exp(gl_c - gc_c)).astype(kd_ref.dtype)
        for c in range(n // CHUNK):
            blk = slice(c * CHUNK, (c + 1) * CHUNK)
            a_ref[0, h, blk, :] = a[blk, blk].astype(a_ref.dtype)
        egl = jnp.exp(gl_c)
        egl_ref[0, h, 0] = jnp.concatenate([egl[c * CHUNK:c * CHUNK + 1] for c in range(n // CHUNK)], axis=0)
        rhs.append(jnp.concatenate([(v.astype(F32) * b_c).astype(BF16), (kf * b_c * egc).astype(BF16)], axis=1))
        ts.append(eye + xb.astype(F32))
        ps.append(xb)
    ps = [jnp.dot(p, p, preferred_element_type=F32).astype(BF16) for p in ps]
    for step in range(5):
        for h in heads:
            if step < 4:
                tp = jnp.dot(jnp.concatenate([ts[h].astype(BF16), ps[h]], axis=0), ps[h], preferred_element_type=F32)
                ts[h] = ts[h] + tp[:n]
                ps[h] = tp[n:].astype(BF16)
            else:
                ts[h] = ts[h] + jnp.dot(ts[h].astype(BF16), ps[h], preferred_element_type=F32)
    for h in heads:
        uw = jnp.dot(ts[h].astype(BF16), rhs[h], preferred_element_type=F32)
        u_ref[0, h] = uw[:, :DV_A].astype(u_ref.dtype)
        w_ref[0, h] = uw[:, DV_A:].astype(w_ref.dtype)


def _gdn_prep(qkv, g_t, beta_t, bsz, s):
    n, ph = GDN_GROUP, GDN_PH
    tok = lambda width, dt: jax.ShapeDtypeStruct((bsz, HA, s, width), dt)
    tspec = lambda width: pl.BlockSpec((1, ph, n, width), lambda b, h, i: (b, h, i, 0))
    qspec = lambda sec: pl.BlockSpec((1, 1, ph, n, DK_A), lambda b, h, i: (sec, b, h, i, 0))
    rspec = pl.BlockSpec((1, ph, 1, n), lambda b, h, i: (b, h, 0, i))
    return pl.pallas_call(
        _gdn_prep_kernel,
        out_shape=(tok(DV_A, BF16), tok(DK_A, BF16), tok(DK_A, BF16), tok(DK_A, BF16), tok(CHUNK, BF16),
                   jax.ShapeDtypeStruct((bsz, HA, s // n, n // CHUNK, LANES), F32)),
        grid=(bsz, HA // ph, s // n),
        in_specs=[qspec(0), qspec(1), qspec(2), rspec, rspec],
        out_specs=(tspec(DV_A), tspec(DK_A), tspec(DK_A), tspec(DK_A), tspec(CHUNK),
                   pl.BlockSpec((1, ph, 1, n // CHUNK, LANES), lambda b, h, i: (b, h, i, 0, 0))),
        compiler_params=_cparams(("parallel", "parallel", "parallel")),
        name="gdn_chunk_prep",
    )(qkv, qkv, qkv, g_t, beta_t)


def _gdn_scan_kernel(u_ref, w_ref, qg_ref, kd_ref, a_ref, egl_ref, z_ref, ng_ref, o_ref, state_ref, *, nchunk):
    @pl.when(pl.program_id(1) == 0)
    def _():
        state_ref[...] = jnp.zeros_like(state_ref)

    tn = (((0,), (0,)), ((), ()))
    heads = range(HA)

    def body(c, carry):
        rows = pl.ds(pl.multiple_of(c * CHUNK, CHUNK), CHUNK)
        st = [state_ref[h] for h in heads]
        sb = [x.astype(BF16) for x in st]
        vb = [(u_ref[0, h, rows, :].astype(F32)
               - jnp.dot(w_ref[0, h, rows, :], sb[h], preferred_element_type=F32)).astype(BF16) for h in heads]
        o = [jnp.dot(qg_ref[0, h, rows, :], sb[h], preferred_element_type=F32)
             + jnp.dot(a_ref[0, h, rows, :], vb[h], preferred_element_type=F32) for h in heads]
        for h in heads:
            state_ref[h] = (st[h] * egl_ref[0, h, pl.ds(c, 1), :]
                            + lax.dot_general(kd_ref[0, h, rows, :], vb[h], tn, preferred_element_type=F32))
        for h in heads:
            cols = slice(h * DV_A, (h + 1) * DV_A)
            z = z_ref[0, rows, cols].astype(F32)
            on = o[h] * lax.rsqrt(jnp.mean(o[h] * o[h], axis=-1, keepdims=True) + EPS) * ng_ref[...]
            o_ref[0, rows, cols] = (on * (z * jax.nn.sigmoid(z))).astype(o_ref.dtype)
        return carry

    lax.fori_loop(0, nchunk, body, 0)


def _gdn_scan(u, w, qg, kd, a, egl, main, norm_gdn, bsz, s):
    ts = GDN_TS
    z_blk0 = 3 * HA * DK_A // (HA * DV_A)
    hspec = lambda width: pl.BlockSpec((1, HA, ts, width), lambda b, i: (b, 0, i, 0))
    return pl.pallas_call(
        functools.partial(_gdn_scan_kernel, nchunk=ts // CHUNK),
        out_shape=jax.ShapeDtypeStruct((bsz, s, HA * DV_A), BF16),
        grid=(bsz, s // ts),
        in_specs=[hspec(DV_A), hspec(DK_A), hspec(DK_A), hspec(DK_A), hspec(CHUNK),
                  pl.BlockSpec((1, HA, ts // CHUNK, LANES), lambda b, i: (b, 0, i, 0)),
                  pl.BlockSpec((1, ts, HA * DV_A), lambda b, i: (b, i, z_blk0)),
                  pl.BlockSpec((1, DV_A), lambda b, i: (0, 0))],
        out_specs=pl.BlockSpec((1, ts, HA * DV_A), lambda b, i: (b, i, 0)),
        scratch_shapes=[pltpu.VMEM((HA, DK_A, DV_A), F32)],
        compiler_params=_cparams(("parallel", "arbitrary")),
        name="gdn_delta_scan",
    )(u, w, qg, kd, a, egl, main, norm_gdn.reshape(1, DV_A))


def _gated_deltanet(main, a_in, b_in, conv_w, a_log, dt_bias, norm_gdn, bsz, s):
    qkv = _gdn_conv(main, conv_w, bsz, s)
    g = -jnp.exp(a_log.astype(F32)) * jax.nn.softplus(a_in.astype(F32) + dt_bias.astype(F32))
    beta = jax.nn.sigmoid(b_in.astype(F32))
    g_t = g.transpose(0, 2, 1).reshape(bsz, HA, 1, s)
    beta_t = beta.transpose(0, 2, 1).reshape(bsz, HA, 1, s)
    u, w, qg, kd, a, egl = _gdn_prep(qkv, g_t, beta_t, bsz, s)
    egl = egl.reshape(bsz, HA, s // CHUNK, LANES)
    return _gdn_scan(u, w, qg, kd, a, egl, main, norm_gdn, bsz, s)


NSA_TS = 1024
BIG = 1e30
DROPPED = -3e38


def _rope(x, cos, sin, lane):
    half = ROT_DIM // 2
    partner = jnp.where(lane < half, pltpu.roll(x, DH_B - half, axis=1), pltpu.roll(x, half, axis=1))
    return x * cos + partner * sin


def _nsa_rope_kernel(q_ref, kc_ref, vc_ref, ks_ref, kw_ref, cos_ref, sin_ref,
                     qo_ref, kco_ref, vco_ref, kso_ref, kwo_ref):
    cos, sin = cos_ref[0], sin_ref[0]
    lane = lax.broadcasted_iota(jnp.int32, cos.shape, 1)
    head = lambda ref, h: ref[0, :, h * DH_B:(h + 1) * DH_B].astype(F32)
    for h in range(HB):
        qo_ref[0, :, h * DH_B:(h + 1) * DH_B] = (_rope(head(q_ref, h), cos, sin, lane) * DH_B ** -0.5).astype(qo_ref.dtype)
    for g in range(G_KV):
        cols = slice(g * DH_B, (g + 1) * DH_B)
        kco_ref[0, g] = _rope(head(kc_ref, g), cos, sin, lane).astype(kco_ref.dtype)
        vco_ref[0, g] = vc_ref[0, :, cols]
        kso_ref[0, :, cols] = _rope(head(ks_ref, g), cos, sin, lane).astype(kso_ref.dtype)
        kwo_ref[0, :, cols] = _rope(head(kw_ref, g), cos, sin, lane).astype(kwo_ref.dtype)


def _nsa_rope(main, cos, sin, col, bsz, s):
    ts = NSA_TS
    kvw = G_KV * DH_B
    tok = lambda width, c0: pl.BlockSpec((1, ts, width), lambda b, i: (b, i, c0 // width))
    tab = pl.BlockSpec((1, ts, DH_B), lambda b, i: (b, i, 0))
    grp = pl.BlockSpec((1, G_KV, ts, DH_B), lambda b, i: (b, 0, i, 0))
    flat = lambda width: pl.BlockSpec((1, ts, width), lambda b, i: (b, i, 0))
    return pl.pallas_call(
        _nsa_rope_kernel,
        out_shape=(jax.ShapeDtypeStruct((bsz, s, HB * DH_B), BF16),
                   jax.ShapeDtypeStruct((bsz, G_KV, s, DH_B), BF16), jax.ShapeDtypeStruct((bsz, G_KV, s, DH_B), BF16),
                   jax.ShapeDtypeStruct((bsz, s, kvw), BF16), jax.ShapeDtypeStruct((bsz, s, kvw), BF16)),
        grid=(bsz, s // ts),
        in_specs=[tok(HB * DH_B, col["qb"]), tok(kvw, col["kc"]), tok(kvw, col["vc"]), tok(kvw, col["ks"]),
                  tok(kvw, col["kw"]), tab, tab],
        out_specs=(flat(HB * DH_B), grp, grp, flat(kvw), flat(kvw)),
        compiler_params=_cparams(("parallel", "parallel")),
        name="nsa_rotary",
    )(main, main, main, main, main, cos, sin)


def _gelu_tanh(x):
    return 0.5 * x * (1.0 + jnp.tanh(math.sqrt(2.0 / math.pi) * (x + 0.044715 * x * x * x)))


def _nsa_compress_kernel(k_ref, v_ref, pek_ref, pev_ref, w1k_ref, w1v_ref, w2k_ref, w2v_ref, ko_ref, vo_ref):
    nseg = k_ref.shape[2]
    for x_ref, pe_ref, w1_ref, w2_ref, o_ref in ((k_ref, pek_ref, w1k_ref, w2k_ref, ko_ref),
                                                 (v_ref, pev_ref, w1v_ref, w2v_ref, vo_ref)):
        for g in range(G_KV):
            x = x_ref[0, g].astype(F32)
            lo = jnp.dot((x + pe_ref[0:1]).astype(BF16), w1_ref[0], preferred_element_type=F32)
            hi = jnp.dot((x + pe_ref[1:2]).astype(BF16), w1_ref[1], preferred_element_type=F32)
            pre = lo + pltpu.roll(hi, nseg - 1, axis=0)
            o_ref[0, g] = jnp.dot(_gelu_tanh(pre).astype(BF16), w2_ref[...],
                                  preferred_element_type=F32).astype(o_ref.dtype)


def _nsa_compress(kc_t, vc_t, pe_ck, w1_ck, w2_ck, pe_cv, w1_cv, w2_cv, bsz, s):
    assert L_CMP == 2 * STRIDE_CMP
    nseg = s // STRIDE_CMP
    width = STRIDE_CMP * DH_B
    seg = lambda t: t.reshape(bsz, G_KV, nseg, width)
    pe2 = lambda pe: pe.reshape(2, width)
    w1h = lambda w: w.reshape(2, width, DH_B).astype(BF16)
    xspec = pl.BlockSpec((1, G_KV, nseg, width), lambda b: (b, 0, 0, 0))
    pspec = pl.BlockSpec((2, width), lambda b: (0, 0))
    w1spec = pl.BlockSpec((2, width, DH_B), lambda b: (0, 0, 0))
    w2spec = pl.BlockSpec((DH_B, DH_B), lambda b: (0, 0))
    ospec = pl.BlockSpec((1, G_KV, nseg, DH_B), lambda b: (b, 0, 0, 0))
    oshape = jax.ShapeDtypeStruct((bsz, G_KV, nseg, DH_B), BF16)
    return pl.pallas_call(
        _nsa_compress_kernel,
        out_shape=(oshape, oshape),
        grid=(bsz,),
        in_specs=[xspec, xspec, pspec, pspec, w1spec, w1spec, w2spec, w2spec],
        out_specs=(ospec, ospec),
        compiler_params=_cparams(("parallel",)),
        name="nsa_compress",
    )(seg(kc_t), seg(vc_t), pe2(pe_ck), pe2(pe_cv), w1h(w1_ck), w1h(w1_cv), w2_ck.astype(BF16), w2_cv.astype(BF16))


def _nsa_cmp_kernel(q_ref, kc_ref, vc_ref, ov_ref, o_ref, mb_ref, *, tq, n_slc):
    i = pl.program_id(2)
    rows = R_GRP * tq
    ncmp = kc_ref.shape[2]
    qa = _stack_heads(q_ref[0], None)
    s = lax.dot_general(qa, kc_ref[0, 0], (((1,), (1,)), ((), ())), preferred_element_type=F32)
    t_row = (lax.broadcasted_iota(jnp.int32, (rows, ncmp), 0) & (tq - 1)) + i * tq
    c_end = lax.broadcasted_iota(jnp.int32, (rows, ncmp), 1) * STRIDE_CMP + (L_CMP - 1)
    valid = c_end <= t_row
    sm = jnp.where(valid, s, NEG)
    p = jnp.where(valid, jnp.exp(sm - jnp.max(sm, axis=1, keepdims=True)), 0.0)
    l = jnp.sum(p, axis=1, keepdims=True)
    p = p * (1.0 / jnp.where(l > 0.0, l, 1.0))
    _unstack_heads(o_ref, jnp.dot(p.astype(BF16), vc_ref[0, 0], preferred_element_type=F32), tq)
    psum = p[0:tq]
    for r in range(1, R_GRP):
        psum = psum + p[r * tq:(r + 1) * tq]
    imp = _dot_hilo(psum, ov_ref[...])
    nb = -(-n_slc // SUBLANES) * SUBLANES
    v = imp.T[:nb]
    blk = lax.broadcasted_iota(jnp.int32, (nb, tq), 0)
    cur = (lax.broadcasted_iota(jnp.int32, (nb, tq), 1) + i * tq) // L_SLC
    forced = (blk == 0) | (blk == cur)
    v = jnp.where(forced, BIG, jnp.where(blk <= cur, v, -BIG))
    sel = jnp.zeros((nb, tq), F32)
    for _ in range(T_SEL):
        m = jnp.max(v, axis=0, keepdims=True)
        first = jnp.min(jnp.where(v == m, blk, LANES), axis=0, keepdims=True)
        hit = blk == first
        sel = jnp.where(hit, 1.0, sel)
        v = jnp.where(hit, DROPPED, v)
    bias = (jnp.where(blk <= cur, sel, 0.0) - 1.0) * BIG
    bias = jnp.concatenate([bias, jnp.zeros((LANES - nb, tq), F32)], axis=0)
    mb_ref[0, 0] = bias.T.astype(mb_ref.dtype)


def _nsa_cmp_select(q, k_cmp, v_cmp, bsz, s):
    tq = CMP_TQ
    nseg = s // STRIDE_CMP
    n_slc = s // L_SLC
    assert nseg <= LANES or nseg % LANES == 0
    c_start = np.arange(nseg) * STRIDE_CMP
    j_start = np.arange(n_slc) * L_SLC
    overlap = ((c_start[:, None] < j_start[None, :] + L_SLC) & (c_start[:, None] + L_CMP > j_start[None, :]))
    overlap = jnp.asarray(np.pad(overlap.astype(np.float32), ((0, 0), (0, LANES - n_slc))), BF16)
    qspec = pl.BlockSpec((1, tq, R_GRP * DH_B), lambda b, g, i: (b, i, g))
    cspec = pl.BlockSpec((1, 1, nseg, DH_B), lambda b, g, i: (b, g, 0, 0))
    return pl.pallas_call(
        functools.partial(_nsa_cmp_kernel, tq=tq, n_slc=n_slc),
        out_shape=(jax.ShapeDtypeStruct((bsz, s, HB * DH_B), BF16), jax.ShapeDtypeStruct((bsz, G_KV, s, LANES), BF16)),
        grid=(bsz, G_KV, s // tq),
        in_specs=[qspec, cspec, cspec, pl.BlockSpec((nseg, LANES), lambda b, g, i: (0, 0))],
        out_specs=(qspec, pl.BlockSpec((1, 1, tq, LANES), lambda b, g, i: (b, g, i, 0))),
        compiler_params=_cparams(("parallel", "parallel", "parallel")),
        name="nsa_compressed_select",
    )(q, k_cmp, v_cmp, overlap)


def _stack_heads(q, extra):
    parts = []
    for r in range(R_GRP):
        qr = q[:, r * DH_B:(r + 1) * DH_B]
        parts.append(qr if extra is None else jnp.concatenate([qr, extra], axis=1))
    return jnp.concatenate(parts, axis=0)


def _unstack_heads(o_ref, o, tq):
    for r in range(R_GRP):
        o_ref[0, :, r * DH_B:(r + 1) * DH_B] = o[r * tq:(r + 1) * tq].astype(o_ref.dtype)


def _heads_t(q):
    return [q[:, r * DH_B:(r + 1) * DH_B].astype(F32).T.astype(BF16) for r in range(R_GRP)]


def _softmax_step_t(ss, vs, carry):
    m, l, acc = carry
    m_new = functools.reduce(jnp.maximum, [jnp.max(s, axis=0, keepdims=True) for s in ss], m)
    alpha = jnp.exp(m - m_new)
    ps = [jnp.exp(s - m_new) for s in ss]
    l = alpha * l + functools.reduce(lambda a, b: a + b, [jnp.sum(p, axis=0, keepdims=True) for p in ps])
    tn = (((0,), (0,)), ((), ()))
    pv = [lax.dot_general(v, p.astype(BF16), tn, preferred_element_type=F32) for v, p in zip(vs, ps)]
    return m_new, l, functools.reduce(lambda a, b: a + b, pv, alpha * acc)


def _softmax_init_t(rows):
    return (jnp.full((1, rows), -jnp.inf, F32), jnp.zeros((1, rows), F32), jnp.zeros((DH_B, rows), F32))


def _selected_branch(i, qts, mb, ks_ref, oh_ref, vs_ref, tq):
    rows = R_GRP * tq
    mbt = mb.astype(F32).T.astype(BF16)
    qat = jnp.concatenate([jnp.concatenate([qt, mbt], axis=0) for qt in qts], axis=1)

    def scores(j):
        keys = pl.ds(pl.multiple_of(j * tq, tq), tq)
        k = jnp.concatenate([ks_ref[0, keys, :], oh_ref[keys, :]], axis=1)
        return jnp.dot(k, qat, preferred_element_type=F32)

    def values(j):
        return vs_ref[0, pl.ds(pl.multiple_of(j * tq, tq), tq), :]

    def pair(jj, c):
        j = 2 * jj
        return _softmax_step_t([scores(j), scores(j + 1)], [values(j), values(j + 1)], c)

    carry = lax.fori_loop(0, i // 2, pair, _softmax_init_t(rows))
    k_loc = lax.broadcasted_iota(jnp.int32, (tq, tq), 0)
    t_loc = lax.broadcasted_iota(jnp.int32, (tq, tq), 1)
    causal = jnp.concatenate([jnp.where(k_loc <= t_loc, 0.0, NEG)] * R_GRP, axis=1)
    diag = lambda c: _softmax_step_t([scores(i) + causal], [values(i)], c)
    both = lambda c: _softmax_step_t([scores(i) + causal, scores(i - 1)], [values(i), values(i - 1)], c)
    m, l, acc = lax.cond((i % 2) == 1, both, diag, carry)
    return acc / l


def _window_branch(i, qts, kw_refs, vw_refs, tq):
    rows = R_GRP * tq
    qt = jnp.concatenate(qts, axis=1)
    k_loc = lax.broadcasted_iota(jnp.int32, (tq, tq), 0)
    t_loc = lax.broadcasted_iota(jnp.int32, (tq, tq), 1)
    bound = jnp.minimum(t_loc[0:1, :] + (i * tq + 1), WINDOW)
    ss = []
    for n in range(3):
        dist = t_loc + (2 - n) * tq - k_loc
        bias = jnp.where(dist.astype(jnp.uint32) < bound.astype(jnp.uint32), 0.0, NEG)
        ss.append(jnp.dot(kw_refs[n][0], qt, preferred_element_type=F32) + jnp.concatenate([bias] * R_GRP, axis=1))
    m, l, acc = _softmax_step_t(ss, [r[0] for r in vw_refs], _softmax_init_t(rows))
    return acc / l


def _nsa_local_kernel(q_ref, mb_ref, ks_ref, oh_ref, vs_ref, kw0, kw1, kw2, vw0, vw1, vw2, oc_ref, gate_ref, o_ref,
                      *, tq, gate_lane0):
    g, i = pl.program_id(1), pl.program_id(2)
    q = q_ref[0]
    qts = _heads_t(q)
    o_slc = _selected_branch(i, qts, mb_ref[0, 0], ks_ref, oh_ref, vs_ref, tq)
    o_win = _window_branch(i, qts, (kw0, kw1, kw2), (vw0, vw1, vw2), tq)
    gates = jax.nn.sigmoid(gate_ref[0])
    lane = lax.broadcasted_iota(jnp.int32, gates.shape, 1)
    pick = lambda idx: jnp.sum(jnp.where(lane == idx, gates, 0.0), axis=1, keepdims=True)
    for r in range(R_GRP):
        base = gate_lane0 + (g * R_GRP + r) * 3
        rows = slice(r * tq, (r + 1) * tq)
        cols = slice(r * DH_B, (r + 1) * DH_B)
        o = (pick(base) * oc_ref[0, :, cols].astype(F32) + pick(base + 1) * o_slc[:, rows].T
             + pick(base + 2) * o_win[:, rows].T)
        o_ref[0, :, cols] = o.astype(o_ref.dtype)


def _nsa_local(q, maskbias, ks, main, kw, o_cmp, small, col, gate_lane0, bsz, s):
    tq = ATT_TQ
    assert 2 * tq >= WINDOW
    onehot = jnp.asarray(np.arange(s)[:, None] // L_SLC == np.arange(LANES)[None, :], BF16)
    qspec = pl.BlockSpec((1, tq, R_GRP * DH_B), lambda b, g, i: (b, i, g))
    seq = lambda c0: pl.BlockSpec((1, s, DH_B), lambda b, g, i: (b, 0, c0 // DH_B + g))
    back = lambda c0, n: pl.BlockSpec((1, tq, DH_B), lambda b, g, i: (b, jnp.maximum(i - n, 0), c0 // DH_B + g))
    return pl.pallas_call(
        functools.partial(_nsa_local_kernel, tq=tq, gate_lane0=gate_lane0),
        out_shape=jax.ShapeDtypeStruct((bsz, s, HB * DH_B), BF16),
        grid=(bsz, G_KV, s // tq),
        in_specs=[qspec,
                  pl.BlockSpec((1, 1, tq, LANES), lambda b, g, i: (b, g, i, 0)),
                  seq(0), pl.BlockSpec((s, LANES), lambda b, g, i: (0, 0)), seq(col["vs"]),
                  back(0, 2), back(0, 1), back(0, 0),
                  back(col["vw"], 2), back(col["vw"], 1), back(col["vw"], 0),
                  qspec, pl.BlockSpec((1, tq, LANES), lambda b, g, i: (b, i, 0))],
        out_specs=qspec,
        compiler_params=_cparams(("parallel", "parallel", "arbitrary")),
        name="nsa_selected_window_combine",
    )(q, maskbias, ks, onehot, main, kw, kw, kw, main, main, main, o_cmp, small)


def _moe_sub_blocks(nv_ref, out_ref, compute):
    nsub = (nv_ref[pl.program_id(1)] + MOE_SUB - 1) // MOE_SUB
    rows_of = lambda sb: pl.ds(pl.multiple_of(sb * MOE_SUB, MOE_SUB), MOE_SUB)

    def run(first, count):
        rows = [rows_of(first + u) for u in range(count)]
        for r, val in zip(rows, compute(rows)):
            out_ref[r, :] = val

    def quad(p, carry):
        run(4 * p, 4)
        return carry

    def dead(sb, carry):
        out_ref[rows_of(sb), :] = jnp.zeros((MOE_SUB, out_ref.shape[1]), out_ref.dtype)
        return carry

    lax.fori_loop(0, nsub // 4, quad, 0)
    done = (nsub // 4) * 4

    @pl.when((nsub & 2) != 0)
    def _():
        run(done, 2)

    @pl.when((nsub & 1) != 0)
    def _():
        run(done + (nsub & 2), 1)

    lax.fori_loop(nsub, MOE_TM // MOE_SUB, dead, 0)


def _moe_new_expert(be_ref):
    i = pl.program_id(1)
    return (i == 0) | (be_ref[i] != be_ref[jnp.maximum(i - 1, 0)])


def _moe_up_kernel(be_ref, nv_ref, st_ref, x_ref, wg_ref, wu_ref, bg_ref, bu_ref, h_ref, wgb_ref, wub_ref):
    @pl.when(_moe_new_expert(be_ref))
    def _():
        wgb_ref[...] = wg_ref[0].astype(BF16)
        wub_ref[...] = wu_ref[0].astype(BF16)

    def compute(rows):
        xs = [jnp.concatenate(_unpack_bf16_pairs(x_ref[r, :]), axis=1).astype(BF16) for r in rows]
        gates = [jnp.dot(x, wgb_ref[...], preferred_element_type=F32) + bg_ref[0] for x in xs]
        ups = [jnp.dot(x, wub_ref[...], preferred_element_type=F32) + bu_ref[0] for x in xs]
        outs = []
        for gate, up in zip(gates, ups):
            gate = jnp.minimum(gate, SWIGLU_LIMIT)
            up = jnp.clip(up, -SWIGLU_LIMIT, SWIGLU_LIMIT)
            outs.append(((up + 1.0) * gate * jax.nn.sigmoid(SWIGLU_ALPHA * gate)).astype(h_ref.dtype))
        return outs

    _moe_sub_blocks(nv_ref, h_ref, compute)


def _moe_down_kernel(be_ref, nv_ref, st_ref, h_ref, wd_ref, bd_ref, y_ref, wdb_ref):
    @pl.when(_moe_new_expert(be_ref))
    def _():
        wdb_ref[...] = wd_ref[0].astype(BF16)

    half = wdb_ref.shape[1] // 2

    def compute(rows):
        ys = [jnp.dot(h_ref[r, :], wdb_ref[...], preferred_element_type=F32) + bd_ref[0] for r in rows]
        return [_pack_bf16_pairs(y[:, :half], y[:, half:]) for y in ys]

    _moe_sub_blocks(nv_ref, y_ref, compute)


def _moe_experts(rows, block_e, n_valid, src_tile, w_gate, b_gate, w_up, b_up, w_down, b_down):
    n_rows, d = rows.shape[0], 2 * rows.shape[1]
    n_blocks = n_rows // MOE_TM
    tile = lambda width: pl.BlockSpec((MOE_TM, width), lambda n, i, be, nv, st: (st[i], 0))
    wcol = lambda k: pl.BlockSpec((1, k, tn), lambda n, i, be, nv, st: (be[i], 0, n))
    bcol = lambda: pl.BlockSpec((1, 1, tn), lambda n, i, be, nv, st: (be[i], 0, n))
    params = _cparams(("arbitrary", "arbitrary"))
    tn = MOE_TN
    h = pl.pallas_call(
        _moe_up_kernel,
        out_shape=jax.ShapeDtypeStruct((n_rows, D_FF), BF16),
        grid_spec=pltpu.PrefetchScalarGridSpec(
            num_scalar_prefetch=3, grid=(D_FF // tn, n_blocks),
            in_specs=[tile(d // 2), wcol(d), wcol(d), bcol(), bcol()],
            out_specs=pl.BlockSpec((MOE_TM, tn), lambda n, i, be, nv, st: (i, n)),
            scratch_shapes=[pltpu.VMEM((d, tn), BF16), pltpu.VMEM((d, tn), BF16)]),
        compiler_params=params, name="moe_expert_up",
    )(block_e, n_valid, src_tile, rows, w_gate, w_up, b_gate.reshape(N_EXP, 1, D_FF), b_up.reshape(N_EXP, 1, D_FF))
    tn = MOE_TN_DOWN
    return pl.pallas_call(
        _moe_down_kernel,
        out_shape=jax.ShapeDtypeStruct((n_rows, d // 2), jnp.uint32),
        grid_spec=pltpu.PrefetchScalarGridSpec(
            num_scalar_prefetch=3, grid=(d // tn, n_blocks),
            in_specs=[tile(D_FF), wcol(D_FF), bcol()],
            out_specs=pl.BlockSpec((MOE_TM, tn // 2), lambda n, i, be, nv, st: (i, n)),
            scratch_shapes=[pltpu.VMEM((D_FF, tn), BF16)]),
        compiler_params=params, name="moe_expert_down",
    )(block_e, n_valid, src_tile, h, w_down, b_down.reshape(N_EXP, 1, d))


ROUTE_TT = 512
MOVE_TT = 512


def _pack_bf16_pairs(lo, hi):
    as_bits = lambda v: pltpu.bitcast(v.astype(BF16).astype(F32), jnp.uint32)
    return (as_bits(lo) >> 16) | (as_bits(hi) & jnp.uint32(0xFFFF0000))


def _unpack_bf16_pairs(w):
    return pltpu.bitcast(w << 16, F32), pltpu.bitcast(w & jnp.uint32(0xFFFF0000), F32)


def _route_kernel(h_ref, wr_ref, br_ref, e_ref, w_ref, p_ref, cnt_ref, run_ref, *, tt):
    @pl.when(pl.program_id(0) == 0)
    def _():
        run_ref[...] = jnp.zeros_like(run_ref)

    h = h_ref[...]
    logits = (jnp.dot(h, wr_ref[0], preferred_element_type=F32) + jnp.dot(h, wr_ref[1], preferred_element_type=F32)
              + br_ref[...])
    lane = lax.broadcasted_iota(jnp.int32, (tt, LANES), 1)
    v = jnp.where(lane < N_EXP, logits, -BIG)
    tops, hits, firsts = [], [], []
    for _ in range(TOP_K):
        m = jnp.max(v, axis=1, keepdims=True)
        first = jnp.min(jnp.where(v == m, lane, LANES), axis=1, keepdims=True)
        hit = lane == first
        v = jnp.where(hit, DROPPED, v)
        tops.append(m), hits.append(hit), firsts.append(first)
    ex = [jnp.exp(m - tops[0]) for m in tops]
    inv = 1.0 / functools.reduce(lambda a, b: a + b, ex)
    onehot = functools.reduce(lambda a, b: a + b, [jnp.where(hh, 1.0, 0.0) for hh in hits]).astype(BF16)
    ri = lax.broadcasted_iota(jnp.int32, (tt, tt), 0)
    ci = lax.broadcasted_iota(jnp.int32, (tt, tt), 1)
    before = jnp.where(ci < ri, 1.0, 0.0).astype(BF16)
    rank = jnp.dot(before, onehot, preferred_element_type=F32) + run_ref[0:1]
    run_ref[...] = run_ref[...] + jnp.dot(jnp.ones((SUBLANES, tt), BF16), onehot, preferred_element_type=F32)
    cnt_ref[...] = run_ref[...].astype(jnp.int32)
    place = lambda cols, zero: functools.reduce(
        lambda acc, kc: jnp.where(lane == kc[0], kc[1], acc), list(enumerate(cols)), zero)
    e_ref[...] = place(firsts, jnp.zeros((tt, LANES), jnp.int32))
    w_ref[...] = place([e * inv for e in ex], jnp.zeros((tt, LANES), F32))
    pos = [jnp.sum(jnp.where(hh, rank, 0.0), axis=1, keepdims=True).astype(jnp.int32) for hh in hits]
    p_ref[...] = place(pos, jnp.zeros((tt, LANES), jnp.int32))


def _route(h, w_router, b_router):
    n, d = h.shape
    tt = ROUTE_TT
    wr = jnp.pad(w_router, ((0, 0), (0, LANES - N_EXP)))
    hi = wr.astype(BF16)
    wr2 = jnp.stack([hi, (wr - hi.astype(F32)).astype(BF16)])
    br = jnp.pad(b_router, (0, LANES - N_EXP)).reshape(1, LANES)
    tok = lambda dt: jax.ShapeDtypeStruct((n, LANES), dt)
    tspec = pl.BlockSpec((tt, LANES), lambda i: (i, 0))
    return pl.pallas_call(
        functools.partial(_route_kernel, tt=tt),
        out_shape=(tok(jnp.int32), tok(F32), tok(jnp.int32), jax.ShapeDtypeStruct((SUBLANES, LANES), jnp.int32)),
        grid=(n // tt,),
        in_specs=[pl.BlockSpec((tt, d), lambda i: (i, 0)), pl.BlockSpec((2, d, LANES), lambda i: (0, 0, 0)),
                  pl.BlockSpec((1, LANES), lambda i: (0, 0))],
        out_specs=(tspec, tspec, tspec, pl.BlockSpec((SUBLANES, LANES), lambda i: (0, 0))),
        scratch_shapes=[pltpu.VMEM((SUBLANES, LANES), F32)],
        compiler_params=_cparams(("arbitrary",)),
        name="moe_route",
    )(h, wr2, br)


def _dispatch_kernel(dest_ref, src_ref, init_ref, rows_ref, sem, *, tt):
    del init_ref
    t0 = pl.program_id(0) * tt

    def copy(j, k):
        return pltpu.make_async_copy(src_ref.at[pl.ds(j, 1)],
                                     rows_ref.at[pl.ds(dest_ref[(t0 + j) * TOP_K + k], 1)], sem)

    def start(j, carry):
        for k in range(TOP_K):
            copy(j, k).start(priority=k % 2)
        return carry

    def wait(j, carry):
        for k in range(TOP_K):
            copy(j, k).wait()
        return carry

    lax.fori_loop(0, tt, start, 0)
    lax.fori_loop(0, tt, wait, 0)


def _dispatch(dest, hpk, n_rows):
    n, width = hpk.shape
    tt = MOVE_TT
    return pl.pallas_call(
        functools.partial(_dispatch_kernel, tt=tt),
        out_shape=jax.ShapeDtypeStruct((n_rows, width), jnp.uint32),
        grid_spec=pltpu.PrefetchScalarGridSpec(
            num_scalar_prefetch=1, grid=(n // tt,),
            in_specs=[pl.BlockSpec((tt, width), lambda i, dest: (i, 0)), pl.BlockSpec(memory_space=pl.ANY)],
            out_specs=pl.BlockSpec(memory_space=pl.ANY),
            scratch_shapes=[pltpu.SemaphoreType.DMA]),
        input_output_aliases={2: 0},
        compiler_params=_cparams(("arbitrary",)),
        name="moe_dispatch",
    )(dest, hpk, jnp.zeros((n_rows, width), jnp.uint32))


def _combine_kernel(dest_ref, y_ref, w_ref, x1_ref, g_ref, o_ref, buf_ref, sem, *, tt, norm):
    step = pl.program_id(0)
    slot = step % 2

    def copy(tile, sl, j, k):
        return pltpu.make_async_copy(y_ref.at[pl.ds(dest_ref[(tile * tt + j) * TOP_K + k], 1)],
                                     buf_ref.at[sl, k, pl.ds(j, 1)], sem.at[sl])

    def gather(tile, sl):
        def start(j, carry):
            for k in range(TOP_K):
                copy(tile, sl, j, k).start(priority=k % 2)
            return carry
        lax.fori_loop(0, tt, start, 0)

    @pl.when(step == 0)
    def _():
        gather(step, slot)

    @pl.when(step + 1 < pl.num_programs(0))
    def _():
        gather(step + 1, 1 - slot)

    def wait(j, carry):
        for k in range(TOP_K):
            copy(step, slot, j, k).wait()
        return carry

    lax.fori_loop(0, tt, wait, 0)
    wts = w_ref[...]
    lane = lax.broadcasted_iota(jnp.int32, wts.shape, 1)
    x = x1_ref[...]
    nq = MOE_TN_DOWN // 2
    for k in range(TOP_K):
        wk = jnp.sum(jnp.where(lane == k, wts, 0.0), axis=1, keepdims=True)
        parts = []
        for n in range(D_MODEL // MOE_TN_DOWN):
            lo, hi = _unpack_bf16_pairs(buf_ref[slot, k, :, n * nq:(n + 1) * nq])
            parts += [lo, hi]
        x = x + wk * jnp.concatenate(parts, axis=1)
    if norm:
        x = x * lax.rsqrt(jnp.mean(x * x, axis=-1, keepdims=True) + EPS) * g_ref[...]
    o_ref[...] = x.astype(o_ref.dtype)


def _combine(dest, ypk, wts, x1, gain):
    n, d = x1.shape
    tt = MOVE_TT
    norm = gain is not None
    gain = gain if norm else jnp.ones((d,), F32)
    return pl.pallas_call(
        functools.partial(_combine_kernel, tt=tt, norm=norm),
        out_shape=jax.ShapeDtypeStruct((n, d), F32),
        grid_spec=pltpu.PrefetchScalarGridSpec(
            num_scalar_prefetch=1, grid=(n // tt,),
            in_specs=[pl.BlockSpec(memory_space=pl.ANY),
                      pl.BlockSpec((tt, LANES), lambda i, dest: (i, 0)),
                      pl.BlockSpec((tt, d), lambda i, dest: (i, 0)),
                      pl.BlockSpec((1, d), lambda i, dest: (0, 0))],
            out_specs=pl.BlockSpec((tt, d), lambda i, dest: (i, 0)),
            scratch_shapes=[pltpu.VMEM((2, TOP_K, tt, d // 2), jnp.uint32), pltpu.SemaphoreType.DMA((2,))]),
        compiler_params=_cparams(("arbitrary",)),
        name="moe_combine_norm",
    )(dest, ypk, wts, x1, gain.reshape(1, d))


def _native_sparse_attention(main, small, positions, pe_ck, w1_ck, w2_ck, pe_cv, w1_cv, w2_cv, col, gate_lane0, bsz, s):
    half = ROT_DIM // 2
    inv_freq = ROPE_THETA ** (-jnp.arange(half, dtype=F32) * 2.0 / ROT_DIM)
    ang = positions.astype(F32)[..., None] * inv_freq
    cos, sin = jnp.cos(ang), jnp.sin(ang)
    rest = (bsz, s, DH_B - ROT_DIM)
    cos_t = jnp.concatenate([cos, cos, jnp.ones(rest, F32)], axis=-1)
    sin_t = jnp.concatenate([-sin, sin, jnp.zeros(rest, F32)], axis=-1)
    q, kc_t, vc_t, ks, kw = _nsa_rope(main, cos_t, sin_t, col, bsz, s)
    k_cmp, v_cmp = _nsa_compress(kc_t, vc_t, pe_ck, w1_ck, w2_ck, pe_cv, w1_cv, w2_cv, bsz, s)
    o_cmp, maskbias = _nsa_cmp_select(q, k_cmp, v_cmp, bsz, s)
    return _nsa_local(q, maskbias, ks, main, kw, o_cmp, small, col, gate_lane0, bsz, s)


def _moe_ffn(hf, hpk, x1, final_gain, w_router, b_router, w_gate, b_gate, w_up, b_up, w_down, b_down):
    n_tok, d = x1.shape
    eidx, wts, pos, cnt = _route(hf, w_router, b_router)
    counts = cnt[0, :N_EXP]
    padded = ((counts + MOE_TM - 1) // MOE_TM) * MOE_TM
    pad_end = jnp.cumsum(padded)
    pad_start = pad_end - padded
    part = counts % MOE_TM
    gap = jnp.where(part > 0, MOE_TM - part, 0)
    pick = lambda table: jnp.sum(jnp.where(eidx[:, :TOP_K, None] == jnp.arange(N_EXP), table, 0), axis=-1)
    rank = pos[:, :TOP_K]
    dest = pick(pad_start) + rank + jnp.where(rank >= pick(part), pick(gap), 0)
    dest = dest.reshape(-1).astype(jnp.int32)
    n_blocks = (n_tok * TOP_K + N_EXP * (MOE_TM - 1) + MOE_TM - 1) // MOE_TM
    rows = _dispatch(dest, hpk, n_blocks * MOE_TM)
    tile0 = jnp.arange(n_blocks, dtype=jnp.int32) * MOE_TM
    block_e = jnp.minimum(jnp.searchsorted(pad_end, tile0, side='right'), N_EXP - 1).astype(jnp.int32)
    first = (tile0 == pad_start[block_e]) & (part[block_e] > 0)
    n_valid = jnp.where(tile0 < pad_end[-1], jnp.where(first, part[block_e], MOE_TM), 0).astype(jnp.int32)
    tile_id = jnp.arange(n_blocks, dtype=jnp.int32)
    src_tile = jnp.where(n_valid > 0, tile_id, jnp.maximum(pad_end[-1] // MOE_TM - 1, 0)).astype(jnp.int32)
    ypk = _moe_experts(rows, block_e, n_valid, src_tile, w_gate, b_gate, w_up, b_up, w_down, b_down)
    return _combine(dest, ypk, wts, x1, final_gain)


def _layer(x, positions, norm_mix, w_in, conv_w, a_log, dt_bias, norm_gdn, pe_ck, w1_ck, w2_ck,
           pe_cv, w1_cv, w2_cv, w_proj_a, w_proj_b, w_out, norm_ffn, w_router, b_router,
           w_gate, b_gate, w_up, b_up, w_down, b_down, final_gain):
    bsz, s, d = x.shape
    n_tok = bsz * s
    x2 = x.reshape(n_tok, d)
    h = _rmsnorm(x2, norm_mix, BF16)
    n_small = 2 * HA + 3 * HB
    sp = SPLIT_POINTS
    w_main = jnp.concatenate([w_in[:, :sp[3]], w_in[:, sp[5]:sp[12]], w_in[:, sp[13]:]], axis=1).astype(BF16)
    w_small = jnp.concatenate([w_in[:, sp[3]:sp[5]], w_in[:, sp[12]:sp[13]]], axis=1)
    w_small = jnp.pad(w_small, ((0, 0), (0, LANES - n_small))).astype(BF16)
    main = _matmul(h, w_main, BF16, tm=INPROJ_TM).reshape(bsz, s, -1)
    small = _matmul(h, w_small, F32).reshape(bsz, s, -1)
    names = ("qa", "ka", "va", "za", "qb", "kc", "vc", "ks", "vs", "kw", "vw", "gm")
    sizes = (HA * DK_A, HA * DK_A, HA * DV_A, HA * DV_A,
             HB * DH_B, G_KV * DH_B, G_KV * DH_B, G_KV * DH_B, G_KV * DH_B, G_KV * DH_B, G_KV * DH_B, 2 * D_MODEL)
    col = {nm: sum(sizes[:i]) for i, nm in enumerate(names)}
    aa, ba = small[..., :HA], small[..., HA:2 * HA]

    o_a = _gated_deltanet(main, aa, ba, conv_w, a_log, dt_bias, norm_gdn, bsz, s)
    o_b = _native_sparse_attention(main, small, positions, pe_ck, w1_ck, w2_ck, pe_cv, w1_cv, w2_cv,
                                   col, 2 * HA, bsz, s)
    merged = _merge(o_a.reshape(n_tok, d), o_b.reshape(n_tok, d), w_proj_a.astype(BF16),
                    w_proj_b.astype(BF16), main.reshape(n_tok, -1), col["gm"])
    x1, hf, hpk = _outproj(merged, w_out.astype(BF16), x2, norm_ffn)
    out = _moe_ffn(hf, hpk, x1, final_gain, w_router, b_router, w_gate, b_gate, w_up, b_up, w_down, b_down)
    return out.reshape(bsz, s, d)


def kernel(x, positions, norm_mix, w_in, conv_w, a_log, dt_bias, norm_gdn, pe_ck, w1_ck, w2_ck, pe_cv, w1_cv, w2_cv, w_proj_a, w_proj_b, w_out, norm_ffn, w_router, b_router, w_gate, b_gate, w_up, b_up, w_down, b_down, norm_final):
    depth = norm_mix.shape[0]
    for l in range(depth):
        x = _layer(x, positions, norm_mix[l], w_in[l], conv_w[l], a_log[l], dt_bias[l], norm_gdn[l],
                   pe_ck[l], w1_ck[l], w2_ck[l], pe_cv[l], w1_cv[l], w2_cv[l],
                   w_proj_a[l], w_proj_b[l], w_out[l], norm_ffn[l], w_router[l], b_router[l],
                   w_gate[l], b_gate[l], w_up[l], b_up[l], w_down[l], b_down[l],
                   norm_final if l + 1 == depth else None)
    return x
```
